```python
import math, functools
import jax, jax.numpy as jnp
from jax import lax
import numpy as np


D_MODEL = 2048
BATCH = 2
SEQ = 4096
DEPTH = 1
DEC_BATCH = 32
DEC_SEQ = 16
PAST_LEN = 4096

CHUNK = 64
Q_BLOCK = 128
A_HEADS = 8
A_DK = 64
A_DV = 2 * A_DK
A_WIDTH = A_HEADS * A_DV
B_HEAD = 64
B_WIDTH = D_MODEL - A_WIDTH
B_HEADS = B_WIDTH // B_HEAD
W_RANK = 64
A_RANK = 64
G_RANK = 128
RW_COLS = 3 * B_WIDTH + W_RANK + A_RANK + G_RANK
RW_SPLITS = [B_WIDTH, 2 * B_WIDTH, 3 * B_WIDTH, 3 * B_WIDTH + W_RANK, 3 * B_WIDTH + W_RANK + A_RANK]
IN_COLS = 3 * A_WIDTH + RW_COLS
D_FF = 5632
FFN_CONV = 3
EPS = 1e-6
GN_EPS = 64e-5

kernel_name = 'hymba_diffattn_rwkv7_convffn_stream'


def rms_norm(x, g):
    xf = x.astype(jnp.float32)
    y = xf * lax.rsqrt(jnp.mean(xf * xf, axis=-1, keepdims=True) + EPS)
    return (y * g.astype(jnp.float32)).astype(x.dtype)


def diff_attend(q, k, v, mask, lam):
    s = jnp.einsum('bqhmd,bkhmd->bhmqk', q, k).astype(jnp.float32) * (A_DK ** -0.5)
    if mask is not None:
        s = jnp.where(mask, s, -jnp.inf)
    p = jax.nn.softmax(s, axis=-1)
    a = p[:, :, 0] - lam * p[:, :, 1]
    return jnp.einsum('bhqk,bkhd->bqhd', a.astype(v.dtype), v)


def prompt_attention(q, k, v, lam):
    Bn, T = q.shape[0], q.shape[1]
    nb = T // Q_BLOCK
    qb = jnp.moveaxis(q.reshape(Bn, nb, Q_BLOCK, A_HEADS, 2, A_DK), 1, 0)
    key_chunk = jnp.arange(T) // CHUNK
    starts = jnp.arange(nb) * Q_BLOCK

    def block(args):
        qi, s0 = args
        q_chunk = (s0 + jnp.arange(Q_BLOCK)) // CHUNK
        mask = key_chunk[None, :] <= q_chunk[:, None]
        return diff_attend(qi, k, v, mask, lam)

    o = lax.map(block, (qb, starts))
    return jnp.moveaxis(o, 0, 1).reshape(Bn, T, A_HEADS, A_DV)


def sample_attention(q, k, v, lam, cache_k, cache_v):
    Bn, P = cache_k.shape[0], cache_k.shape[1]
    k_all = jnp.concatenate([cache_k.reshape(Bn, P, A_HEADS, 2, A_DK).astype(k.dtype), k], axis=1)
    v_all = jnp.concatenate([cache_v.astype(v.dtype), v], axis=1)
    return diff_attend(q, k_all, v_all, None, lam)


def wkv_scan(r, decay, k, v, a, b, s0):
    def step(S, inp):
        r_t, d_t, k_t, v_t, a_t, b_t = inp
        sa = jnp.einsum('bhvk,bhk->bhv', S, a_t)
        S = S * d_t[:, :, None, :] + sa[..., None] * b_t[:, :, None, :] + v_t[..., None] * k_t[:, :, None, :]
        y = jnp.einsum('bhvk,bhk->bhv', S, r_t)
        return S, y

    xs = tuple(jnp.moveaxis(t, 1, 0) for t in (r, decay, k, v, a, b))
    S, ys = lax.scan(step, s0, xs)
    return jnp.moveaxis(ys, 0, 1), S


def rwkv7_mix(p, prev, wkv0, lp):
    Bn, T = p.shape[0], p.shape[1]
    p_prev = jnp.concatenate([prev[:, None, :].astype(p.dtype), p[:, :-1]], axis=1)
    xs = p + (p_prev - p) * lp['mu_shift']
    r, k, v, wd, ad, gd = jnp.split(xs, RW_SPLITS, axis=-1)
    w = -jax.nn.softplus(-(lp['w0'] + jnp.tanh(wd) @ lp['w_w2'])) - 0.5
    a = jax.nn.sigmoid(lp['a0'] + ad @ lp['w_a2'])
    g = jax.nn.sigmoid(gd) @ lp['w_g2']

    def heads(t):
        return t.reshape(Bn, T, B_HEADS, B_HEAD).astype(jnp.float32)

    kk = heads(k * lp['k_k'])
    kk = kk / jnp.maximum(jnp.sqrt(jnp.sum(kk * kk, axis=-1, keepdims=True)), 1e-12)
    k = k * (1 + (a - 1) * lp['k_a'])
    r_h, k_h, v_h, a_h = heads(r), heads(k), heads(v), heads(a)
    decay = jnp.exp(-jnp.exp(heads(w)))
    y, s_last = wkv_scan(r_h, decay, k_h, v_h, -kk, kk * a_h, wkv0.astype(jnp.float32))
    mu = jnp.mean(y, axis=-1, keepdims=True)
    var = jnp.mean(jnp.square(y - mu), axis=-1, keepdims=True)
    y = ((y - mu) * lax.rsqrt(var + GN_EPS)).reshape(Bn, T, B_WIDTH)
    y = y * lp['ln_x_w'].astype(jnp.float32) + lp['ln_x_b'].astype(jnp.float32)
    bonus = jnp.sum(r_h * k_h * lp['r_k'].astype(jnp.float32), axis=-1, keepdims=True) * v_h
    y = (y + bonus.reshape(Bn, T, B_WIDTH)) * g.astype(jnp.float32)
    return y.astype(p.dtype), s_last.astype(wkv0.dtype), p[:, -1]


def conv_ffn(h, conv_prev, w_up, w_conv, w_down):
    T = h.shape[1]
    u = h @ w_up
    ext = jnp.concatenate([conv_prev.astype(u.dtype), u], axis=1)
    z = ext[:, 0:T] * w_conv[0]
    for j in range(1, FFN_CONV):
        z = z + ext[:, j:j + T] * w_conv[j]
    gate, val = jnp.split(z, 2, axis=-1)
    out = (jax.nn.silu(gate) * val) @ w_down
    return out, ext[:, -(FFN_CONV - 1):]


def layer_forward(x, c, lp, layer_idx, attn_fn, shift_prev, wkv0, conv_prev):
    Bn, T = x.shape[0], x.shape[1]
    mod = (jax.nn.silu(c) @ lp['w_ada'] + lp['b_ada'])[:, None, :]
    sh_m, sc_m, gt_m, sh_f, sc_f, gt_f = jnp.split(mod, 6, axis=-1)
    h = rms_norm(x, lp['g_pre_mix']) * (1 + sc_m) + sh_m
    proj = h @ lp['w_in']
    q = proj[..., :A_WIDTH].reshape(Bn, T, A_HEADS, 2, A_DK)
    k = proj[..., A_WIDTH:2 * A_WIDTH].reshape(Bn, T, A_HEADS, 2, A_DK)
    v = proj[..., 2 * A_WIDTH:3 * A_WIDTH].reshape(Bn, T, A_HEADS, A_DV)
    lam_init = 0.8 - 0.6 * math.exp(-0.3 * layer_idx)
    f32 = jnp.float32
    lam = (jnp.exp(jnp.sum(lp['lam_q1'].astype(f32) * lp['lam_k1'].astype(f32)))
           - jnp.exp(jnp.sum(lp['lam_q2'].astype(f32) * lp['lam_k2'].astype(f32))) + lam_init)
    o_a = attn_fn(q, k, v, lam)
    o_a = (rms_norm(o_a, lp['g_subln']) * (1 - lam_init)).reshape(Bn, T, A_WIDTH)
    o_b, wkv_last, shift_last = rwkv7_mix(proj[..., 3 * A_WIDTH:], shift_prev, wkv0, lp)
    mix = jnp.concatenate([o_a, o_b.astype(o_a.dtype)], axis=-1) @ lp['w_out']
    x = x + gt_m * rms_norm(mix, lp['g_post_mix'])
    h = rms_norm(x, lp['g_pre_ffn']) * (1 + sc_f) + sh_f
    f, conv_last = conv_ffn(h, conv_prev, lp['w_up'], lp['w_conv_ffn'], lp['w_down'])
    x = x + gt_f * rms_norm(f, lp['g_post_ffn'])
    return x, k.reshape(Bn, T, A_HEADS, 2 * A_DK), v, wkv_last, shift_last, conv_last


def setup_inputs(seed: int = 0) -> dict:
    key = jax.random.key(seed)
    ks = iter(jax.random.split(key, 40))
    L, D = DEPTH, D_MODEL

    def nrm(shape, s=1.0):
        return s * jax.random.normal(next(ks), shape, jnp.float32)

    def unif(shape, lo, hi):
        return jax.random.uniform(next(ks), shape, jnp.float32, lo, hi)

    return {
        'x_prompt': nrm((BATCH, SEQ, D)),
        'x_sample': nrm((DEC_BATCH, DEC_SEQ, D)),
        'c_prompt': nrm((BATCH, D)),
        'c_sample': nrm((DEC_BATCH, D)),
        'cache_k': nrm((L, DEC_BATCH, PAST_LEN, A_HEADS, 2 * A_DK)),
        'cache_v': nrm((L, DEC_BATCH, PAST_LEN, A_HEADS, A_DV)),
        'state_wkv': nrm((L, DEC_BATCH, B_HEADS, B_HEAD, B_HEAD), 0.5),
        'state_shift': nrm((L, DEC_BATCH, RW_COLS)),
        'state_ffn_conv': nrm((L, DEC_BATCH, FFN_CONV - 1, 2 * D_FF)),
        'w_ada': nrm((L, D, 6 * D), D ** -0.5),
        'b_ada': nrm((L, 6 * D), 0.01),
        'g_pre_mix': 1.0 + nrm((L, D), 0.01),
        'g_post_mix': 1.0 + nrm((L, D), 0.01),
        'g_pre_ffn': 1.0 + nrm((L, D), 0.01),
        'g_post_ffn': 1.0 + nrm((L, D), 0.01),
        'w_in': nrm((L, D, IN_COLS), D ** -0.5),
        'lam_q1': nrm((L, A_DK), 0.1),
        'lam_k1': nrm((L, A_DK), 0.1),
        'lam_q2': nrm((L, A_DK), 0.1),
        'lam_k2': nrm((L, A_DK), 0.1),
        'g_subln': 1.0 + nrm((L, A_DV), 0.01),
        'mu_shift': unif((L, RW_COLS), 0.0, 1.0),
        'w0': unif((L, B_WIDTH), -3.0, 1.0),
        'w_w2': nrm((L, W_RANK, B_WIDTH), 0.1 * W_RANK ** -0.5),
        'a0': nrm((L, B_WIDTH), 0.1),
        'w_a2': nrm((L, A_RANK, B_WIDTH), 0.1 * A_RANK ** -0.5),
        'w_g2': nrm((L, G_RANK, B_WIDTH), G_RANK ** -0.5),
        'k_k': 0.85 + nrm((L, B_WIDTH), 0.05),
        'k_a': 1.0 + nrm((L, B_WIDTH), 0.05),
        'r_k': nrm((L, B_HEADS, B_HEAD), 0.1),
        'ln_x_w': 1.0 + nrm((L, B_WIDTH), 0.01),
        'ln_x_b': nrm((L, B_WIDTH), 0.01),
        'w_out': nrm((L, D, D), D ** -0.5),
        'w_up': nrm((L, D, 2 * D_FF), D ** -0.5),
        'w_conv_ffn': nrm((L, FFN_CONV, 2 * D_FF), FFN_CONV ** -0.5),
        'w_down': nrm((L, D_FF, D), D_FF ** -0.5),
    }


def reference(x_prompt, x_sample, c_prompt, c_sample, cache_k, cache_v, state_wkv, state_shift,
              state_ffn_conv, w_ada, b_ada, g_pre_mix, g_post_mix, g_pre_ffn, g_post_ffn, w_in,
              lam_q1, lam_k1, lam_q2, lam_k2, g_subln, mu_shift, w0, w_w2, a0, w_a2, w_g2, k_k, k_a,
              r_k, ln_x_w, ln_x_b, w_out, w_up, w_conv_ffn, w_down):
    yp, ys = x_prompt, x_sample
    bp = x_prompt.shape[0]
    kp_l, vp_l, sp_l, shp_l, cp_l = [], [], [], [], []
    ks_l, vs_l, ss_l, shs_l, cs_l = [], [], [], [], []
    for l in range(DEPTH):
        lp = {'w_ada': w_ada[l], 'b_ada': b_ada[l], 'g_pre_mix': g_pre_mix[l], 'g_post_mix': g_post_mix[l],
              'g_pre_ffn': g_pre_ffn[l], 'g_post_ffn': g_post_ffn[l], 'w_in': w_in[l],
              'lam_q1': lam_q1[l], 'lam_k1': lam_k1[l], 'lam_q2': lam_q2[l], 'lam_k2': lam_k2[l],
              'g_subln': g_subln[l], 'mu_shift': mu_shift[l], 'w0': w0[l], 'w_w2': w_w2[l], 'a0': a0[l],
              'w_a2': w_a2[l], 'w_g2': w_g2[l], 'k_k': k_k[l], 'k_a': k_a[l], 'r_k': r_k[l],
              'ln_x_w': ln_x_w[l], 'ln_x_b': ln_x_b[l], 'w_out': w_out[l], 'w_up': w_up[l],
              'w_conv_ffn': w_conv_ffn[l], 'w_down': w_down[l]}
        yp, kp, vp, sp, shp, cp = layer_forward(
            yp, c_prompt, lp, l, prompt_attention,
            jnp.zeros((bp, RW_COLS), x_prompt.dtype),
            jnp.zeros((bp, B_HEADS, B_HEAD, B_HEAD), state_wkv.dtype),
            jnp.zeros((bp, FFN_CONV - 1, 2 * D_FF), x_prompt.dtype))
        attn_s = functools.partial(sample_attention, cache_k=cache_k[l], cache_v=cache_v[l])
        ys, ksm, vsm, ssm, shs, csm = layer_forward(
            ys, c_sample, lp, l, attn_s, state_shift[l], state_wkv[l], state_ffn_conv[l])
        kp_l.append(kp); vp_l.append(vp); sp_l.append(sp); shp_l.append(shp); cp_l.append(cp)
        ks_l.append(ksm); vs_l.append(vsm); ss_l.append(ssm); shs_l.append(shs); cs_l.append(csm)
    return (yp, ys,
            jnp.stack(kp_l), jnp.stack(vp_l), jnp.stack(sp_l), jnp.stack(shp_l), jnp.stack(cp_l),
            jnp.stack(ks_l), jnp.stack(vs_l), jnp.stack(ss_l), jnp.stack(shs_l), jnp.stack(cs_l))
```

```python
import functools
import math

import jax
import jax.numpy as jnp
from jax import lax
from jax.experimental import pallas as pl
from jax.experimental.pallas import tpu as pltpu

F32 = jnp.float32
BF16 = jnp.bfloat16

D_MODEL = 2048
A_HEADS = 8
A_DK = 64
A_DV = 128
A_WIDTH = A_HEADS * A_DV
B_HEAD = 64
B_WIDTH = D_MODEL - A_WIDTH
B_HEADS = B_WIDTH // B_HEAD
W_RANK = 64
A_RANK = 64
G_RANK = 128
RW_COLS = 3 * B_WIDTH + W_RANK + A_RANK + G_RANK
D_FF = 5632
FFN_CONV = 3
ATTN_CHUNK = 64
EPS = 1e-6
GN_EPS = 64e-5
LAM_INIT = 0.8 - 0.6 * math.exp(-0.3 * 0)

LANES = 128
SUBLANES = 8
BF16_ROWS = 16
MXU_DIM = 256
GROUP_HEADS = MXU_DIM // B_HEAD
N_GROUPS = B_WIDTH // MXU_DIM
VMEM_LIMIT_BYTES = 56 * 1024 * 1024
NEG_BIG = -1e30


def _cparams(*sem):
    return pltpu.CompilerParams(dimension_semantics=sem, vmem_limit_bytes=VMEM_LIMIT_BYTES)


def _dot(a, b):
    return jnp.dot(a, b, preferred_element_type=F32)


def _dot_nt(a, b):
    return lax.dot_general(a, b, (((1,), (1,)), ((), ())), preferred_element_type=F32)


def _dot_tn(a, b):
    return lax.dot_general(a, b, (((0,), (0,)), ((), ())), preferred_element_type=F32)


def _idiv(x, n):
    assert n > 0 and n & (n - 1) == 0
    return x >> (n.bit_length() - 1)


def _sigmoid(x):
    return 1.0 / (1.0 + jnp.exp(-x))


def _split_bf16(x, terms):
    out = []
    rem = x
    for _ in range(terms):
        piece = rem.astype(BF16)
        out.append(piece)
        rem = rem - piece.astype(F32)
    return out


def _norm_mod(x, g, sc, sh):
    ms = jnp.mean(x * x, axis=-1, keepdims=True)
    return (x * lax.rsqrt(ms + EPS) * g) * (1.0 + sc) + sh


def _rms(x, g):
    ms = jnp.mean(x * x, axis=-1, keepdims=True)
    return x * lax.rsqrt(ms + EPS) * g


def _mod(ref):
    return ref[0] if len(ref.shape) == 3 else ref[...]


def _lam(q1_ref, k1_ref, q2_ref, k2_ref):
    s1 = jnp.sum(q1_ref[...] * k1_ref[...], axis=-1, keepdims=True)
    s2 = jnp.sum(q2_ref[...] * k2_ref[...], axis=-1, keepdims=True)
    return jnp.exp(s1) - jnp.exp(s2) + LAM_INIT


def _ada_kernel(c_ref, w_ref, b_ref, o_ref):
    c = c_ref[...]
    s = (c * _sigmoid(c)).astype(BF16)
    o_ref[...] = _dot(s, w_ref[...].astype(BF16)) + b_ref[...]


def _ada_call(c_all, w_ada, b_ada):
    rows = c_all.shape[0]
    n = w_ada.shape[1]
    tn = 1024
    return pl.pallas_call(
        _ada_kernel,
        grid=(n // tn,),
        in_specs=[
            pl.BlockSpec((rows, D_MODEL), lambda j: (0, 0)),
            pl.BlockSpec((D_MODEL, tn), lambda j: (0, j)),
            pl.BlockSpec((1, tn), lambda j: (0, j)),
        ],
        out_specs=pl.BlockSpec((rows, tn), lambda j: (0, j)),
        out_shape=jax.ShapeDtypeStruct((rows, n), F32),
        compiler_params=_cparams("arbitrary"),
        name="ada_mod",
    )(c_all, w_ada, b_ada)


def _qkv_kernel(x_ref, g_ref, sc_ref, sh_ref, w_ref, kf_ref, vf_ref, qkvb_ref, h_scr):
    j = pl.program_id(1)

    @pl.when(j == 0)
    def _():
        h_scr[...] = _norm_mod(x_ref[...], g_ref[...], _mod(sc_ref), _mod(sh_ref)).astype(BF16)

    acc = _dot(h_scr[...], w_ref[...])

    @pl.when(j == 0)
    def _():
        qkvb_ref[...] = (acc * (A_DK ** -0.5)).astype(BF16)

    @pl.when(j == 1)
    def _():
        kf_ref[...] = acc
        qkvb_ref[...] = acc.astype(BF16)

    @pl.when(j == 2)
    def _():
        vf_ref[...] = acc
        qkvb_ref[...] = acc.astype(BF16)


def _rw_kernel(x_ref, g_ref, sc_ref, sh_ref, w_ref, o_ref, h_scr):
    @pl.when(pl.program_id(1) == 0)
    def _():
        h_scr[...] = _norm_mod(x_ref[...], g_ref[...], _mod(sc_ref), _mod(sh_ref)).astype(BF16)

    o_ref[...] = _dot(h_scr[...], w_ref[...])


def _mod_spec(mod, tm, tiles_per_seq, nidx):
    if mod.ndim == 3:
        if nidx == 2:
            return pl.BlockSpec((1, 1, D_MODEL), lambda i, j: (i // tiles_per_seq, 0, 0))
        return pl.BlockSpec((1, 1, D_MODEL), lambda i: (i // tiles_per_seq, 0, 0))
    if nidx == 2:
        return pl.BlockSpec((tm, D_MODEL), lambda i, j: (i, 0))
    return pl.BlockSpec((tm, D_MODEL), lambda i: (i, 0))


def _in_proj_call(x, g, sc, sh, w_qkv, w_rw, tm, tiles_per_seq):
    m = x.shape[0]
    nm = m // tm
    common = [
        pl.BlockSpec((tm, D_MODEL), lambda i, j: (i, 0)),
        pl.BlockSpec((1, D_MODEL), lambda i, j: (0, 0)),
        _mod_spec(sc, tm, tiles_per_seq, 2),
        _mod_spec(sh, tm, tiles_per_seq, 2),
    ]
    kf, vf, qkvb = pl.pallas_call(
        _qkv_kernel,
        grid=(nm, 3),
        in_specs=common + [pl.BlockSpec((D_MODEL, A_WIDTH), lambda i, j: (0, j))],
        out_specs=[
            pl.BlockSpec((tm, A_WIDTH), lambda i, j: (i, 0)),
            pl.BlockSpec((tm, A_WIDTH), lambda i, j: (i, 0)),
            pl.BlockSpec((tm, A_WIDTH), lambda i, j: (i, j)),
        ],
        out_shape=[
            jax.ShapeDtypeStruct((m, A_WIDTH), F32),
            jax.ShapeDtypeStruct((m, A_WIDTH), F32),
            jax.ShapeDtypeStruct((m, 3 * A_WIDTH), BF16),
        ],
        scratch_shapes=[pltpu.VMEM((tm, D_MODEL), BF16)],
        compiler_params=_cparams("arbitrary", "arbitrary"),
        name="in_proj_qkv",
    )(x, g, sc, sh, w_qkv)
    tn = RW_COLS // 2
    rw = pl.pallas_call(
        _rw_kernel,
        grid=(nm, 2),
        in_specs=common + [pl.BlockSpec((D_MODEL, tn), lambda i, j: (0, j))],
        out_specs=pl.BlockSpec((tm, tn), lambda i, j: (i, j)),
        out_shape=jax.ShapeDtypeStruct((m, RW_COLS), F32),
        scratch_shapes=[pltpu.VMEM((tm, D_MODEL), BF16)],
        compiler_params=_cparams("arbitrary", "arbitrary"),
        name="in_proj_rw",
    )(x, g, sc, sh, w_rw)
    return kf, vf, qkvb, rw


def _softmax_step(s, vb, m_ref, l_ref, acc_ref, idx):
    m_prev = m_ref[idx]
    m_new = jnp.maximum(m_prev, jnp.max(s, axis=-1, keepdims=True))
    alpha = jnp.exp(m_prev - m_new)
    p = jnp.exp(s - m_new)
    l_ref[idx] = alpha * l_ref[idx] + jnp.sum(p, axis=-1, keepdims=True)
    acc_ref[idx] = alpha * acc_ref[idx] + _dot(p.astype(BF16), vb)
    m_ref[idx] = m_new


def _subln(o, g):
    return _rms(o, g) * (1.0 - LAM_INIT)


def _pattn_kernel(q1_ref, k1_ref, q2_ref, k2_ref, q_ref, k_ref, v_ref, g_ref, o_ref,
                  m_scr, l_scr, acc_scr, *, tq):
    i = pl.program_id(2)
    q = q_ref[0]
    lane = lax.broadcasted_iota(jnp.int32, (tq, A_DV), 1)
    zero = jnp.zeros_like(q)
    qm = (jnp.where(lane < A_DK, q, zero), jnp.where(lane >= A_DK, q, zero))
    m_scr[...] = jnp.full(m_scr.shape, NEG_BIG, F32)
    l_scr[...] = jnp.zeros(l_scr.shape, F32)
    acc_scr[...] = jnp.zeros(acc_scr.shape, F32)

    def block(j, mask):
        r0 = pl.multiple_of(j * tq, tq)
        kb = k_ref[0, pl.ds(r0, tq), :]
        vb = v_ref[0, pl.ds(r0, tq), :]
        for mp in range(2):
            s = _dot_nt(qm[mp], kb)
            if mask is not None:
                s = jnp.where(mask, s, NEG_BIG)
            _softmax_step(s, vb, m_scr, l_scr, acc_scr, mp)

    def body(j, carry):
        block(j, None)
        return carry

    lax.fori_loop(0, i, body, 0)
    rq = lax.broadcasted_iota(jnp.int32, (tq, tq), 0)
    ck = lax.broadcasted_iota(jnp.int32, (tq, tq), 1)
    block(i, _idiv(ck, ATTN_CHUNK) <= _idiv(rq, ATTN_CHUNK))

    lam = _lam(q1_ref, k1_ref, q2_ref, k2_ref)
    o = acc_scr[0] / l_scr[0] - lam * (acc_scr[1] / l_scr[1])
    o_ref[0] = _subln(o, g_ref[...]).astype(BF16)


def _lam_specs(nidx):
    if nidx == 3:
        return [pl.BlockSpec((1, A_DK), lambda b, h, i: (0, 0))] * 4
    return [pl.BlockSpec((1, A_DK), lambda b, j: (0, 0))] * 4


def _pattn_call(lams, qkvb, g_subln, tq):
    bsz, t, _ = qkvb.shape
    return pl.pallas_call(
        functools.partial(_pattn_kernel, tq=tq),
        grid=(bsz, A_HEADS, t // tq),
        in_specs=_lam_specs(3) + [
            pl.BlockSpec((1, tq, A_DV), lambda b, h, i: (b, i, h)),
            pl.BlockSpec((1, t, A_DV), lambda b, h, i: (b, 0, A_HEADS + h)),
            pl.BlockSpec((1, t, A_DV), lambda b, h, i: (b, 0, 2 * A_HEADS + h)),
            pl.BlockSpec((1, A_DV), lambda b, h, i: (0, 0)),
        ],
        out_specs=pl.BlockSpec((1, tq, A_DV), lambda b, h, i: (b, i, h)),
        out_shape=jax.ShapeDtypeStruct((bsz, t, A_WIDTH), BF16),
        scratch_shapes=[
            pltpu.VMEM((2, tq, 1), F32),
            pltpu.VMEM((2, tq, 1), F32),
            pltpu.VMEM((2, tq, A_DV), F32),
        ],
        compiler_params=_cparams("arbitrary", "arbitrary", "arbitrary"),
        name="prompt_attention",
    )(*lams, qkvb, qkvb, qkvb, g_subln)


N_VQ = 2 * A_HEADS


def _sattn_kernel(q1_ref, k1_ref, q2_ref, k2_ref, q_ref, kn_ref, vn_ref, ck_ref, cv_ref, g_ref, o_ref,
                  qb_scr, m_scr, l_scr, acc_scr, *, tnew):
    j = pl.program_id(1)
    nvq = N_VQ * tnew

    @pl.when(j == 0)
    def _():
        q = q_ref[0]
        qt = jnp.concatenate([q] * N_VQ, axis=0)
        row = lax.broadcasted_iota(jnp.int32, (nvq, A_WIDTH), 0)
        col = lax.broadcasted_iota(jnp.int32, (nvq, A_WIDTH), 1)
        qb_scr[...] = jnp.where(_idiv(row, tnew) == _idiv(col, A_DK), qt, jnp.zeros_like(qt))
        m_scr[...] = jnp.full(m_scr.shape, NEG_BIG, F32)
        l_scr[...] = jnp.zeros(l_scr.shape, F32)
        acc_scr[...] = jnp.zeros(acc_scr.shape, F32)

    def step(kb, vb):
        _softmax_step(_dot_nt(qb_scr[...], kb), vb, m_scr, l_scr, acc_scr, 0)

    step(ck_ref[0].astype(BF16), cv_ref[0].astype(BF16))

    @pl.when(j == pl.num_programs(1) - 1)
    def _():
        step(kn_ref[0], vn_ref[0])
        lam = _lam(q1_ref, k1_ref, q2_ref, k2_ref)
        g = g_ref[...]
        for h in range(A_HEADS):
            cs = slice(h * A_DV, (h + 1) * A_DV)
            r0 = slice((2 * h) * tnew, (2 * h + 1) * tnew)
            r1 = slice((2 * h + 1) * tnew, (2 * h + 2) * tnew)
            o0 = acc_scr[0, r0, cs] / l_scr[0, r0, :]
            o1 = acc_scr[0, r1, cs] / l_scr[0, r1, :]
            o_ref[0, :, cs] = _subln(o0 - lam * o1, g).astype(BF16)


def _sattn_call(lams, qkvb, cache_k, cache_v, g_subln, tk):
    bsz, tnew, _ = qkvb.shape
    past = cache_k.shape[1]
    nvq = N_VQ * tnew
    return pl.pallas_call(
        functools.partial(_sattn_kernel, tnew=tnew),
        grid=(bsz, past // tk),
        in_specs=_lam_specs(2) + [
            pl.BlockSpec((1, tnew, A_WIDTH), lambda b, j: (b, 0, 0)),
            pl.BlockSpec((1, tnew, A_WIDTH), lambda b, j: (b, 0, 1)),
            pl.BlockSpec((1, tnew, A_WIDTH), lambda b, j: (b, 0, 2)),
            pl.BlockSpec((1, tk, A_WIDTH), lambda b, j: (b, j, 0)),
            pl.BlockSpec((1, tk, A_WIDTH), lambda b, j: (b, j, 0)),
            pl.BlockSpec((1, A_DV), lambda b, j: (0, 0)),
        ],
        out_specs=pl.BlockSpec((1, tnew, A_WIDTH), lambda b, j: (b, 0, 0)),
        out_shape=jax.ShapeDtypeStruct((bsz, tnew, A_WIDTH), BF16),
        scratch_shapes=[
            pltpu.VMEM((nvq, A_WIDTH), BF16),
            pltpu.VMEM((1, nvq, 1), F32),
            pltpu.VMEM((1, nvq, 1), F32),
            pltpu.VMEM((1, nvq, A_WIDTH), F32),
        ],
        compiler_params=_cparams("arbitrary", "arbitrary"),
        name="sample_attention",
    )(*lams, qkvb, qkvb, qkvb, cache_k, cache_v, g_subln)


def _head_sum(x, ones_bd, terms=2):
    out = None
    for piece in _split_bf16(x, terms):
        d = _dot(piece, ones_bd)
        out = d if out is None else out + d
    return out


def _rwkv_kernel(p_ref, ph_ref, st_ref, s0_ref, mu_ref, w0_ref, a0_ref, kk_ref, ka_ref, rk_ref,
                 lnw_ref, lnb_ref, ww2_ref, wa2_ref, wg2_ref, ob_ref, so_ref,
                 r_scr, k_scr, v_scr, na_scr, nb_scr, ld_scr, cum_scr, g_scr, bon_scr, y_scr,
                 *, tb, chunk):
    i = pl.program_id(1)
    c_len = chunk
    n_chunks = tb // c_len

    @pl.when(i == 0)
    def _():
        so_ref[...] = s0_ref[...]

    p = p_ref[0]
    prev_row = jnp.where(i == 0, st_ref[0], ph_ref[0, SUBLANES - 1:SUBLANES, :])
    rowi = lax.broadcasted_iota(jnp.int32, p.shape, 0)
    p_prev = jnp.where(rowi == 0, prev_row, pltpu.roll(p, 1, 0))
    xs = p + (p_prev - p) * mu_ref[...]
    r = xs[:, 0:B_WIDTH]
    k = xs[:, B_WIDTH:2 * B_WIDTH]
    v = xs[:, 2 * B_WIDTH:3 * B_WIDTH]
    wa = xs[:, 3 * B_WIDTH:3 * B_WIDTH + W_RANK + A_RANK]
    gd = xs[:, 3 * B_WIDTH + W_RANK + A_RANK:]

    lane_wa = lax.broadcasted_iota(jnp.int32, wa.shape, 1)
    w_cat = jnp.concatenate([ww2_ref[...], wa2_ref[...]], axis=0)
    zero_wa = jnp.zeros_like(wa)
    w_lin = _dot(jnp.where(lane_wa < W_RANK, jnp.tanh(wa), zero_wa).astype(BF16), w_cat)
    a_lin = _dot(jnp.where(lane_wa >= W_RANK, wa, zero_wa).astype(BF16), w_cat)
    zw = -(w0_ref[...] + w_lin)
    softplus = jnp.maximum(zw, 0.0) + jnp.log(1.0 + jnp.exp(-jnp.abs(zw)))
    logd = -jnp.exp(-softplus - 0.5)
    a = _sigmoid(a0_ref[...] + a_lin)
    g_scr[...] = _dot(_sigmoid(gd).astype(BF16), wg2_ref[...])

    row_g = lax.broadcasted_iota(jnp.int32, (MXU_DIM, MXU_DIM), 0)
    col_g = lax.broadcasted_iota(jnp.int32, (MXU_DIM, MXU_DIM), 1)
    bd_mask = _idiv(row_g, B_HEAD) == _idiv(col_g, B_HEAD)
    ones_bd = jnp.where(bd_mask, 1.0, 0.0).astype(BF16)

    k2 = k * (1.0 + (a - 1.0) * ka_ref[...])
    kk = k * kk_ref[...]
    rkk = r * k2 * rk_ref[...]
    for q in range(N_GROUPS):
        ls = slice(q * MXU_DIM, (q + 1) * MXU_DIM)
        kq = kk[:, ls]
        nrm = jnp.maximum(jnp.sqrt(_head_sum(kq * kq, ones_bd)), 1e-12)
        kn = kq / nrm
        na_scr[:, ls] = -kn
        nb_scr[:, ls] = kn * a[:, ls]
        bon_scr[:, ls] = _head_sum(rkk[:, ls], ones_bd) * v[:, ls]
    r_scr[...] = r
    k_scr[...] = k2
    v_scr[...] = v
    ld_scr[...] = logd

    row_t = lax.broadcasted_iota(jnp.int32, (tb, tb), 0)
    col_t = lax.broadcasted_iota(jnp.int32, (tb, tb), 1)
    tri = jnp.where((_idiv(row_t, c_len) == _idiv(col_t, c_len)) & (col_t <= row_t), 1.0, 0.0).astype(BF16)
    cum = None
    for piece in _split_bf16(logd, 3):
        d = _dot(tri, piece)
        cum = d if cum is None else cum + d
    cum_scr[...] = cum

    lane_q = lax.broadcasted_iota(jnp.int32, (1, MXU_DIM), 1)
    head_masks = [_idiv(lane_q, B_HEAD) == j for j in range(GROUP_HEADS)]
    row_c = lax.broadcasted_iota(jnp.int32, (c_len, c_len), 0)
    col_c = lax.broadcasted_iota(jnp.int32, (c_len, c_len), 1)
    strict = col_c < row_c
    incl = col_c <= row_c
    eye = jnp.where(col_c == row_c, 1.0, 0.0).astype(F32)
    n_double = int(math.log2(c_len)) - 1

    def diag_blocks(z):
        out = None
        for j in range(GROUP_HEADS):
            t = jnp.where(head_masks[j], z[j * c_len:(j + 1) * c_len], 0.0)
            out = t if out is None else out + t
        return out

    def chunk_body(c, carry):
        r0 = pl.multiple_of(c * c_len, c_len)
        rows = pl.ds(r0, c_len)
        for q in range(N_GROUPS):
            ls = slice(q * MXU_DIM, (q + 1) * MXU_DIM)
            cum_q = cum_scr[rows, ls]
            ld_q = ld_scr[rows, ls]
            r_q = r_scr[rows, ls]
            k_q = k_scr[rows, ls]
            v_q = v_scr[rows, ls].astype(BF16)
            na_q = na_scr[rows, ls]
            nb_q = nb_scr[rows, ls]
            cum_last = cum_q[c_len - 1:c_len, :]
            e_in = jnp.exp(cum_q)
            e_out = jnp.exp(-cum_q)
            e_end = jnp.exp(cum_last - cum_q)
            a_t = (na_q * jnp.exp(cum_q - ld_q)).astype(BF16)
            r_t = (r_q * e_in).astype(BF16)
            b_t = (nb_q * e_out).astype(BF16)
            k_t = (k_q * e_out).astype(BF16)
            b_e = (nb_q * e_end).astype(BF16)
            k_e = (k_q * e_end).astype(BF16)
            decay_end = jnp.exp(cum_last)

            zero_t = jnp.zeros_like(a_t)
            stack = []
            for j in range(GROUP_HEADS):
                stack.append(jnp.where(head_masks[j], a_t, zero_t))
                stack.append(jnp.where(head_masks[j], r_t, zero_t))
            lhs = jnp.concatenate(stack, axis=0)
            gb = _dot_nt(lhs, b_t)
            gk = _dot_nt(lhs, k_t)
            t_l, aak_l, arb_l, ark_l = [], [], [], []
            for j in range(GROUP_HEADS):
                b0 = 2 * j * c_len
                aab = jnp.where(strict, gb[b0:b0 + c_len], 0.0)
                aak_l.append(jnp.where(strict, gk[b0:b0 + c_len], 0.0).astype(BF16))
                arb_l.append(jnp.where(incl, gb[b0 + c_len:b0 + 2 * c_len], 0.0).astype(BF16))
                ark_l.append(jnp.where(incl, gk[b0 + c_len:b0 + 2 * c_len], 0.0).astype(BF16))
                t_m = eye + aab
                pw = aab
                for _ in range(n_double):
                    pb = pw.astype(BF16)
                    pw = _dot(pb, pb)
                    t_m = t_m + _dot(t_m.astype(BF16), pw.astype(BF16))
                t_l.append(t_m.astype(BF16))
            t_s = jnp.concatenate(t_l, axis=0)
            aak_s = jnp.concatenate(aak_l, axis=0)
            arb_s = jnp.concatenate(arb_l, axis=0)
            ark_s = jnp.concatenate(ark_l, axis=0)

            w_t = diag_blocks(_dot(t_s, a_t))
            av = diag_blocks(_dot(aak_s, v_q))
            u_0 = diag_blocks(_dot(t_s, av.astype(BF16)))
            y_0 = diag_blocks(_dot(ark_s, v_q))
            d_e = _dot_tn(v_q, k_e)

            s_prev = so_ref[0, q]
            wr = _dot_nt(jnp.concatenate([w_t.astype(BF16), r_t], axis=0), s_prev.astype(BF16))
            u = wr[0:c_len] + u_0
            u_b = u.astype(BF16)
            y = wr[c_len:2 * c_len] + diag_blocks(_dot(arb_s, u_b)) + y_0
            s_new = s_prev * decay_end + _dot_tn(u_b, b_e) + d_e
            so_ref[0, q] = jnp.where(bd_mask, s_new, 0.0)
            y_scr[rows, ls] = y
        return carry

    lax.fori_loop(0, n_chunks, chunk_body, 0)

    inv_n = 1.0 / B_HEAD
    for q in range(N_GROUPS):
        ls = slice(q * MXU_DIM, (q + 1) * MXU_DIM)
        y = y_scr[:, ls]
        mu = _head_sum(y, ones_bd) * inv_n
        d = y - mu
        var = _head_sum(d * d, ones_bd) * inv_n
        yn = d * lax.rsqrt(var + GN_EPS) * lnw_ref[:, ls] + lnb_ref[:, ls]
        ob_ref[0, :, ls] = ((yn + bon_scr[:, ls]) * g_scr[:, ls]).astype(BF16)


def _rwkv_call(rw, shift0, s0, prm, tb, chunk):
    bsz, t, _ = rw.shape
    nt = t // tb
    hb = tb // SUBLANES
    vec = lambda n: pl.BlockSpec((1, n), lambda b, i: (0, 0))
    return pl.pallas_call(
        functools.partial(_rwkv_kernel, tb=tb, chunk=chunk),
        grid=(bsz, nt),
        in_specs=[
            pl.BlockSpec((1, tb, RW_COLS), lambda b, i: (b, i, 0)),
            pl.BlockSpec((1, SUBLANES, RW_COLS), lambda b, i: (b, jnp.maximum(i * hb - 1, 0), 0)),
            pl.BlockSpec((1, 1, RW_COLS), lambda b, i: (b, 0, 0)),
            pl.BlockSpec((1, N_GROUPS, MXU_DIM, MXU_DIM), lambda b, i: (b, 0, 0, 0)),
            vec(RW_COLS), vec(B_WIDTH), vec(B_WIDTH), vec(B_WIDTH), vec(B_WIDTH), vec(B_WIDTH),
            vec(B_WIDTH), vec(B_WIDTH),
            pl.BlockSpec((W_RANK, B_WIDTH), lambda b, i: (0, 0)),
            pl.BlockSpec((A_RANK, B_WIDTH), lambda b, i: (0, 0)),
            pl.BlockSpec((G_RANK, B_WIDTH), lambda b, i: (0, 0)),
        ],
        out_specs=[
            pl.BlockSpec((1, tb, B_WIDTH), lambda b, i: (b, i, 0)),
            pl.BlockSpec((1, N_GROUPS, MXU_DIM, MXU_DIM), lambda b, i: (b, 0, 0, 0)),
        ],
        out_shape=[
            jax.ShapeDtypeStruct((bsz, t, B_WIDTH), BF16),
            jax.ShapeDtypeStruct((bsz, N_GROUPS, MXU_DIM, MXU_DIM), F32),
        ],
        scratch_shapes=[pltpu.VMEM((tb, B_WIDTH), F32)] * 10,
        compiler_params=_cparams("arbitrary", "arbitrary"),
        name="rwkv7_mix",
    )(rw, rw, shift0, s0, prm["mu_shift"], prm["w0"], prm["a0"], prm["k_k"], prm["k_a"], prm["r_k"],
      prm["ln_x_w"], prm["ln_x_b"], prm["w_w2"], prm["w_a2"], prm["w_g2"])


def _state_to_groups(s):
    bsz = s.shape[0]
    s5 = s.reshape(bsz, N_GROUPS, GROUP_HEADS, B_HEAD, B_HEAD)
    out = jnp.zeros((bsz, N_GROUPS, GROUP_HEADS, B_HEAD, GROUP_HEADS, B_HEAD), s.dtype)
    for j in range(GROUP_HEADS):
        out = out.at[:, :, j, :, j, :].set(s5[:, :, j])
    return out.reshape(bsz, N_GROUPS, MXU_DIM, MXU_DIM)


def _groups_to_state(sg):
    bsz = sg.shape[0]
    s6 = sg.reshape(bsz, N_GROUPS, GROUP_HEADS, B_HEAD, GROUP_HEADS, B_HEAD)
    s5 = jnp.stack([s6[:, :, j, :, j, :] for j in range(GROUP_HEADS)], axis=2)
    return s5.reshape(bsz, B_HEADS, B_HEAD, B_HEAD)


def _mixout_kernel(oa_ref, ob_ref, x_ref, w_ref, g_ref, gt_ref, o_ref):
    mix = _dot(oa_ref[...], w_ref[0:A_WIDTH, :]) + _dot(ob_ref[...], w_ref[A_WIDTH:D_MODEL, :])
    o_ref[...] = x_ref[...] + _mod(gt_ref) * _rms(mix, g_ref[...])


def _mixout_call(oa, ob, x, w_out, g, gt, tm, tiles_per_seq):
    m = x.shape[0]
    return pl.pallas_call(
        _mixout_kernel,
        grid=(m // tm,),
        in_specs=[
            pl.BlockSpec((tm, A_WIDTH), lambda i: (i, 0)),
            pl.BlockSpec((tm, B_WIDTH), lambda i: (i, 0)),
            pl.BlockSpec((tm, D_MODEL), lambda i: (i, 0)),
            pl.BlockSpec((D_MODEL, D_MODEL), lambda i: (0, 0)),
            pl.BlockSpec((1, D_MODEL), lambda i: (0, 0)),
            _mod_spec(gt, tm, tiles_per_seq, 1),
        ],
        out_specs=pl.BlockSpec((tm, D_MODEL), lambda i: (i, 0)),
        out_shape=jax.ShapeDtypeStruct((m, D_MODEL), F32),
        compiler_params=_cparams("arbitrary"),
        name="mix_out",
    )(oa, ob, x, w_out, g, gt)


FFN_HALO = BF16_ROWS
FFN_TF = 512


def _gated(zg, zv):
    return (zg * _sigmoid(zg) * zv).astype(BF16)


def _ffn_kernel(x_ref, xh_ref, cpg_ref, cpv_ref, g_ref, sc_ref, sh_ref, gt_ref, wg_ref, wv_ref,
                cg_ref, cv_ref, wd_ref, gp_ref, o_ref, clg_ref, clv_ref, h_scr, acc_scr,
                *, tm, tiles_per_seq):
    mi = pl.program_id(0)
    f = pl.program_id(1)
    first = (mi % tiles_per_seq) == 0

    @pl.when(f == 0)
    def _():
        g, sc, sh = g_ref[...], sc_ref[0], sh_ref[0]
        h_scr[FFN_HALO:, :] = _norm_mod(x_ref[...], g, sc, sh).astype(BF16)
        h_scr[:FFN_HALO, :] = _norm_mod(xh_ref[...], g, sc, sh).astype(BF16)
        acc_scr[...] = jnp.zeros(acc_scr.shape, F32)

    h = h_scr[...]
    row = lax.broadcasted_iota(jnp.int32, (FFN_HALO, FFN_TF), 0)

    def conv(u, cp_ref, c_ref, cl_ref):
        cp = cp_ref[0]
        head = u[:FFN_HALO]
        head_first = jnp.where(row == FFN_HALO - 2, cp[0:1],
                               jnp.where(row == FFN_HALO - 1, cp[1:2], head))
        ue = jnp.concatenate([jnp.where(first, head_first, head), u[FFN_HALO:]], axis=0)
        cl_ref[0] = ue[tm + FFN_HALO - SUBLANES:]
        c = c_ref[...]
        return (c[0:1] * pltpu.roll(ue, 2, 0)[FFN_HALO:] + c[1:2] * pltpu.roll(ue, 1, 0)[FFN_HALO:]
                + c[2:3] * ue[FFN_HALO:])

    zg = conv(_dot(h, wg_ref[...]), cpg_ref, cg_ref, clg_ref)
    zv = conv(_dot(h, wv_ref[...]), cpv_ref, cv_ref, clv_ref)
    acc_scr[...] += _dot(_gated(zg, zv), wd_ref[...])

    @pl.when(f == pl.num_programs(1) - 1)
    def _():
        o_ref[...] = x_ref[...] + gt_ref[0] * _rms(acc_scr[...], gp_ref[...])


def _ffn_call(x, conv_prev, g, sc, sh, gt, w_up, w_conv, w_down, gp, tm, tiles_per_seq):
    m = x.shape[0]
    nseq = conv_prev.shape[0]
    nf = D_FF // FFN_TF
    hb = tm // FFN_HALO
    seq = lambda i: i // tiles_per_seq
    modspec = pl.BlockSpec((1, 1, D_MODEL), lambda i, f: (seq(i), 0, 0))
    return pl.pallas_call(
        functools.partial(_ffn_kernel, tm=tm, tiles_per_seq=tiles_per_seq),
        grid=(m // tm, nf),
        in_specs=[
            pl.BlockSpec((tm, D_MODEL), lambda i, f: (i, 0)),
            pl.BlockSpec((FFN_HALO, D_MODEL), lambda i, f: (jnp.maximum(i * hb - 1, 0), 0)),
            pl.BlockSpec((1, FFN_CONV - 1, FFN_TF), lambda i, f: (seq(i), 0, f)),
            pl.BlockSpec((1, FFN_CONV - 1, FFN_TF), lambda i, f: (seq(i), 0, f + nf)),
            pl.BlockSpec((1, D_MODEL), lambda i, f: (0, 0)),
            modspec, modspec, modspec,
            pl.BlockSpec((D_MODEL, FFN_TF), lambda i, f: (0, f)),
            pl.BlockSpec((D_MODEL, FFN_TF), lambda i, f: (0, f + nf)),
            pl.BlockSpec((FFN_CONV, FFN_TF), lambda i, f: (0, f)),
            pl.BlockSpec((FFN_CONV, FFN_TF), lambda i, f: (0, f + nf)),
            pl.BlockSpec((FFN_TF, D_MODEL), lambda i, f: (f, 0)),
            pl.BlockSpec((1, D_MODEL), lambda i, f: (0, 0)),
        ],
        out_specs=[
            pl.BlockSpec((tm, D_MODEL), lambda i, f: (i, 0)),
            pl.BlockSpec((1, SUBLANES, FFN_TF), lambda i, f: (i, 0, f)),
            pl.BlockSpec((1, SUBLANES, FFN_TF), lambda i, f: (i, 0, f)),
        ],
        out_shape=[
            jax.ShapeDtypeStruct((m, D_MODEL), F32),
            jax.ShapeDtypeStruct((m // tm, SUBLANES, D_FF), F32),
            jax.ShapeDtypeStruct((m // tm, SUBLANES, D_FF), F32),
        ],
        scratch_shapes=[
            pltpu.VMEM((tm + FFN_HALO, D_MODEL), BF16),
            pltpu.VMEM((tm, D_MODEL), F32),
        ],
        compiler_params=_cparams("arbitrary", "arbitrary"),
        name="conv_ffn_prompt",
    )(x, x, conv_prev, conv_prev, g, sc, sh, gt, w_up, w_up, w_conv, w_conv, w_down, gp)


def _ffn_s_kernel(x_ref, p1g_ref, p1v_ref, p2g_ref, p2v_ref, g_ref, sc_ref, sh_ref, gt_ref, wg_ref, wv_ref,
                  cg_ref, cv_ref, wd_ref, gp_ref, o_ref, ug_ref, uv_ref, h_scr, acc_scr, *, seq_len):
    f = pl.program_id(1)

    @pl.when(f == 0)
    def _():
        h_scr[...] = _norm_mod(x_ref[...], g_ref[...], sc_ref[...], sh_ref[...]).astype(BF16)
        acc_scr[...] = jnp.zeros(acc_scr.shape, F32)

    h = h_scr[...]
    pos = lax.broadcasted_iota(jnp.int32, (h.shape[0], FFN_TF), 0) & (seq_len - 1)

    def conv(u, p1_ref, p2_ref, c_ref, u_ref):
        u_ref[...] = u
        u1 = jnp.where(pos >= 1, pltpu.roll(u, 1, 0), p1_ref[...])
        u2 = jnp.where(pos >= 2, pltpu.roll(u, 2, 0), p2_ref[...])
        c = c_ref[...]
        return c[0:1] * u2 + c[1:2] * u1 + c[2:3] * u

    zg = conv(_dot(h, wg_ref[...]), p1g_ref, p2g_ref, cg_ref, ug_ref)
    zv = conv(_dot(h, wv_ref[...]), p1v_ref, p2v_ref, cv_ref, uv_ref)
    acc_scr[...] += _dot(_gated(zg, zv), wd_ref[...])

    @pl.when(f == pl.num_programs(1) - 1)
    def _():
        o_ref[...] = x_ref[...] + gt_ref[...] * _rms(acc_scr[...], gp_ref[...])


def _ffn_s_call(x, prev1, prev2, g, sc, sh, gt, w_up, w_conv, w_down, gp, seq_len, tm):
    m = x.shape[0]
    assert tm % seq_len == 0 and seq_len & (seq_len - 1) == 0
    nf = D_FF // FFN_TF
    full = pl.BlockSpec((tm, D_MODEL), lambda i, f: (i, 0))
    lo = pl.BlockSpec((tm, FFN_TF), lambda i, f: (i, f))
    hi = pl.BlockSpec((tm, FFN_TF), lambda i, f: (i, f + nf))
    return pl.pallas_call(
        functools.partial(_ffn_s_kernel, seq_len=seq_len),
        grid=(m // tm, nf),
        in_specs=[
            full, lo, hi, lo, hi,
            pl.BlockSpec((1, D_MODEL), lambda i, f: (0, 0)),
            full, full, full,
            pl.BlockSpec((D_MODEL, FFN_TF), lambda i, f: (0, f)),
            pl.BlockSpec((D_MODEL, FFN_TF), lambda i, f: (0, f + nf)),
            pl.BlockSpec((FFN_CONV, FFN_TF), lambda i, f: (0, f)),
            pl.BlockSpec((FFN_CONV, FFN_TF), lambda i, f: (0, f + nf)),
            pl.BlockSpec((FFN_TF, D_MODEL), lambda i, f: (f, 0)),
            pl.BlockSpec((1, D_MODEL), lambda i, f: (0, 0)),
        ],
        out_specs=[full, lo, lo],
        out_shape=[
            jax.ShapeDtypeStruct((m, D_MODEL), F32),
            jax.ShapeDtypeStruct((m, D_FF), F32),
            jax.ShapeDtypeStruct((m, D_FF), F32),
        ],
        scratch_shapes=[pltpu.VMEM((tm, D_MODEL), BF16), pltpu.VMEM((tm, D_MODEL), F32)],
        compiler_params=_cparams("arbitrary", "arbitrary"),
        name="conv_ffn_sample",
    )(x, prev1, prev1, prev2, prev2, g, sc, sh, gt, w_up, w_up, w_conv, w_conv, w_down, gp)


def _seq_tile(t, cap):
    tm = min(t, cap)
    assert t % tm == 0
    return tm


@jax.jit
def _forward(x_prompt, x_sample, c_prompt, c_sample, cache_k, cache_v, state_wkv, state_shift,
             state_ffn_conv, w_ada, b_ada, g_pre_mix, g_post_mix, g_pre_ffn, g_post_ffn, w_in,
             lam_q1, lam_k1, lam_q2, lam_k2, g_subln, mu_shift, w0, w_w2, a0, w_a2, w_g2, k_k, k_a,
             r_k, ln_x_w, ln_x_b, w_out, w_up, w_conv_ffn, w_down):
    bp, t, _ = x_prompt.shape
    bs, ts, _ = x_sample.shape
    past = cache_k.shape[2]
    assert w_ada.shape[0] == 1, "single-layer problem"

    n_c = bp + bs
    n_c_pad = -(-n_c // SUBLANES) * SUBLANES
    c_all = jnp.concatenate([c_prompt, c_sample, jnp.zeros((n_c_pad - n_c, D_MODEL), F32)], axis=0)
    mod = _ada_call(c_all, w_ada[0], b_ada)
    sh_m, sc_m, gt_m, sh_f, sc_f, gt_f = [mod[:, i * D_MODEL:(i + 1) * D_MODEL] for i in range(6)]
    p_mod = lambda a: a[:bp].reshape(bp, 1, D_MODEL)
    s_mod = lambda a: jnp.repeat(a[bp:n_c], ts, axis=0)

    w_in_b = w_in[0].astype(BF16)
    w_qkv, w_rw = w_in_b[:, :3 * A_WIDTH], w_in_b[:, 3 * A_WIDTH:]
    w_out_b = w_out[0].astype(BF16)
    w_up_b = w_up[0].astype(BF16)
    w_down_b = w_down[0].astype(BF16)
    lams = (lam_q1, lam_k1, lam_q2, lam_k2)
    prm = {
        "mu_shift": mu_shift, "w0": w0, "a0": a0, "k_k": k_k, "k_a": k_a,
        "r_k": r_k.reshape(1, B_WIDTH), "ln_x_w": ln_x_w, "ln_x_b": ln_x_b,
        "w_w2": w_w2[0].astype(BF16), "w_a2": w_a2[0].astype(BF16), "w_g2": w_g2[0].astype(BF16),
    }

    tm = _seq_tile(t, 512)
    tps = t // tm
    xp = x_prompt.reshape(bp * t, D_MODEL)
    kf, vf, qkvb, rw = _in_proj_call(xp, g_pre_mix, p_mod(sc_m), p_mod(sh_m), w_qkv, w_rw, tm, tps)
    oa = _pattn_call(lams, qkvb.reshape(bp, t, 3 * A_WIDTH), g_subln, _seq_tile(t, 256))
    ob, s_last = _rwkv_call(rw.reshape(bp, t, RW_COLS), jnp.zeros((bp, 1, RW_COLS), F32),
                            jnp.zeros((bp, N_GROUPS, MXU_DIM, MXU_DIM), state_wkv.dtype), prm,
                            _seq_tile(t, 256), 32)
    x1 = _mixout_call(oa.reshape(bp * t, A_WIDTH), ob.reshape(bp * t, B_WIDTH), xp, w_out_b,
                      g_post_mix, p_mod(gt_m), tm, tps)
    yp, clg, clv = _ffn_call(x1, jnp.zeros((bp, FFN_CONV - 1, 2 * D_FF), F32), g_pre_ffn, p_mod(sc_f),
                             p_mod(sh_f), p_mod(gt_f), w_up_b, w_conv_ffn[0], w_down_b, g_post_ffn,
                             tm, tps)
    rw3 = rw.reshape(bp, t, RW_COLS)
    out_prompt = (
        yp.reshape(bp, t, D_MODEL),
        kf.reshape(1, bp, t, A_HEADS, 2 * A_DK),
        vf.reshape(1, bp, t, A_HEADS, A_DV),
        _groups_to_state(s_last)[None],
        rw3[:, -1][None],
        jnp.concatenate([clg[tps - 1::tps, -(FFN_CONV - 1):], clv[tps - 1::tps, -(FFN_CONV - 1):]],
                        axis=-1)[None],
    )

    ms = bs * ts
    xs = x_sample.reshape(ms, D_MODEL)
    kf_s, vf_s, qkvb_s, rw_s = _in_proj_call(xs, g_pre_mix, s_mod(sc_m), s_mod(sh_m), w_qkv, w_rw, ms, 1)
    oa_s = _sattn_call(lams, qkvb_s.reshape(bs, ts, 3 * A_WIDTH), cache_k[0].reshape(bs, past, A_WIDTH),
                       cache_v[0].reshape(bs, past, A_WIDTH), g_subln, _seq_tile(past, 512))
    ob_s, s_last_s = _rwkv_call(rw_s.reshape(bs, ts, RW_COLS), state_shift[0][:, None, :],
                                _state_to_groups(state_wkv[0]), prm, ts, ts)
    x1_s = _mixout_call(oa_s.reshape(ms, A_WIDTH), ob_s.reshape(ms, B_WIDTH), xs, w_out_b,
                        g_post_mix, s_mod(gt_m), ms, 1)
    cprev = state_ffn_conv[0]
    zrow = jnp.zeros((bs, ts, 2 * D_FF), F32)
    prev1 = zrow.at[:, 0].set(cprev[:, 1]).reshape(ms, 2 * D_FF)
    prev2 = zrow.at[:, 0].set(cprev[:, 0]).at[:, 1].set(cprev[:, 1]).reshape(ms, 2 * D_FF)
    ys, ug, uv = _ffn_s_call(x1_s, prev1, prev2, g_pre_ffn, s_mod(sc_f), s_mod(sh_f), s_mod(gt_f),
                             w_up_b, w_conv_ffn[0], w_down_b, g_post_ffn, ts, _seq_tile(ms, 256))
    u_all = jnp.concatenate([ug, uv], axis=-1).reshape(bs, ts, 2 * D_FF)
    out_sample = (
        ys.reshape(bs, ts, D_MODEL),
        kf_s.reshape(1, bs, ts, A_HEADS, 2 * A_DK),
        vf_s.reshape(1, bs, ts, A_HEADS, A_DV),
        _groups_to_state(s_last_s)[None],
        rw_s.reshape(bs, ts, RW_COLS)[:, -1][None],
        u_all[:, -(FFN_CONV - 1):][None],
    )
    return (out_prompt[0], out_sample[0]) + out_prompt[1:] + out_sample[1:]


def kernel(x_prompt, x_sample, c_prompt, c_sample, cache_k, cache_v, state_wkv, state_shift, state_ffn_conv, w_ada, b_ada, g_pre_mix, g_post_mix, g_pre_ffn, g_post_ffn, w_in, lam_q1, lam_k1, lam_q2, lam_k2, g_subln, mu_shift, w0, w_w2, a0, w_a2, w_g2, k_k, k_a, r_k, ln_x_w, ln_x_b, w_out, w_up, w_conv_ffn, w_down):
    return _forward(x_prompt, x_sample, c_prompt, c_sample, cache_k, cache_v, state_wkv, state_shift,
                    state_ffn_conv, w_ada, b_ada, g_pre_mix, g_post_mix, g_pre_ffn, g_post_ffn, w_in,
                    lam_q1, lam_k1, lam_q2, lam_k2, g_subln, mu_shift, w0, w_w2, a0, w_a2, w_g2, k_k,
                    k_a, r_k, ln_x_w, ln_x_b, w_out, w_up, w_conv_ffn, w_down)
```

```python
import functools
import math

import jax
import jax.numpy as jnp
from jax import lax
from jax.experimental import pallas as pl
from jax.experimental.pallas import tpu as pltpu

F32 = jnp.float32
BF16 = jnp.bfloat16

D_MODEL = 2048
A_HEADS = 8
A_DK = 64
A_DV = 128
A_WIDTH = A_HEADS * A_DV
B_HEAD = 64
B_WIDTH = D_MODEL - A_WIDTH
B_HEADS = B_WIDTH // B_HEAD
W_RANK = 64
A_RANK = 64
G_RANK = 128
RW_COLS = 3 * B_WIDTH + W_RANK + A_RANK + G_RANK
D_FF = 5632
FFN_CONV = 3
ATTN_CHUNK = 64
EPS = 1e-6
GN_EPS = 64e-5
LAM_INIT = 0.8 - 0.6 * math.exp(-0.3 * 0)

LANES = 128
SUBLANES = 8
BF16_ROWS = 16
MXU_DIM = 256
GROUP_HEADS = MXU_DIM // B_HEAD
N_GROUPS = B_WIDTH // MXU_DIM
VMEM_LIMIT_BYTES = 56 * 1024 * 1024
NEG_BIG = -1e30


def _cparams(*sem):
    return pltpu.CompilerParams(dimension_semantics=sem, vmem_limit_bytes=VMEM_LIMIT_BYTES)


def _dot(a, b):
    return jnp.dot(a, b, preferred_element_type=F32)


def _dot_nt(a, b):
    return lax.dot_general(a, b, (((1,), (1,)), ((), ())), preferred_element_type=F32)


def _dot_tn(a, b):
    return lax.dot_general(a, b, (((0,), (0,)), ((), ())), preferred_element_type=F32)


def _idiv(x, n):
    assert n > 0 and n & (n - 1) == 0
    return x >> (n.bit_length() - 1)


def _sigmoid(x):
    return 1.0 / (1.0 + jnp.exp(-x))


def _split_bf16(x, terms):
    out = []
    rem = x
    for _ in range(terms):
        piece = rem.astype(BF16)
        out.append(piece)
        rem = rem - piece.astype(F32)
    return out


def _norm_mod(x, g, sc, sh):
    ms = jnp.mean(x * x, axis=-1, keepdims=True)
    return (x * lax.rsqrt(ms + EPS) * g) * (1.0 + sc) + sh


def _rms(x, g):
    ms = jnp.mean(x * x, axis=-1, keepdims=True)
    return x * lax.rsqrt(ms + EPS) * g


def _mod(ref):
    return ref[0] if len(ref.shape) == 3 else ref[...]


def _lam(q1_ref, k1_ref, q2_ref, k2_ref):
    s1 = jnp.sum(q1_ref[...] * k1_ref[...], axis=-1, keepdims=True)
    s2 = jnp.sum(q2_ref[...] * k2_ref[...], axis=-1, keepdims=True)
    return jnp.exp(s1) - jnp.exp(s2) + LAM_INIT


def _ada_kernel(c_ref, w_ref, b_ref, o_ref):
    c = c_ref[...]
    s = (c * _sigmoid(c)).astype(BF16)
    o_ref[...] = _dot(s, w_ref[...].astype(BF16)) + b_ref[...]


def _ada_call(c_all, w_ada, b_ada):
    rows = c_all.shape[0]
    n = w_ada.shape[1]
    tn = 1024
    return pl.pallas_call(
        _ada_kernel,
        grid=(n // tn,),
        in_specs=[
            pl.BlockSpec((rows, D_MODEL), lambda j: (0, 0)),
            pl.BlockSpec((D_MODEL, tn), lambda j: (0, j)),
            pl.BlockSpec((1, tn), lambda j: (0, j)),
        ],
        out_specs=pl.BlockSpec((rows, tn), lambda j: (0, j)),
        out_shape=jax.ShapeDtypeStruct((rows, n), F32),
        compiler_params=_cparams("arbitrary"),
        name="ada_mod",
    )(c_all, w_ada, b_ada)


def _qkv_kernel(x_ref, g_ref, sc_ref, sh_ref, w_ref, kf_ref, vf_ref, qkvb_ref, h_scr):
    j = pl.program_id(1)

    @pl.when(j == 0)
    def _():
        h_scr[...] = _norm_mod(x_ref[...], g_ref[...], _mod(sc_ref), _mod(sh_ref)).astype(BF16)

    acc = _dot(h_scr[...], w_ref[...])

    @pl.when(j == 0)
    def _():
        qkvb_ref[...] = (acc * (A_DK ** -0.5)).astype(BF16)

    @pl.when(j == 1)
    def _():
        kf_ref[...] = acc
        qkvb_ref[...] = acc.astype(BF16)

    @pl.when(j == 2)
    def _():
        vf_ref[...] = acc
        qkvb_ref[...] = acc.astype(BF16)


def _rw_kernel(x_ref, g_ref, sc_ref, sh_ref, w_ref, o_ref, h_scr):
    @pl.when(pl.program_id(1) == 0)
    def _():
        h_scr[...] = _norm_mod(x_ref[...], g_ref[...], _mod(sc_ref), _mod(sh_ref)).astype(BF16)

    o_ref[...] = _dot(h_scr[...], w_ref[...])


def _mod_spec(mod, tm, tiles_per_seq, nidx):
    if mod.ndim == 3:
        if nidx == 2:
            return pl.BlockSpec((1, 1, D_MODEL), lambda i, j: (i // tiles_per_seq, 0, 0))
        return pl.BlockSpec((1, 1, D_MODEL), lambda i: (i // tiles_per_seq, 0, 0))
    if nidx == 2:
        return pl.BlockSpec((tm, D_MODEL), lambda i, j: (i, 0))
    return pl.BlockSpec((tm, D_MODEL), lambda i: (i, 0))


def _in_proj_call(x, g, sc, sh, w_qkv, w_rw, tm, tiles_per_seq):
    m = x.shape[0]
    nm = m // tm
    common = [
        pl.BlockSpec((tm, D_MODEL), lambda i, j: (i, 0)),
        pl.BlockSpec((1, D_MODEL), lambda i, j: (0, 0)),
        _mod_spec(sc, tm, tiles_per_seq, 2),
        _mod_spec(sh, tm, tiles_per_seq, 2),
    ]
    kf, vf, qkvb = pl.pallas_call(
        _qkv_kernel,
        grid=(nm, 3),
        in_specs=common + [pl.BlockSpec((D_MODEL, A_WIDTH), lambda i, j: (0, j))],
        out_specs=[
            pl.BlockSpec((tm, A_WIDTH), lambda i, j: (i, 0)),
            pl.BlockSpec((tm, A_WIDTH), lambda i, j: (i, 0)),
            pl.BlockSpec((tm, A_WIDTH), lambda i, j: (i, j)),
        ],
        out_shape=[
            jax.ShapeDtypeStruct((m, A_WIDTH), F32),
            jax.ShapeDtypeStruct((m, A_WIDTH), F32),
            jax.ShapeDtypeStruct((m, 3 * A_WIDTH), BF16),
        ],
        scratch_shapes=[pltpu.VMEM((tm, D_MODEL), BF16)],
        compiler_params=_cparams("arbitrary", "arbitrary"),
        name="in_proj_qkv",
    )(x, g, sc, sh, w_qkv)
    tn = RW_COLS // 2
    rw = pl.pallas_call(
        _rw_kernel,
        grid=(nm, 2),
        in_specs=common + [pl.BlockSpec((D_MODEL, tn), lambda i, j: (0, j))],
        out_specs=pl.BlockSpec((tm, tn), lambda i, j: (i, j)),
        out_shape=jax.ShapeDtypeStruct((m, RW_COLS), F32),
        scratch_shapes=[pltpu.VMEM((tm, D_MODEL), BF16)],
        compiler_params=_cparams("arbitrary", "arbitrary"),
        name="in_proj_rw",
    )(x, g, sc, sh, w_rw)
    return kf, vf, qkvb, rw


def _softmax_step(s, vb, m_ref, l_ref, acc_ref, idx):
    m_prev = m_ref[idx]
    m_new = jnp.maximum(m_prev, jnp.max(s, axis=-1, keepdims=True))
    alpha = jnp.exp(m_prev - m_new)
    p = jnp.exp(s - m_new)
    l_ref[idx] = alpha * l_ref[idx] + jnp.sum(p, axis=-1, keepdims=True)
    acc_ref[idx] = alpha * acc_ref[idx] + _dot(p.astype(BF16), vb)
    m_ref[idx] = m_new


def _subln(o, g):
    return _rms(o, g) * (1.0 - LAM_INIT)


def _pattn_kernel(q1_ref, k1_ref, q2_ref, k2_ref, q_ref, k_ref, v_ref, g_ref, o_ref,
                  m_scr, l_scr, acc_scr, *, tq):
    i = pl.program_id(2)
    q = q_ref[0]
    lane = lax.broadcasted_iota(jnp.int32, (tq, A_DV), 1)
    zero = jnp.zeros_like(q)
    q_st = jnp.concatenate([jnp.where(lane < A_DK, q, zero), jnp.where(lane >= A_DK, q, zero)], axis=0)
    m_scr[...] = jnp.full(m_scr.shape, NEG_BIG, F32)
    l_scr[...] = jnp.zeros(l_scr.shape, F32)
    acc_scr[...] = jnp.zeros(acc_scr.shape, F32)

    def block(j, mask):
        r0 = pl.multiple_of(j * tq, tq)
        kb = k_ref[0, pl.ds(r0, tq), :]
        vb = v_ref[0, pl.ds(r0, tq), :]
        s_t = _dot_nt(kb, q_st)
        if mask is not None:
            s_t = jnp.where(mask, s_t, NEG_BIG)
        m_prev = m_scr[...]
        m_new = jnp.maximum(m_prev, jnp.max(s_t, axis=0, keepdims=True))
        alpha = jnp.exp(m_prev - m_new)
        p_t = jnp.exp(s_t - m_new)
        l_scr[...] = alpha * l_scr[...] + jnp.sum(p_t, axis=0, keepdims=True)
        acc_scr[...] = alpha * acc_scr[...] + _dot_tn(vb, p_t.astype(BF16))
        m_scr[...] = m_new

    def body(j, carry):
        block(j, None)
        return carry

    lax.fori_loop(0, i, body, 0)
    rk = lax.broadcasted_iota(jnp.int32, (tq, 2 * tq), 0)
    cq = lax.broadcasted_iota(jnp.int32, (tq, 2 * tq), 1) & (tq - 1)
    block(i, _idiv(rk, ATTN_CHUNK) <= _idiv(cq, ATTN_CHUNK))

    lam = _lam(q1_ref, k1_ref, q2_ref, k2_ref)
    o_t = acc_scr[:, 0:tq] / l_scr[:, 0:tq] - lam * (acc_scr[:, tq:2 * tq] / l_scr[:, tq:2 * tq])
    o_ref[0] = _subln(o_t.T, g_ref[...]).astype(BF16)


def _lam_specs(nidx):
    if nidx == 3:
        return [pl.BlockSpec((1, A_DK), lambda b, h, i: (0, 0))] * 4
    return [pl.BlockSpec((1, A_DK), lambda b, j: (0, 0))] * 4


def _pattn_call(lams, qkvb, g_subln, tq):
    bsz, t, _ = qkvb.shape
    return pl.pallas_call(
        functools.partial(_pattn_kernel, tq=tq),
        grid=(bsz, A_HEADS, t // tq),
        in_specs=_lam_specs(3) + [
            pl.BlockSpec((1, tq, A_DV), lambda b, h, i: (b, i, h)),
            pl.BlockSpec((1, t, A_DV), lambda b, h, i: (b, 0, A_HEADS + h)),
            pl.BlockSpec((1, t, A_DV), lambda b, h, i: (b, 0, 2 * A_HEADS + h)),
            pl.BlockSpec((1, A_DV), lambda b, h, i: (0, 0)),
        ],
        out_specs=pl.BlockSpec((1, tq, A_DV), lambda b, h, i: (b, i, h)),
        out_shape=jax.ShapeDtypeStruct((bsz, t, A_WIDTH), BF16),
        scratch_shapes=[
            pltpu.VMEM((1, 2 * tq), F32),
            pltpu.VMEM((1, 2 * tq), F32),
            pltpu.VMEM((A_DV, 2 * tq), F32),
        ],
        compiler_params=_cparams("arbitrary", "arbitrary", "arbitrary"),
        name="prompt_attention",
    )(*lams, qkvb, qkvb, qkvb, g_subln)


N_VQ = 2 * A_HEADS


def _sattn_kernel(q1_ref, k1_ref, q2_ref, k2_ref, q_ref, kn_ref, vn_ref, ck_ref, cv_ref, g_ref, o_ref,
                  qb_scr, m_scr, l_scr, acc_scr, *, tnew):
    j = pl.program_id(1)
    nvq = N_VQ * tnew

    @pl.when(j == 0)
    def _():
        q = q_ref[0]
        qt = jnp.concatenate([q] * N_VQ, axis=0)
        row = lax.broadcasted_iota(jnp.int32, (nvq, A_WIDTH), 0)
        col = lax.broadcasted_iota(jnp.int32, (nvq, A_WIDTH), 1)
        qb_scr[...] = jnp.where(_idiv(row, tnew) == _idiv(col, A_DK), qt, jnp.zeros_like(qt))
        m_scr[...] = jnp.full(m_scr.shape, NEG_BIG, F32)
        l_scr[...] = jnp.zeros(l_scr.shape, F32)
        acc_scr[...] = jnp.zeros(acc_scr.shape, F32)

    def step(kb, vb):
        _softmax_step(_dot_nt(qb_scr[...], kb), vb, m_scr, l_scr, acc_scr, 0)

    def cached_rows(ref):
        tk = ref.shape[1] // A_HEADS
        heads = [ref[0, pl.ds(h, tk, stride=A_HEADS), :] for h in range(A_HEADS)]
        return jnp.concatenate(heads, axis=-1).astype(BF16)

    step(cached_rows(ck_ref), cached_rows(cv_ref))

    @pl.when(j == pl.num_programs(1) - 1)
    def _():
        step(kn_ref[0], vn_ref[0])
        lam = _lam(q1_ref, k1_ref, q2_ref, k2_ref)
        g = g_ref[...]
        for h in range(A_HEADS):
            cs = slice(h * A_DV, (h + 1) * A_DV)
            r0 = slice((2 * h) * tnew, (2 * h + 1) * tnew)
            r1 = slice((2 * h + 1) * tnew, (2 * h + 2) * tnew)
            o0 = acc_scr[0, r0, cs] / l_scr[0, r0, :]
            o1 = acc_scr[0, r1, cs] / l_scr[0, r1, :]
            o_ref[0, :, cs] = _subln(o0 - lam * o1, g).astype(BF16)


def _sattn_call(lams, qkvb, cache_k, cache_v, g_subln, tk):
    bsz, tnew, _ = qkvb.shape
    past = cache_k.shape[1] // A_HEADS
    nvq = N_VQ * tnew
    return pl.pallas_call(
        functools.partial(_sattn_kernel, tnew=tnew),
        grid=(bsz, past // tk),
        in_specs=_lam_specs(2) + [
            pl.BlockSpec((1, tnew, A_WIDTH), lambda b, j: (b, 0, 0)),
            pl.BlockSpec((1, tnew, A_WIDTH), lambda b, j: (b, 0, 1)),
            pl.BlockSpec((1, tnew, A_WIDTH), lambda b, j: (b, 0, 2)),
            pl.BlockSpec((1, tk * A_HEADS, A_DV), lambda b, j: (b, j, 0)),
            pl.BlockSpec((1, tk * A_HEADS, A_DV), lambda b, j: (b, j, 0)),
            pl.BlockSpec((1, A_DV), lambda b, j: (0, 0)),
        ],
        out_specs=pl.BlockSpec((1, tnew, A_WIDTH), lambda b, j: (b, 0, 0)),
        out_shape=jax.ShapeDtypeStruct((bsz, tnew, A_WIDTH), BF16),
        scratch_shapes=[
            pltpu.VMEM((nvq, A_WIDTH), BF16),
            pltpu.VMEM((1, nvq, 1), F32),
            pltpu.VMEM((1, nvq, 1), F32),
            pltpu.VMEM((1, nvq, A_WIDTH), F32),
        ],
        compiler_params=_cparams("arbitrary", "arbitrary"),
        name="sample_attention",
    )(*lams, qkvb, qkvb, qkvb, cache_k, cache_v, g_subln)


def _head_sum(x, ones_bd, terms=2):
    out = None
    for piece in _split_bf16(x, terms):
        d = _dot(piece, ones_bd)
        out = d if out is None else out + d
    return out


def _rwkv_kernel(p_ref, ph_ref, st_ref, s0_ref, mu_ref, w0_ref, a0_ref, kk_ref, ka_ref, rk_ref,
                 lnw_ref, lnb_ref, ww2_ref, wa2_ref, wg2_ref, ob_ref, so_ref,
                 r_scr, k_scr, v_scr, na_scr, nb_scr, ld_scr, cum_scr, g_scr, bon_scr, y_scr, s_scr,
                 *, tb, chunk):
    i = pl.program_id(1)
    c_len = chunk
    n_chunks = tb // c_len

    row_g = lax.broadcasted_iota(jnp.int32, (MXU_DIM, MXU_DIM), 0)
    col_g = lax.broadcasted_iota(jnp.int32, (MXU_DIM, MXU_DIM), 1)
    bd_mask = _idiv(row_g, B_HEAD) == _idiv(col_g, B_HEAD)
    ones_bd = jnp.where(bd_mask, 1.0, 0.0).astype(BF16)

    @pl.when(i == 0)
    def _():
        rep = jnp.where(
            lax.broadcasted_iota(jnp.int32, (B_HEAD, MXU_DIM), 0)
            == (lax.broadcasted_iota(jnp.int32, (B_HEAD, MXU_DIM), 1) & (B_HEAD - 1)), 1.0, 0.0).astype(BF16)
        for q in range(N_GROUPS):
            wide = None
            for piece in _split_bf16(s0_ref[0, q], 3):
                d = _dot(piece, rep)
                wide = d if wide is None else wide + d
            s_scr[q] = jnp.where(bd_mask, wide, 0.0)

    p = p_ref[0]
    prev_row = jnp.where(i == 0, st_ref[0], ph_ref[0, SUBLANES - 1:SUBLANES, :])
    rowi = lax.broadcasted_iota(jnp.int32, p.shape, 0)
    p_prev = jnp.where(rowi == 0, prev_row, pltpu.roll(p, 1, 0))
    xs = p + (p_prev - p) * mu_ref[...]
    r = xs[:, 0:B_WIDTH]
    k = xs[:, B_WIDTH:2 * B_WIDTH]
    v = xs[:, 2 * B_WIDTH:3 * B_WIDTH]
    wa = xs[:, 3 * B_WIDTH:3 * B_WIDTH + W_RANK + A_RANK]
    gd = xs[:, 3 * B_WIDTH + W_RANK + A_RANK:]

    lane_wa = lax.broadcasted_iota(jnp.int32, wa.shape, 1)
    w_cat = jnp.concatenate([ww2_ref[...], wa2_ref[...]], axis=0)
    zero_wa = jnp.zeros_like(wa)
    w_lin = _dot(jnp.where(lane_wa < W_RANK, jnp.tanh(wa), zero_wa).astype(BF16), w_cat)
    a_lin = _dot(jnp.where(lane_wa >= W_RANK, wa, zero_wa).astype(BF16), w_cat)
    zw = -(w0_ref[...] + w_lin)
    softplus = jnp.maximum(zw, 0.0) + jnp.log(1.0 + jnp.exp(-jnp.abs(zw)))
    logd = -jnp.exp(-softplus - 0.5)
    a = _sigmoid(a0_ref[...] + a_lin)
    g_scr[...] = _dot(_sigmoid(gd).astype(BF16), wg2_ref[...])

    k2 = k * (1.0 + (a - 1.0) * ka_ref[...])
    kk = k * kk_ref[...]
    rkk = r * k2 * rk_ref[...]
    for q in range(N_GROUPS):
        ls = slice(q * MXU_DIM, (q + 1) * MXU_DIM)
        kq = kk[:, ls]
        nrm = jnp.maximum(jnp.sqrt(_head_sum(kq * kq, ones_bd)), 1e-12)
        kn = kq / nrm
        na_scr[:, ls] = -kn
        nb_scr[:, ls] = kn * a[:, ls]
        bon_scr[:, ls] = _head_sum(rkk[:, ls], ones_bd) * v[:, ls]
    r_scr[...] = r
    k_scr[...] = k2
    v_scr[...] = v
    ld_scr[...] = logd

    row_t = lax.broadcasted_iota(jnp.int32, (tb, tb), 0)
    col_t = lax.broadcasted_iota(jnp.int32, (tb, tb), 1)
    tri = jnp.where((_idiv(row_t, c_len) == _idiv(col_t, c_len)) & (col_t <= row_t), 1.0, 0.0).astype(BF16)
    cum = None
    for piece in _split_bf16(logd, 3):
        d = _dot(tri, piece)
        cum = d if cum is None else cum + d
    cum_scr[...] = cum

    gc = GROUP_HEADS * c_len
    lane_q = lax.broadcasted_iota(jnp.int32, (1, MXU_DIM), 1)
    head_masks = [_idiv(lane_q, B_HEAD) == j for j in range(GROUP_HEADS)]
    row_s = lax.broadcasted_iota(jnp.int32, (gc, gc), 0)
    col_s = lax.broadcasted_iota(jnp.int32, (gc, gc), 1)
    same_head = _idiv(row_s, c_len) == _idiv(col_s, c_len)
    strict = same_head & (col_s < row_s)
    incl = same_head & (col_s <= row_s)
    eye = jnp.where(col_s == row_s, 1.0, 0.0).astype(F32)
    n_double = int(math.log2(c_len)) - 1
    groups = range(N_GROUPS)

    def stack(x):
        z = jnp.zeros_like(x)
        return jnp.concatenate([jnp.where(head_masks[j], x, z) for j in range(GROUP_HEADS)], axis=0)

    def unstack(x):
        out = x[0:c_len]
        for j in range(1, GROUP_HEADS):
            out = out + x[j * c_len:(j + 1) * c_len]
        return out

    def chunk_body(c, carry):
        r0 = pl.multiple_of(c * c_len, c_len)
        rows = pl.ds(r0, c_len)
        a_s, r_s, b_s, k_s, be_s, ke_s, v_s, dec = [], [], [], [], [], [], [], []
        for q in groups:
            ls = slice(q * MXU_DIM, (q + 1) * MXU_DIM)
            cum_q = cum_scr[rows, ls]
            k_q = k_scr[rows, ls]
            nb_q = nb_scr[rows, ls]
            cum_last = cum_q[c_len - 1:c_len, :]
            e_out = jnp.exp(-cum_q)
            e_end = jnp.exp(cum_last - cum_q)
            a_s.append(stack((na_scr[rows, ls] * jnp.exp(cum_q - ld_scr[rows, ls])).astype(BF16)))
            r_s.append(stack((r_scr[rows, ls] * jnp.exp(cum_q)).astype(BF16)))
            b_s.append(stack((nb_q * e_out).astype(BF16)))
            k_s.append(stack((k_q * e_out).astype(BF16)))
            be_s.append(stack((nb_q * e_end).astype(BF16)))
            ke_s.append(stack((k_q * e_end).astype(BF16)))
            v_s.append(stack(v_scr[rows, ls].astype(BF16)))
            dec.append(jnp.exp(cum_last))

        if gc % LANES == 0:
            g_all = [_dot_nt(jnp.concatenate([a_s[q], r_s[q]], axis=0),
                             jnp.concatenate([b_s[q], k_s[q]], axis=0)) for q in groups]
            g_ab = [g[0:gc, 0:gc] for g in g_all]
            g_ak = [g[0:gc, gc:2 * gc] for g in g_all]
            g_rb = [g[gc:2 * gc, 0:gc] for g in g_all]
            g_rk = [g[gc:2 * gc, gc:2 * gc] for g in g_all]
        else:
            g_ab = [_dot_nt(a_s[q], b_s[q]) for q in groups]
            g_ak = [_dot_nt(a_s[q], k_s[q]) for q in groups]
            g_rb = [_dot_nt(r_s[q], b_s[q]) for q in groups]
            g_rk = [_dot_nt(r_s[q], k_s[q]) for q in groups]
        aab = [jnp.where(strict, g, 0.0) for g in g_ab]
        akrk = [jnp.concatenate([jnp.where(strict, g_ak[q], 0.0), jnp.where(incl, g_rk[q], 0.0)],
                                axis=0).astype(BF16) for q in groups]
        arb = [jnp.where(incl, g, 0.0).astype(BF16) for g in g_rb]

        t_m = [eye + x for x in aab]
        pw = [x.astype(BF16) for x in aab]
        pw = [_dot(x, x) for x in pw]
        for m in range(n_double):
            pb = [x.astype(BF16) for x in pw]
            if m + 1 < n_double:
                pw = [_dot(x, x) for x in pb]
            t_m = [t_m[q] + _dot(t_m[q].astype(BF16), pb[q]) for q in groups]
        t_b = [x.astype(BF16) for x in t_m]

        akv = [_dot(akrk[q], v_s[q]) for q in groups]
        tw = [_dot(t_b[q], jnp.concatenate([a_s[q], akv[q][0:gc].astype(BF16)], axis=1)) for q in groups]
        d_e = [_dot_tn(v_s[q], ke_s[q]) for q in groups]

        s_prev = [s_scr[q] for q in groups]
        wr = [_dot_nt(jnp.concatenate([tw[q][:, 0:MXU_DIM].astype(BF16), r_s[q]], axis=0),
                      s_prev[q].astype(BF16)) for q in groups]
        u_b = [(wr[q][0:gc] + tw[q][:, MXU_DIM:2 * MXU_DIM]).astype(BF16) for q in groups]
        y_st = [wr[q][gc:2 * gc] + _dot(arb[q], u_b[q]) + akv[q][gc:2 * gc] for q in groups]
        s_new = [s_prev[q] * dec[q] + _dot_tn(u_b[q], be_s[q]) + d_e[q] for q in groups]
        for q in groups:
            s_scr[q] = s_new[q]
            y_scr[rows, q * MXU_DIM:(q + 1) * MXU_DIM] = unstack(y_st[q])
        return carry

    lax.fori_loop(0, n_chunks, chunk_body, 0)

    @pl.when(i == pl.num_programs(1) - 1)
    def _():
        rep_t = jnp.where(
            (lax.broadcasted_iota(jnp.int32, (MXU_DIM, B_HEAD), 0) & (B_HEAD - 1))
            == lax.broadcasted_iota(jnp.int32, (MXU_DIM, B_HEAD), 1), 1.0, 0.0).astype(BF16)
        for q in groups:
            narrow = None
            for piece in _split_bf16(s_scr[q], 3):
                d = _dot(piece, rep_t)
                narrow = d if narrow is None else narrow + d
            so_ref[0, q] = narrow

    inv_n = 1.0 / B_HEAD
    for q in range(N_GROUPS):
        ls = slice(q * MXU_DIM, (q + 1) * MXU_DIM)
        y = y_scr[:, ls]
        mu = _head_sum(y, ones_bd) * inv_n
        d = y - mu
        var = _head_sum(d * d, ones_bd) * inv_n
        yn = d * lax.rsqrt(var + GN_EPS) * lnw_ref[:, ls] + lnb_ref[:, ls]
        ob_ref[0, :, ls] = ((yn + bon_scr[:, ls]) * g_scr[:, ls]).astype(BF16)


def _rwkv_call(rw, shift0, s0, prm, tb, chunk):
    bsz, t, _ = rw.shape
    nt = t // tb
    hb = tb // SUBLANES
    vec = lambda n: pl.BlockSpec((1, n), lambda b, i: (0, 0))
    return pl.pallas_call(
        functools.partial(_rwkv_kernel, tb=tb, chunk=chunk),
        grid=(bsz, nt),
        in_specs=[
            pl.BlockSpec((1, tb, RW_COLS), lambda b, i: (b, i, 0)),
            pl.BlockSpec((1, SUBLANES, RW_COLS), lambda b, i: (b, jnp.maximum(i * hb - 1, 0), 0)),
            pl.BlockSpec((1, 1, RW_COLS), lambda b, i: (b, 0, 0)),
            pl.BlockSpec((1, N_GROUPS, MXU_DIM, B_HEAD), lambda b, i: (b, 0, 0, 0)),
            vec(RW_COLS), vec(B_WIDTH), vec(B_WIDTH), vec(B_WIDTH), vec(B_WIDTH), vec(B_WIDTH),
            vec(B_WIDTH), vec(B_WIDTH),
            pl.BlockSpec((W_RANK, B_WIDTH), lambda b, i: (0, 0)),
            pl.BlockSpec((A_RANK, B_WIDTH), lambda b, i: (0, 0)),
            pl.BlockSpec((G_RANK, B_WIDTH), lambda b, i: (0, 0)),
        ],
        out_specs=[
            pl.BlockSpec((1, tb, B_WIDTH), lambda b, i: (b, i, 0)),
            pl.BlockSpec((1, N_GROUPS, MXU_DIM, B_HEAD), lambda b, i: (b, 0, 0, 0)),
        ],
        out_shape=[
            jax.ShapeDtypeStruct((bsz, t, B_WIDTH), BF16),
            jax.ShapeDtypeStruct((bsz, N_GROUPS, MXU_DIM, B_HEAD), F32),
        ],
        scratch_shapes=[pltpu.VMEM((tb, B_WIDTH), F32)] * 10
        + [pltpu.VMEM((N_GROUPS, MXU_DIM, MXU_DIM), F32)],
        compiler_params=_cparams("arbitrary", "arbitrary"),
        name="rwkv7_mix",
    )(rw, rw, shift0, s0, prm["mu_shift"], prm["w0"], prm["a0"], prm["k_k"], prm["k_a"], prm["r_k"],
      prm["ln_x_w"], prm["ln_x_b"], prm["w_w2"], prm["w_a2"], prm["w_g2"])


def _state_to_groups(s):
    return s.reshape(s.shape[0], N_GROUPS, MXU_DIM, B_HEAD)


def _groups_to_state(sg):
    return sg.reshape(sg.shape[0], B_HEADS, B_HEAD, B_HEAD)


def _mixout_kernel(oa_ref, ob_ref, x_ref, w_ref, g_ref, gt_ref, o_ref):
    mix = _dot(oa_ref[...], w_ref[0:A_WIDTH, :]) + _dot(ob_ref[...], w_ref[A_WIDTH:D_MODEL, :])
    o_ref[...] = x_ref[...] + _mod(gt_ref) * _rms(mix, g_ref[...])


def _mixout_call(oa, ob, x, w_out, g, gt, tm, tiles_per_seq):
    m = x.shape[0]
    return pl.pallas_call(
        _mixout_kernel,
        grid=(m // tm,),
        in_specs=[
            pl.BlockSpec((tm, A_WIDTH), lambda i: (i, 0)),
            pl.BlockSpec((tm, B_WIDTH), lambda i: (i, 0)),
            pl.BlockSpec((tm, D_MODEL), lambda i: (i, 0)),
            pl.BlockSpec((D_MODEL, D_MODEL), lambda i: (0, 0)),
            pl.BlockSpec((1, D_MODEL), lambda i: (0, 0)),
            _mod_spec(gt, tm, tiles_per_seq, 1),
        ],
        out_specs=pl.BlockSpec((tm, D_MODEL), lambda i: (i, 0)),
        out_shape=jax.ShapeDtypeStruct((m, D_MODEL), F32),
        compiler_params=_cparams("arbitrary"),
        name="mix_out",
    )(oa, ob, x, w_out, g, gt)


FFN_HALO = BF16_ROWS
FFN_TF = 512


def _gated(zg, zv):
    return (zg * _sigmoid(zg) * zv).astype(BF16)


def _ffn_kernel(x_ref, xh_ref, cpg_ref, cpv_ref, g_ref, sc_ref, sh_ref, gt_ref, wg_ref, wv_ref,
                cg_ref, cv_ref, wd_ref, gp_ref, o_ref, clg_ref, clv_ref, h_scr, acc_scr,
                *, tm, tiles_per_seq):
    mi = pl.program_id(0)
    f = pl.program_id(1)
    first = (mi % tiles_per_seq) == 0

    @pl.when(f == 0)
    def _():
        g, sc, sh = g_ref[...], sc_ref[0], sh_ref[0]
        h_scr[FFN_HALO:, :] = _norm_mod(x_ref[...], g, sc, sh).astype(BF16)
        h_scr[:FFN_HALO, :] = _norm_mod(xh_ref[...], g, sc, sh).astype(BF16)
        acc_scr[...] = jnp.zeros(acc_scr.shape, F32)

    h = h_scr[...]
    row = lax.broadcasted_iota(jnp.int32, (FFN_HALO, FFN_TF), 0)

    def conv(u, cp_ref, c_ref, cl_ref):
        cp = cp_ref[0]
        head = u[:FFN_HALO]
        head_first = jnp.where(row == FFN_HALO - 2, cp[0:1],
                               jnp.where(row == FFN_HALO - 1, cp[1:2], head))
        ue = jnp.concatenate([jnp.where(first, head_first, head), u[FFN_HALO:]], axis=0)
        cl_ref[0] = ue[tm + FFN_HALO - SUBLANES:]
        c = c_ref[...]
        return (c[0:1] * pltpu.roll(ue, 2, 0)[FFN_HALO:] + c[1:2] * pltpu.roll(ue, 1, 0)[FFN_HALO:]
                + c[2:3] * ue[FFN_HALO:])

    zg = conv(_dot(h, wg_ref[...]), cpg_ref, cg_ref, clg_ref)
    zv = conv(_dot(h, wv_ref[...]), cpv_ref, cv_ref, clv_ref)
    acc_scr[...] += _dot(_gated(zg, zv), wd_ref[...])

    @pl.when(f == pl.num_programs(1) - 1)
    def _():
        o_ref[...] = x_ref[...] + gt_ref[0] * _rms(acc_scr[...], gp_ref[...])


def _ffn_call(x, conv_prev, g, sc, sh, gt, w_up, w_conv, w_down, gp, tm, tiles_per_seq):
    m = x.shape[0]
    nseq = conv_prev.shape[0]
    nf = D_FF // FFN_TF
    hb = tm // FFN_HALO
    seq = lambda i: i // tiles_per_seq
    modspec = pl.BlockSpec((1, 1, D_MODEL), lambda i, f: (seq(i), 0, 0))
    return pl.pallas_call(
        functools.partial(_ffn_kernel, tm=tm, tiles_per_seq=tiles_per_seq),
        grid=(m // tm, nf),
        in_specs=[
            pl.BlockSpec((tm, D_MODEL), lambda i, f: (i, 0)),
            pl.BlockSpec((FFN_HALO, D_MODEL), lambda i, f: (jnp.maximum(i * hb - 1, 0), 0)),
            pl.BlockSpec((1, FFN_CONV - 1, FFN_TF), lambda i, f: (seq(i), 0, f)),
            pl.BlockSpec((1, FFN_CONV - 1, FFN_TF), lambda i, f: (seq(i), 0, f + nf)),
            pl.BlockSpec((1, D_MODEL), lambda i, f: (0, 0)),
            modspec, modspec, modspec,
            pl.BlockSpec((D_MODEL, FFN_TF), lambda i, f: (0, f)),
            pl.BlockSpec((D_MODEL, FFN_TF), lambda i, f: (0, f + nf)),
            pl.BlockSpec((FFN_CONV, FFN_TF), lambda i, f: (0, f)),
            pl.BlockSpec((FFN_CONV, FFN_TF), lambda i, f: (0, f + nf)),
            pl.BlockSpec((FFN_TF, D_MODEL), lambda i, f: (f, 0)),
            pl.BlockSpec((1, D_MODEL), lambda i, f: (0, 0)),
        ],
        out_specs=[
            pl.BlockSpec((tm, D_MODEL), lambda i, f: (i, 0)),
            pl.BlockSpec((1, SUBLANES, FFN_TF), lambda i, f: (i, 0, f)),
            pl.BlockSpec((1, SUBLANES, FFN_TF), lambda i, f: (i, 0, f)),
        ],
        out_shape=[
            jax.ShapeDtypeStruct((m, D_MODEL), F32),
            jax.ShapeDtypeStruct((m // tm, SUBLANES, D_FF), F32),
            jax.ShapeDtypeStruct((m // tm, SUBLANES, D_FF), F32),
        ],
        scratch_shapes=[
            pltpu.VMEM((tm + FFN_HALO, D_MODEL), BF16),
            pltpu.VMEM((tm, D_MODEL), F32),
        ],
        compiler_params=_cparams("arbitrary", "arbitrary"),
        name="conv_ffn_prompt",
    )(x, x, conv_prev, conv_prev, g, sc, sh, gt, w_up, w_up, w_conv, w_conv, w_down, gp)


def _ffn_s_kernel(x_ref, cpg_ref, cpv_ref, g_ref, sc_ref, sh_ref, gt_ref, wg_ref, wv_ref,
                  cg_ref, cv_ref, wd_ref, gp_ref, o_ref, ug_ref, uv_ref, h_scr, acc_scr, *, seq_len):
    f = pl.program_id(1)

    @pl.when(f == 0)
    def _():
        h_scr[...] = _norm_mod(x_ref[...], g_ref[...], sc_ref[...], sh_ref[...]).astype(BF16)
        acc_scr[...] = jnp.zeros(acc_scr.shape, F32)

    h = h_scr[...]
    tm = h.shape[0]
    pos = lax.broadcasted_iota(jnp.int32, (tm, FFN_TF), 0) & (seq_len - 1)
    n_cp = cpg_ref.shape[0]
    trow = lax.broadcasted_iota(jnp.int32, (tm, n_cp), 0)
    ccol = lax.broadcasted_iota(jnp.int32, (tm, n_cp), 1)
    tpos = trow & (seq_len - 1)
    base = 2 * _idiv(trow, seq_len)
    e1 = jnp.where((tpos == 0) & (ccol == base + 1), 1.0, 0.0).astype(BF16)
    e2 = jnp.where(((tpos == 0) & (ccol == base)) | ((tpos == 1) & (ccol == base + 1)), 1.0, 0.0).astype(BF16)

    def conv(u, cp_ref, c_ref, u_ref):
        u_ref[...] = u
        p1 = p2 = None
        for piece in _split_bf16(cp_ref[...], 3):
            d1, d2 = _dot(e1, piece), _dot(e2, piece)
            p1 = d1 if p1 is None else p1 + d1
            p2 = d2 if p2 is None else p2 + d2
        u1 = jnp.where(pos >= 1, pltpu.roll(u, 1, 0), p1)
        u2 = jnp.where(pos >= 2, pltpu.roll(u, 2, 0), p2)
        c = c_ref[...]
        return c[0:1] * u2 + c[1:2] * u1 + c[2:3] * u

    zg = conv(_dot(h, wg_ref[...]), cpg_ref, cg_ref, ug_ref)
    zv = conv(_dot(h, wv_ref[...]), cpv_ref, cv_ref, uv_ref)
    acc_scr[...] += _dot(_gated(zg, zv), wd_ref[...])

    @pl.when(f == pl.num_programs(1) - 1)
    def _():
        o_ref[...] = x_ref[...] + gt_ref[...] * _rms(acc_scr[...], gp_ref[...])


def _ffn_s_call(x, conv_prev, g, sc, sh, gt, w_up, w_conv, w_down, gp, seq_len, tm):
    m = x.shape[0]
    assert tm % seq_len == 0 and seq_len & (seq_len - 1) == 0
    nf = D_FF // FFN_TF
    n_cp = (tm // seq_len) * (FFN_CONV - 1)
    full = pl.BlockSpec((tm, D_MODEL), lambda i, f: (i, 0))
    lo = pl.BlockSpec((tm, FFN_TF), lambda i, f: (i, f))
    return pl.pallas_call(
        functools.partial(_ffn_s_kernel, seq_len=seq_len),
        grid=(m // tm, nf),
        in_specs=[
            full,
            pl.BlockSpec((n_cp, FFN_TF), lambda i, f: (i, f)),
            pl.BlockSpec((n_cp, FFN_TF), lambda i, f: (i, f + nf)),
            pl.BlockSpec((1, D_MODEL), lambda i, f: (0, 0)),
            full, full, full,
            pl.BlockSpec((D_MODEL, FFN_TF), lambda i, f: (0, f)),
            pl.BlockSpec((D_MODEL, FFN_TF), lambda i, f: (0, f + nf)),
            pl.BlockSpec((FFN_CONV, FFN_TF), lambda i, f: (0, f)),
            pl.BlockSpec((FFN_CONV, FFN_TF), lambda i, f: (0, f + nf)),
            pl.BlockSpec((FFN_TF, D_MODEL), lambda i, f: (f, 0)),
            pl.BlockSpec((1, D_MODEL), lambda i, f: (0, 0)),
        ],
        out_specs=[full, lo, lo],
        out_shape=[
            jax.ShapeDtypeStruct((m, D_MODEL), F32),
            jax.ShapeDtypeStruct((m, D_FF), F32),
            jax.ShapeDtypeStruct((m, D_FF), F32),
        ],
        scratch_shapes=[pltpu.VMEM((tm, D_MODEL), BF16), pltpu.VMEM((tm, D_MODEL), F32)],
        compiler_params=_cparams("arbitrary", "arbitrary"),
        name="conv_ffn_sample",
    )(x, conv_prev, conv_prev, g, sc, sh, gt, w_up, w_up, w_conv, w_conv, w_down, gp)


def _seq_tile(t, cap):
    tm = min(t, cap)
    assert t % tm == 0
    return tm


@jax.jit
def _forward(x_prompt, x_sample, c_prompt, c_sample, cache_k, cache_v, state_wkv, state_shift,
             state_ffn_conv, w_ada, b_ada, g_pre_mix, g_post_mix, g_pre_ffn, g_post_ffn, w_in,
             lam_q1, lam_k1, lam_q2, lam_k2, g_subln, mu_shift, w0, w_w2, a0, w_a2, w_g2, k_k, k_a,
             r_k, ln_x_w, ln_x_b, w_out, w_up, w_conv_ffn, w_down):
    bp, t, _ = x_prompt.shape
    bs, ts, _ = x_sample.shape
    past = cache_k.shape[2]
    assert w_ada.shape[0] == 1, "single-layer problem"

    n_c = bp + bs
    n_c_pad = -(-n_c // SUBLANES) * SUBLANES
    c_all = jnp.concatenate([c_prompt, c_sample, jnp.zeros((n_c_pad - n_c, D_MODEL), F32)], axis=0)
    mod = _ada_call(c_all, w_ada[0], b_ada)
    sh_m, sc_m, gt_m, sh_f, sc_f, gt_f = [mod[:, i * D_MODEL:(i + 1) * D_MODEL] for i in range(6)]
    p_mod = lambda a: a[:bp].reshape(bp, 1, D_MODEL)
    s_mod = lambda a: jnp.repeat(a[bp:n_c], ts, axis=0)

    w_in_b = w_in[0].astype(BF16)
    w_qkv, w_rw = w_in_b[:, :3 * A_WIDTH], w_in_b[:, 3 * A_WIDTH:]
    w_out_b = w_out[0].astype(BF16)
    w_up_b = w_up[0].astype(BF16)
    w_down_b = w_down[0].astype(BF16)
    lams = (lam_q1, lam_k1, lam_q2, lam_k2)
    prm = {
        "mu_shift": mu_shift, "w0": w0, "a0": a0, "k_k": k_k, "k_a": k_a,
        "r_k": r_k.reshape(1, B_WIDTH), "ln_x_w": ln_x_w, "ln_x_b": ln_x_b,
        "w_w2": w_w2[0].astype(BF16), "w_a2": w_a2[0].astype(BF16), "w_g2": w_g2[0].astype(BF16),
    }

    tm = _seq_tile(t, 512)
    tps = t // tm
    xp = x_prompt.reshape(bp * t, D_MODEL)
    kf, vf, qkvb, rw = _in_proj_call(xp, g_pre_mix, p_mod(sc_m), p_mod(sh_m), w_qkv, w_rw, tm, tps)
    oa = _pattn_call(lams, qkvb.reshape(bp, t, 3 * A_WIDTH), g_subln, _seq_tile(t, 512))
    ob, s_last = _rwkv_call(rw.reshape(bp, t, RW_COLS), jnp.zeros((bp, 1, RW_COLS), F32),
                            jnp.zeros((bp, N_GROUPS, MXU_DIM, B_HEAD), state_wkv.dtype), prm,
                            _seq_tile(t, 256), 32)
    x1 = _mixout_call(oa.reshape(bp * t, A_WIDTH), ob.reshape(bp * t, B_WIDTH), xp, w_out_b,
                      g_post_mix, p_mod(gt_m), tm, tps)
    yp, clg, clv = _ffn_call(x1, jnp.zeros((bp, FFN_CONV - 1, 2 * D_FF), F32), g_pre_ffn, p_mod(sc_f),
                             p_mod(sh_f), p_mod(gt_f), w_up_b, w_conv_ffn[0], w_down_b, g_post_ffn,
                             tm, tps)
    rw3 = rw.reshape(bp, t, RW_COLS)
    out_prompt = (
        yp.reshape(bp, t, D_MODEL),
        kf.reshape(1, bp, t, A_HEADS, 2 * A_DK),
        vf.reshape(1, bp, t, A_HEADS, A_DV),
        _groups_to_state(s_last)[None],
        rw3[:, -1][None],
        jnp.concatenate([clg[tps - 1::tps, -(FFN_CONV - 1):], clv[tps - 1::tps, -(FFN_CONV - 1):]],
                        axis=-1)[None],
    )

    ms = bs * ts
    xs = x_sample.reshape(ms, D_MODEL)
    kf_s, vf_s, qkvb_s, rw_s = _in_proj_call(xs, g_pre_mix, s_mod(sc_m), s_mod(sh_m), w_qkv, w_rw, ms, 1)
    oa_s = _sattn_call(lams, qkvb_s.reshape(bs, ts, 3 * A_WIDTH),
                       cache_k[0].reshape(bs, past * A_HEADS, A_DV),
                       cache_v[0].reshape(bs, past * A_HEADS, A_DV), g_subln, _seq_tile(past, 512))
    ob_s, s_last_s = _rwkv_call(rw_s.reshape(bs, ts, RW_COLS), state_shift[0][:, None, :],
                                _state_to_groups(state_wkv[0]), prm, ts, ts)
    x1_s = _mixout_call(oa_s.reshape(ms, A_WIDTH), ob_s.reshape(ms, B_WIDTH), xs, w_out_b,
                        g_post_mix, s_mod(gt_m), ms, 1)
    cprev = state_ffn_conv[0].reshape(bs * (FFN_CONV - 1), 2 * D_FF)
    ys, ug, uv = _ffn_s_call(x1_s, cprev, g_pre_ffn, s_mod(sc_f), s_mod(sh_f), s_mod(gt_f),
                             w_up_b, w_conv_ffn[0], w_down_b, g_post_ffn, ts, _seq_tile(ms, 256))
    u_all = jnp.concatenate([ug, uv], axis=-1).reshape(bs, ts, 2 * D_FF)
    out_sample = (
        ys.reshape(bs, ts, D_MODEL),
        kf_s.reshape(1, bs, ts, A_HEADS, 2 * A_DK),
        vf_s.reshape(1, bs, ts, A_HEADS, A_DV),
        _groups_to_state(s_last_s)[None],
        rw_s.reshape(bs, ts, RW_COLS)[:, -1][None],
        u_all[:, -(FFN_CONV - 1):][None],
    )
    return (out_prompt[0], out_sample[0]) + out_prompt[1:] + out_sample[1:]


def kernel(x_prompt, x_sample, c_prompt, c_sample, cache_k, cache_v, state_wkv, state_shift, state_ffn_conv, w_ada, b_ada, g_pre_mix, g_post_mix, g_pre_ffn, g_post_ffn, w_in, lam_q1, lam_k1, lam_q2, lam_k2, g_subln, mu_shift, w0, w_w2, a0, w_a2, w_g2, k_k, k_a, r_k, ln_x_w, ln_x_b, w_out, w_up, w_conv_ffn, w_down):
    return _forward(x_prompt, x_sample, c_prompt, c_sample, cache_k, cache_v, state_wkv, state_shift,
                    state_ffn_conv, w_ada, b_ada, g_pre_mix, g_post_mix, g_pre_ffn, g_post_ffn, w_in,
                    lam_q1, lam_k1, lam_q2, lam_k2, g_subln, mu_shift, w0, w_w2, a0, w_a2, w_g2, k_k,
                    k_a, r_k, ln_x_w, ln_x_b, w_out, w_up, w_conv_ffn, w_down)
```

```python
import functools
import math

import jax
import jax.numpy as jnp
from jax import lax
from jax.experimental import pallas as pl
from jax.experimental.pallas import tpu as pltpu

F32 = jnp.float32
BF16 = jnp.bfloat16

D_MODEL = 2048
A_HEADS = 8
A_DK = 64
A_DV = 128
A_WIDTH = A_HEADS * A_DV
B_HEAD = 64
B_WIDTH = D_MODEL - A_WIDTH
B_HEADS = B_WIDTH // B_HEAD
W_RANK = 64
A_RANK = 64
G_RANK = 128
RW_COLS = 3 * B_WIDTH + W_RANK + A_RANK + G_RANK
D_FF = 5632
FFN_CONV = 3
ATTN_CHUNK = 64
EPS = 1e-6
GN_EPS = 64e-5
LAM_INIT = 0.8 - 0.6 * math.exp(-0.3 * 0)
Q_SCALE = (A_DK ** -0.5) * math.log2(math.e)

LANES = 128
SUBLANES = 8
BF16_ROWS = 16
MXU_DIM = 256
GROUP_HEADS = MXU_DIM // B_HEAD
N_GROUPS = B_WIDTH // MXU_DIM
VMEM_LIMIT_BYTES = 56 * 1024 * 1024
NEG_BIG = -1e30


def _cparams(*sem):
    return pltpu.CompilerParams(dimension_semantics=sem, vmem_limit_bytes=VMEM_LIMIT_BYTES)


def _dot(a, b):
    return jnp.dot(a, b, preferred_element_type=F32)


def _dot_nt(a, b):
    return lax.dot_general(a, b, (((1,), (1,)), ((), ())), preferred_element_type=F32)


def _dot_tn(a, b):
    return lax.dot_general(a, b, (((0,), (0,)), ((), ())), preferred_element_type=F32)


def _idiv(x, n):
    assert n > 0 and n & (n - 1) == 0
    return x >> (n.bit_length() - 1)


def _sigmoid(x):
    return 1.0 / (1.0 + jnp.exp(-x))


def _split_bf16(x, terms):
    out = []
    rem = x
    for _ in range(terms):
        piece = rem.astype(BF16)
        out.append(piece)
        rem = rem - piece.astype(F32)
    return out


def _norm_mod(x, g, sc, sh):
    ms = jnp.mean(x * x, axis=-1, keepdims=True)
    return (x * lax.rsqrt(ms + EPS) * g) * (1.0 + sc) + sh


def _rms(x, g):
    ms = jnp.mean(x * x, axis=-1, keepdims=True)
    return x * lax.rsqrt(ms + EPS) * g


def _mod(ref):
    return ref[0] if len(ref.shape) == 3 else ref[...]


def _lam(q1_ref, k1_ref, q2_ref, k2_ref):
    s1 = jnp.sum(q1_ref[...] * k1_ref[...], axis=-1, keepdims=True)
    s2 = jnp.sum(q2_ref[...] * k2_ref[...], axis=-1, keepdims=True)
    return jnp.exp(s1) - jnp.exp(s2) + LAM_INIT


def _ada_kernel(c_ref, w_ref, b_ref, o_ref):
    c = c_ref[...]
    s = (c * _sigmoid(c)).astype(BF16)
    o_ref[...] = _dot(s, w_ref[...].astype(BF16)) + b_ref[...]


def _ada_call(c_all, w_ada, b_ada):
    rows = c_all.shape[0]
    n = w_ada.shape[1]
    tn = 1024
    return pl.pallas_call(
        _ada_kernel,
        grid=(n // tn,),
        in_specs=[
            pl.BlockSpec((rows, D_MODEL), lambda j: (0, 0)),
            pl.BlockSpec((D_MODEL, tn), lambda j: (0, j)),
            pl.BlockSpec((1, tn), lambda j: (0, j)),
        ],
        out_specs=pl.BlockSpec((rows, tn), lambda j: (0, j)),
        out_shape=jax.ShapeDtypeStruct((rows, n), F32),
        compiler_params=_cparams("arbitrary"),
        name="ada_mod",
    )(c_all, w_ada, b_ada)


def _qkv_kernel(x_ref, g_ref, sc_ref, sh_ref, w_ref, kf_ref, vf_ref, qkvb_ref):
    h = _norm_mod(x_ref[...], g_ref[...], _mod(sc_ref), _mod(sh_ref)).astype(BF16)
    q = _dot(h, w_ref[:, 0:A_WIDTH])
    qkvb_ref[:, 0:A_WIDTH] = (q * Q_SCALE).astype(BF16)
    k = _dot(h, w_ref[:, A_WIDTH:2 * A_WIDTH])
    kf_ref[...] = k
    qkvb_ref[:, A_WIDTH:2 * A_WIDTH] = k.astype(BF16)
    v = _dot(h, w_ref[:, 2 * A_WIDTH:3 * A_WIDTH])
    vf_ref[...] = v
    qkvb_ref[:, 2 * A_WIDTH:3 * A_WIDTH] = v.astype(BF16)


def _rw_kernel(x_ref, g_ref, sc_ref, sh_ref, w_ref, o_ref):
    h = _norm_mod(x_ref[...], g_ref[...], _mod(sc_ref), _mod(sh_ref)).astype(BF16)
    o_ref[...] = _dot(h, w_ref[...])


def _mod_spec(mod, tm, tiles_per_seq, nidx):
    if mod.ndim == 3:
        if nidx == 2:
            return pl.BlockSpec((1, 1, D_MODEL), lambda i, j: (i // tiles_per_seq, 0, 0))
        return pl.BlockSpec((1, 1, D_MODEL), lambda i: (i // tiles_per_seq, 0, 0))
    if nidx == 2:
        return pl.BlockSpec((tm, D_MODEL), lambda i, j: (i, 0))
    return pl.BlockSpec((tm, D_MODEL), lambda i: (i, 0))


def _resident(shape):
    return pl.BlockSpec(shape, lambda i: (0,) * len(shape), pipeline_mode=pl.Buffered(1))


def _in_proj_call(x, g, sc, sh, w_qkv, w_rw, tm, seq_len):
    m = x.shape[0]
    common = [
        pl.BlockSpec((tm, D_MODEL), lambda i: (i, 0)),
        _resident((1, D_MODEL)),
        _mod_spec(sc, tm, seq_len // tm, 1),
        _mod_spec(sh, tm, seq_len // tm, 1),
    ]
    kf, vf, qkvb = pl.pallas_call(
        _qkv_kernel,
        grid=(m // tm,),
        in_specs=common + [_resident((D_MODEL, 3 * A_WIDTH))],
        out_specs=[
            pl.BlockSpec((tm, A_WIDTH), lambda i: (i, 0)),
            pl.BlockSpec((tm, A_WIDTH), lambda i: (i, 0)),
            pl.BlockSpec((tm, 3 * A_WIDTH), lambda i: (i, 0)),
        ],
        out_shape=[
            jax.ShapeDtypeStruct((m, A_WIDTH), F32),
            jax.ShapeDtypeStruct((m, A_WIDTH), F32),
            jax.ShapeDtypeStruct((m, 3 * A_WIDTH), BF16),
        ],
        compiler_params=_cparams("arbitrary"),
        name="in_proj_qkv",
    )(x, g, sc, sh, w_qkv)
    rw = pl.pallas_call(
        _rw_kernel,
        grid=(m // tm,),
        in_specs=common + [_resident((D_MODEL, RW_COLS))],
        out_specs=pl.BlockSpec((tm, RW_COLS), lambda i: (i, 0)),
        out_shape=jax.ShapeDtypeStruct((m, RW_COLS), F32),
        compiler_params=_cparams("arbitrary"),
        name="in_proj_rw",
    )(x, g, sc, sh, w_rw)
    return kf, vf, qkvb, rw


def _softmax_step(s, vb, m_ref, l_ref, acc_ref, idx):
    m_prev = m_ref[idx]
    m_new = jnp.maximum(m_prev, jnp.max(s, axis=-1, keepdims=True))
    alpha = jnp.exp2(m_prev - m_new)
    p = jnp.exp2(s - m_new)
    l_ref[idx] = alpha * l_ref[idx] + jnp.sum(p, axis=-1, keepdims=True)
    acc_ref[idx] = alpha * acc_ref[idx] + _dot(p.astype(BF16), vb)
    m_ref[idx] = m_new


def _subln(o, g):
    return _rms(o, g) * (1.0 - LAM_INIT)


PATTN_HEADS = 2
PATTN_QG = 512


def _pattn_kernel(q1_ref, k1_ref, q2_ref, k2_ref, q_ref, k_ref, v_ref, g_ref, o_ref,
                  m_scr, l_scr, acc_scr, *, tq):
    i = pl.program_id(2)
    lane = lax.broadcasted_iota(jnp.int32, (tq, A_DV), 1)
    m_scr[...] = jnp.full(m_scr.shape, NEG_BIG, F32)
    l_scr[...] = jnp.zeros(l_scr.shape, F32)
    acc_scr[...] = jnp.zeros(acc_scr.shape, F32)

    qg = min(PATTN_QG, 2 * tq)
    n_qg = 2 * tq // qg
    q_gs = []
    for hh in range(PATTN_HEADS):
        q = q_ref[0, :, hh * A_DV:(hh + 1) * A_DV]
        zero = jnp.zeros_like(q)
        q_st = jnp.concatenate([jnp.where(lane < A_DK, q, zero), jnp.where(lane >= A_DK, q, zero)], axis=0)
        q_gs.append([q_st[g * qg:(g + 1) * qg] for g in range(n_qg)])
    ones_k = jnp.ones((SUBLANES, tq), BF16)
    chains = [(hh, g) for g in range(n_qg) for hh in range(PATTN_HEADS)]

    def block(j, masks):
        r0 = pl.multiple_of(j * tq, tq)
        kbs = [k_ref[0, pl.ds(r0, tq), hh * A_DV:(hh + 1) * A_DV] for hh in range(PATTN_HEADS)]
        vbs = [v_ref[0, pl.ds(r0, tq), hh * A_DV:(hh + 1) * A_DV] for hh in range(PATTN_HEADS)]
        s_ts = {(hh, g): _dot_nt(kbs[hh], q_gs[hh][g]) for hh, g in chains}
        for hh, g in chains:
            cs = slice(g * qg, (g + 1) * qg)
            s_t = s_ts[(hh, g)]
            if masks is not None:
                s_t = jnp.where(masks[g], s_t, NEG_BIG)
            m_prev = m_scr[hh, :, cs]
            m_new = jnp.maximum(m_prev, jnp.max(s_t, axis=0, keepdims=True))
            alpha = jnp.exp2(m_prev - m_new)
            p_b = jnp.exp2(s_t - m_new).astype(BF16)
            l_scr[hh, :, cs] = alpha * l_scr[hh, :, cs] + _dot(ones_k, p_b)[0:1]
            acc_scr[hh, :, cs] = alpha * acc_scr[hh, :, cs] + _dot_tn(vbs[hh], p_b)
            m_scr[hh, :, cs] = m_new

    def body(j, carry):
        block(j, None)
        return carry

    lax.fori_loop(0, i, body, 0)
    rk = lax.broadcasted_iota(jnp.int32, (tq, qg), 0)
    cq = lax.broadcasted_iota(jnp.int32, (tq, qg), 1)
    block(i, [_idiv(rk, ATTN_CHUNK) <= _idiv((cq + g * qg) & (tq - 1), ATTN_CHUNK)
              for g in range(n_qg)])

    lam = _lam(q1_ref, k1_ref, q2_ref, k2_ref)
    for hh in range(PATTN_HEADS):
        o_t = (acc_scr[hh, :, 0:tq] / l_scr[hh, :, 0:tq]
               - lam * (acc_scr[hh, :, tq:2 * tq] / l_scr[hh, :, tq:2 * tq]))
        o_ref[0, :, hh * A_DV:(hh + 1) * A_DV] = _subln(o_t.T, g_ref[...]).astype(BF16)


def _lam_specs(nidx):
    if nidx == 3:
        return [pl.BlockSpec((1, A_DK), lambda b, h, i: (0, 0))] * 4
    return [pl.BlockSpec((1, A_DK), lambda b, j: (0, 0))] * 4


def _pattn_call(lams, qkvb, g_subln, tq):
    bsz, t, _ = qkvb.shape
    hw = PATTN_HEADS * A_DV
    ng = A_HEADS // PATTN_HEADS
    return pl.pallas_call(
        functools.partial(_pattn_kernel, tq=tq),
        grid=(bsz, ng, t // tq),
        in_specs=_lam_specs(3) + [
            pl.BlockSpec((1, tq, hw), lambda b, h, i: (b, i, h)),
            pl.BlockSpec((1, t, hw), lambda b, h, i: (b, 0, ng + h)),
            pl.BlockSpec((1, t, hw), lambda b, h, i: (b, 0, 2 * ng + h)),
            pl.BlockSpec((1, A_DV), lambda b, h, i: (0, 0)),
        ],
        out_specs=pl.BlockSpec((1, tq, hw), lambda b, h, i: (b, i, h)),
        out_shape=jax.ShapeDtypeStruct((bsz, t, A_WIDTH), BF16),
        scratch_shapes=[
            pltpu.VMEM((PATTN_HEADS, 1, 2 * tq), F32),
            pltpu.VMEM((PATTN_HEADS, 1, 2 * tq), F32),
            pltpu.VMEM((PATTN_HEADS, A_DV, 2 * tq), F32),
        ],
        compiler_params=_cparams("arbitrary", "arbitrary", "arbitrary"),
        name="prompt_attention",
    )(*lams, qkvb, qkvb, qkvb, g_subln)


N_VQ = 2 * A_HEADS


def _sattn_kernel(q1_ref, k1_ref, q2_ref, k2_ref, q_ref, kn_ref, vn_ref, ck_ref, cv_ref, g_ref, o_ref,
                  qb_scr, m_scr, l_scr, acc_scr, *, tnew):
    j = pl.program_id(1)
    nvq = N_VQ * tnew

    @pl.when(j == 0)
    def _():
        q = q_ref[0]
        qt = jnp.concatenate([q] * N_VQ, axis=0)
        row = lax.broadcasted_iota(jnp.int32, (nvq, A_WIDTH), 0)
        col = lax.broadcasted_iota(jnp.int32, (nvq, A_WIDTH), 1)
        qb_scr[...] = jnp.where(_idiv(row, tnew) == _idiv(col, A_DK), qt, jnp.zeros_like(qt))
        m_scr[...] = jnp.full(m_scr.shape, NEG_BIG, F32)
        l_scr[...] = jnp.zeros(l_scr.shape, F32)
        acc_scr[...] = jnp.zeros(acc_scr.shape, F32)

    def step(kb, vb):
        _softmax_step(_dot_nt(qb_scr[...], kb), vb, m_scr, l_scr, acc_scr, 0)

    def cached_rows(ref, p0, n):
        heads = [ref[0, pl.ds(p0 * A_HEADS + h, n, stride=A_HEADS), :] for h in range(A_HEADS)]
        return jnp.concatenate(heads, axis=-1).astype(BF16)

    tk = ck_ref.shape[1] // A_HEADS
    n_sub = max(1, tk // MXU_DIM)
    sub = tk // n_sub
    kbs = [cached_rows(ck_ref, c * sub, sub) for c in range(n_sub)]
    s_all = [_dot_nt(qb_scr[...], kb) for kb in kbs]
    for c in range(n_sub):
        _softmax_step(s_all[c], cached_rows(cv_ref, c * sub, sub), m_scr, l_scr, acc_scr, 0)

    @pl.when(j == pl.num_programs(1) - 1)
    def _():
        step(kn_ref[0], vn_ref[0])
        lam = _lam(q1_ref, k1_ref, q2_ref, k2_ref)
        g = g_ref[...]
        for h in range(A_HEADS):
            cs = slice(h * A_DV, (h + 1) * A_DV)
            r0 = slice((2 * h) * tnew, (2 * h + 1) * tnew)
            r1 = slice((2 * h + 1) * tnew, (2 * h + 2) * tnew)
            o0 = acc_scr[0, r0, cs] / l_scr[0, r0, :]
            o1 = acc_scr[0, r1, cs] / l_scr[0, r1, :]
            o_ref[0, :, cs] = _subln(o0 - lam * o1, g).astype(BF16)


def _sattn_call(lams, qkvb, cache_k, cache_v, g_subln, tk):
    bsz, tnew, _ = qkvb.shape
    past = cache_k.shape[1] // A_HEADS
    nvq = N_VQ * tnew
    return pl.pallas_call(
        functools.partial(_sattn_kernel, tnew=tnew),
        grid=(bsz, past // tk),
        in_specs=_lam_specs(2) + [
            pl.BlockSpec((1, tnew, A_WIDTH), lambda b, j: (b, 0, 0)),
            pl.BlockSpec((1, tnew, A_WIDTH), lambda b, j: (b, 0, 1)),
            pl.BlockSpec((1, tnew, A_WIDTH), lambda b, j: (b, 0, 2)),
            pl.BlockSpec((1, tk * A_HEADS, A_DV), lambda b, j: (b, j, 0)),
            pl.BlockSpec((1, tk * A_HEADS, A_DV), lambda b, j: (b, j, 0)),
            pl.BlockSpec((1, A_DV), lambda b, j: (0, 0)),
        ],
        out_specs=pl.BlockSpec((1, tnew, A_WIDTH), lambda b, j: (b, 0, 0)),
        out_shape=jax.ShapeDtypeStruct((bsz, tnew, A_WIDTH), BF16),
        scratch_shapes=[
            pltpu.VMEM((nvq, A_WIDTH), BF16),
            pltpu.VMEM((1, nvq, 1), F32),
            pltpu.VMEM((1, nvq, 1), F32),
            pltpu.VMEM((1, nvq, A_WIDTH), F32),
        ],
        compiler_params=_cparams("arbitrary", "arbitrary"),
        name="sample_attention",
    )(*lams, qkvb, qkvb, qkvb, cache_k, cache_v, g_subln)


def _head_sum(x, ones_bd):
    return _dot(x.astype(BF16), ones_bd)


def _rwkv_kernel(p_ref, ph_ref, st_ref, s0_ref, mu_ref, w0_ref, a0_ref, kk_ref, ka_ref, rk_ref,
                 lnw_ref, lnb_ref, ww2_ref, wa2_ref, wg2_ref, ob_ref, so_ref,
                 r_scr, k_scr, v_scr, na_scr, nb_scr, ld_scr, cum_scr, g_scr, bon_scr, y_scr, s_scr,
                 *, tb, chunk):
    i = pl.program_id(1)
    c_len = chunk
    n_chunks = tb // c_len

    row_g = lax.broadcasted_iota(jnp.int32, (MXU_DIM, MXU_DIM), 0)
    col_g = lax.broadcasted_iota(jnp.int32, (MXU_DIM, MXU_DIM), 1)
    bd_mask = _idiv(row_g, B_HEAD) == _idiv(col_g, B_HEAD)
    ones_bd = jnp.where(bd_mask, 1.0, 0.0).astype(BF16)

    @pl.when(i == 0)
    def _():
        rep = jnp.where(
            lax.broadcasted_iota(jnp.int32, (B_HEAD, MXU_DIM), 0)
            == (lax.broadcasted_iota(jnp.int32, (B_HEAD, MXU_DIM), 1) & (B_HEAD - 1)), 1.0, 0.0).astype(BF16)
        for q in range(N_GROUPS):
            wide = None
            for piece in _split_bf16(s0_ref[0, q], 3):
                d = _dot(piece, rep)
                wide = d if wide is None else wide + d
            s_scr[q] = jnp.where(bd_mask, wide, 0.0)

    p = p_ref[0]
    prev_row = jnp.where(i == 0, st_ref[0], ph_ref[0, SUBLANES - 1:SUBLANES, :])
    rowi = lax.broadcasted_iota(jnp.int32, p.shape, 0)
    p_prev = jnp.where(rowi == 0, prev_row, pltpu.roll(p, 1, 0))
    xs = p + (p_prev - p) * mu_ref[...]
    r = xs[:, 0:B_WIDTH]
    k = xs[:, B_WIDTH:2 * B_WIDTH]
    v = xs[:, 2 * B_WIDTH:3 * B_WIDTH]
    wa = xs[:, 3 * B_WIDTH:3 * B_WIDTH + W_RANK + A_RANK]
    gd = xs[:, 3 * B_WIDTH + W_RANK + A_RANK:]

    lane_wa = lax.broadcasted_iota(jnp.int32, wa.shape, 1)
    w_cat = jnp.concatenate([ww2_ref[...], wa2_ref[...]], axis=0)
    zero_wa = jnp.zeros_like(wa)
    w_lin = _dot(jnp.where(lane_wa < W_RANK, jnp.tanh(wa), zero_wa).astype(BF16), w_cat)
    a_lin = _dot(jnp.where(lane_wa >= W_RANK, wa, zero_wa).astype(BF16), w_cat)
    zw = -(w0_ref[...] + w_lin)
    softplus = jnp.maximum(zw, 0.0) + jnp.log(1.0 + jnp.exp(-jnp.abs(zw)))
    logd = -jnp.exp(-softplus - 0.5)
    a = _sigmoid(a0_ref[...] + a_lin)
    g_scr[...] = _dot(_sigmoid(gd).astype(BF16), wg2_ref[...])

    k2 = k * (1.0 + (a - 1.0) * ka_ref[...])
    kk = k * kk_ref[...]
    rkk = r * k2 * rk_ref[...]
    for q in range(N_GROUPS):
        ls = slice(q * MXU_DIM, (q + 1) * MXU_DIM)
        kq = kk[:, ls]
        nrm = jnp.maximum(jnp.sqrt(_head_sum(kq * kq, ones_bd)), 1e-12)
        kn = kq / nrm
        na_scr[:, ls] = -kn
        nb_scr[:, ls] = kn * a[:, ls]
        bon_scr[:, ls] = _head_sum(rkk[:, ls], ones_bd) * v[:, ls]
    r_scr[...] = r
    k_scr[...] = k2
    v_scr[...] = v
    ld_scr[...] = logd

    row_t = lax.broadcasted_iota(jnp.int32, (tb, tb), 0)
    col_t = lax.broadcasted_iota(jnp.int32, (tb, tb), 1)
    tri = jnp.where((_idiv(row_t, c_len) == _idiv(col_t, c_len)) & (col_t <= row_t), 1.0, 0.0).astype(BF16)
    cum = None
    for piece in _split_bf16(logd, 2):
        d = _dot(tri, piece)
        cum = d if cum is None else cum + d
    cum_scr[...] = cum

    gc = GROUP_HEADS * c_len
    lane_q = lax.broadcasted_iota(jnp.int32, (1, MXU_DIM), 1)
    head_masks = [_idiv(lane_q, B_HEAD) == j for j in range(GROUP_HEADS)]
    row_s = lax.broadcasted_iota(jnp.int32, (gc, gc), 0)
    col_s = lax.broadcasted_iota(jnp.int32, (gc, gc), 1)
    same_head = _idiv(row_s, c_len) == _idiv(col_s, c_len)
    strict = same_head & (col_s < row_s)
    incl = same_head & (col_s <= row_s)
    eye = jnp.where(col_s == row_s, 1.0, 0.0).astype(F32)
    n_double = int(math.log2(c_len)) - 1
    groups = range(N_GROUPS)

    def stack(x):
        z = jnp.zeros_like(x)
        return jnp.concatenate([jnp.where(head_masks[j], x, z) for j in range(GROUP_HEADS)], axis=0)

    def unstack(x):
        out = x[0:c_len]
        for j in range(1, GROUP_HEADS):
            out = out + x[j * c_len:(j + 1) * c_len]
        return out

    def chunk_body(c, carry):
        r0 = pl.multiple_of(c * c_len, c_len)
        rows = pl.ds(r0, c_len)
        a_s, r_s, b_s, k_s, be_s, ke_s, v_s, dec = [], [], [], [], [], [], [], []
        for q in groups:
            ls = slice(q * MXU_DIM, (q + 1) * MXU_DIM)
            cum_q = cum_scr[rows, ls]
            k_q = k_scr[rows, ls]
            nb_q = nb_scr[rows, ls]
            cum_last = cum_q[c_len - 1:c_len, :]
            e_out = jnp.exp(-cum_q)
            e_end = jnp.exp(cum_last - cum_q)
            a_s.append(stack((na_scr[rows, ls] * jnp.exp(cum_q - ld_scr[rows, ls])).astype(BF16)))
            r_s.append(stack((r_scr[rows, ls] * jnp.exp(cum_q)).astype(BF16)))
            b_s.append(stack((nb_q * e_out).astype(BF16)))
            k_s.append(stack((k_q * e_out).astype(BF16)))
            be_s.append(stack((nb_q * e_end).astype(BF16)))
            ke_s.append(stack((k_q * e_end).astype(BF16)))
            v_s.append(stack(v_scr[rows, ls].astype(BF16)))
            dec.append(jnp.exp(cum_last))

        if gc % LANES == 0:
            g_all = [_dot_nt(jnp.concatenate([a_s[q], r_s[q]], axis=0),
                             jnp.concatenate([b_s[q], k_s[q]], axis=0)) for q in groups]
            g_ab = [g[0:gc, 0:gc] for g in g_all]
            g_ak = [g[0:gc, gc:2 * gc] for g in g_all]
            g_rb = [g[gc:2 * gc, 0:gc] for g in g_all]
            g_rk = [g[gc:2 * gc, gc:2 * gc] for g in g_all]
        else:
            g_ab = [_dot_nt(a_s[q], b_s[q]) for q in groups]
            g_ak = [_dot_nt(a_s[q], k_s[q]) for q in groups]
            g_rb = [_dot_nt(r_s[q], b_s[q]) for q in groups]
            g_rk = [_dot_nt(r_s[q], k_s[q]) for q in groups]
        aab = [jnp.where(strict, g, 0.0) for g in g_ab]
        akrk = [jnp.concatenate([jnp.where(strict, g_ak[q], 0.0), jnp.where(incl, g_rk[q], 0.0)],
                                axis=0).astype(BF16) for q in groups]
        arb = [jnp.where(incl, g, 0.0).astype(BF16) for g in g_rb]

        t_m = [eye + x for x in aab]
        pw = [x.astype(BF16) for x in aab]
        pw = [_dot(x, x) for x in pw]
        for m in range(n_double):
            pb = [x.astype(BF16) for x in pw]
            if m + 1 < n_double:
                pw = [_dot(x, x) for x in pb]
            t_m = [t_m[q] + _dot(t_m[q].astype(BF16), pb[q]) for q in groups]
        t_b = [x.astype(BF16) for x in t_m]

        akv = [_dot(akrk[q], v_s[q]) for q in groups]
        tw = [_dot(t_b[q], jnp.concatenate([a_s[q], akv[q][0:gc].astype(BF16)], axis=1)) for q in groups]
        d_e = [_dot_tn(v_s[q], ke_s[q]) for q in groups]

        s_prev = [s_scr[q] for q in groups]
        wr = [_dot_nt(jnp.concatenate([tw[q][:, 0:MXU_DIM].astype(BF16), r_s[q]], axis=0),
                      s_prev[q].astype(BF16)) for q in groups]
        u_b = [(wr[q][0:gc] + tw[q][:, MXU_DIM:2 * MXU_DIM]).astype(BF16) for q in groups]
        y_st = [wr[q][gc:2 * gc] + _dot(arb[q], u_b[q]) + akv[q][gc:2 * gc] for q in groups]
        s_new = [s_prev[q] * dec[q] + _dot_tn(u_b[q], be_s[q]) + d_e[q] for q in groups]
        for q in groups:
            s_scr[q] = s_new[q]
            y_scr[rows, q * MXU_DIM:(q + 1) * MXU_DIM] = unstack(y_st[q])
        return carry

    lax.fori_loop(0, n_chunks, chunk_body, 0)

    @pl.when(i == pl.num_programs(1) - 1)
    def _():
        rep_t = jnp.where(
            (lax.broadcasted_iota(jnp.int32, (MXU_DIM, B_HEAD), 0) & (B_HEAD - 1))
            == lax.broadcasted_iota(jnp.int32, (MXU_DIM, B_HEAD), 1), 1.0, 0.0).astype(BF16)
        for q in groups:
            narrow = None
            for piece in _split_bf16(s_scr[q], 3):
                d = _dot(piece, rep_t)
                narrow = d if narrow is None else narrow + d
            so_ref[0, q] = narrow

    inv_n = 1.0 / B_HEAD
    for q in range(N_GROUPS):
        ls = slice(q * MXU_DIM, (q + 1) * MXU_DIM)
        y = y_scr[:, ls]
        mu = _head_sum(y, ones_bd) * inv_n
        d = y - mu
        var = _head_sum(d * d, ones_bd) * inv_n
        yn = d * lax.rsqrt(var + GN_EPS) * lnw_ref[:, ls] + lnb_ref[:, ls]
        ob_ref[0, :, ls] = ((yn + bon_scr[:, ls]) * g_scr[:, ls]).astype(BF16)


def _rwkv_call(rw, shift0, s0, prm, tb, chunk):
    bsz, t, _ = rw.shape
    nt = t // tb
    hb = tb // SUBLANES
    vec = lambda n: pl.BlockSpec((1, n), lambda b, i: (0, 0))
    return pl.pallas_call(
        functools.partial(_rwkv_kernel, tb=tb, chunk=chunk),
        grid=(bsz, nt),
        in_specs=[
            pl.BlockSpec((1, tb, RW_COLS), lambda b, i: (b, i, 0)),
            pl.BlockSpec((1, SUBLANES, RW_COLS), lambda b, i: (b, jnp.maximum(i * hb - 1, 0), 0)),
            pl.BlockSpec((1, 1, RW_COLS), lambda b, i: (b, 0, 0)),
            pl.BlockSpec((1, N_GROUPS, MXU_DIM, B_HEAD), lambda b, i: (b, 0, 0, 0)),
            vec(RW_COLS), vec(B_WIDTH), vec(B_WIDTH), vec(B_WIDTH), vec(B_WIDTH), vec(B_WIDTH),
            vec(B_WIDTH), vec(B_WIDTH),
            pl.BlockSpec((W_RANK, B_WIDTH), lambda b, i: (0, 0)),
            pl.BlockSpec((A_RANK, B_WIDTH), lambda b, i: (0, 0)),
            pl.BlockSpec((G_RANK, B_WIDTH), lambda b, i: (0, 0)),
        ],
        out_specs=[
            pl.BlockSpec((1, tb, B_WIDTH), lambda b, i: (b, i, 0)),
            pl.BlockSpec((1, N_GROUPS, MXU_DIM, B_HEAD), lambda b, i: (b, 0, 0, 0)),
        ],
        out_shape=[
            jax.ShapeDtypeStruct((bsz, t, B_WIDTH), BF16),
            jax.ShapeDtypeStruct((bsz, N_GROUPS, MXU_DIM, B_HEAD), F32),
        ],
        scratch_shapes=[pltpu.VMEM((tb, B_WIDTH), F32)] * 10
        + [pltpu.VMEM((N_GROUPS, MXU_DIM, MXU_DIM), F32)],
        compiler_params=_cparams("arbitrary", "arbitrary"),
        name="rwkv7_mix",
    )(rw, rw, shift0, s0, prm["mu_shift"], prm["w0"], prm["a0"], prm["k_k"], prm["k_a"], prm["r_k"],
      prm["ln_x_w"], prm["ln_x_b"], prm["w_w2"], prm["w_a2"], prm["w_g2"])


def _state_to_groups(s):
    return s.reshape(s.shape[0], N_GROUPS, MXU_DIM, B_HEAD)


def _groups_to_state(sg):
    return sg.reshape(sg.shape[0], B_HEADS, B_HEAD, B_HEAD)


def _mixout_kernel(oa_ref, ob_ref, x_ref, w_ref, g_ref, gt_ref, o_ref):
    mix = _dot(oa_ref[...], w_ref[0:A_WIDTH, :]) + _dot(ob_ref[...], w_ref[A_WIDTH:D_MODEL, :])
    o_ref[...] = x_ref[...] + _mod(gt_ref) * _rms(mix, g_ref[...])


def _mixout_call(oa, ob, x, w_out, g, gt, tm, tiles_per_seq):
    m = x.shape[0]
    return pl.pallas_call(
        _mixout_kernel,
        grid=(m // tm,),
        in_specs=[
            pl.BlockSpec((tm, A_WIDTH), lambda i: (i, 0)),
            pl.BlockSpec((tm, B_WIDTH), lambda i: (i, 0)),
            pl.BlockSpec((tm, D_MODEL), lambda i: (i, 0)),
            pl.BlockSpec((D_MODEL, D_MODEL), lambda i: (0, 0)),
            pl.BlockSpec((1, D_MODEL), lambda i: (0, 0)),
            _mod_spec(gt, tm, tiles_per_seq, 1),
        ],
        out_specs=pl.BlockSpec((tm, D_MODEL), lambda i: (i, 0)),
        out_shape=jax.ShapeDtypeStruct((m, D_MODEL), F32),
        compiler_params=_cparams("arbitrary"),
        name="mix_out",
    )(oa, ob, x, w_out, g, gt)


FFN_HALO = BF16_ROWS
FFN_TF = 512


def _gated(zg, zv):
    return (zg * _sigmoid(zg) * zv).astype(BF16)


def _ffn_kernel(x_ref, xh_ref, cpg_ref, cpv_ref, g_ref, sc_ref, sh_ref, gt_ref, wg_ref, wv_ref,
                cg_ref, cv_ref, wd_ref, gp_ref, o_ref, clg_ref, clv_ref, h_scr, acc_scr,
                *, tm, tiles_per_seq):
    mi = pl.program_id(0)
    f = pl.program_id(1)
    first = (mi % tiles_per_seq) == 0

    @pl.when(f == 0)
    def _():
        g, sc, sh = g_ref[...], sc_ref[0], sh_ref[0]
        h_scr[FFN_HALO:, :] = _norm_mod(x_ref[...], g, sc, sh).astype(BF16)
        h_scr[:FFN_HALO, :] = _norm_mod(xh_ref[...], g, sc, sh).astype(BF16)
        acc_scr[...] = jnp.zeros(acc_scr.shape, F32)

    h = h_scr[...]
    row = lax.broadcasted_iota(jnp.int32, (FFN_HALO, FFN_TF), 0)

    def conv(u, cp_ref, c_ref, cl_ref):
        cp = cp_ref[0]
        head = u[:FFN_HALO]
        head_first = jnp.where(row == FFN_HALO - 2, cp[0:1],
                               jnp.where(row == FFN_HALO - 1, cp[1:2], head))
        ue = jnp.concatenate([jnp.where(first, head_first, head), u[FFN_HALO:]], axis=0)
        cl_ref[0] = ue[tm + FFN_HALO - SUBLANES:]
        c = c_ref[...]
        return (c[0:1] * pltpu.roll(ue, 2, 0)[FFN_HALO:] + c[1:2] * pltpu.roll(ue, 1, 0)[FFN_HALO:]
                + c[2:3] * ue[FFN_HALO:])

    zg = conv(_dot(h, wg_ref[...]), cpg_ref, cg_ref, clg_ref)
    zv = conv(_dot(h, wv_ref[...]), cpv_ref, cv_ref, clv_ref)
    acc_scr[...] += _dot(_gated(zg, zv), wd_ref[...])

    @pl.when(f == pl.num_programs(1) - 1)
    def _():
        o_ref[...] = x_ref[...] + gt_ref[0] * _rms(acc_scr[...], gp_ref[...])


def _ffn_call(x, conv_prev, g, sc, sh, gt, w_up, w_conv, w_down, gp, tm, tiles_per_seq):
    m = x.shape[0]
    nseq = conv_prev.shape[0]
    nf = D_FF // FFN_TF
    hb = tm // FFN_HALO
    seq = lambda i: i // tiles_per_seq
    modspec = pl.BlockSpec((1, 1, D_MODEL), lambda i, f: (seq(i), 0, 0))
    return pl.pallas_call(
        functools.partial(_ffn_kernel, tm=tm, tiles_per_seq=tiles_per_seq),
        grid=(m // tm, nf),
        in_specs=[
            pl.BlockSpec((tm, D_MODEL), lambda i, f: (i, 0)),
            pl.BlockSpec((FFN_HALO, D_MODEL), lambda i, f: (jnp.maximum(i * hb - 1, 0), 0)),
            pl.BlockSpec((1, FFN_CONV - 1, FFN_TF), lambda i, f: (seq(i), 0, f)),
            pl.BlockSpec((1, FFN_CONV - 1, FFN_TF), lambda i, f: (seq(i), 0, f + nf)),
            pl.BlockSpec((1, D_MODEL), lambda i, f: (0, 0)),
            modspec, modspec, modspec,
            pl.BlockSpec((D_MODEL, FFN_TF), lambda i, f: (0, f)),
            pl.BlockSpec((D_MODEL, FFN_TF), lambda i, f: (0, f + nf)),
            pl.BlockSpec((FFN_CONV, FFN_TF), lambda i, f: (0, f)),
            pl.BlockSpec((FFN_CONV, FFN_TF), lambda i, f: (0, f + nf)),
            pl.BlockSpec((FFN_TF, D_MODEL), lambda i, f: (f, 0)),
            pl.BlockSpec((1, D_MODEL), lambda i, f: (0, 0)),
        ],
        out_specs=[
            pl.BlockSpec((tm, D_MODEL), lambda i, f: (i, 0)),
            pl.BlockSpec((1, SUBLANES, FFN_TF), lambda i, f: (i, 0, f)),
            pl.BlockSpec((1, SUBLANES, FFN_TF), lambda i, f: (i, 0, f)),
        ],
        out_shape=[
            jax.ShapeDtypeStruct((m, D_MODEL), F32),
            jax.ShapeDtypeStruct((m // tm, SUBLANES, D_FF), F32),
            jax.ShapeDtypeStruct((m // tm, SUBLANES, D_FF), F32),
        ],
        scratch_shapes=[
            pltpu.VMEM((tm + FFN_HALO, D_MODEL), BF16),
            pltpu.VMEM((tm, D_MODEL), F32),
        ],
        compiler_params=_cparams("arbitrary", "arbitrary"),
        name="conv_ffn_prompt",
    )(x, x, conv_prev, conv_prev, g, sc, sh, gt, w_up, w_up, w_conv, w_conv, w_down, gp)


def _ffn_s_kernel(x_ref, cpg_ref, cpv_ref, g_ref, sc_ref, sh_ref, gt_ref, wg_ref, wv_ref,
                  cg_ref, cv_ref, wd_ref, gp_ref, o_ref, ug_ref, uv_ref, h_scr, acc_scr, *, seq_len):
    f = pl.program_id(1)

    @pl.when(f == 0)
    def _():
        h_scr[...] = _norm_mod(x_ref[...], g_ref[...], sc_ref[...], sh_ref[...]).astype(BF16)
        acc_scr[...] = jnp.zeros(acc_scr.shape, F32)

    h = h_scr[...]
    tm = h.shape[0]
    pos = lax.broadcasted_iota(jnp.int32, (tm, FFN_TF), 0) & (seq_len - 1)
    n_cp = cpg_ref.shape[0]
    trow = lax.broadcasted_iota(jnp.int32, (tm, n_cp), 0)
    ccol = lax.broadcasted_iota(jnp.int32, (tm, n_cp), 1)
    tpos = trow & (seq_len - 1)
    base = 2 * _idiv(trow, seq_len)
    e1 = jnp.where((tpos == 0) & (ccol == base + 1), 1.0, 0.0).astype(BF16)
    e2 = jnp.where(((tpos == 0) & (ccol == base)) | ((tpos == 1) & (ccol == base + 1)), 1.0, 0.0).astype(BF16)

    def conv(u, cp_ref, c_ref, u_ref):
        u_ref[...] = u
        p1 = p2 = None
        for piece in _split_bf16(cp_ref[...], 3):
            d1, d2 = _dot(e1, piece), _dot(e2, piece)
            p1 = d1 if p1 is None else p1 + d1
            p2 = d2 if p2 is None else p2 + d2
        u1 = jnp.where(pos >= 1, pltpu.roll(u, 1, 0), p1)
        u2 = jnp.where(pos >= 2, pltpu.roll(u, 2, 0), p2)
        c = c_ref[...]
        return c[0:1] * u2 + c[1:2] * u1 + c[2:3] * u

    zg = conv(_dot(h, wg_ref[...]), cpg_ref, cg_ref, ug_ref)
    zv = conv(_dot(h, wv_ref[...]), cpv_ref, cv_ref, uv_ref)
    acc_scr[...] += _dot(_gated(zg, zv), wd_ref[...])

    @pl.when(f == pl.num_programs(1) - 1)
    def _():
        o_ref[...] = x_ref[...] + gt_ref[...] * _rms(acc_scr[...], gp_ref[...])


def _ffn_s_call(x, conv_prev, g, sc, sh, gt, w_up, w_conv, w_down, gp, seq_len, tm):
    m = x.shape[0]
    assert tm % seq_len == 0 and seq_len & (seq_len - 1) == 0
    nf = D_FF // FFN_TF
    n_cp = (tm // seq_len) * (FFN_CONV - 1)
    full = pl.BlockSpec((tm, D_MODEL), lambda i, f: (i, 0))
    lo = pl.BlockSpec((tm, FFN_TF), lambda i, f: (i, f))
    return pl.pallas_call(
        functools.partial(_ffn_s_kernel, seq_len=seq_len),
        grid=(m // tm, nf),
        in_specs=[
            full,
            pl.BlockSpec((n_cp, FFN_TF), lambda i, f: (i, f)),
            pl.BlockSpec((n_cp, FFN_TF), lambda i, f: (i, f + nf)),
            pl.BlockSpec((1, D_MODEL), lambda i, f: (0, 0)),
            full, full, full,
            pl.BlockSpec((D_MODEL, FFN_TF), lambda i, f: (0, f)),
            pl.BlockSpec((D_MODEL, FFN_TF), lambda i, f: (0, f + nf)),
            pl.BlockSpec((FFN_CONV, FFN_TF), lambda i, f: (0, f)),
            pl.BlockSpec((FFN_CONV, FFN_TF), lambda i, f: (0, f + nf)),
            pl.BlockSpec((FFN_TF, D_MODEL), lambda i, f: (f, 0)),
            pl.BlockSpec((1, D_MODEL), lambda i, f: (0, 0)),
        ],
        out_specs=[full, lo, lo],
        out_shape=[
            jax.ShapeDtypeStruct((m, D_MODEL), F32),
            jax.ShapeDtypeStruct((m, D_FF), F32),
            jax.ShapeDtypeStruct((m, D_FF), F32),
        ],
        scratch_shapes=[pltpu.VMEM((tm, D_MODEL), BF16), pltpu.VMEM((tm, D_MODEL), F32)],
        compiler_params=_cparams("arbitrary", "arbitrary"),
        name="conv_ffn_sample",
    )(x, conv_prev, conv_prev, g, sc, sh, gt, w_up, w_up, w_conv, w_conv, w_down, gp)


def _seq_tile(t, cap):
    tm = min(t, cap)
    assert t % tm == 0
    return tm


@jax.jit
def _forward(x_prompt, x_sample, c_prompt, c_sample, cache_k, cache_v, state_wkv, state_shift,
             state_ffn_conv, w_ada, b_ada, g_pre_mix, g_post_mix, g_pre_ffn, g_post_ffn, w_in,
             lam_q1, lam_k1, lam_q2, lam_k2, g_subln, mu_shift, w0, w_w2, a0, w_a2, w_g2, k_k, k_a,
             r_k, ln_x_w, ln_x_b, w_out, w_up, w_conv_ffn, w_down):
    bp, t, _ = x_prompt.shape
    bs, ts, _ = x_sample.shape
    past = cache_k.shape[2]
    assert w_ada.shape[0] == 1, "single-layer problem"

    n_c = bp + bs
    n_c_pad = -(-n_c // SUBLANES) * SUBLANES
    c_all = jnp.concatenate([c_prompt, c_sample, jnp.zeros((n_c_pad - n_c, D_MODEL), F32)], axis=0)
    mod = _ada_call(c_all, w_ada[0], b_ada)
    sh_m, sc_m, gt_m, sh_f, sc_f, gt_f = [mod[:, i * D_MODEL:(i + 1) * D_MODEL] for i in range(6)]
    p_mod = lambda a: a[:bp].reshape(bp, 1, D_MODEL)
    s_mod = lambda a: jnp.repeat(a[bp:n_c], ts, axis=0)

    w_qkv = w_in[0, :, :3 * A_WIDTH].astype(BF16)
    w_rw = w_in[0, :, 3 * A_WIDTH:].astype(BF16)
    w_out_b = w_out[0].astype(BF16)
    w_up_b = w_up[0].astype(BF16)
    w_down_b = w_down[0].astype(BF16)
    lams = (lam_q1, lam_k1, lam_q2, lam_k2)
    prm = {
        "mu_shift": mu_shift, "w0": w0, "a0": a0, "k_k": k_k, "k_a": k_a,
        "r_k": r_k.reshape(1, B_WIDTH), "ln_x_w": ln_x_w, "ln_x_b": ln_x_b,
        "w_w2": w_w2[0].astype(BF16), "w_a2": w_a2[0].astype(BF16), "w_g2": w_g2[0].astype(BF16),
    }

    tm = _seq_tile(t, 512)
    tps = t // tm
    xp = x_prompt.reshape(bp * t, D_MODEL)
    kf, vf, qkvb, rw = _in_proj_call(xp, g_pre_mix, p_mod(sc_m), p_mod(sh_m), w_qkv, w_rw, tm, t)
    oa = _pattn_call(lams, qkvb.reshape(bp, t, 3 * A_WIDTH), g_subln, _seq_tile(t, 512))
    ob, s_last = _rwkv_call(rw.reshape(bp, t, RW_COLS), jnp.zeros((bp, 1, RW_COLS), F32),
                            jnp.zeros((bp, N_GROUPS, MXU_DIM, B_HEAD), state_wkv.dtype), prm,
                            _seq_tile(t, 256), 32)
    x1 = _mixout_call(oa.reshape(bp * t, A_WIDTH), ob.reshape(bp * t, B_WIDTH), xp, w_out_b,
                      g_post_mix, p_mod(gt_m), tm, tps)
    yp, clg, clv = _ffn_call(x1, jnp.zeros((bp, FFN_CONV - 1, 2 * D_FF), F32), g_pre_ffn, p_mod(sc_f),
                             p_mod(sh_f), p_mod(gt_f), w_up_b, w_conv_ffn[0], w_down_b, g_post_ffn,
                             tm, tps)
    rw3 = rw.reshape(bp, t, RW_COLS)
    out_prompt = (
        yp.reshape(bp, t, D_MODEL),
        kf.reshape(1, bp, t, A_HEADS, 2 * A_DK),
        vf.reshape(1, bp, t, A_HEADS, A_DV),
        _groups_to_state(s_last)[None],
        rw3[:, -1][None],
        jnp.concatenate([clg[tps - 1::tps, -(FFN_CONV - 1):], clv[tps - 1::tps, -(FFN_CONV - 1):]],
                        axis=-1)[None],
    )

    ms = bs * ts
    xs = x_sample.reshape(ms, D_MODEL)
    kf_s, vf_s, qkvb_s, rw_s = _in_proj_call(xs, g_pre_mix, s_mod(sc_m), s_mod(sh_m), w_qkv, w_rw, ms, ms)
    oa_s = _sattn_call(lams, qkvb_s.reshape(bs, ts, 3 * A_WIDTH),
                       cache_k[0].reshape(bs, past * A_HEADS, A_DV),
                       cache_v[0].reshape(bs, past * A_HEADS, A_DV), g_subln, _seq_tile(past, 1024))
    ob_s, s_last_s = _rwkv_call(rw_s.reshape(bs, ts, RW_COLS), state_shift[0][:, None, :],
                                _state_to_groups(state_wkv[0]), prm, ts, ts)
    x1_s = _mixout_call(oa_s.reshape(ms, A_WIDTH), ob_s.reshape(ms, B_WIDTH), xs, w_out_b,
                        g_post_mix, s_mod(gt_m), ms, 1)
    cprev = state_ffn_conv[0].reshape(bs * (FFN_CONV - 1), 2 * D_FF)
    ys, ug, uv = _ffn_s_call(x1_s, cprev, g_pre_ffn, s_mod(sc_f), s_mod(sh_f), s_mod(gt_f),
                             w_up_b, w_conv_ffn[0], w_down_b, g_post_ffn, ts, _seq_tile(ms, 256))
    u_all = jnp.concatenate([ug, uv], axis=-1).reshape(bs, ts, 2 * D_FF)
    out_sample = (
        ys.reshape(bs, ts, D_MODEL),
        kf_s.reshape(1, bs, ts, A_HEADS, 2 * A_DK),
        vf_s.reshape(1, bs, ts, A_HEADS, A_DV),
        _groups_to_state(s_last_s)[None],
        rw_s.reshape(bs, ts, RW_COLS)[:, -1][None],
        u_all[:, -(FFN_CONV - 1):][None],
    )
    return (out_prompt[0], out_sample[0]) + out_prompt[1:] + out_sample[1:]


def kernel(x_prompt, x_sample, c_prompt, c_sample, cache_k, cache_v, state_wkv, state_shift, state_ffn_conv, w_ada, b_ada, g_pre_mix, g_post_mix, g_pre_ffn, g_post_ffn, w_in, lam_q1, lam_k1, lam_q2, lam_k2, g_subln, mu_shift, w0, w_w2, a0, w_a2, w_g2, k_k, k_a, r_k, ln_x_w, ln_x_b, w_out, w_up, w_conv_ffn, w_down):
    return _forward(x_prompt, x_sample, c_prompt, c_sample, cache_k, cache_v, state_wkv, state_shift,
                    state_ffn_conv, w_ada, b_ada, g_pre_mix, g_post_mix, g_pre_ffn, g_post_ffn, w_in,
                    lam_q1, lam_k1, lam_q2, lam_k2, g_subln, mu_shift, w0, w_w2, a0, w_a2, w_g2, k_k,
                    k_a, r_k, ln_x_w, ln_x_b, w_out, w_up, w_conv_ffn, w_down)
```

```python
import functools
import math

import jax
import jax.numpy as jnp
from jax import lax
from jax.experimental import pallas as pl
from jax.experimental.pallas import tpu as pltpu

F32 = jnp.float32
BF16 = jnp.bfloat16

D_MODEL = 2048
A_HEADS = 8
A_DK = 64
A_DV = 128
A_WIDTH = A_HEADS * A_DV
B_HEAD = 64
B_WIDTH = D_MODEL - A_WIDTH
B_HEADS = B_WIDTH // B_HEAD
W_RANK = 64
A_RANK = 64
G_RANK = 128
RW_COLS = 3 * B_WIDTH + W_RANK + A_RANK + G_RANK
D_FF = 5632
FFN_CONV = 3
ATTN_CHUNK = 64
EPS = 1e-6
GN_EPS = 64e-5
LAM_INIT = 0.8 - 0.6 * math.exp(-0.3 * 0)
Q_SCALE = (A_DK ** -0.5) * math.log2(math.e)

LANES = 128
SUBLANES = 8
BF16_ROWS = 16
MXU_DIM = 256
GROUP_HEADS = MXU_DIM // B_HEAD
N_GROUPS = B_WIDTH // MXU_DIM
VMEM_LIMIT_BYTES = 56 * 1024 * 1024
NEG_BIG = -1e30


def _cparams(*sem):
    return pltpu.CompilerParams(dimension_semantics=sem, vmem_limit_bytes=VMEM_LIMIT_BYTES)


def _dot(a, b):
    return jnp.dot(a, b, preferred_element_type=F32)


def _dot_nt(a, b):
    return lax.dot_general(a, b, (((1,), (1,)), ((), ())), preferred_element_type=F32)


def _dot_tn(a, b):
    return lax.dot_general(a, b, (((0,), (0,)), ((), ())), preferred_element_type=F32)


def _idiv(x, n):
    assert n > 0 and n & (n - 1) == 0
    return x >> (n.bit_length() - 1)


def _sigmoid(x):
    return 1.0 / (1.0 + jnp.exp(-x))


def _split_bf16(x, terms):
    out = []
    rem = x
    for _ in range(terms):
        piece = rem.astype(BF16)
        out.append(piece)
        rem = rem - piece.astype(F32)
    return out


def _norm_mod(x, g, sc, sh):
    ms = jnp.mean(x * x, axis=-1, keepdims=True)
    return (x * lax.rsqrt(ms + EPS) * g) * (1.0 + sc) + sh


def _rms(x, g):
    ms = jnp.mean(x * x, axis=-1, keepdims=True)
    return x * lax.rsqrt(ms + EPS) * g


def _mod(ref):
    return ref[0] if len(ref.shape) == 3 else ref[...]


def _lam(q1_ref, k1_ref, q2_ref, k2_ref):
    s1 = jnp.sum(q1_ref[...] * k1_ref[...], axis=-1, keepdims=True)
    s2 = jnp.sum(q2_ref[...] * k2_ref[...], axis=-1, keepdims=True)
    return jnp.exp(s1) - jnp.exp(s2) + LAM_INIT


def _ada_kernel(c_ref, w_ref, b_ref, o_ref):
    c = c_ref[...]
    s = (c * _sigmoid(c)).astype(BF16)
    o_ref[...] = _dot(s, w_ref[...].astype(BF16)) + b_ref[...]


def _ada_call(c_all, w_ada, b_ada):
    rows = c_all.shape[0]
    n = w_ada.shape[1]
    tn = 1024
    return pl.pallas_call(
        _ada_kernel,
        grid=(n // tn,),
        in_specs=[
            pl.BlockSpec((rows, D_MODEL), lambda j: (0, 0)),
            pl.BlockSpec((D_MODEL, tn), lambda j: (0, j)),
            pl.BlockSpec((1, tn), lambda j: (0, j)),
        ],
        out_specs=pl.BlockSpec((rows, tn), lambda j: (0, j)),
        out_shape=jax.ShapeDtypeStruct((rows, n), F32),
        compiler_params=_cparams("arbitrary"),
        name="ada_mod",
    )(c_all, w_ada, b_ada)


def _qkv_kernel(x_ref, g_ref, sc_ref, sh_ref, w_ref, kf_ref, vf_ref, qkvb_ref):
    h = _norm_mod(x_ref[...], g_ref[...], _mod(sc_ref), _mod(sh_ref)).astype(BF16)
    q = _dot(h, w_ref[:, 0:A_WIDTH])
    qkvb_ref[:, 0:A_WIDTH] = (q * Q_SCALE).astype(BF16)
    k = _dot(h, w_ref[:, A_WIDTH:2 * A_WIDTH])
    kf_ref[...] = k
    qkvb_ref[:, A_WIDTH:2 * A_WIDTH] = k.astype(BF16)
    v = _dot(h, w_ref[:, 2 * A_WIDTH:3 * A_WIDTH])
    vf_ref[...] = v
    qkvb_ref[:, 2 * A_WIDTH:3 * A_WIDTH] = v.astype(BF16)


def _rw_kernel(x_ref, g_ref, sc_ref, sh_ref, w_ref, o_ref):
    h = _norm_mod(x_ref[...], g_ref[...], _mod(sc_ref), _mod(sh_ref)).astype(BF16)
    o_ref[...] = _dot(h, w_ref[:, 3 * A_WIDTH:])


def _mod_spec(mod, tm, tiles_per_seq, nidx):
    if mod.ndim == 3:
        if nidx == 2:
            return pl.BlockSpec((1, 1, D_MODEL), lambda i, j: (i // tiles_per_seq, 0, 0))
        return pl.BlockSpec((1, 1, D_MODEL), lambda i: (i // tiles_per_seq, 0, 0))
    if nidx == 2:
        return pl.BlockSpec((tm, D_MODEL), lambda i, j: (i, 0))
    return pl.BlockSpec((tm, D_MODEL), lambda i: (i, 0))


def _resident(shape):
    return pl.BlockSpec(shape, lambda i: (0,) * len(shape), pipeline_mode=pl.Buffered(1))


def _in_proj_call(x, g, sc, sh, w_in, tm, seq_len):
    m = x.shape[0]
    n_in = w_in.shape[1]
    common = [
        pl.BlockSpec((tm, D_MODEL), lambda i: (i, 0)),
        _resident((1, D_MODEL)),
        _mod_spec(sc, tm, seq_len // tm, 1),
        _mod_spec(sh, tm, seq_len // tm, 1),
    ]
    kf, vf, qkvb = pl.pallas_call(
        _qkv_kernel,
        grid=(m // tm,),
        in_specs=common + [_resident((D_MODEL, n_in))],
        out_specs=[
            pl.BlockSpec((tm, A_WIDTH), lambda i: (i, 0)),
            pl.BlockSpec((tm, A_WIDTH), lambda i: (i, 0)),
            pl.BlockSpec((tm, 3 * A_WIDTH), lambda i: (i, 0)),
        ],
        out_shape=[
            jax.ShapeDtypeStruct((m, A_WIDTH), F32),
            jax.ShapeDtypeStruct((m, A_WIDTH), F32),
            jax.ShapeDtypeStruct((m, 3 * A_WIDTH), BF16),
        ],
        compiler_params=_cparams("arbitrary"),
        name="in_proj_qkv",
    )(x, g, sc, sh, w_in)
    rw = pl.pallas_call(
        _rw_kernel,
        grid=(m // tm,),
        in_specs=common + [_resident((D_MODEL, n_in))],
        out_specs=pl.BlockSpec((tm, RW_COLS), lambda i: (i, 0)),
        out_shape=jax.ShapeDtypeStruct((m, RW_COLS), F32),
        compiler_params=_cparams("arbitrary"),
        name="in_proj_rw",
    )(x, g, sc, sh, w_in)
    return kf, vf, qkvb, rw


def _softmax_step(s, vb, m_ref, l_ref, acc_ref, idx):
    m_prev = m_ref[idx]
    m_new = jnp.maximum(m_prev, jnp.max(s, axis=-1, keepdims=True))
    alpha = jnp.exp2(m_prev - m_new)
    p = jnp.exp2(s - m_new)
    l_ref[idx] = alpha * l_ref[idx] + jnp.sum(p, axis=-1, keepdims=True)
    acc_ref[idx] = alpha * acc_ref[idx] + _dot(p.astype(BF16), vb)
    m_ref[idx] = m_new


def _subln(o, g):
    return _rms(o, g) * (1.0 - LAM_INIT)


PATTN_HEADS = 2
PATTN_QG = 512


def _pattn_kernel(q1_ref, k1_ref, q2_ref, k2_ref, q_ref, k_ref, v_ref, g_ref, o_ref,
                  m_scr, l_scr, acc_scr, *, tq):
    i = pl.program_id(2)
    lane = lax.broadcasted_iota(jnp.int32, (tq, A_DV), 1)
    m_scr[...] = jnp.full(m_scr.shape, NEG_BIG, F32)
    l_scr[...] = jnp.zeros(l_scr.shape, F32)
    acc_scr[...] = jnp.zeros(acc_scr.shape, F32)

    qg = min(PATTN_QG, 2 * tq)
    n_qg = 2 * tq // qg
    q_gs = []
    for hh in range(PATTN_HEADS):
        q = q_ref[0, :, hh * A_DV:(hh + 1) * A_DV]
        zero = jnp.zeros_like(q)
        q_st = jnp.concatenate([jnp.where(lane < A_DK, q, zero), jnp.where(lane >= A_DK, q, zero)], axis=0)
        q_gs.append([q_st[g * qg:(g + 1) * qg] for g in range(n_qg)])
    ones_k = jnp.ones((SUBLANES, tq), BF16)
    chains = [(hh, g) for g in range(n_qg) for hh in range(PATTN_HEADS)]

    def block(j, masks):
        r0 = pl.multiple_of(j * tq, tq)
        kbs = [k_ref[0, pl.ds(r0, tq), hh * A_DV:(hh + 1) * A_DV] for hh in range(PATTN_HEADS)]
        vbs = [v_ref[0, pl.ds(r0, tq), hh * A_DV:(hh + 1) * A_DV] for hh in range(PATTN_HEADS)]
        s_ts = {(hh, g): _dot_nt(kbs[hh], q_gs[hh][g]) for hh, g in chains}
        for hh, g in chains:
            cs = slice(g * qg, (g + 1) * qg)
            s_t = s_ts[(hh, g)]
            if masks is not None:
                s_t = jnp.where(masks[g], s_t, NEG_BIG)
            m_prev = m_scr[hh, :, cs]
            m_new = jnp.maximum(m_prev, jnp.max(s_t, axis=0, keepdims=True))
            alpha = jnp.exp2(m_prev - m_new)
            p_b = jnp.exp2(s_t - m_new).astype(BF16)
            l_scr[hh, :, cs] = alpha * l_scr[hh, :, cs] + _dot(ones_k, p_b)[0:1]
            acc_scr[hh, :, cs] = alpha * acc_scr[hh, :, cs] + _dot_tn(vbs[hh], p_b)
            m_scr[hh, :, cs] = m_new

    def body(j, carry):
        block(j, None)
        return carry

    lax.fori_loop(0, i, body, 0)
    rk = lax.broadcasted_iota(jnp.int32, (tq, qg), 0)
    cq = lax.broadcasted_iota(jnp.int32, (tq, qg), 1)
    block(i, [_idiv(rk, ATTN_CHUNK) <= _idiv((cq + g * qg) & (tq - 1), ATTN_CHUNK)
              for g in range(n_qg)])

    lam = _lam(q1_ref, k1_ref, q2_ref, k2_ref)
    for hh in range(PATTN_HEADS):
        o_t = (acc_scr[hh, :, 0:tq] / l_scr[hh, :, 0:tq]
               - lam * (acc_scr[hh, :, tq:2 * tq] / l_scr[hh, :, tq:2 * tq]))
        o_ref[0, :, hh * A_DV:(hh + 1) * A_DV] = _subln(o_t.T, g_ref[...]).astype(BF16)


def _lam_specs(nidx):
    if nidx == 3:
        return [pl.BlockSpec((1, A_DK), lambda b, h, i: (0, 0))] * 4
    return [pl.BlockSpec((1, A_DK), lambda b, j: (0, 0))] * 4


def _pattn_call(lams, qkvb, g_subln, tq):
    bsz, t, _ = qkvb.shape
    hw = PATTN_HEADS * A_DV
    ng = A_HEADS // PATTN_HEADS
    return pl.pallas_call(
        functools.partial(_pattn_kernel, tq=tq),
        grid=(bsz, ng, t // tq),
        in_specs=_lam_specs(3) + [
            pl.BlockSpec((1, tq, hw), lambda b, h, i: (b, i, h)),
            pl.BlockSpec((1, t, hw), lambda b, h, i: (b, 0, ng + h)),
            pl.BlockSpec((1, t, hw), lambda b, h, i: (b, 0, 2 * ng + h)),
            pl.BlockSpec((1, A_DV), lambda b, h, i: (0, 0)),
        ],
        out_specs=pl.BlockSpec((1, tq, hw), lambda b, h, i: (b, i, h)),
        out_shape=jax.ShapeDtypeStruct((bsz, t, A_WIDTH), BF16),
        scratch_shapes=[
            pltpu.VMEM((PATTN_HEADS, 1, 2 * tq), F32),
            pltpu.VMEM((PATTN_HEADS, 1, 2 * tq), F32),
            pltpu.VMEM((PATTN_HEADS, A_DV, 2 * tq), F32),
        ],
        compiler_params=_cparams("arbitrary", "arbitrary", "arbitrary"),
        name="prompt_attention",
    )(*lams, qkvb, qkvb, qkvb, g_subln)


N_VQ = 2 * A_HEADS


def _sattn_kernel(q1_ref, k1_ref, q2_ref, k2_ref, q_ref, kn_ref, vn_ref, ck_ref, cv_ref, g_ref, o_ref,
                  qb_scr, m_scr, l_scr, acc_scr, *, tnew):
    j = pl.program_id(1)
    nvq = N_VQ * tnew

    @pl.when(j == 0)
    def _():
        q = q_ref[0]
        qt = jnp.concatenate([q] * N_VQ, axis=0)
        row = lax.broadcasted_iota(jnp.int32, (nvq, A_WIDTH), 0)
        col = lax.broadcasted_iota(jnp.int32, (nvq, A_WIDTH), 1)
        qb_scr[...] = jnp.where(_idiv(row, tnew) == _idiv(col, A_DK), qt, jnp.zeros_like(qt))
        m_scr[...] = jnp.full(m_scr.shape, NEG_BIG, F32)
        l_scr[...] = jnp.zeros(l_scr.shape, F32)
        acc_scr[...] = jnp.zeros(acc_scr.shape, F32)

    def step(kb, vb):
        _softmax_step(_dot_nt(qb_scr[...], kb), vb, m_scr, l_scr, acc_scr, 0)

    def cached_rows(ref, p0, n):
        heads = [ref[0, pl.ds(p0 * A_HEADS + h, n, stride=A_HEADS), :] for h in range(A_HEADS)]
        return jnp.concatenate(heads, axis=-1).astype(BF16)

    tk = ck_ref.shape[1] // A_HEADS
    n_sub = max(1, tk // MXU_DIM)
    sub = tk // n_sub
    kbs = [cached_rows(ck_ref, c * sub, sub) for c in range(n_sub)]
    s_all = [_dot_nt(qb_scr[...], kb) for kb in kbs]
    for c in range(n_sub):
        _softmax_step(s_all[c], cached_rows(cv_ref, c * sub, sub), m_scr, l_scr, acc_scr, 0)

    @pl.when(j == pl.num_programs(1) - 1)
    def _():
        step(kn_ref[0], vn_ref[0])
        lam = _lam(q1_ref, k1_ref, q2_ref, k2_ref)
        g = g_ref[...]
        for h in range(A_HEADS):
            cs = slice(h * A_DV, (h + 1) * A_DV)
            r0 = slice((2 * h) * tnew, (2 * h + 1) * tnew)
            r1 = slice((2 * h + 1) * tnew, (2 * h + 2) * tnew)
            o0 = acc_scr[0, r0, cs] / l_scr[0, r0, :]
            o1 = acc_scr[0, r1, cs] / l_scr[0, r1, :]
            o_ref[0, :, cs] = _subln(o0 - lam * o1, g).astype(BF16)


def _sattn_call(lams, qkvb, cache_k, cache_v, g_subln, tk):
    bsz, tnew, _ = qkvb.shape
    past = cache_k.shape[1] // A_HEADS
    nvq = N_VQ * tnew
    return pl.pallas_call(
        functools.partial(_sattn_kernel, tnew=tnew),
        grid=(bsz, past // tk),
        in_specs=_lam_specs(2) + [
            pl.BlockSpec((1, tnew, A_WIDTH), lambda b, j: (b, 0, 0)),
            pl.BlockSpec((1, tnew, A_WIDTH), lambda b, j: (b, 0, 1)),
            pl.BlockSpec((1, tnew, A_WIDTH), lambda b, j: (b, 0, 2)),
            pl.BlockSpec((1, tk * A_HEADS, A_DV), lambda b, j: (b, j, 0)),
            pl.BlockSpec((1, tk * A_HEADS, A_DV), lambda b, j: (b, j, 0)),
            pl.BlockSpec((1, A_DV), lambda b, j: (0, 0)),
        ],
        out_specs=pl.BlockSpec((1, tnew, A_WIDTH), lambda b, j: (b, 0, 0)),
        out_shape=jax.ShapeDtypeStruct((bsz, tnew, A_WIDTH), BF16),
        scratch_shapes=[
            pltpu.VMEM((nvq, A_WIDTH), BF16),
            pltpu.VMEM((1, nvq, 1), F32),
            pltpu.VMEM((1, nvq, 1), F32),
            pltpu.VMEM((1, nvq, A_WIDTH), F32),
        ],
        compiler_params=_cparams("arbitrary", "arbitrary"),
        name="sample_attention",
    )(*lams, qkvb, qkvb, qkvb, cache_k, cache_v, g_subln)


def _head_sum(x, ones_bd):
    return _dot(x.astype(BF16), ones_bd)


def _rwkv_kernel(p_ref, ph_ref, st_ref, s0_ref, mu_ref, w0_ref, a0_ref, kk_ref, ka_ref, rk_ref,
                 lnw_ref, lnb_ref, ww2_ref, wa2_ref, wg2_ref, ob_ref, so_ref,
                 r_scr, k_scr, v_scr, na_scr, nb_scr, ld_scr, cum_scr, g_scr, bon_scr, y_scr, s_scr,
                 *, tb, chunk):
    i = pl.program_id(1)
    n_seq = p_ref.shape[0]
    c_len = chunk
    n_chunks = tb // c_len

    row_g = lax.broadcasted_iota(jnp.int32, (MXU_DIM, MXU_DIM), 0)
    col_g = lax.broadcasted_iota(jnp.int32, (MXU_DIM, MXU_DIM), 1)
    bd_mask = _idiv(row_g, B_HEAD) == _idiv(col_g, B_HEAD)
    ones_bd = jnp.where(bd_mask, 1.0, 0.0).astype(BF16)

    @pl.when(i == 0)
    def _():
        rep = jnp.where(
            lax.broadcasted_iota(jnp.int32, (B_HEAD, MXU_DIM), 0)
            == (lax.broadcasted_iota(jnp.int32, (B_HEAD, MXU_DIM), 1) & (B_HEAD - 1)), 1.0, 0.0).astype(BF16)
        for b in range(n_seq):
            for q in range(N_GROUPS):
                wide = None
                for piece in _split_bf16(s0_ref[b, q], 3):
                    d = _dot(piece, rep)
                    wide = d if wide is None else wide + d
                s_scr[b, q] = jnp.where(bd_mask, wide, 0.0)

    row_t = lax.broadcasted_iota(jnp.int32, (tb, tb), 0)
    col_t = lax.broadcasted_iota(jnp.int32, (tb, tb), 1)
    tri = jnp.where((_idiv(row_t, c_len) == _idiv(col_t, c_len)) & (col_t <= row_t), 1.0, 0.0).astype(BF16)
    w_cat = jnp.concatenate([ww2_ref[...], wa2_ref[...]], axis=0)

    for b in range(n_seq):
        p = p_ref[b]
        prev_row = jnp.where(i == 0, st_ref[b], ph_ref[b, SUBLANES - 1:SUBLANES, :])
        rowi = lax.broadcasted_iota(jnp.int32, p.shape, 0)
        p_prev = jnp.where(rowi == 0, prev_row, pltpu.roll(p, 1, 0))
        xs = p + (p_prev - p) * mu_ref[...]
        r = xs[:, 0:B_WIDTH]
        k = xs[:, B_WIDTH:2 * B_WIDTH]
        v = xs[:, 2 * B_WIDTH:3 * B_WIDTH]
        wa = xs[:, 3 * B_WIDTH:3 * B_WIDTH + W_RANK + A_RANK]
        gd = xs[:, 3 * B_WIDTH + W_RANK + A_RANK:]

        lane_wa = lax.broadcasted_iota(jnp.int32, wa.shape, 1)
        zero_wa = jnp.zeros_like(wa)
        w_lin = _dot(jnp.where(lane_wa < W_RANK, jnp.tanh(wa), zero_wa).astype(BF16), w_cat)
        a_lin = _dot(jnp.where(lane_wa >= W_RANK, wa, zero_wa).astype(BF16), w_cat)
        zw = -(w0_ref[...] + w_lin)
        softplus = jnp.maximum(zw, 0.0) + jnp.log(1.0 + jnp.exp(-jnp.abs(zw)))
        logd = -jnp.exp(-softplus - 0.5)
        a = _sigmoid(a0_ref[...] + a_lin)
        g_scr[b] = _dot(_sigmoid(gd).astype(BF16), wg2_ref[...])

        k2 = k * (1.0 + (a - 1.0) * ka_ref[...])
        kk = k * kk_ref[...]
        rkk = r * k2 * rk_ref[...]
        for q in range(N_GROUPS):
            ls = slice(q * MXU_DIM, (q + 1) * MXU_DIM)
            kq = kk[:, ls]
            nrm = jnp.maximum(jnp.sqrt(_head_sum(kq * kq, ones_bd)), 1e-12)
            kn = kq / nrm
            na_scr[b, :, ls] = -kn
            nb_scr[b, :, ls] = kn * a[:, ls]
            bon_scr[b, :, ls] = _head_sum(rkk[:, ls], ones_bd) * v[:, ls]
        r_scr[b] = r
        k_scr[b] = k2
        v_scr[b] = v
        ld_scr[b] = logd

        cum = None
        for piece in _split_bf16(logd, 2):
            d = _dot(tri, piece)
            cum = d if cum is None else cum + d
        cum_scr[b] = cum

    gc = GROUP_HEADS * c_len
    lane_q = lax.broadcasted_iota(jnp.int32, (1, MXU_DIM), 1)
    head_masks = [_idiv(lane_q, B_HEAD) == j for j in range(GROUP_HEADS)]
    row_s = lax.broadcasted_iota(jnp.int32, (gc, gc), 0)
    col_s = lax.broadcasted_iota(jnp.int32, (gc, gc), 1)
    same_head = _idiv(row_s, c_len) == _idiv(col_s, c_len)
    strict = same_head & (col_s < row_s)
    incl = same_head & (col_s <= row_s)
    eye = jnp.where(col_s == row_s, 1.0, 0.0).astype(F32)
    n_double = int(math.log2(c_len)) - 1
    chains = [(b, q) for q in range(N_GROUPS) for b in range(n_seq)]
    groups = range(len(chains))

    def stack(x):
        z = jnp.zeros_like(x)
        return jnp.concatenate([jnp.where(head_masks[j], x, z) for j in range(GROUP_HEADS)], axis=0)

    def unstack(x):
        out = x[0:c_len]
        for j in range(1, GROUP_HEADS):
            out = out + x[j * c_len:(j + 1) * c_len]
        return out

    def chunk_body(c, carry):
        r0 = pl.multiple_of(c * c_len, c_len)
        rows = pl.ds(r0, c_len)
        a_s, r_s, b_s, k_s, be_s, ke_s, v_s, dec = [], [], [], [], [], [], [], []
        for b, q in chains:
            ls = slice(q * MXU_DIM, (q + 1) * MXU_DIM)
            cum_q = cum_scr[b, rows, ls]
            k_q = k_scr[b, rows, ls]
            nb_q = nb_scr[b, rows, ls]
            cum_last = cum_q[c_len - 1:c_len, :]
            e_out = jnp.exp(-cum_q)
            e_end = jnp.exp(cum_last - cum_q)
            a_s.append(stack((na_scr[b, rows, ls] * jnp.exp(cum_q - ld_scr[b, rows, ls])).astype(BF16)))
            r_s.append(stack((r_scr[b, rows, ls] * jnp.exp(cum_q)).astype(BF16)))
            b_s.append(stack((nb_q * e_out).astype(BF16)))
            k_s.append(stack((k_q * e_out).astype(BF16)))
            be_s.append(stack((nb_q * e_end).astype(BF16)))
            ke_s.append(stack((k_q * e_end).astype(BF16)))
            v_s.append(stack(v_scr[b, rows, ls].astype(BF16)))
            dec.append(jnp.exp(cum_last))

        if gc % LANES == 0:
            g_all = [_dot_nt(jnp.concatenate([a_s[q], r_s[q]], axis=0),
                             jnp.concatenate([b_s[q], k_s[q]], axis=0)) for q in groups]
            g_ab = [g[0:gc, 0:gc] for g in g_all]
            g_ak = [g[0:gc, gc:2 * gc] for g in g_all]
            g_rb = [g[gc:2 * gc, 0:gc] for g in g_all]
            g_rk = [g[gc:2 * gc, gc:2 * gc] for g in g_all]
        else:
            g_ab = [_dot_nt(a_s[q], b_s[q]) for q in groups]
            g_ak = [_dot_nt(a_s[q], k_s[q]) for q in groups]
            g_rb = [_dot_nt(r_s[q], b_s[q]) for q in groups]
            g_rk = [_dot_nt(r_s[q], k_s[q]) for q in groups]
        aab = [jnp.where(strict, g, 0.0) for g in g_ab]
        aak = [jnp.where(strict, g, 0.0).astype(BF16) for g in g_ak]
        arb = [jnp.where(incl, g, 0.0).astype(BF16) for g in g_rb]
        ark = [jnp.where(incl, g, 0.0).astype(BF16) for g in g_rk]

        t_m = [eye + x for x in aab]
        pw = [x.astype(BF16) for x in aab]
        pw = [_dot(x, x) for x in pw]
        for m in range(n_double):
            pb = [x.astype(BF16) for x in pw]
            if m + 1 < n_double:
                pw = [_dot(x, x) for x in pb]
            t_m = [t_m[q] + _dot(t_m[q].astype(BF16), pb[q]) for q in groups]
        t_b = [x.astype(BF16) for x in t_m]

        av = [_dot(aak[q], v_s[q]) for q in groups]

        s_prev = [s_scr[b, q] for b, q in chains]
        ar = [_dot_nt(jnp.concatenate([a_s[q], r_s[q]], axis=0), s_prev[q].astype(BF16)) for q in groups]
        u_b = [_dot(t_b[q], (ar[q][0:gc] + av[q]).astype(BF16)).astype(BF16) for q in groups]
        uv = [jnp.concatenate([u_b[q], v_s[q]], axis=0) for q in groups]
        if gc % LANES == 0:
            y_st = [ar[q][gc:2 * gc] + _dot(jnp.concatenate([arb[q], ark[q]], axis=1), uv[q])
                    for q in groups]
        else:
            y_st = [ar[q][gc:2 * gc] + _dot(arb[q], u_b[q]) + _dot(ark[q], v_s[q]) for q in groups]
        s_new = [s_prev[q] * dec[q]
                 + _dot_tn(uv[q], jnp.concatenate([be_s[q], ke_s[q]], axis=0)) for q in groups]
        for ci, (b, q) in enumerate(chains):
            s_scr[b, q] = s_new[ci]
            y_scr[b, rows, q * MXU_DIM:(q + 1) * MXU_DIM] = unstack(y_st[ci])
        return carry

    lax.fori_loop(0, n_chunks, chunk_body, 0)

    @pl.when(i == pl.num_programs(1) - 1)
    def _():
        rep_t = jnp.where(
            (lax.broadcasted_iota(jnp.int32, (MXU_DIM, B_HEAD), 0) & (B_HEAD - 1))
            == lax.broadcasted_iota(jnp.int32, (MXU_DIM, B_HEAD), 1), 1.0, 0.0).astype(BF16)
        for b, q in chains:
            narrow = None
            for piece in _split_bf16(s_scr[b, q], 3):
                d = _dot(piece, rep_t)
                narrow = d if narrow is None else narrow + d
            so_ref[b, q] = narrow

    inv_n = 1.0 / B_HEAD
    for b, q in chains:
        ls = slice(q * MXU_DIM, (q + 1) * MXU_DIM)
        y = y_scr[b, :, ls]
        mu = _head_sum(y, ones_bd) * inv_n
        d = y - mu
        var = _head_sum(d * d, ones_bd) * inv_n
        yn = d * lax.rsqrt(var + GN_EPS) * lnw_ref[:, ls] + lnb_ref[:, ls]
        ob_ref[b, :, ls] = ((yn + bon_scr[b, :, ls]) * g_scr[b, :, ls]).astype(BF16)


def _rwkv_call(rw, shift0, s0, prm, tb, chunk, n_seq):
    bsz, t, _ = rw.shape
    assert bsz % n_seq == 0
    nt = t // tb
    hb = tb // SUBLANES
    vec = lambda n: pl.BlockSpec((1, n), lambda b, i: (0, 0))
    return pl.pallas_call(
        functools.partial(_rwkv_kernel, tb=tb, chunk=chunk),
        grid=(bsz // n_seq, nt),
        in_specs=[
            pl.BlockSpec((n_seq, tb, RW_COLS), lambda b, i: (b, i, 0)),
            pl.BlockSpec((n_seq, SUBLANES, RW_COLS), lambda b, i: (b, jnp.maximum(i * hb - 1, 0), 0)),
            pl.BlockSpec((n_seq, 1, RW_COLS), lambda b, i: (b, 0, 0)),
            pl.BlockSpec((n_seq, N_GROUPS, MXU_DIM, B_HEAD), lambda b, i: (b, 0, 0, 0)),
            vec(RW_COLS), vec(B_WIDTH), vec(B_WIDTH), vec(B_WIDTH), vec(B_WIDTH), vec(B_WIDTH),
            vec(B_WIDTH), vec(B_WIDTH),
            pl.BlockSpec((W_RANK, B_WIDTH), lambda b, i: (0, 0)),
            pl.BlockSpec((A_RANK, B_WIDTH), lambda b, i: (0, 0)),
            pl.BlockSpec((G_RANK, B_WIDTH), lambda b, i: (0, 0)),
        ],
        out_specs=[
            pl.BlockSpec((n_seq, tb, B_WIDTH), lambda b, i: (b, i, 0)),
            pl.BlockSpec((n_seq, N_GROUPS, MXU_DIM, B_HEAD), lambda b, i: (b, 0, 0, 0)),
        ],
        out_shape=[
            jax.ShapeDtypeStruct((bsz, t, B_WIDTH), BF16),
            jax.ShapeDtypeStruct((bsz, N_GROUPS, MXU_DIM, B_HEAD), F32),
        ],
        scratch_shapes=[pltpu.VMEM((n_seq, tb, B_WIDTH), F32)] * 10
        + [pltpu.VMEM((n_seq, N_GROUPS, MXU_DIM, MXU_DIM), F32)],
        compiler_params=_cparams("arbitrary", "arbitrary"),
        name="rwkv7_mix",
    )(rw, rw, shift0, s0, prm["mu_shift"], prm["w0"], prm["a0"], prm["k_k"], prm["k_a"], prm["r_k"],
      prm["ln_x_w"], prm["ln_x_b"], prm["w_w2"], prm["w_a2"], prm["w_g2"])


def _state_to_groups(s):
    return s.reshape(s.shape[0], N_GROUPS, MXU_DIM, B_HEAD)


def _groups_to_state(sg):
    return sg.reshape(sg.shape[0], B_HEADS, B_HEAD, B_HEAD)


def _mixout_kernel(oa_ref, ob_ref, x_ref, w_ref, g_ref, gt_ref, o_ref):
    mix = _dot(oa_ref[...], w_ref[0:A_WIDTH, :]) + _dot(ob_ref[...], w_ref[A_WIDTH:D_MODEL, :])
    o_ref[...] = x_ref[...] + _mod(gt_ref) * _rms(mix, g_ref[...])


def _mixout_call(oa, ob, x, w_out, g, gt, tm, tiles_per_seq):
    m = x.shape[0]
    return pl.pallas_call(
        _mixout_kernel,
        grid=(m // tm,),
        in_specs=[
            pl.BlockSpec((tm, A_WIDTH), lambda i: (i, 0)),
            pl.BlockSpec((tm, B_WIDTH), lambda i: (i, 0)),
            pl.BlockSpec((tm, D_MODEL), lambda i: (i, 0)),
            pl.BlockSpec((D_MODEL, D_MODEL), lambda i: (0, 0)),
            pl.BlockSpec((1, D_MODEL), lambda i: (0, 0)),
            _mod_spec(gt, tm, tiles_per_seq, 1),
        ],
        out_specs=pl.BlockSpec((tm, D_MODEL), lambda i: (i, 0)),
        out_shape=jax.ShapeDtypeStruct((m, D_MODEL), F32),
        compiler_params=_cparams("arbitrary"),
        name="mix_out",
    )(oa, ob, x, w_out, g, gt)


FFN_HALO = BF16_ROWS
FFN_TF = 512


def _gated(zg, zv):
    return (zg * _sigmoid(zg) * zv).astype(BF16)


def _ffn_kernel(x_ref, xh_ref, cpg_ref, cpv_ref, g_ref, sc_ref, sh_ref, gt_ref, wg_ref, wv_ref,
                cg_ref, cv_ref, wd_ref, gp_ref, o_ref, clg_ref, clv_ref, h_scr, acc_scr,
                *, tm, tiles_per_seq):
    mi = pl.program_id(0)
    f = pl.program_id(1)
    first = (mi % tiles_per_seq) == 0

    @pl.when(f == 0)
    def _():
        g, sc, sh = g_ref[...], sc_ref[0], sh_ref[0]
        h_scr[FFN_HALO:, :] = _norm_mod(x_ref[...], g, sc, sh).astype(BF16)
        h_scr[:FFN_HALO, :] = _norm_mod(xh_ref[...], g, sc, sh).astype(BF16)
        acc_scr[...] = jnp.zeros(acc_scr.shape, F32)

    h = h_scr[...]
    row = lax.broadcasted_iota(jnp.int32, (FFN_HALO, FFN_TF), 0)

    def conv(u, cp_ref, c_ref, cl_ref):
        cp = cp_ref[0]
        head = u[:FFN_HALO]
        head_first = jnp.where(row == FFN_HALO - 2, cp[0:1],
                               jnp.where(row == FFN_HALO - 1, cp[1:2], head))
        ue = jnp.concatenate([jnp.where(first, head_first, head), u[FFN_HALO:]], axis=0)
        cl_ref[0] = ue[tm + FFN_HALO - SUBLANES:]
        c = c_ref[...]
        return (c[0:1] * pltpu.roll(ue, 2, 0)[FFN_HALO:] + c[1:2] * pltpu.roll(ue, 1, 0)[FFN_HALO:]
                + c[2:3] * ue[FFN_HALO:])

    zg = conv(_dot(h, wg_ref[...]), cpg_ref, cg_ref, clg_ref)
    zv = conv(_dot(h, wv_ref[...]), cpv_ref, cv_ref, clv_ref)
    acc_scr[...] += _dot(_gated(zg, zv), wd_ref[...])

    @pl.when(f == pl.num_programs(1) - 1)
    def _():
        o_ref[...] = x_ref[...] + gt_ref[0] * _rms(acc_scr[...], gp_ref[...])


def _ffn_call(x, conv_prev, g, sc, sh, gt, w_up, w_conv, w_down, gp, tm, tiles_per_seq):
    m = x.shape[0]
    nseq = conv_prev.shape[0]
    nf = D_FF // FFN_TF
    hb = tm // FFN_HALO
    seq = lambda i: i // tiles_per_seq
    modspec = pl.BlockSpec((1, 1, D_MODEL), lambda i, f: (seq(i), 0, 0))
    return pl.pallas_call(
        functools.partial(_ffn_kernel, tm=tm, tiles_per_seq=tiles_per_seq),
        grid=(m // tm, nf),
        in_specs=[
            pl.BlockSpec((tm, D_MODEL), lambda i, f: (i, 0)),
            pl.BlockSpec((FFN_HALO, D_MODEL), lambda i, f: (jnp.maximum(i * hb - 1, 0), 0)),
            pl.BlockSpec((1, FFN_CONV - 1, FFN_TF), lambda i, f: (seq(i), 0, f)),
            pl.BlockSpec((1, FFN_CONV - 1, FFN_TF), lambda i, f: (seq(i), 0, f + nf)),
            pl.BlockSpec((1, D_MODEL), lambda i, f: (0, 0)),
            modspec, modspec, modspec,
            pl.BlockSpec((D_MODEL, FFN_TF), lambda i, f: (0, f)),
            pl.BlockSpec((D_MODEL, FFN_TF), lambda i, f: (0, f + nf)),
            pl.BlockSpec((FFN_CONV, FFN_TF), lambda i, f: (0, f)),
            pl.BlockSpec((FFN_CONV, FFN_TF), lambda i, f: (0, f + nf)),
            pl.BlockSpec((FFN_TF, D_MODEL), lambda i, f: (f, 0)),
            pl.BlockSpec((1, D_MODEL), lambda i, f: (0, 0)),
        ],
        out_specs=[
            pl.BlockSpec((tm, D_MODEL), lambda i, f: (i, 0)),
            pl.BlockSpec((1, SUBLANES, FFN_TF), lambda i, f: (i, 0, f)),
            pl.BlockSpec((1, SUBLANES, FFN_TF), lambda i, f: (i, 0, f)),
        ],
        out_shape=[
            jax.ShapeDtypeStruct((m, D_MODEL), F32),
            jax.ShapeDtypeStruct((m // tm, SUBLANES, D_FF), F32),
            jax.ShapeDtypeStruct((m // tm, SUBLANES, D_FF), F32),
        ],
        scratch_shapes=[
            pltpu.VMEM((tm + FFN_HALO, D_MODEL), BF16),
            pltpu.VMEM((tm, D_MODEL), F32),
        ],
        compiler_params=_cparams("arbitrary", "arbitrary"),
        name="conv_ffn_prompt",
    )(x, x, conv_prev, conv_prev, g, sc, sh, gt, w_up, w_up, w_conv, w_conv, w_down, gp)


def _ffn_s_kernel(x_ref, cpg_ref, cpv_ref, g_ref, sc_ref, sh_ref, gt_ref, wg_ref, wv_ref,
                  cg_ref, cv_ref, wd_ref, gp_ref, o_ref, ug_ref, uv_ref, h_scr, acc_scr, *, seq_len):
    f = pl.program_id(1)

    @pl.when(f == 0)
    def _():
        h_scr[...] = _norm_mod(x_ref[...], g_ref[...], sc_ref[...], sh_ref[...]).astype(BF16)
        acc_scr[...] = jnp.zeros(acc_scr.shape, F32)

    h = h_scr[...]
    tm = h.shape[0]
    pos = lax.broadcasted_iota(jnp.int32, (tm, FFN_TF), 0) & (seq_len - 1)
    n_cp = cpg_ref.shape[0]
    trow = lax.broadcasted_iota(jnp.int32, (tm, n_cp), 0)
    ccol = lax.broadcasted_iota(jnp.int32, (tm, n_cp), 1)
    tpos = trow & (seq_len - 1)
    base = 2 * _idiv(trow, seq_len)
    e1 = jnp.where((tpos == 0) & (ccol == base + 1), 1.0, 0.0).astype(BF16)
    e2 = jnp.where(((tpos == 0) & (ccol == base)) | ((tpos == 1) & (ccol == base + 1)), 1.0, 0.0).astype(BF16)

    def conv(u, cp_ref, c_ref, u_ref):
        u_ref[...] = u
        p1 = p2 = None
        for piece in _split_bf16(cp_ref[...], 3):
            d1, d2 = _dot(e1, piece), _dot(e2, piece)
            p1 = d1 if p1 is None else p1 + d1
            p2 = d2 if p2 is None else p2 + d2
        u1 = jnp.where(pos >= 1, pltpu.roll(u, 1, 0), p1)
        u2 = jnp.where(pos >= 2, pltpu.roll(u, 2, 0), p2)
        c = c_ref[...]
        return c[0:1] * u2 + c[1:2] * u1 + c[2:3] * u

    zg = conv(_dot(h, wg_ref[...]), cpg_ref, cg_ref, ug_ref)
    zv = conv(_dot(h, wv_ref[...]), cpv_ref, cv_ref, uv_ref)
    acc_scr[...] += _dot(_gated(zg, zv), wd_ref[...])

    @pl.when(f == pl.num_programs(1) - 1)
    def _():
        o_ref[...] = x_ref[...] + gt_ref[...] * _rms(acc_scr[...], gp_ref[...])


def _ffn_s_call(x, conv_prev, g, sc, sh, gt, w_up, w_conv, w_down, gp, seq_len, tm):
    m = x.shape[0]
    assert tm % seq_len == 0 and seq_len & (seq_len - 1) == 0
    nf = D_FF // FFN_TF
    n_cp = (tm // seq_len) * (FFN_CONV - 1)
    full = pl.BlockSpec((tm, D_MODEL), lambda i, f: (i, 0))
    full_in = pl.BlockSpec((tm, D_MODEL), lambda i, f: (i, 0), pipeline_mode=pl.Buffered(1))
    lo = pl.BlockSpec((tm, FFN_TF), lambda i, f: (i, f))
    return pl.pallas_call(
        functools.partial(_ffn_s_kernel, seq_len=seq_len),
        grid=(m // tm, nf),
        in_specs=[
            full_in,
            pl.BlockSpec((n_cp, FFN_TF), lambda i, f: (i, f)),
            pl.BlockSpec((n_cp, FFN_TF), lambda i, f: (i, f + nf)),
            pl.BlockSpec((1, D_MODEL), lambda i, f: (0, 0)),
            full_in, full_in, full_in,
            pl.BlockSpec((D_MODEL, FFN_TF), lambda i, f: (0, f)),
            pl.BlockSpec((D_MODEL, FFN_TF), lambda i, f: (0, f + nf)),
            pl.BlockSpec((FFN_CONV, FFN_TF), lambda i, f: (0, f)),
            pl.BlockSpec((FFN_CONV, FFN_TF), lambda i, f: (0, f + nf)),
            pl.BlockSpec((FFN_TF, D_MODEL), lambda i, f: (f, 0)),
            pl.BlockSpec((1, D_MODEL), lambda i, f: (0, 0)),
        ],
        out_specs=[full, lo, lo],
        out_shape=[
            jax.ShapeDtypeStruct((m, D_MODEL), F32),
            jax.ShapeDtypeStruct((m, D_FF), F32),
            jax.ShapeDtypeStruct((m, D_FF), F32),
        ],
        scratch_shapes=[pltpu.VMEM((tm, D_MODEL), BF16), pltpu.VMEM((tm, D_MODEL), F32)],
        compiler_params=_cparams("arbitrary", "arbitrary"),
        name="conv_ffn_sample",
    )(x, conv_prev, conv_prev, g, sc, sh, gt, w_up, w_up, w_conv, w_conv, w_down, gp)


def _seq_tile(t, cap):
    tm = min(t, cap)
    assert t % tm == 0
    return tm


@jax.jit
def _forward(x_prompt, x_sample, c_prompt, c_sample, cache_k, cache_v, state_wkv, state_shift,
             state_ffn_conv, w_ada, b_ada, g_pre_mix, g_post_mix, g_pre_ffn, g_post_ffn, w_in,
             lam_q1, lam_k1, lam_q2, lam_k2, g_subln, mu_shift, w0, w_w2, a0, w_a2, w_g2, k_k, k_a,
             r_k, ln_x_w, ln_x_b, w_out, w_up, w_conv_ffn, w_down):
    bp, t, _ = x_prompt.shape
    bs, ts, _ = x_sample.shape
    past = cache_k.shape[2]
    assert w_ada.shape[0] == 1, "single-layer problem"

    n_c = bp + bs
    n_c_pad = -(-n_c // SUBLANES) * SUBLANES
    c_all = jnp.concatenate([c_prompt, c_sample, jnp.zeros((n_c_pad - n_c, D_MODEL), F32)], axis=0)
    mod = _ada_call(c_all, w_ada[0], b_ada)
    sh_m, sc_m, gt_m, sh_f, sc_f, gt_f = [mod[:, i * D_MODEL:(i + 1) * D_MODEL] for i in range(6)]
    p_mod = lambda a: a[:bp].reshape(bp, 1, D_MODEL)
    s_mod = lambda a: jnp.repeat(a[bp:n_c], ts, axis=0)

    w_in_b = w_in[0].astype(BF16)
    w_out_b = w_out[0].astype(BF16)
    w_up_b = w_up[0].astype(BF16)
    w_down_b = w_down[0].astype(BF16)
    lams = (lam_q1, lam_k1, lam_q2, lam_k2)
    prm = {
        "mu_shift": mu_shift, "w0": w0, "a0": a0, "k_k": k_k, "k_a": k_a,
        "r_k": r_k.reshape(1, B_WIDTH), "ln_x_w": ln_x_w, "ln_x_b": ln_x_b,
        "w_w2": w_w2[0].astype(BF16), "w_a2": w_a2[0].astype(BF16), "w_g2": w_g2[0].astype(BF16),
    }

    tm = _seq_tile(t, 512)
    tps = t // tm
    xp = x_prompt.reshape(bp * t, D_MODEL)
    kf, vf, qkvb, rw = _in_proj_call(xp, g_pre_mix, p_mod(sc_m), p_mod(sh_m), w_in_b, tm, t)
    oa = _pattn_call(lams, qkvb.reshape(bp, t, 3 * A_WIDTH), g_subln, _seq_tile(t, 512))
    ob, s_last = _rwkv_call(rw.reshape(bp, t, RW_COLS), jnp.zeros((bp, 1, RW_COLS), F32),
                            jnp.zeros((bp, N_GROUPS, MXU_DIM, B_HEAD), state_wkv.dtype), prm,
                            _seq_tile(t, 256), 32, 2 if bp % 2 == 0 else 1)
    x1 = _mixout_call(oa.reshape(bp * t, A_WIDTH), ob.reshape(bp * t, B_WIDTH), xp, w_out_b,
                      g_post_mix, p_mod(gt_m), tm, tps)
    yp, clg, clv = _ffn_call(x1, jnp.zeros((bp, FFN_CONV - 1, 2 * D_FF), F32), g_pre_ffn, p_mod(sc_f),
                             p_mod(sh_f), p_mod(gt_f), w_up_b, w_conv_ffn[0], w_down_b, g_post_ffn,
                             tm, tps)
    rw3 = rw.reshape(bp, t, RW_COLS)
    out_prompt = (
        yp.reshape(bp, t, D_MODEL),
        kf.reshape(1, bp, t, A_HEADS, 2 * A_DK),
        vf.reshape(1, bp, t, A_HEADS, A_DV),
        _groups_to_state(s_last)[None],
        rw3[:, -1][None],
        jnp.concatenate([clg[tps - 1::tps, -(FFN_CONV - 1):], clv[tps - 1::tps, -(FFN_CONV - 1):]],
                        axis=-1)[None],
    )

    ms = bs * ts
    xs = x_sample.reshape(ms, D_MODEL)
    kf_s, vf_s, qkvb_s, rw_s = _in_proj_call(xs, g_pre_mix, s_mod(sc_m), s_mod(sh_m), w_in_b, ms, ms)
    oa_s = _sattn_call(lams, qkvb_s.reshape(bs, ts, 3 * A_WIDTH),
                       cache_k[0].reshape(bs, past * A_HEADS, A_DV),
                       cache_v[0].reshape(bs, past * A_HEADS, A_DV), g_subln, _seq_tile(past, 1024))
    ob_s, s_last_s = _rwkv_call(rw_s.reshape(bs, ts, RW_COLS), state_shift[0][:, None, :],
                                _state_to_groups(state_wkv[0]), prm, ts, ts, 4 if bs % 4 == 0 else 1)
    x1_s = _mixout_call(oa_s.reshape(ms, A_WIDTH), ob_s.reshape(ms, B_WIDTH), xs, w_out_b,
                        g_post_mix, s_mod(gt_m), ms, 1)
    cprev = state_ffn_conv[0].reshape(bs * (FFN_CONV - 1), 2 * D_FF)
    ys, ug, uv = _ffn_s_call(x1_s, cprev, g_pre_ffn, s_mod(sc_f), s_mod(sh_f), s_mod(gt_f),
                             w_up_b, w_conv_ffn[0], w_down_b, g_post_ffn, ts, _seq_tile(ms, 512))
    u_all = jnp.concatenate([ug, uv], axis=-1).reshape(bs, ts, 2 * D_FF)
    out_sample = (
        ys.reshape(bs, ts, D_MODEL),
        kf_s.reshape(1, bs, ts, A_HEADS, 2 * A_DK),
        vf_s.reshape(1, bs, ts, A_HEADS, A_DV),
        _groups_to_state(s_last_s)[None],
        rw_s.reshape(bs, ts, RW_COLS)[:, -1][None],
        u_all[:, -(FFN_CONV - 1):][None],
    )
    return (out_prompt[0], out_sample[0]) + out_prompt[1:] + out_sample[1:]


def kernel(x_prompt, x_sample, c_prompt, c_sample, cache_k, cache_v, state_wkv, state_shift, state_ffn_conv, w_ada, b_ada, g_pre_mix, g_post_mix, g_pre_ffn, g_post_ffn, w_in, lam_q1, lam_k1, lam_q2, lam_k2, g_subln, mu_shift, w0, w_w2, a0, w_a2, w_g2, k_k, k_a, r_k, ln_x_w, ln_x_b, w_out, w_up, w_conv_ffn, w_down):
    return _forward(x_prompt, x_sample, c_prompt, c_sample, cache_k, cache_v, state_wkv, state_shift,
                    state_ffn_conv, w_ada, b_ada, g_pre_mix, g_post_mix, g_pre_ffn, g_post_ffn, w_in,
                    lam_q1, lam_k1, lam_q2, lam_k2, g_subln, mu_shift, w0, w_w2, a0, w_a2, w_g2, k_k,
                    k_a, r_k, ln_x_w, ln_x_b, w_out, w_up, w_conv_ffn, w_down)
```

```python
import functools
import math

import jax
import jax.numpy as jnp
from jax import lax
from jax.experimental import pallas as pl
from jax.experimental.pallas import tpu as pltpu

F32 = jnp.float32
BF16 = jnp.bfloat16

D_MODEL = 2048
A_HEADS = 8
A_DK = 64
A_DV = 128
A_WIDTH = A_HEADS * A_DV
B_HEAD = 64
B_WIDTH = D_MODEL - A_WIDTH
B_HEADS = B_WIDTH // B_HEAD
W_RANK = 64
A_RANK = 64
G_RANK = 128
RW_COLS = 3 * B_WIDTH + W_RANK + A_RANK + G_RANK
D_FF = 5632
FFN_CONV = 3
ATTN_CHUNK = 64
EPS = 1e-6
GN_EPS = 64e-5
LAM_INIT = 0.8 - 0.6 * math.exp(-0.3 * 0)
Q_SCALE = (A_DK ** -0.5) * math.log2(math.e)

LANES = 128
SUBLANES = 8
BF16_ROWS = 16
MXU_DIM = 256
GROUP_HEADS = MXU_DIM // B_HEAD
N_GROUPS = B_WIDTH // MXU_DIM
VMEM_LIMIT_BYTES = 56 * 1024 * 1024
NEG_BIG = -1e30


def _cparams(*sem):
    return pltpu.CompilerParams(dimension_semantics=sem, vmem_limit_bytes=VMEM_LIMIT_BYTES)


def _dot(a, b):
    return jnp.dot(a, b, preferred_element_type=F32)


def _dot_nt(a, b):
    return lax.dot_general(a, b, (((1,), (1,)), ((), ())), preferred_element_type=F32)


def _dot_tn(a, b):
    return lax.dot_general(a, b, (((0,), (0,)), ((), ())), preferred_element_type=F32)


def _idiv(x, n):
    assert n > 0 and n & (n - 1) == 0
    return x >> (n.bit_length() - 1)


def _sigmoid(x):
    return 1.0 / (1.0 + jnp.exp(-x))


def _split_bf16(x, terms):
    out = []
    rem = x
    for _ in range(terms):
        piece = rem.astype(BF16)
        out.append(piece)
        rem = rem - piece.astype(F32)
    return out


def _norm_mod(x, g, sc, sh):
    ms = jnp.mean(x * x, axis=-1, keepdims=True)
    return (x * lax.rsqrt(ms + EPS) * g) * (1.0 + sc) + sh


def _rms(x, g):
    ms = jnp.mean(x * x, axis=-1, keepdims=True)
    return x * lax.rsqrt(ms + EPS) * g


def _mod(ref):
    return ref[0] if len(ref.shape) == 3 else ref[...]


def _lam(q1_ref, k1_ref, q2_ref, k2_ref):
    s1 = jnp.sum(q1_ref[...] * k1_ref[...], axis=-1, keepdims=True)
    s2 = jnp.sum(q2_ref[...] * k2_ref[...], axis=-1, keepdims=True)
    return jnp.exp(s1) - jnp.exp(s2) + LAM_INIT


def _ada_kernel(c_ref, w_ref, b_ref, o_ref):
    c = c_ref[...]
    s = (c * _sigmoid(c)).astype(BF16)
    o_ref[...] = _dot(s, w_ref[...].astype(BF16)) + b_ref[...]


def _ada_call(c_all, w_ada, b_ada):
    rows = c_all.shape[0]
    n = w_ada.shape[1]
    tn = 1024
    return pl.pallas_call(
        _ada_kernel,
        grid=(n // tn,),
        in_specs=[
            pl.BlockSpec((rows, D_MODEL), lambda j: (0, 0)),
            pl.BlockSpec((D_MODEL, tn), lambda j: (0, j)),
            pl.BlockSpec((1, tn), lambda j: (0, j)),
        ],
        out_specs=pl.BlockSpec((rows, tn), lambda j: (0, j)),
        out_shape=jax.ShapeDtypeStruct((rows, n), F32),
        compiler_params=_cparams("arbitrary"),
        name="ada_mod",
    )(c_all, w_ada, b_ada)


def _qkv_kernel(x_ref, g_ref, sc_ref, sh_ref, w_ref, kf_ref, vf_ref, qkvb_ref):
    h = _norm_mod(x_ref[...], g_ref[...], _mod(sc_ref), _mod(sh_ref)).astype(BF16)
    q = _dot(h, w_ref[:, 0:A_WIDTH])
    qkvb_ref[:, 0:A_WIDTH] = (q * Q_SCALE).astype(BF16)
    k = _dot(h, w_ref[:, A_WIDTH:2 * A_WIDTH])
    kf_ref[...] = k
    qkvb_ref[:, A_WIDTH:2 * A_WIDTH] = k.astype(BF16)
    v = _dot(h, w_ref[:, 2 * A_WIDTH:3 * A_WIDTH])
    vf_ref[...] = v
    qkvb_ref[:, 2 * A_WIDTH:3 * A_WIDTH] = v.astype(BF16)


def _rw_kernel(x_ref, g_ref, sc_ref, sh_ref, w_ref, o_ref):
    h = _norm_mod(x_ref[...], g_ref[...], _mod(sc_ref), _mod(sh_ref)).astype(BF16)
    o_ref[...] = _dot(h, w_ref[:, 3 * A_WIDTH:])


def _mod_spec(mod, tm, tiles_per_seq, nidx):
    if mod.ndim == 3:
        if nidx == 2:
            return pl.BlockSpec((1, 1, D_MODEL), lambda i, j: (i // tiles_per_seq, 0, 0))
        return pl.BlockSpec((1, 1, D_MODEL), lambda i: (i // tiles_per_seq, 0, 0))
    if nidx == 2:
        return pl.BlockSpec((tm, D_MODEL), lambda i, j: (i, 0))
    return pl.BlockSpec((tm, D_MODEL), lambda i: (i, 0))


def _resident(shape):
    return pl.BlockSpec(shape, lambda i: (0,) * len(shape), pipeline_mode=pl.Buffered(1))


def _in_proj_call(x, g, sc, sh, w_in, tm, seq_len):
    m = x.shape[0]
    n_in = w_in.shape[1]
    common = [
        pl.BlockSpec((tm, D_MODEL), lambda i: (i, 0)),
        _resident((1, D_MODEL)),
        _mod_spec(sc, tm, seq_len // tm, 1),
        _mod_spec(sh, tm, seq_len // tm, 1),
    ]
    kf, vf, qkvb = pl.pallas_call(
        _qkv_kernel,
        grid=(m // tm,),
        in_specs=common + [_resident((D_MODEL, n_in))],
        out_specs=[
            pl.BlockSpec((tm, A_WIDTH), lambda i: (i, 0)),
            pl.BlockSpec((tm, A_WIDTH), lambda i: (i, 0)),
            pl.BlockSpec((tm, 3 * A_WIDTH), lambda i: (i, 0)),
        ],
        out_shape=[
            jax.ShapeDtypeStruct((m, A_WIDTH), F32),
            jax.ShapeDtypeStruct((m, A_WIDTH), F32),
            jax.ShapeDtypeStruct((m, 3 * A_WIDTH), BF16),
        ],
        compiler_params=_cparams("arbitrary"),
        name="in_proj_qkv",
    )(x, g, sc, sh, w_in)
    rw = pl.pallas_call(
        _rw_kernel,
        grid=(m // tm,),
        in_specs=common + [_resident((D_MODEL, n_in))],
        out_specs=pl.BlockSpec((tm, RW_COLS), lambda i: (i, 0)),
        out_shape=jax.ShapeDtypeStruct((m, RW_COLS), F32),
        compiler_params=_cparams("arbitrary"),
        name="in_proj_rw",
    )(x, g, sc, sh, w_in)
    return kf, vf, qkvb, rw


def _softmax_step(s, vb, m_ref, l_ref, acc_ref, idx):
    m_prev = m_ref[idx]
    m_new = jnp.maximum(m_prev, jnp.max(s, axis=-1, keepdims=True))
    alpha = jnp.exp2(m_prev - m_new)
    p = jnp.exp2(s - m_new)
    l_ref[idx] = alpha * l_ref[idx] + jnp.sum(p, axis=-1, keepdims=True)
    acc_ref[idx] = alpha * acc_ref[idx] + _dot(p.astype(BF16), vb)
    m_ref[idx] = m_new


def _subln(o, g):
    return _rms(o, g) * (1.0 - LAM_INIT)


PATTN_HEADS = 2
PATTN_QG = 512


def _pattn_kernel(q1_ref, k1_ref, q2_ref, k2_ref, q_ref, k_ref, v_ref, g_ref, o_ref,
                  m_scr, l_scr, acc_scr, *, tq):
    i = pl.program_id(2)
    lane = lax.broadcasted_iota(jnp.int32, (tq, A_DV), 1)
    m_scr[...] = jnp.full(m_scr.shape, NEG_BIG, F32)
    l_scr[...] = jnp.zeros(l_scr.shape, F32)
    acc_scr[...] = jnp.zeros(acc_scr.shape, F32)

    qg = min(PATTN_QG, 2 * tq)
    n_qg = 2 * tq // qg
    q_gs = []
    for hh in range(PATTN_HEADS):
        q = q_ref[0, :, hh * A_DV:(hh + 1) * A_DV]
        zero = jnp.zeros_like(q)
        q_st = jnp.concatenate([jnp.where(lane < A_DK, q, zero), jnp.where(lane >= A_DK, q, zero)], axis=0)
        q_gs.append([q_st[g * qg:(g + 1) * qg] for g in range(n_qg)])
    ones_k = jnp.ones((SUBLANES, tq), BF16)
    chains = [(hh, g) for g in range(n_qg) for hh in range(PATTN_HEADS)]

    def block(j, masks):
        r0 = pl.multiple_of(j * tq, tq)
        kbs = [k_ref[0, pl.ds(r0, tq), hh * A_DV:(hh + 1) * A_DV] for hh in range(PATTN_HEADS)]
        vbs = [v_ref[0, pl.ds(r0, tq), hh * A_DV:(hh + 1) * A_DV] for hh in range(PATTN_HEADS)]
        s_ts = {(hh, g): _dot_nt(kbs[hh], q_gs[hh][g]) for hh, g in chains}
        for hh, g in chains:
            cs = slice(g * qg, (g + 1) * qg)
            s_t = s_ts[(hh, g)]
            if masks is not None:
                s_t = jnp.where(masks[g], s_t, NEG_BIG)
            m_prev = m_scr[hh, :, cs]
            m_new = jnp.maximum(m_prev, jnp.max(s_t, axis=0, keepdims=True))
            alpha = jnp.exp2(m_prev - m_new)
            p_b = jnp.exp2(s_t - m_new).astype(BF16)
            l_scr[hh, :, cs] = alpha * l_scr[hh, :, cs] + _dot(ones_k, p_b)[0:1]
            acc_scr[hh, :, cs] = alpha * acc_scr[hh, :, cs] + _dot_tn(vbs[hh], p_b)
            m_scr[hh, :, cs] = m_new

    def body(j, carry):
        block(j, None)
        return carry

    lax.fori_loop(0, i, body, 0)
    rk = lax.broadcasted_iota(jnp.int32, (tq, qg), 0)
    cq = lax.broadcasted_iota(jnp.int32, (tq, qg), 1)
    block(i, [_idiv(rk, ATTN_CHUNK) <= _idiv((cq + g * qg) & (tq - 1), ATTN_CHUNK)
              for g in range(n_qg)])

    lam = _lam(q1_ref, k1_ref, q2_ref, k2_ref)
    for hh in range(PATTN_HEADS):
        o_t = (acc_scr[hh, :, 0:tq] / l_scr[hh, :, 0:tq]
               - lam * (acc_scr[hh, :, tq:2 * tq] / l_scr[hh, :, tq:2 * tq]))
        o_ref[0, :, hh * A_DV:(hh + 1) * A_DV] = _subln(o_t.T, g_ref[...]).astype(BF16)


def _lam_specs(nidx):
    if nidx == 3:
        return [pl.BlockSpec((1, A_DK), lambda b, h, i: (0, 0))] * 4
    return [pl.BlockSpec((1, A_DK), lambda b, j: (0, 0))] * 4


def _pattn_call(lams, qkvb, g_subln, tq):
    bsz, t, _ = qkvb.shape
    hw = PATTN_HEADS * A_DV
    ng = A_HEADS // PATTN_HEADS
    return pl.pallas_call(
        functools.partial(_pattn_kernel, tq=tq),
        grid=(bsz, ng, t // tq),
        in_specs=_lam_specs(3) + [
            pl.BlockSpec((1, tq, hw), lambda b, h, i: (b, i, h)),
            pl.BlockSpec((1, t, hw), lambda b, h, i: (b, 0, ng + h)),
            pl.BlockSpec((1, t, hw), lambda b, h, i: (b, 0, 2 * ng + h)),
            pl.BlockSpec((1, A_DV), lambda b, h, i: (0, 0)),
        ],
        out_specs=pl.BlockSpec((1, tq, hw), lambda b, h, i: (b, i, h)),
        out_shape=jax.ShapeDtypeStruct((bsz, t, A_WIDTH), BF16),
        scratch_shapes=[
            pltpu.VMEM((PATTN_HEADS, 1, 2 * tq), F32),
            pltpu.VMEM((PATTN_HEADS, 1, 2 * tq), F32),
            pltpu.VMEM((PATTN_HEADS, A_DV, 2 * tq), F32),
        ],
        compiler_params=_cparams("arbitrary", "arbitrary", "arbitrary"),
        name="prompt_attention",
    )(*lams, qkvb, qkvb, qkvb, g_subln)


N_VQ = 2 * A_HEADS


def _sattn_kernel(q1_ref, k1_ref, q2_ref, k2_ref, q_ref, kn_ref, vn_ref, ck_ref, cv_ref, g_ref, o_ref,
                  qb_scr, m_scr, l_scr, acc_scr, *, tnew):
    j = pl.program_id(1)
    nvq = N_VQ * tnew

    @pl.when(j == 0)
    def _():
        q = q_ref[0]
        qt = jnp.concatenate([q] * N_VQ, axis=0)
        row = lax.broadcasted_iota(jnp.int32, (nvq, A_WIDTH), 0)
        col = lax.broadcasted_iota(jnp.int32, (nvq, A_WIDTH), 1)
        qb_scr[...] = jnp.where(_idiv(row, tnew) == _idiv(col, A_DK), qt, jnp.zeros_like(qt))
        m_scr[...] = jnp.full(m_scr.shape, NEG_BIG, F32)
        l_scr[...] = jnp.zeros(l_scr.shape, F32)
        acc_scr[...] = jnp.zeros(acc_scr.shape, F32)

    def step(kb, vb):
        _softmax_step(_dot_nt(qb_scr[...], kb), vb, m_scr, l_scr, acc_scr, 0)

    def cached_rows(ref, p0, n):
        heads = [ref[0, pl.ds(p0 * A_HEADS + h, n, stride=A_HEADS), :] for h in range(A_HEADS)]
        return jnp.concatenate(heads, axis=-1).astype(BF16)

    tk = ck_ref.shape[1] // A_HEADS
    n_sub = max(1, tk // MXU_DIM)
    sub = tk // n_sub
    kbs = [cached_rows(ck_ref, c * sub, sub) for c in range(n_sub)]
    s_all = [_dot_nt(qb_scr[...], kb) for kb in kbs]
    for c in range(n_sub):
        _softmax_step(s_all[c], cached_rows(cv_ref, c * sub, sub), m_scr, l_scr, acc_scr, 0)

    @pl.when(j == pl.num_programs(1) - 1)
    def _():
        step(kn_ref[0], vn_ref[0])
        lam = _lam(q1_ref, k1_ref, q2_ref, k2_ref)
        g = g_ref[...]
        for h in range(A_HEADS):
            cs = slice(h * A_DV, (h + 1) * A_DV)
            r0 = slice((2 * h) * tnew, (2 * h + 1) * tnew)
            r1 = slice((2 * h + 1) * tnew, (2 * h + 2) * tnew)
            o0 = acc_scr[0, r0, cs] / l_scr[0, r0, :]
            o1 = acc_scr[0, r1, cs] / l_scr[0, r1, :]
            o_ref[0, :, cs] = _subln(o0 - lam * o1, g).astype(BF16)


def _sattn_call(lams, qkvb, cache_k, cache_v, g_subln, tk):
    bsz, tnew, _ = qkvb.shape
    past = cache_k.shape[1] // A_HEADS
    nvq = N_VQ * tnew
    return pl.pallas_call(
        functools.partial(_sattn_kernel, tnew=tnew),
        grid=(bsz, past // tk),
        in_specs=_lam_specs(2) + [
            pl.BlockSpec((1, tnew, A_WIDTH), lambda b, j: (b, 0, 0)),
            pl.BlockSpec((1, tnew, A_WIDTH), lambda b, j: (b, 0, 1)),
            pl.BlockSpec((1, tnew, A_WIDTH), lambda b, j: (b, 0, 2)),
            pl.BlockSpec((1, tk * A_HEADS, A_DV), lambda b, j: (b, j, 0)),
            pl.BlockSpec((1, tk * A_HEADS, A_DV), lambda b, j: (b, j, 0)),
            pl.BlockSpec((1, A_DV), lambda b, j: (0, 0)),
        ],
        out_specs=pl.BlockSpec((1, tnew, A_WIDTH), lambda b, j: (b, 0, 0)),
        out_shape=jax.ShapeDtypeStruct((bsz, tnew, A_WIDTH), BF16),
        scratch_shapes=[
            pltpu.VMEM((nvq, A_WIDTH), BF16),
            pltpu.VMEM((1, nvq, 1), F32),
            pltpu.VMEM((1, nvq, 1), F32),
            pltpu.VMEM((1, nvq, A_WIDTH), F32),
        ],
        compiler_params=_cparams("arbitrary", "arbitrary"),
        name="sample_attention",
    )(*lams, qkvb, qkvb, qkvb, cache_k, cache_v, g_subln)


def _head_sum(x, ones_bd):
    return _dot(x.astype(BF16), ones_bd)


def _rwkv_kernel(p_ref, ph_ref, st_ref, s0_ref, mu_ref, w0_ref, a0_ref, kk_ref, ka_ref, rk_ref,
                 lnw_ref, lnb_ref, ww2_ref, wa2_ref, wg2_ref, ob_ref, so_ref,
                 r_scr, k_scr, v_scr, na_scr, nb_scr, ld_scr, cum_scr, g_scr, bon_scr, y_scr, s_scr,
                 *, tb, chunk):
    i = pl.program_id(1)
    n_seq = p_ref.shape[0]
    c_len = chunk
    n_chunks = tb // c_len

    row_g = lax.broadcasted_iota(jnp.int32, (MXU_DIM, MXU_DIM), 0)
    col_g = lax.broadcasted_iota(jnp.int32, (MXU_DIM, MXU_DIM), 1)
    bd_mask = _idiv(row_g, B_HEAD) == _idiv(col_g, B_HEAD)
    ones_bd = jnp.where(bd_mask, 1.0, 0.0).astype(BF16)

    @pl.when(i == 0)
    def _():
        rep = jnp.where(
            lax.broadcasted_iota(jnp.int32, (B_HEAD, MXU_DIM), 0)
            == (lax.broadcasted_iota(jnp.int32, (B_HEAD, MXU_DIM), 1) & (B_HEAD - 1)), 1.0, 0.0).astype(BF16)
        for b in range(n_seq):
            for q in range(N_GROUPS):
                wide = None
                for piece in _split_bf16(s0_ref[b, q], 3):
                    d = _dot(piece, rep)
                    wide = d if wide is None else wide + d
                s_scr[b, q] = jnp.where(bd_mask, wide, 0.0)

    row_t = lax.broadcasted_iota(jnp.int32, (tb, tb), 0)
    col_t = lax.broadcasted_iota(jnp.int32, (tb, tb), 1)
    tri = jnp.where((_idiv(row_t, c_len) == _idiv(col_t, c_len)) & (col_t <= row_t), 1.0, 0.0).astype(BF16)
    w_cat = jnp.concatenate([ww2_ref[...], wa2_ref[...]], axis=0)

    for b in range(n_seq):
        p = p_ref[b]
        prev_row = jnp.where(i == 0, st_ref[b], ph_ref[b, SUBLANES - 1:SUBLANES, :])
        rowi = lax.broadcasted_iota(jnp.int32, p.shape, 0)
        p_prev = jnp.where(rowi == 0, prev_row, pltpu.roll(p, 1, 0))
        xs = p + (p_prev - p) * mu_ref[...]
        r = xs[:, 0:B_WIDTH]
        k = xs[:, B_WIDTH:2 * B_WIDTH]
        v = xs[:, 2 * B_WIDTH:3 * B_WIDTH]
        wa = xs[:, 3 * B_WIDTH:3 * B_WIDTH + W_RANK + A_RANK]
        gd = xs[:, 3 * B_WIDTH + W_RANK + A_RANK:]

        lane_wa = lax.broadcasted_iota(jnp.int32, wa.shape, 1)
        zero_wa = jnp.zeros_like(wa)
        w_lin = _dot(jnp.where(lane_wa < W_RANK, jnp.tanh(wa), zero_wa).astype(BF16), w_cat)
        a_lin = _dot(jnp.where(lane_wa >= W_RANK, wa, zero_wa).astype(BF16), w_cat)
        logd = -math.exp(-0.5) * _sigmoid(w0_ref[...] + w_lin)
        a = _sigmoid(a0_ref[...] + a_lin)
        g_scr[b] = _dot(_sigmoid(gd).astype(BF16), wg2_ref[...])

        k2 = k * (1.0 + (a - 1.0) * ka_ref[...])
        kk = k * kk_ref[...]
        rkk = r * k2 * rk_ref[...]
        for q in range(N_GROUPS):
            ls = slice(q * MXU_DIM, (q + 1) * MXU_DIM)
            kq = kk[:, ls]
            kn = kq * lax.rsqrt(jnp.maximum(_head_sum(kq * kq, ones_bd), 1e-24))
            na_scr[b, :, ls] = -kn
            nb_scr[b, :, ls] = kn * a[:, ls]
            bon_scr[b, :, ls] = _head_sum(rkk[:, ls], ones_bd) * v[:, ls]
        r_scr[b] = r
        k_scr[b] = k2
        v_scr[b] = v
        ld_scr[b] = logd

        cum = None
        for piece in _split_bf16(logd, 2):
            d = _dot(tri, piece)
            cum = d if cum is None else cum + d
        cum_scr[b] = cum

    gc = GROUP_HEADS * c_len
    lane_q = lax.broadcasted_iota(jnp.int32, (1, MXU_DIM), 1)
    head_masks = [_idiv(lane_q, B_HEAD) == j for j in range(GROUP_HEADS)]
    row_s = lax.broadcasted_iota(jnp.int32, (gc, gc), 0)
    col_s = lax.broadcasted_iota(jnp.int32, (gc, gc), 1)
    same_head = _idiv(row_s, c_len) == _idiv(col_s, c_len)
    strict = same_head & (col_s < row_s)
    incl = same_head & (col_s <= row_s)
    eye = jnp.where(col_s == row_s, 1.0, 0.0).astype(F32)
    n_double = int(math.log2(c_len)) - 1
    chains = [(b, q) for q in range(N_GROUPS) for b in range(n_seq)]
    groups = range(len(chains))

    def stack(x):
        z = jnp.zeros_like(x)
        return jnp.concatenate([jnp.where(head_masks[j], x, z) for j in range(GROUP_HEADS)], axis=0)

    def unstack(x):
        out = x[0:c_len]
        for j in range(1, GROUP_HEADS):
            out = out + x[j * c_len:(j + 1) * c_len]
        return out

    def chunk_body(c, carry):
        r0 = pl.multiple_of(c * c_len, c_len)
        rows = pl.ds(r0, c_len)
        a_s, r_s, b_s, k_s, be_s, ke_s, v_s, dec = [], [], [], [], [], [], [], []
        for b, q in chains:
            ls = slice(q * MXU_DIM, (q + 1) * MXU_DIM)
            cum_q = cum_scr[b, rows, ls]
            k_q = k_scr[b, rows, ls]
            nb_q = nb_scr[b, rows, ls]
            cum_last = cum_q[c_len - 1:c_len, :]
            e_out = jnp.exp(-cum_q)
            e_end = jnp.exp(cum_last - cum_q)
            a_s.append(stack((na_scr[b, rows, ls] * jnp.exp(cum_q - ld_scr[b, rows, ls])).astype(BF16)))
            r_s.append(stack((r_scr[b, rows, ls] * jnp.exp(cum_q)).astype(BF16)))
            b_s.append(stack((nb_q * e_out).astype(BF16)))
            k_s.append(stack((k_q * e_out).astype(BF16)))
            be_s.append(stack((nb_q * e_end).astype(BF16)))
            ke_s.append(stack((k_q * e_end).astype(BF16)))
            v_s.append(stack(v_scr[b, rows, ls].astype(BF16)))
            dec.append(jnp.exp(cum_last))

        if gc % LANES == 0:
            g_all = [_dot_nt(jnp.concatenate([a_s[q], r_s[q]], axis=0),
                             jnp.concatenate([b_s[q], k_s[q]], axis=0)) for q in groups]
            g_ab = [g[0:gc, 0:gc] for g in g_all]
            g_ak = [g[0:gc, gc:2 * gc] for g in g_all]
            g_rb = [g[gc:2 * gc, 0:gc] for g in g_all]
            g_rk = [g[gc:2 * gc, gc:2 * gc] for g in g_all]
        else:
            g_ab = [_dot_nt(a_s[q], b_s[q]) for q in groups]
            g_ak = [_dot_nt(a_s[q], k_s[q]) for q in groups]
            g_rb = [_dot_nt(r_s[q], b_s[q]) for q in groups]
            g_rk = [_dot_nt(r_s[q], k_s[q]) for q in groups]
        aab = [jnp.where(strict, g, 0.0) for g in g_ab]
        aak = [jnp.where(strict, g, 0.0).astype(BF16) for g in g_ak]
        arb = [jnp.where(incl, g, 0.0).astype(BF16) for g in g_rb]
        ark = [jnp.where(incl, g, 0.0).astype(BF16) for g in g_rk]

        t_m = [eye + x for x in aab]
        pw = [x.astype(BF16) for x in aab]
        pw = [_dot(x, x) for x in pw]
        for m in range(n_double):
            pb = [x.astype(BF16) for x in pw]
            if m + 1 < n_double:
                pw = [_dot(x, x) for x in pb]
            t_m = [t_m[q] + _dot(t_m[q].astype(BF16), pb[q]) for q in groups]
        t_b = [x.astype(BF16) for x in t_m]

        av = [_dot(aak[q], v_s[q]) for q in groups]

        s_prev = [s_scr[b, q] for b, q in chains]
        ar = [_dot_nt(jnp.concatenate([a_s[q], r_s[q]], axis=0), s_prev[q].astype(BF16)) for q in groups]
        u_b = [_dot(t_b[q], (ar[q][0:gc] + av[q]).astype(BF16)).astype(BF16) for q in groups]
        uv = [jnp.concatenate([u_b[q], v_s[q]], axis=0) for q in groups]
        if gc % LANES == 0:
            y_st = [ar[q][gc:2 * gc] + _dot(jnp.concatenate([arb[q], ark[q]], axis=1), uv[q])
                    for q in groups]
        else:
            y_st = [ar[q][gc:2 * gc] + _dot(arb[q], u_b[q]) + _dot(ark[q], v_s[q]) for q in groups]
        s_new = [s_prev[q] * dec[q]
                 + _dot_tn(uv[q], jnp.concatenate([be_s[q], ke_s[q]], axis=0)) for q in groups]
        for ci, (b, q) in enumerate(chains):
            s_scr[b, q] = s_new[ci]
            y_scr[b, rows, q * MXU_DIM:(q + 1) * MXU_DIM] = unstack(y_st[ci])
        return carry

    lax.fori_loop(0, n_chunks, chunk_body, 0)

    @pl.when(i == pl.num_programs(1) - 1)
    def _():
        rep_t = jnp.where(
            (lax.broadcasted_iota(jnp.int32, (MXU_DIM, B_HEAD), 0) & (B_HEAD - 1))
            == lax.broadcasted_iota(jnp.int32, (MXU_DIM, B_HEAD), 1), 1.0, 0.0).astype(BF16)
        for b, q in chains:
            narrow = None
            for piece in _split_bf16(s_scr[b, q], 3):
                d = _dot(piece, rep_t)
                narrow = d if narrow is None else narrow + d
            so_ref[b, q] = narrow

    inv_n = 1.0 / B_HEAD
    for b, q in chains:
        ls = slice(q * MXU_DIM, (q + 1) * MXU_DIM)
        y = y_scr[b, :, ls]
        mu = _head_sum(y, ones_bd) * inv_n
        d = y - mu
        var = _head_sum(d * d, ones_bd) * inv_n
        yn = d * lax.rsqrt(var + GN_EPS) * lnw_ref[:, ls] + lnb_ref[:, ls]
        ob_ref[b, :, ls] = ((yn + bon_scr[b, :, ls]) * g_scr[b, :, ls]).astype(BF16)


def _rwkv_call(rw, shift0, s0, prm, tb, chunk, n_seq):
    bsz, t, _ = rw.shape
    assert bsz % n_seq == 0
    nt = t // tb
    hb = tb // SUBLANES
    vec = lambda n: pl.BlockSpec((1, n), lambda b, i: (0, 0))
    return pl.pallas_call(
        functools.partial(_rwkv_kernel, tb=tb, chunk=chunk),
        grid=(bsz // n_seq, nt),
        in_specs=[
            pl.BlockSpec((n_seq, tb, RW_COLS), lambda b, i: (b, i, 0)),
            pl.BlockSpec((n_seq, SUBLANES, RW_COLS), lambda b, i: (b, jnp.maximum(i * hb - 1, 0), 0)),
            pl.BlockSpec((n_seq, 1, RW_COLS), lambda b, i: (b, 0, 0)),
            pl.BlockSpec((n_seq, N_GROUPS, MXU_DIM, B_HEAD), lambda b, i: (b, 0, 0, 0)),
            vec(RW_COLS), vec(B_WIDTH), vec(B_WIDTH), vec(B_WIDTH), vec(B_WIDTH), vec(B_WIDTH),
            vec(B_WIDTH), vec(B_WIDTH),
            pl.BlockSpec((W_RANK, B_WIDTH), lambda b, i: (0, 0)),
            pl.BlockSpec((A_RANK, B_WIDTH), lambda b, i: (0, 0)),
            pl.BlockSpec((G_RANK, B_WIDTH), lambda b, i: (0, 0)),
        ],
        out_specs=[
            pl.BlockSpec((n_seq, tb, B_WIDTH), lambda b, i: (b, i, 0)),
            pl.BlockSpec((n_seq, N_GROUPS, MXU_DIM, B_HEAD), lambda b, i: (b, 0, 0, 0)),
        ],
        out_shape=[
            jax.ShapeDtypeStruct((bsz, t, B_WIDTH), BF16),
            jax.ShapeDtypeStruct((bsz, N_GROUPS, MXU_DIM, B_HEAD), F32),
        ],
        scratch_shapes=[pltpu.VMEM((n_seq, tb, B_WIDTH), F32)] * 10
        + [pltpu.VMEM((n_seq, N_GROUPS, MXU_DIM, MXU_DIM), F32)],
        compiler_params=_cparams("arbitrary", "arbitrary"),
        name="rwkv7_mix",
    )(rw, rw, shift0, s0, prm["mu_shift"], prm["w0"], prm["a0"], prm["k_k"], prm["k_a"], prm["r_k"],
      prm["ln_x_w"], prm["ln_x_b"], prm["w_w2"], prm["w_a2"], prm["w_g2"])


def _state_to_groups(s):
    return s.reshape(s.shape[0], N_GROUPS, MXU_DIM, B_HEAD)


def _groups_to_state(sg):
    return sg.reshape(sg.shape[0], B_HEADS, B_HEAD, B_HEAD)


def _mixout_kernel(oa_ref, ob_ref, x_ref, w_ref, g_ref, gt_ref, o_ref):
    mix = _dot(oa_ref[...], w_ref[0:A_WIDTH, :]) + _dot(ob_ref[...], w_ref[A_WIDTH:D_MODEL, :])
    o_ref[...] = x_ref[...] + _mod(gt_ref) * _rms(mix, g_ref[...])


def _mixout_call(oa, ob, x, w_out, g, gt, tm, tiles_per_seq):
    m = x.shape[0]
    return pl.pallas_call(
        _mixout_kernel,
        grid=(m // tm,),
        in_specs=[
            pl.BlockSpec((tm, A_WIDTH), lambda i: (i, 0)),
            pl.BlockSpec((tm, B_WIDTH), lambda i: (i, 0)),
            pl.BlockSpec((tm, D_MODEL), lambda i: (i, 0)),
            pl.BlockSpec((D_MODEL, D_MODEL), lambda i: (0, 0)),
            pl.BlockSpec((1, D_MODEL), lambda i: (0, 0)),
            _mod_spec(gt, tm, tiles_per_seq, 1),
        ],
        out_specs=pl.BlockSpec((tm, D_MODEL), lambda i: (i, 0)),
        out_shape=jax.ShapeDtypeStruct((m, D_MODEL), F32),
        compiler_params=_cparams("arbitrary"),
        name="mix_out",
    )(oa, ob, x, w_out, g, gt)


FFN_HALO = BF16_ROWS
FFN_TF = 512


def _gated(zg, zv):
    return (zg * _sigmoid(zg) * zv).astype(BF16)


def _ffn_kernel(x_ref, xh_ref, cpg_ref, cpv_ref, g_ref, sc_ref, sh_ref, gt_ref, wg_ref, wv_ref,
                cg_ref, cv_ref, wd_ref, gp_ref, o_ref, clg_ref, clv_ref, h_scr, acc_scr,
                *, tm, tiles_per_seq):
    mi = pl.program_id(0)
    f = pl.program_id(1)
    first = (mi % tiles_per_seq) == 0

    @pl.when(f == 0)
    def _():
        g, sc, sh = g_ref[...], sc_ref[0], sh_ref[0]
        h_scr[FFN_HALO:, :] = _norm_mod(x_ref[...], g, sc, sh).astype(BF16)
        h_scr[:FFN_HALO, :] = _norm_mod(xh_ref[...], g, sc, sh).astype(BF16)
        acc_scr[...] = jnp.zeros(acc_scr.shape, F32)

    h = h_scr[...]
    row = lax.broadcasted_iota(jnp.int32, (FFN_HALO, FFN_TF), 0)

    def conv(u, cp_ref, c_ref, cl_ref):
        cp = cp_ref[0]
        head = u[:FFN_HALO]
        head_first = jnp.where(row == FFN_HALO - 2, cp[0:1],
                               jnp.where(row == FFN_HALO - 1, cp[1:2], head))
        ue = jnp.concatenate([jnp.where(first, head_first, head), u[FFN_HALO:]], axis=0)
        cl_ref[0] = ue[tm + FFN_HALO - SUBLANES:]
        c = c_ref[...]
        return (c[0:1] * pltpu.roll(ue, 2, 0)[FFN_HALO:] + c[1:2] * pltpu.roll(ue, 1, 0)[FFN_HALO:]
                + c[2:3] * ue[FFN_HALO:])

    zg = conv(_dot(h, wg_ref[...]), cpg_ref, cg_ref, clg_ref)
    zv = conv(_dot(h, wv_ref[...]), cpv_ref, cv_ref, clv_ref)
    acc_scr[...] += _dot(_gated(zg, zv), wd_ref[...])

    @pl.when(f == pl.num_programs(1) - 1)
    def _():
        o_ref[...] = x_ref[...] + gt_ref[0] * _rms(acc_scr[...], gp_ref[...])


def _ffn_call(x, conv_prev, g, sc, sh, gt, w_up, w_conv, w_down, gp, tm, tiles_per_seq):
    m = x.shape[0]
    nseq = conv_prev.shape[0]
    nf = D_FF // FFN_TF
    hb = tm // FFN_HALO
    seq = lambda i: i // tiles_per_seq
    modspec = pl.BlockSpec((1, 1, D_MODEL), lambda i, f: (seq(i), 0, 0))
    return pl.pallas_call(
        functools.partial(_ffn_kernel, tm=tm, tiles_per_seq=tiles_per_seq),
        grid=(m // tm, nf),
        in_specs=[
            pl.BlockSpec((tm, D_MODEL), lambda i, f: (i, 0)),
            pl.BlockSpec((FFN_HALO, D_MODEL), lambda i, f: (jnp.maximum(i * hb - 1, 0), 0)),
            pl.BlockSpec((1, FFN_CONV - 1, FFN_TF), lambda i, f: (seq(i), 0, f)),
            pl.BlockSpec((1, FFN_CONV - 1, FFN_TF), lambda i, f: (seq(i), 0, f + nf)),
            pl.BlockSpec((1, D_MODEL), lambda i, f: (0, 0)),
            modspec, modspec, modspec,
            pl.BlockSpec((D_MODEL, FFN_TF), lambda i, f: (0, f)),
            pl.BlockSpec((D_MODEL, FFN_TF), lambda i, f: (0, f + nf)),
            pl.BlockSpec((FFN_CONV, FFN_TF), lambda i, f: (0, f)),
            pl.BlockSpec((FFN_CONV, FFN_TF), lambda i, f: (0, f + nf)),
            pl.BlockSpec((FFN_TF, D_MODEL), lambda i, f: (f, 0)),
            pl.BlockSpec((1, D_MODEL), lambda i, f: (0, 0)),
        ],
        out_specs=[
            pl.BlockSpec((tm, D_MODEL), lambda i, f: (i, 0)),
            pl.BlockSpec((1, SUBLANES, FFN_TF), lambda i, f: (i, 0, f)),
            pl.BlockSpec((1, SUBLANES, FFN_TF), lambda i, f: (i, 0, f)),
        ],
        out_shape=[
            jax.ShapeDtypeStruct((m, D_MODEL), F32),
            jax.ShapeDtypeStruct((m // tm, SUBLANES, D_FF), F32),
            jax.ShapeDtypeStruct((m // tm, SUBLANES, D_FF), F32),
        ],
        scratch_shapes=[
            pltpu.VMEM((tm + FFN_HALO, D_MODEL), BF16),
            pltpu.VMEM((tm, D_MODEL), F32),
        ],
        compiler_params=_cparams("arbitrary", "arbitrary"),
        name="conv_ffn_prompt",
    )(x, x, conv_prev, conv_prev, g, sc, sh, gt, w_up, w_up, w_conv, w_conv, w_down, gp)


def _ffn_s_kernel(x_ref, cpg_ref, cpv_ref, g_ref, sc_ref, sh_ref, gt_ref, wg_ref, wv_ref,
                  cg_ref, cv_ref, wd_ref, gp_ref, o_ref, ug_ref, uv_ref, h_scr, acc_scr, *, seq_len):
    f = pl.program_id(1)

    @pl.when(f == 0)
    def _():
        h_scr[...] = _norm_mod(x_ref[...], g_ref[...], sc_ref[...], sh_ref[...]).astype(BF16)
        acc_scr[...] = jnp.zeros(acc_scr.shape, F32)

    h = h_scr[...]
    tm = h.shape[0]
    pos = lax.broadcasted_iota(jnp.int32, (tm, FFN_TF), 0) & (seq_len - 1)
    n_cp = cpg_ref.shape[0]
    trow = lax.broadcasted_iota(jnp.int32, (tm, n_cp), 0)
    ccol = lax.broadcasted_iota(jnp.int32, (tm, n_cp), 1)
    tpos = trow & (seq_len - 1)
    base = 2 * _idiv(trow, seq_len)
    e1 = jnp.where((tpos == 0) & (ccol == base + 1), 1.0, 0.0).astype(BF16)
    e2 = jnp.where(((tpos == 0) & (ccol == base)) | ((tpos == 1) & (ccol == base + 1)), 1.0, 0.0).astype(BF16)

    def conv(u, cp_ref, c_ref, u_ref):
        u_ref[...] = u
        p1 = p2 = None
        for piece in _split_bf16(cp_ref[...], 3):
            d1, d2 = _dot(e1, piece), _dot(e2, piece)
            p1 = d1 if p1 is None else p1 + d1
            p2 = d2 if p2 is None else p2 + d2
        u1 = jnp.where(pos >= 1, pltpu.roll(u, 1, 0), p1)
        u2 = jnp.where(pos >= 2, pltpu.roll(u, 2, 0), p2)
        c = c_ref[...]
        return c[0:1] * u2 + c[1:2] * u1 + c[2:3] * u

    zg = conv(_dot(h, wg_ref[...]), cpg_ref, cg_ref, ug_ref)
    zv = conv(_dot(h, wv_ref[...]), cpv_ref, cv_ref, uv_ref)
    acc_scr[...] += _dot(_gated(zg, zv), wd_ref[...])

    @pl.when(f == pl.num_programs(1) - 1)
    def _():
        o_ref[...] = x_ref[...] + gt_ref[...] * _rms(acc_scr[...], gp_ref[...])


def _ffn_s_call(x, conv_prev, g, sc, sh, gt, w_up, w_conv, w_down, gp, seq_len, tm):
    m = x.shape[0]
    assert tm % seq_len == 0 and seq_len & (seq_len - 1) == 0
    nf = D_FF // FFN_TF
    n_cp = (tm // seq_len) * (FFN_CONV - 1)
    full = pl.BlockSpec((tm, D_MODEL), lambda i, f: (i, 0))
    full_in = pl.BlockSpec((tm, D_MODEL), lambda i, f: (i, 0), pipeline_mode=pl.Buffered(1))
    lo = pl.BlockSpec((tm, FFN_TF), lambda i, f: (i, f))
    return pl.pallas_call(
        functools.partial(_ffn_s_kernel, seq_len=seq_len),
        grid=(m // tm, nf),
        in_specs=[
            full_in,
            pl.BlockSpec((n_cp, FFN_TF), lambda i, f: (i, f)),
            pl.BlockSpec((n_cp, FFN_TF), lambda i, f: (i, f + nf)),
            pl.BlockSpec((1, D_MODEL), lambda i, f: (0, 0)),
            full_in, full_in, full_in,
            pl.BlockSpec((D_MODEL, FFN_TF), lambda i, f: (0, f)),
            pl.BlockSpec((D_MODEL, FFN_TF), lambda i, f: (0, f + nf)),
            pl.BlockSpec((FFN_CONV, FFN_TF), lambda i, f: (0, f)),
            pl.BlockSpec((FFN_CONV, FFN_TF), lambda i, f: (0, f + nf)),
            pl.BlockSpec((FFN_TF, D_MODEL), lambda i, f: (f, 0)),
            pl.BlockSpec((1, D_MODEL), lambda i, f: (0, 0)),
        ],
        out_specs=[full, lo, lo],
        out_shape=[
            jax.ShapeDtypeStruct((m, D_MODEL), F32),
            jax.ShapeDtypeStruct((m, D_FF), F32),
            jax.ShapeDtypeStruct((m, D_FF), F32),
        ],
        scratch_shapes=[pltpu.VMEM((tm, D_MODEL), BF16), pltpu.VMEM((tm, D_MODEL), F32)],
        compiler_params=_cparams("arbitrary", "arbitrary"),
        name="conv_ffn_sample",
    )(x, conv_prev, conv_prev, g, sc, sh, gt, w_up, w_up, w_conv, w_conv, w_down, gp)


def _seq_tile(t, cap):
    tm = min(t, cap)
    assert t % tm == 0
    return tm


@jax.jit
def _forward(x_prompt, x_sample, c_prompt, c_sample, cache_k, cache_v, state_wkv, state_shift,
             state_ffn_conv, w_ada, b_ada, g_pre_mix, g_post_mix, g_pre_ffn, g_post_ffn, w_in,
             lam_q1, lam_k1, lam_q2, lam_k2, g_subln, mu_shift, w0, w_w2, a0, w_a2, w_g2, k_k, k_a,
             r_k, ln_x_w, ln_x_b, w_out, w_up, w_conv_ffn, w_down):
    bp, t, _ = x_prompt.shape
    bs, ts, _ = x_sample.shape
    past = cache_k.shape[2]
    assert w_ada.shape[0] == 1, "single-layer problem"

    n_c = bp + bs
    n_c_pad = -(-n_c // SUBLANES) * SUBLANES
    c_all = jnp.concatenate([c_prompt, c_sample, jnp.zeros((n_c_pad - n_c, D_MODEL), F32)], axis=0)
    mod = _ada_call(c_all, w_ada[0], b_ada)
    sh_m, sc_m, gt_m, sh_f, sc_f, gt_f = [mod[:, i * D_MODEL:(i + 1) * D_MODEL] for i in range(6)]
    p_mod = lambda a: a[:bp].reshape(bp, 1, D_MODEL)
    s_mod = lambda a: jnp.repeat(a[bp:n_c], ts, axis=0)

    w_in_b = w_in[0].astype(BF16)
    w_out_b = w_out[0].astype(BF16)
    w_up_b = w_up[0].astype(BF16)
    w_down_b = w_down[0].astype(BF16)
    lams = (lam_q1, lam_k1, lam_q2, lam_k2)
    prm = {
        "mu_shift": mu_shift, "w0": w0, "a0": a0, "k_k": k_k, "k_a": k_a,
        "r_k": r_k.reshape(1, B_WIDTH), "ln_x_w": ln_x_w, "ln_x_b": ln_x_b,
        "w_w2": w_w2[0].astype(BF16), "w_a2": w_a2[0].astype(BF16), "w_g2": w_g2[0].astype(BF16),
    }

    tm = _seq_tile(t, 512)
    tps = t // tm
    xp = x_prompt.reshape(bp * t, D_MODEL)
    kf, vf, qkvb, rw = _in_proj_call(xp, g_pre_mix, p_mod(sc_m), p_mod(sh_m), w_in_b, tm, t)
    oa = _pattn_call(lams, qkvb.reshape(bp, t, 3 * A_WIDTH), g_subln, _seq_tile(t, 512))
    ob, s_last = _rwkv_call(rw.reshape(bp, t, RW_COLS), jnp.zeros((bp, 1, RW_COLS), F32),
                            jnp.zeros((bp, N_GROUPS, MXU_DIM, B_HEAD), state_wkv.dtype), prm,
                            _seq_tile(t, 256), 32, 2 if bp % 2 == 0 else 1)
    x1 = _mixout_call(oa.reshape(bp * t, A_WIDTH), ob.reshape(bp * t, B_WIDTH), xp, w_out_b,
                      g_post_mix, p_mod(gt_m), tm, tps)
    yp, clg, clv = _ffn_call(x1, jnp.zeros((bp, FFN_CONV - 1, 2 * D_FF), F32), g_pre_ffn, p_mod(sc_f),
                             p_mod(sh_f), p_mod(gt_f), w_up_b, w_conv_ffn[0], w_down_b, g_post_ffn,
                             tm, tps)
    rw3 = rw.reshape(bp, t, RW_COLS)
    out_prompt = (
        yp.reshape(bp, t, D_MODEL),
        kf.reshape(1, bp, t, A_HEADS, 2 * A_DK),
        vf.reshape(1, bp, t, A_HEADS, A_DV),
        _groups_to_state(s_last)[None],
        rw3[:, -1][None],
        jnp.concatenate([clg[tps - 1::tps, -(FFN_CONV - 1):], clv[tps - 1::tps, -(FFN_CONV - 1):]],
                        axis=-1)[None],
    )

    ms = bs * ts
    xs = x_sample.reshape(ms, D_MODEL)
    kf_s, vf_s, qkvb_s, rw_s = _in_proj_call(xs, g_pre_mix, s_mod(sc_m), s_mod(sh_m), w_in_b, ms, ms)
    oa_s = _sattn_call(lams, qkvb_s.reshape(bs, ts, 3 * A_WIDTH),
                       cache_k[0].reshape(bs, past * A_HEADS, A_DV),
                       cache_v[0].reshape(bs, past * A_HEADS, A_DV), g_subln, _seq_tile(past, 2048))
    ob_s, s_last_s = _rwkv_call(rw_s.reshape(bs, ts, RW_COLS), state_shift[0][:, None, :],
                                _state_to_groups(state_wkv[0]), prm, ts, ts, 4 if bs % 4 == 0 else 1)
    x1_s = _mixout_call(oa_s.reshape(ms, A_WIDTH), ob_s.reshape(ms, B_WIDTH), xs, w_out_b,
                        g_post_mix, s_mod(gt_m), ms, 1)
    cprev = state_ffn_conv[0].reshape(bs * (FFN_CONV - 1), 2 * D_FF)
    ys, ug, uv = _ffn_s_call(x1_s, cprev, g_pre_ffn, s_mod(sc_f), s_mod(sh_f), s_mod(gt_f),
                             w_up_b, w_conv_ffn[0], w_down_b, g_post_ffn, ts, _seq_tile(ms, 512))
    u_all = jnp.concatenate([ug, uv], axis=-1).reshape(bs, ts, 2 * D_FF)
    out_sample = (
        ys.reshape(bs, ts, D_MODEL),
        kf_s.reshape(1, bs, ts, A_HEADS, 2 * A_DK),
        vf_s.reshape(1, bs, ts, A_HEADS, A_DV),
        _groups_to_state(s_last_s)[None],
        rw_s.reshape(bs, ts, RW_COLS)[:, -1][None],
        u_all[:, -(FFN_CONV - 1):][None],
    )
    return (out_prompt[0], out_sample[0]) + out_prompt[1:] + out_sample[1:]


def kernel(x_prompt, x_sample, c_prompt, c_sample, cache_k, cache_v, state_wkv, state_shift, state_ffn_conv, w_ada, b_ada, g_pre_mix, g_post_mix, g_pre_ffn, g_post_ffn, w_in, lam_q1, lam_k1, lam_q2, lam_k2, g_subln, mu_shift, w0, w_w2, a0, w_a2, w_g2, k_k, k_a, r_k, ln_x_w, ln_x_b, w_out, w_up, w_conv_ffn, w_down):
    return _forward(x_prompt, x_sample, c_prompt, c_sample, cache_k, cache_v, state_wkv, state_shift,
                    state_ffn_conv, w_ada, b_ada, g_pre_mix, g_post_mix, g_pre_ffn, g_post_ffn, w_in,
                    lam_q1, lam_k1, lam_q2, lam_k2, g_subln, mu_shift, w0, w_w2, a0, w_a2, w_g2, k_k,
                    k_a, r_k, ln_x_w, ln_x_b, w_out, w_up, w_conv_ffn, w_down)
```

```python
import functools
import math

import jax
import jax.numpy as jnp
from jax import lax
from jax.experimental import pallas as pl
from jax.experimental.pallas import tpu as pltpu

F32 = jnp.float32
BF16 = jnp.bfloat16

D_MODEL = 2048
A_HEADS = 8
A_DK = 64
A_DV = 128
A_WIDTH = A_HEADS * A_DV
B_HEAD = 64
B_WIDTH = D_MODEL - A_WIDTH
B_HEADS = B_WIDTH // B_HEAD
W_RANK = 64
A_RANK = 64
G_RANK = 128
RW_COLS = 3 * B_WIDTH + W_RANK + A_RANK + G_RANK
D_FF = 5632
FFN_CONV = 3
ATTN_CHUNK = 64
EPS = 1e-6
GN_EPS = 64e-5
LAM_INIT = 0.8 - 0.6 * math.exp(-0.3 * 0)
Q_SCALE = (A_DK ** -0.5) * math.log2(math.e)

LANES = 128
SUBLANES = 8
BF16_ROWS = 16
MXU_DIM = 256
GROUP_HEADS = MXU_DIM // B_HEAD
N_GROUPS = B_WIDTH // MXU_DIM
VMEM_LIMIT_BYTES = 56 * 1024 * 1024
NEG_BIG = -1e30


def _cparams(*sem):
    return pltpu.CompilerParams(dimension_semantics=sem, vmem_limit_bytes=VMEM_LIMIT_BYTES)


def _dot(a, b):
    return jnp.dot(a, b, preferred_element_type=F32)


def _dot_nt(a, b):
    return lax.dot_general(a, b, (((1,), (1,)), ((), ())), preferred_element_type=F32)


def _dot_tn(a, b):
    return lax.dot_general(a, b, (((0,), (0,)), ((), ())), preferred_element_type=F32)


def _idiv(x, n):
    assert n > 0 and n & (n - 1) == 0
    return x >> (n.bit_length() - 1)


def _sigmoid(x):
    return 1.0 / (1.0 + jnp.exp(-x))


def _split_bf16(x, terms):
    out = []
    rem = x
    for _ in range(terms):
        piece = rem.astype(BF16)
        out.append(piece)
        rem = rem - piece.astype(F32)
    return out


def _norm_mod(x, g, sc, sh):
    ms = jnp.mean(x * x, axis=-1, keepdims=True)
    return (x * lax.rsqrt(ms + EPS) * g) * (1.0 + sc) + sh


def _rms(x, g):
    ms = jnp.mean(x * x, axis=-1, keepdims=True)
    return x * lax.rsqrt(ms + EPS) * g


def _mod(ref):
    return ref[0] if len(ref.shape) == 3 else ref[...]


def _lam(q1_ref, k1_ref, q2_ref, k2_ref):
    s1 = jnp.sum(q1_ref[...] * k1_ref[...], axis=-1, keepdims=True)
    s2 = jnp.sum(q2_ref[...] * k2_ref[...], axis=-1, keepdims=True)
    return jnp.exp(s1) - jnp.exp(s2) + LAM_INIT


def _ada_kernel(c_ref, w_ref, b_ref, o_ref):
    c = c_ref[...]
    s = (c * _sigmoid(c)).astype(BF16)
    o_ref[...] = _dot(s, w_ref[...].astype(BF16)) + b_ref[...]


def _ada_call(c_all, w_ada, b_ada):
    rows = c_all.shape[0]
    n = w_ada.shape[1]
    tn = 1024
    return pl.pallas_call(
        _ada_kernel,
        grid=(n // tn,),
        in_specs=[
            pl.BlockSpec((rows, D_MODEL), lambda j: (0, 0)),
            pl.BlockSpec((D_MODEL, tn), lambda j: (0, j)),
            pl.BlockSpec((1, tn), lambda j: (0, j)),
        ],
        out_specs=pl.BlockSpec((rows, tn), lambda j: (0, j)),
        out_shape=jax.ShapeDtypeStruct((rows, n), F32),
        compiler_params=_cparams("arbitrary"),
        name="ada_mod",
    )(c_all, w_ada, b_ada)


def _qkv_kernel(x_ref, g_ref, sc_ref, sh_ref, w_ref, kf_ref, vf_ref, qkvb_ref):
    h = _norm_mod(x_ref[...], g_ref[...], _mod(sc_ref), _mod(sh_ref)).astype(BF16)
    q = _dot(h, w_ref[:, 0:A_WIDTH])
    qkvb_ref[:, 0:A_WIDTH] = (q * Q_SCALE).astype(BF16)
    k = _dot(h, w_ref[:, A_WIDTH:2 * A_WIDTH])
    kf_ref[...] = k
    qkvb_ref[:, A_WIDTH:2 * A_WIDTH] = k.astype(BF16)
    v = _dot(h, w_ref[:, 2 * A_WIDTH:3 * A_WIDTH])
    vf_ref[...] = v
    qkvb_ref[:, 2 * A_WIDTH:3 * A_WIDTH] = v.astype(BF16)


def _rw_kernel(x_ref, g_ref, sc_ref, sh_ref, w_ref, o_ref):
    h = _norm_mod(x_ref[...], g_ref[...], _mod(sc_ref), _mod(sh_ref)).astype(BF16)
    o_ref[...] = _dot(h, w_ref[:, 3 * A_WIDTH:])


def _mod_spec(mod, tm, tiles_per_seq, nidx):
    if mod.ndim == 3:
        if nidx == 2:
            return pl.BlockSpec((1, 1, D_MODEL), lambda i, j: (i // tiles_per_seq, 0, 0))
        return pl.BlockSpec((1, 1, D_MODEL), lambda i: (i // tiles_per_seq, 0, 0))
    if nidx == 2:
        return pl.BlockSpec((tm, D_MODEL), lambda i, j: (i, 0))
    return pl.BlockSpec((tm, D_MODEL), lambda i: (i, 0))


def _resident(shape):
    return pl.BlockSpec(shape, lambda i: (0,) * len(shape), pipeline_mode=pl.Buffered(1))


def _in_proj_call(x, g, sc, sh, w_in, tm, seq_len):
    m = x.shape[0]
    n_in = w_in.shape[1]
    common = [
        pl.BlockSpec((tm, D_MODEL), lambda i: (i, 0)),
        _resident((1, D_MODEL)),
        _mod_spec(sc, tm, seq_len // tm, 1),
        _mod_spec(sh, tm, seq_len // tm, 1),
    ]
    kf, vf, qkvb = pl.pallas_call(
        _qkv_kernel,
        grid=(m // tm,),
        in_specs=common + [_resident((D_MODEL, n_in))],
        out_specs=[
            pl.BlockSpec((tm, A_WIDTH), lambda i: (i, 0)),
            pl.BlockSpec((tm, A_WIDTH), lambda i: (i, 0)),
            pl.BlockSpec((tm, 3 * A_WIDTH), lambda i: (i, 0)),
        ],
        out_shape=[
            jax.ShapeDtypeStruct((m, A_WIDTH), F32),
            jax.ShapeDtypeStruct((m, A_WIDTH), F32),
            jax.ShapeDtypeStruct((m, 3 * A_WIDTH), BF16),
        ],
        compiler_params=_cparams("arbitrary"),
        name="in_proj_qkv",
    )(x, g, sc, sh, w_in)
    rw = pl.pallas_call(
        _rw_kernel,
        grid=(m // tm,),
        in_specs=common + [_resident((D_MODEL, n_in))],
        out_specs=pl.BlockSpec((tm, RW_COLS), lambda i: (i, 0)),
        out_shape=jax.ShapeDtypeStruct((m, RW_COLS), F32),
        compiler_params=_cparams("arbitrary"),
        name="in_proj_rw",
    )(x, g, sc, sh, w_in)
    return kf, vf, qkvb, rw


def _softmax_step(s, vb, m_ref, l_ref, acc_ref, idx):
    m_prev = m_ref[idx]
    m_new = jnp.maximum(m_prev, jnp.max(s, axis=-1, keepdims=True))
    alpha = jnp.exp2(m_prev - m_new)
    p = jnp.exp2(s - m_new)
    l_ref[idx] = alpha * l_ref[idx] + jnp.sum(p, axis=-1, keepdims=True)
    acc_ref[idx] = alpha * acc_ref[idx] + _dot(p.astype(BF16), vb)
    m_ref[idx] = m_new


def _subln(o, g):
    return _rms(o, g) * (1.0 - LAM_INIT)


PATTN_HEADS = 2
PATTN_QG = 512


def _pattn_kernel(q1_ref, k1_ref, q2_ref, k2_ref, q_ref, k_ref, v_ref, g_ref, o_ref,
                  m_scr, l_scr, acc_scr, s_scr, t_scr, *, tq):
    i = pl.program_id(2)
    lane = lax.broadcasted_iota(jnp.int32, (tq, A_DV), 1)
    m_scr[...] = jnp.full(m_scr.shape, NEG_BIG, F32)
    l_scr[...] = jnp.zeros(l_scr.shape, F32)
    acc_scr[...] = jnp.zeros(acc_scr.shape, F32)

    qg = min(PATTN_QG, 2 * tq)
    n_qg = 2 * tq // qg
    q_gs = []
    for hh in range(PATTN_HEADS):
        q = q_ref[0, :, hh * A_DV:(hh + 1) * A_DV]
        zero = jnp.zeros_like(q)
        q_st = jnp.concatenate([jnp.where(lane < A_DK, q, zero), jnp.where(lane >= A_DK, q, zero)], axis=0)
        q_gs.append([q_st[g * qg:(g + 1) * qg] for g in range(n_qg)])
    ones_k = jnp.ones((SUBLANES, tq), BF16)
    chains = [(hh, g) for g in range(n_qg) for hh in range(PATTN_HEADS)]

    def scores(j):
        r0 = pl.multiple_of(j * tq, tq)
        kbs = [k_ref[0, pl.ds(r0, tq), hh * A_DV:(hh + 1) * A_DV] for hh in range(PATTN_HEADS)]
        return [_dot_nt(kbs[hh], q_gs[hh][g]) for hh, g in chains]

    def update(j, s_ref, masks):
        r0 = pl.multiple_of(j * tq, tq)
        vbs = [v_ref[0, pl.ds(r0, tq), hh * A_DV:(hh + 1) * A_DV] for hh in range(PATTN_HEADS)]
        for ci, (hh, g) in enumerate(chains):
            cs = slice(g * qg, (g + 1) * qg)
            s_t = s_ref[ci]
            if masks is not None:
                s_t = jnp.where(masks[g], s_t, NEG_BIG)
            m_prev = m_scr[hh, :, cs]
            m_new = jnp.maximum(m_prev, jnp.max(s_t, axis=0, keepdims=True))
            alpha = jnp.exp2(m_prev - m_new)
            p_b = jnp.exp2(s_t - m_new).astype(BF16)
            l_scr[hh, :, cs] = alpha * l_scr[hh, :, cs] + _dot(ones_k, p_b)[0:1]
            acc_scr[hh, :, cs] = alpha * acc_scr[hh, :, cs] + _dot_tn(vbs[hh], p_b)
            m_scr[hh, :, cs] = m_new

    def put(s_ref, j):
        for ci, s_t in enumerate(scores(j)):
            s_ref[ci] = s_t

    rk = lax.broadcasted_iota(jnp.int32, (tq, qg), 0)
    cq = lax.broadcasted_iota(jnp.int32, (tq, qg), 1)
    diag = [_idiv(rk, ATTN_CHUNK) <= _idiv((cq + g * qg) & (tq - 1), ATTN_CHUNK) for g in range(n_qg)]

    put(s_scr, 0)

    def body(jp, carry):
        j = 2 * jp
        put(t_scr, j + 1)
        update(j, s_scr, None)
        put(s_scr, j + 2)
        update(j + 1, t_scr, None)
        return carry

    lax.fori_loop(0, i // 2, body, 0)

    @pl.when(i % 2 == 0)
    def _():
        update(i, s_scr, diag)

    @pl.when(i % 2 == 1)
    def _():
        put(t_scr, i)
        update(i - 1, s_scr, None)
        update(i, t_scr, diag)

    lam = _lam(q1_ref, k1_ref, q2_ref, k2_ref)
    for hh in range(PATTN_HEADS):
        o_t = (acc_scr[hh, :, 0:tq] / l_scr[hh, :, 0:tq]
               - lam * (acc_scr[hh, :, tq:2 * tq] / l_scr[hh, :, tq:2 * tq]))
        o_ref[0, :, hh * A_DV:(hh + 1) * A_DV] = _subln(o_t.T, g_ref[...]).astype(BF16)


def _lam_specs(nidx):
    if nidx == 3:
        return [pl.BlockSpec((1, A_DK), lambda b, h, i: (0, 0))] * 4
    return [pl.BlockSpec((1, A_DK), lambda b, j: (0, 0))] * 4


def _pattn_call(lams, qkvb, g_subln, tq):
    bsz, t, _ = qkvb.shape
    hw = PATTN_HEADS * A_DV
    ng = A_HEADS // PATTN_HEADS
    return pl.pallas_call(
        functools.partial(_pattn_kernel, tq=tq),
        grid=(bsz, ng, t // tq),
        in_specs=_lam_specs(3) + [
            pl.BlockSpec((1, tq, hw), lambda b, h, i: (b, i, h)),
            pl.BlockSpec((1, t, hw), lambda b, h, i: (b, 0, ng + h)),
            pl.BlockSpec((1, t, hw), lambda b, h, i: (b, 0, 2 * ng + h)),
            pl.BlockSpec((1, A_DV), lambda b, h, i: (0, 0)),
        ],
        out_specs=pl.BlockSpec((1, tq, hw), lambda b, h, i: (b, i, h)),
        out_shape=jax.ShapeDtypeStruct((bsz, t, A_WIDTH), BF16),
        scratch_shapes=[
            pltpu.VMEM((PATTN_HEADS, 1, 2 * tq), F32),
            pltpu.VMEM((PATTN_HEADS, 1, 2 * tq), F32),
            pltpu.VMEM((PATTN_HEADS, A_DV, 2 * tq), F32),
        ] + [pltpu.VMEM((PATTN_HEADS * (2 * tq // min(PATTN_QG, 2 * tq)), tq, min(PATTN_QG, 2 * tq)), F32)] * 2,
        compiler_params=_cparams("arbitrary", "arbitrary", "arbitrary"),
        name="prompt_attention",
    )(*lams, qkvb, qkvb, qkvb, g_subln)


N_VQ = 2 * A_HEADS


def _sattn_kernel(q1_ref, k1_ref, q2_ref, k2_ref, q_ref, kn_ref, vn_ref, ck_ref, cv_ref, g_ref, o_ref,
                  qb_scr, m_scr, l_scr, acc_scr, *, tnew):
    j = pl.program_id(1)
    nvq = N_VQ * tnew

    @pl.when(j == 0)
    def _():
        q = q_ref[0]
        qt = jnp.concatenate([q] * N_VQ, axis=0)
        row = lax.broadcasted_iota(jnp.int32, (nvq, A_WIDTH), 0)
        col = lax.broadcasted_iota(jnp.int32, (nvq, A_WIDTH), 1)
        qb_scr[...] = jnp.where(_idiv(row, tnew) == _idiv(col, A_DK), qt, jnp.zeros_like(qt))
        m_scr[...] = jnp.full(m_scr.shape, NEG_BIG, F32)
        l_scr[...] = jnp.zeros(l_scr.shape, F32)
        acc_scr[...] = jnp.zeros(acc_scr.shape, F32)

    def step(kb, vb):
        _softmax_step(_dot_nt(qb_scr[...], kb), vb, m_scr, l_scr, acc_scr, 0)

    def cached_rows(ref, p0, n):
        heads = [ref[0, pl.ds(p0 * A_HEADS + h, n, stride=A_HEADS), :] for h in range(A_HEADS)]
        return jnp.concatenate(heads, axis=-1).astype(BF16)

    tk = ck_ref.shape[1] // A_HEADS
    n_sub = max(1, tk // MXU_DIM)
    sub = tk // n_sub
    kbs = [cached_rows(ck_ref, c * sub, sub) for c in range(n_sub)]
    s_all = [_dot_nt(qb_scr[...], kb) for kb in kbs]
    for c in range(n_sub):
        _softmax_step(s_all[c], cached_rows(cv_ref, c * sub, sub), m_scr, l_scr, acc_scr, 0)

    @pl.when(j == pl.num_programs(1) - 1)
    def _():
        step(kn_ref[0], vn_ref[0])
        lam = _lam(q1_ref, k1_ref, q2_ref, k2_ref)
        g = g_ref[...]
        for h in range(A_HEADS):
            cs = slice(h * A_DV, (h + 1) * A_DV)
            r0 = slice((2 * h) * tnew, (2 * h + 1) * tnew)
            r1 = slice((2 * h + 1) * tnew, (2 * h + 2) * tnew)
            o0 = acc_scr[0, r0, cs] / l_scr[0, r0, :]
            o1 = acc_scr[0, r1, cs] / l_scr[0, r1, :]
            o_ref[0, :, cs] = _subln(o0 - lam * o1, g).astype(BF16)


def _sattn_call(lams, qkvb, cache_k, cache_v, g_subln, tk):
    bsz, tnew, _ = qkvb.shape
    past = cache_k.shape[1] // A_HEADS
    nvq = N_VQ * tnew
    return pl.pallas_call(
        functools.partial(_sattn_kernel, tnew=tnew),
        grid=(bsz, past // tk),
        in_specs=_lam_specs(2) + [
            pl.BlockSpec((1, tnew, A_WIDTH), lambda b, j: (b, 0, 0)),
            pl.BlockSpec((1, tnew, A_WIDTH), lambda b, j: (b, 0, 1)),
            pl.BlockSpec((1, tnew, A_WIDTH), lambda b, j: (b, 0, 2)),
            pl.BlockSpec((1, tk * A_HEADS, A_DV), lambda b, j: (b, j, 0)),
            pl.BlockSpec((1, tk * A_HEADS, A_DV), lambda b, j: (b, j, 0)),
            pl.BlockSpec((1, A_DV), lambda b, j: (0, 0)),
        ],
        out_specs=pl.BlockSpec((1, tnew, A_WIDTH), lambda b, j: (b, 0, 0)),
        out_shape=jax.ShapeDtypeStruct((bsz, tnew, A_WIDTH), BF16),
        scratch_shapes=[
            pltpu.VMEM((nvq, A_WIDTH), BF16),
            pltpu.VMEM((1, nvq, 1), F32),
            pltpu.VMEM((1, nvq, 1), F32),
            pltpu.VMEM((1, nvq, A_WIDTH), F32),
        ],
        compiler_params=_cparams("arbitrary", "arbitrary"),
        name="sample_attention",
    )(*lams, qkvb, qkvb, qkvb, cache_k, cache_v, g_subln)


def _head_sum(x, ones_bd):
    return _dot(x.astype(BF16), ones_bd)


def _rwkv_kernel(p_ref, ph_ref, st_ref, s0_ref, mu_ref, w0_ref, a0_ref, kk_ref, ka_ref, rk_ref,
                 lnw_ref, lnb_ref, ww2_ref, wa2_ref, wg2_ref, ob_ref, so_ref,
                 r_scr, k_scr, v_scr, na_scr, nb_scr, ld_scr, cum_scr, g_scr, bon_scr, y_scr, s_scr,
                 *, tb, chunk):
    i = pl.program_id(1)
    n_seq = p_ref.shape[0]
    c_len = chunk
    n_chunks = tb // c_len

    row_g = lax.broadcasted_iota(jnp.int32, (MXU_DIM, MXU_DIM), 0)
    col_g = lax.broadcasted_iota(jnp.int32, (MXU_DIM, MXU_DIM), 1)
    bd_mask = _idiv(row_g, B_HEAD) == _idiv(col_g, B_HEAD)
    ones_bd = jnp.where(bd_mask, 1.0, 0.0).astype(BF16)

    @pl.when(i == 0)
    def _():
        rep = jnp.where(
            lax.broadcasted_iota(jnp.int32, (B_HEAD, MXU_DIM), 0)
            == (lax.broadcasted_iota(jnp.int32, (B_HEAD, MXU_DIM), 1) & (B_HEAD - 1)), 1.0, 0.0).astype(BF16)
        for b in range(n_seq):
            for q in range(N_GROUPS):
                wide = None
                for piece in _split_bf16(s0_ref[b, q], 3):
                    d = _dot(piece, rep)
                    wide = d if wide is None else wide + d
                s_scr[b, q] = jnp.where(bd_mask, wide, 0.0)

    row_t = lax.broadcasted_iota(jnp.int32, (tb, tb), 0)
    col_t = lax.broadcasted_iota(jnp.int32, (tb, tb), 1)
    tri = jnp.where((_idiv(row_t, c_len) == _idiv(col_t, c_len)) & (col_t <= row_t), 1.0, 0.0).astype(BF16)
    w_cat = jnp.concatenate([ww2_ref[...], wa2_ref[...]], axis=0)

    for b in range(n_seq):
        p = p_ref[b]
        prev_row = jnp.where(i == 0, st_ref[b], ph_ref[b, SUBLANES - 1:SUBLANES, :])
        rowi = lax.broadcasted_iota(jnp.int32, p.shape, 0)
        p_prev = jnp.where(rowi == 0, prev_row, pltpu.roll(p, 1, 0))
        xs = p + (p_prev - p) * mu_ref[...]
        r = xs[:, 0:B_WIDTH]
        k = xs[:, B_WIDTH:2 * B_WIDTH]
        v = xs[:, 2 * B_WIDTH:3 * B_WIDTH]
        wa = xs[:, 3 * B_WIDTH:3 * B_WIDTH + W_RANK + A_RANK]
        gd = xs[:, 3 * B_WIDTH + W_RANK + A_RANK:]

        lane_wa = lax.broadcasted_iota(jnp.int32, wa.shape, 1)
        zero_wa = jnp.zeros_like(wa)
        w_lin = _dot(jnp.where(lane_wa < W_RANK, jnp.tanh(wa), zero_wa).astype(BF16), w_cat)
        a_lin = _dot(jnp.where(lane_wa >= W_RANK, wa, zero_wa).astype(BF16), w_cat)
        logd = -math.exp(-0.5) * _sigmoid(w0_ref[...] + w_lin)
        a = _sigmoid(a0_ref[...] + a_lin)
        g_scr[b] = _dot(_sigmoid(gd).astype(BF16), wg2_ref[...])

        k2 = k * (1.0 + (a - 1.0) * ka_ref[...])
        kk = k * kk_ref[...]
        rkk = r * k2 * rk_ref[...]
        for q in range(N_GROUPS):
            ls = slice(q * MXU_DIM, (q + 1) * MXU_DIM)
            kq = kk[:, ls]
            kn = kq * lax.rsqrt(jnp.maximum(_head_sum(kq * kq, ones_bd), 1e-24))
            na_scr[b, :, ls] = -kn
            nb_scr[b, :, ls] = kn * a[:, ls]
            bon_scr[b, :, ls] = _head_sum(rkk[:, ls], ones_bd) * v[:, ls]
        r_scr[b] = r
        k_scr[b] = k2
        v_scr[b] = v
        ld_scr[b] = logd

        cum = None
        for piece in _split_bf16(logd, 2):
            d = _dot(tri, piece)
            cum = d if cum is None else cum + d
        cum_scr[b] = cum

    gc = GROUP_HEADS * c_len
    lane_q = lax.broadcasted_iota(jnp.int32, (1, MXU_DIM), 1)
    head_masks = [_idiv(lane_q, B_HEAD) == j for j in range(GROUP_HEADS)]
    row_s = lax.broadcasted_iota(jnp.int32, (gc, gc), 0)
    col_s = lax.broadcasted_iota(jnp.int32, (gc, gc), 1)
    same_head = _idiv(row_s, c_len) == _idiv(col_s, c_len)
    strict = same_head & (col_s < row_s)
    incl = same_head & (col_s <= row_s)
    eye = jnp.where(col_s == row_s, 1.0, 0.0).astype(F32)
    n_double = int(math.log2(c_len)) - 1
    chains = [(b, q) for q in range(N_GROUPS) for b in range(n_seq)]
    groups = range(len(chains))

    def stack(x):
        z = jnp.zeros_like(x)
        return jnp.concatenate([jnp.where(head_masks[j], x, z) for j in range(GROUP_HEADS)], axis=0)

    def unstack(x):
        out = x[0:c_len]
        for j in range(1, GROUP_HEADS):
            out = out + x[j * c_len:(j + 1) * c_len]
        return out

    def chunk_body(c, carry):
        r0 = pl.multiple_of(c * c_len, c_len)
        rows = pl.ds(r0, c_len)
        a_s, r_s, b_s, k_s, be_s, ke_s, v_s, dec = [], [], [], [], [], [], [], []
        for b, q in chains:
            ls = slice(q * MXU_DIM, (q + 1) * MXU_DIM)
            cum_q = cum_scr[b, rows, ls]
            k_q = k_scr[b, rows, ls]
            nb_q = nb_scr[b, rows, ls]
            cum_last = cum_q[c_len - 1:c_len, :]
            e_out = jnp.exp(-cum_q)
            e_end = jnp.exp(cum_last - cum_q)
            a_s.append(stack((na_scr[b, rows, ls] * jnp.exp(cum_q - ld_scr[b, rows, ls])).astype(BF16)))
            r_s.append(stack((r_scr[b, rows, ls] * jnp.exp(cum_q)).astype(BF16)))
            b_s.append(stack((nb_q * e_out).astype(BF16)))
            k_s.append(stack((k_q * e_out).astype(BF16)))
            be_s.append(stack((nb_q * e_end).astype(BF16)))
            ke_s.append(stack((k_q * e_end).astype(BF16)))
            v_s.append(stack(v_scr[b, rows, ls].astype(BF16)))
            dec.append(jnp.exp(cum_last))

        if gc % LANES == 0:
            g_all = [_dot_nt(jnp.concatenate([a_s[q], r_s[q]], axis=0),
                             jnp.concatenate([b_s[q], k_s[q]], axis=0)) for q in groups]
            g_ab = [g[0:gc, 0:gc] for g in g_all]
            g_ak = [g[0:gc, gc:2 * gc] for g in g_all]
            g_rb = [g[gc:2 * gc, 0:gc] for g in g_all]
            g_rk = [g[gc:2 * gc, gc:2 * gc] for g in g_all]
        else:
            g_ab = [_dot_nt(a_s[q], b_s[q]) for q in groups]
            g_ak = [_dot_nt(a_s[q], k_s[q]) for q in groups]
            g_rb = [_dot_nt(r_s[q], b_s[q]) for q in groups]
            g_rk = [_dot_nt(r_s[q], k_s[q]) for q in groups]
        aab = [jnp.where(strict, g, 0.0) for g in g_ab]
        aak = [jnp.where(strict, g, 0.0).astype(BF16) for g in g_ak]
        arb = [jnp.where(incl, g, 0.0).astype(BF16) for g in g_rb]
        ark = [jnp.where(incl, g, 0.0).astype(BF16) for g in g_rk]

        t_m = [eye + x for x in aab]
        pw = [x.astype(BF16) for x in aab]
        pw = [_dot(x, x) for x in pw]
        for m in range(n_double):
            pb = [x.astype(BF16) for x in pw]
            if m + 1 < n_double:
                pw = [_dot(x, x) for x in pb]
            t_m = [t_m[q] + _dot(t_m[q].astype(BF16), pb[q]) for q in groups]
        t_b = [x.astype(BF16) for x in t_m]

        av = [_dot(aak[q], v_s[q]) for q in groups]

        s_prev = [s_scr[b, q] for b, q in chains]
        ar = [_dot_nt(jnp.concatenate([a_s[q], r_s[q]], axis=0), s_prev[q].astype(BF16)) for q in groups]
        u_b = [_dot(t_b[q], (ar[q][0:gc] + av[q]).astype(BF16)).astype(BF16) for q in groups]
        uv = [jnp.concatenate([u_b[q], v_s[q]], axis=0) for q in groups]
        if gc % LANES == 0:
            y_st = [ar[q][gc:2 * gc] + _dot(jnp.concatenate([arb[q], ark[q]], axis=1), uv[q])
                    for q in groups]
        else:
            y_st = [ar[q][gc:2 * gc] + _dot(arb[q], u_b[q]) + _dot(ark[q], v_s[q]) for q in groups]
        s_new = [s_prev[q] * dec[q]
                 + _dot_tn(uv[q], jnp.concatenate([be_s[q], ke_s[q]], axis=0)) for q in groups]
        for ci, (b, q) in enumerate(chains):
            s_scr[b, q] = s_new[ci]
            y_scr[b, rows, q * MXU_DIM:(q + 1) * MXU_DIM] = unstack(y_st[ci])
        return carry

    lax.fori_loop(0, n_chunks, chunk_body, 0)

    @pl.when(i == pl.num_programs(1) - 1)
    def _():
        rep_t = jnp.where(
            (lax.broadcasted_iota(jnp.int32, (MXU_DIM, B_HEAD), 0) & (B_HEAD - 1))
            == lax.broadcasted_iota(jnp.int32, (MXU_DIM, B_HEAD), 1), 1.0, 0.0).astype(BF16)
        for b, q in chains:
            narrow = None
            for piece in _split_bf16(s_scr[b, q], 3):
                d = _dot(piece, rep_t)
                narrow = d if narrow is None else narrow + d
            so_ref[b, q] = narrow

    inv_n = 1.0 / B_HEAD
    for b, q in chains:
        ls = slice(q * MXU_DIM, (q + 1) * MXU_DIM)
        y = y_scr[b, :, ls]
        mu = _head_sum(y, ones_bd) * inv_n
        d = y - mu
        var = _head_sum(d * d, ones_bd) * inv_n
        yn = d * lax.rsqrt(var + GN_EPS) * lnw_ref[:, ls] + lnb_ref[:, ls]
        ob_ref[b, :, ls] = ((yn + bon_scr[b, :, ls]) * g_scr[b, :, ls]).astype(BF16)


def _rwkv_call(rw, shift0, s0, prm, tb, chunk, n_seq):
    bsz, t, _ = rw.shape
    assert bsz % n_seq == 0
    nt = t // tb
    hb = tb // SUBLANES
    vec = lambda n: pl.BlockSpec((1, n), lambda b, i: (0, 0))
    return pl.pallas_call(
        functools.partial(_rwkv_kernel, tb=tb, chunk=chunk),
        grid=(bsz // n_seq, nt),
        in_specs=[
            pl.BlockSpec((n_seq, tb, RW_COLS), lambda b, i: (b, i, 0)),
            pl.BlockSpec((n_seq, SUBLANES, RW_COLS), lambda b, i: (b, jnp.maximum(i * hb - 1, 0), 0)),
            pl.BlockSpec((n_seq, 1, RW_COLS), lambda b, i: (b, 0, 0)),
            pl.BlockSpec((n_seq, N_GROUPS, MXU_DIM, B_HEAD), lambda b, i: (b, 0, 0, 0)),
            vec(RW_COLS), vec(B_WIDTH), vec(B_WIDTH), vec(B_WIDTH), vec(B_WIDTH), vec(B_WIDTH),
            vec(B_WIDTH), vec(B_WIDTH),
            pl.BlockSpec((W_RANK, B_WIDTH), lambda b, i: (0, 0)),
            pl.BlockSpec((A_RANK, B_WIDTH), lambda b, i: (0, 0)),
            pl.BlockSpec((G_RANK, B_WIDTH), lambda b, i: (0, 0)),
        ],
        out_specs=[
            pl.BlockSpec((n_seq, tb, B_WIDTH), lambda b, i: (b, i, 0)),
            pl.BlockSpec((n_seq, N_GROUPS, MXU_DIM, B_HEAD), lambda b, i: (b, 0, 0, 0)),
        ],
        out_shape=[
            jax.ShapeDtypeStruct((bsz, t, B_WIDTH), BF16),
            jax.ShapeDtypeStruct((bsz, N_GROUPS, MXU_DIM, B_HEAD), F32),
        ],
        scratch_shapes=[pltpu.VMEM((n_seq, tb, B_WIDTH), F32)] * 10
        + [pltpu.VMEM((n_seq, N_GROUPS, MXU_DIM, MXU_DIM), F32)],
        compiler_params=_cparams("arbitrary", "arbitrary"),
        name="rwkv7_mix",
    )(rw, rw, shift0, s0, prm["mu_shift"], prm["w0"], prm["a0"], prm["k_k"], prm["k_a"], prm["r_k"],
      prm["ln_x_w"], prm["ln_x_b"], prm["w_w2"], prm["w_a2"], prm["w_g2"])


def _state_to_groups(s):
    return s.reshape(s.shape[0], N_GROUPS, MXU_DIM, B_HEAD)


def _groups_to_state(sg):
    return sg.reshape(sg.shape[0], B_HEADS, B_HEAD, B_HEAD)


def _mixout_kernel(oa_ref, ob_ref, x_ref, w_ref, g_ref, gt_ref, o_ref):
    mix = _dot(oa_ref[...], w_ref[0:A_WIDTH, :]) + _dot(ob_ref[...], w_ref[A_WIDTH:D_MODEL, :])
    o_ref[...] = x_ref[...] + _mod(gt_ref) * _rms(mix, g_ref[...])


def _mixout_call(oa, ob, x, w_out, g, gt, tm, tiles_per_seq):
    m = x.shape[0]
    return pl.pallas_call(
        _mixout_kernel,
        grid=(m // tm,),
        in_specs=[
            pl.BlockSpec((tm, A_WIDTH), lambda i: (i, 0)),
            pl.BlockSpec((tm, B_WIDTH), lambda i: (i, 0)),
            pl.BlockSpec((tm, D_MODEL), lambda i: (i, 0)),
            pl.BlockSpec((D_MODEL, D_MODEL), lambda i: (0, 0)),
            pl.BlockSpec((1, D_MODEL), lambda i: (0, 0)),
            _mod_spec(gt, tm, tiles_per_seq, 1),
        ],
        out_specs=pl.BlockSpec((tm, D_MODEL), lambda i: (i, 0)),
        out_shape=jax.ShapeDtypeStruct((m, D_MODEL), F32),
        compiler_params=_cparams("arbitrary"),
        name="mix_out",
    )(oa, ob, x, w_out, g, gt)


FFN_HALO = BF16_ROWS
FFN_TF = 512


def _gated(zg, zv):
    return (zg * _sigmoid(zg) * zv).astype(BF16)


def _ffn_kernel(x_ref, xh_ref, cpg_ref, cpv_ref, g_ref, sc_ref, sh_ref, gt_ref, wg_ref, wv_ref,
                cg_ref, cv_ref, wd_ref, gp_ref, o_ref, clg_ref, clv_ref, h_scr, acc_scr,
                *, tm, tiles_per_seq):
    mi = pl.program_id(0)
    f = pl.program_id(1)
    first = (mi % tiles_per_seq) == 0

    @pl.when(f == 0)
    def _():
        g, sc, sh = g_ref[...], sc_ref[0], sh_ref[0]
        h_scr[FFN_HALO:, :] = _norm_mod(x_ref[...], g, sc, sh).astype(BF16)
        h_scr[:FFN_HALO, :] = _norm_mod(xh_ref[...], g, sc, sh).astype(BF16)
        acc_scr[...] = jnp.zeros(acc_scr.shape, F32)

    h = h_scr[...]
    row = lax.broadcasted_iota(jnp.int32, (FFN_HALO, FFN_TF), 0)

    def conv(u, cp_ref, c_ref, cl_ref):
        cp = cp_ref[0]
        head = u[:FFN_HALO]
        head_first = jnp.where(row == FFN_HALO - 2, cp[0:1],
                               jnp.where(row == FFN_HALO - 1, cp[1:2], head))
        ue = jnp.concatenate([jnp.where(first, head_first, head), u[FFN_HALO:]], axis=0)
        cl_ref[0] = ue[tm + FFN_HALO - SUBLANES:]
        c = c_ref[...]
        return (c[0:1] * pltpu.roll(ue, 2, 0)[FFN_HALO:] + c[1:2] * pltpu.roll(ue, 1, 0)[FFN_HALO:]
                + c[2:3] * ue[FFN_HALO:])

    zg = conv(_dot(h, wg_ref[...]), cpg_ref, cg_ref, clg_ref)
    zv = conv(_dot(h, wv_ref[...]), cpv_ref, cv_ref, clv_ref)
    acc_scr[...] += _dot(_gated(zg, zv), wd_ref[...])

    @pl.when(f == pl.num_programs(1) - 1)
    def _():
        o_ref[...] = x_ref[...] + gt_ref[0] * _rms(acc_scr[...], gp_ref[...])


def _ffn_call(x, conv_prev, g, sc, sh, gt, w_up, w_conv, w_down, gp, tm, tiles_per_seq):
    m = x.shape[0]
    nseq = conv_prev.shape[0]
    nf = D_FF // FFN_TF
    hb = tm // FFN_HALO
    seq = lambda i: i // tiles_per_seq
    modspec = pl.BlockSpec((1, 1, D_MODEL), lambda i, f: (seq(i), 0, 0))
    return pl.pallas_call(
        functools.partial(_ffn_kernel, tm=tm, tiles_per_seq=tiles_per_seq),
        grid=(m // tm, nf),
        in_specs=[
            pl.BlockSpec((tm, D_MODEL), lambda i, f: (i, 0)),
            pl.BlockSpec((FFN_HALO, D_MODEL), lambda i, f: (jnp.maximum(i * hb - 1, 0), 0)),
            pl.BlockSpec((1, FFN_CONV - 1, FFN_TF), lambda i, f: (seq(i), 0, f)),
            pl.BlockSpec((1, FFN_CONV - 1, FFN_TF), lambda i, f: (seq(i), 0, f + nf)),
            pl.BlockSpec((1, D_MODEL), lambda i, f: (0, 0)),
            modspec, modspec, modspec,
            pl.BlockSpec((D_MODEL, FFN_TF), lambda i, f: (0, f)),
            pl.BlockSpec((D_MODEL, FFN_TF), lambda i, f: (0, f + nf)),
            pl.BlockSpec((FFN_CONV, FFN_TF), lambda i, f: (0, f)),
            pl.BlockSpec((FFN_CONV, FFN_TF), lambda i, f: (0, f + nf)),
            pl.BlockSpec((FFN_TF, D_MODEL), lambda i, f: (f, 0)),
            pl.BlockSpec((1, D_MODEL), lambda i, f: (0, 0)),
        ],
        out_specs=[
            pl.BlockSpec((tm, D_MODEL), lambda i, f: (i, 0)),
            pl.BlockSpec((1, SUBLANES, FFN_TF), lambda i, f: (i, 0, f)),
            pl.BlockSpec((1, SUBLANES, FFN_TF), lambda i, f: (i, 0, f)),
        ],
        out_shape=[
            jax.ShapeDtypeStruct((m, D_MODEL), F32),
            jax.ShapeDtypeStruct((m // tm, SUBLANES, D_FF), F32),
            jax.ShapeDtypeStruct((m // tm, SUBLANES, D_FF), F32),
        ],
        scratch_shapes=[
            pltpu.VMEM((tm + FFN_HALO, D_MODEL), BF16),
            pltpu.VMEM((tm, D_MODEL), F32),
        ],
        compiler_params=_cparams("arbitrary", "arbitrary"),
        name="conv_ffn_prompt",
    )(x, x, conv_prev, conv_prev, g, sc, sh, gt, w_up, w_up, w_conv, w_conv, w_down, gp)


def _ffn_s_kernel(x_ref, cpg_ref, cpv_ref, g_ref, sc_ref, sh_ref, gt_ref, wg_ref, wv_ref,
                  cg_ref, cv_ref, wd_ref, gp_ref, o_ref, ug_ref, uv_ref, h_scr, acc_scr, *, seq_len):
    f = pl.program_id(1)

    @pl.when(f == 0)
    def _():
        h_scr[...] = _norm_mod(x_ref[...], g_ref[...], sc_ref[...], sh_ref[...]).astype(BF16)
        acc_scr[...] = jnp.zeros(acc_scr.shape, F32)

    h = h_scr[...]
    tm = h.shape[0]
    pos = lax.broadcasted_iota(jnp.int32, (tm, FFN_TF), 0) & (seq_len - 1)
    n_cp = cpg_ref.shape[0]
    trow = lax.broadcasted_iota(jnp.int32, (tm, n_cp), 0)
    ccol = lax.broadcasted_iota(jnp.int32, (tm, n_cp), 1)
    tpos = trow & (seq_len - 1)
    base = 2 * _idiv(trow, seq_len)
    e1 = jnp.where((tpos == 0) & (ccol == base + 1), 1.0, 0.0).astype(BF16)
    e2 = jnp.where(((tpos == 0) & (ccol == base)) | ((tpos == 1) & (ccol == base + 1)), 1.0, 0.0).astype(BF16)

    def conv(u, cp_ref, c_ref, u_ref):
        u_ref[...] = u
        p1 = p2 = None
        for piece in _split_bf16(cp_ref[...], 3):
            d1, d2 = _dot(e1, piece), _dot(e2, piece)
            p1 = d1 if p1 is None else p1 + d1
            p2 = d2 if p2 is None else p2 + d2
        u1 = jnp.where(pos >= 1, pltpu.roll(u, 1, 0), p1)
        u2 = jnp.where(pos >= 2, pltpu.roll(u, 2, 0), p2)
        c = c_ref[...]
        return c[0:1] * u2 + c[1:2] * u1 + c[2:3] * u

    zg = conv(_dot(h, wg_ref[...]), cpg_ref, cg_ref, ug_ref)
    zv = conv(_dot(h, wv_ref[...]), cpv_ref, cv_ref, uv_ref)
    acc_scr[...] += _dot(_gated(zg, zv), wd_ref[...])

    @pl.when(f == pl.num_programs(1) - 1)
    def _():
        o_ref[...] = x_ref[...] + gt_ref[...] * _rms(acc_scr[...], gp_ref[...])


def _ffn_s_call(x, conv_prev, g, sc, sh, gt, w_up, w_conv, w_down, gp, seq_len, tm):
    m = x.shape[0]
    assert tm % seq_len == 0 and seq_len & (seq_len - 1) == 0
    nf = D_FF // FFN_TF
    n_cp = (tm // seq_len) * (FFN_CONV - 1)
    full = pl.BlockSpec((tm, D_MODEL), lambda i, f: (i, 0))
    full_in = pl.BlockSpec((tm, D_MODEL), lambda i, f: (i, 0), pipeline_mode=pl.Buffered(1))
    lo = pl.BlockSpec((tm, FFN_TF), lambda i, f: (i, f))
    return pl.pallas_call(
        functools.partial(_ffn_s_kernel, seq_len=seq_len),
        grid=(m // tm, nf),
        in_specs=[
            full_in,
            pl.BlockSpec((n_cp, FFN_TF), lambda i, f: (i, f)),
            pl.BlockSpec((n_cp, FFN_TF), lambda i, f: (i, f + nf)),
            pl.BlockSpec((1, D_MODEL), lambda i, f: (0, 0)),
            full_in, full_in, full_in,
            pl.BlockSpec((D_MODEL, FFN_TF), lambda i, f: (0, f)),
            pl.BlockSpec((D_MODEL, FFN_TF), lambda i, f: (0, f + nf)),
            pl.BlockSpec((FFN_CONV, FFN_TF), lambda i, f: (0, f)),
            pl.BlockSpec((FFN_CONV, FFN_TF), lambda i, f: (0, f + nf)),
            pl.BlockSpec((FFN_TF, D_MODEL), lambda i, f: (f, 0)),
            pl.BlockSpec((1, D_MODEL), lambda i, f: (0, 0)),
        ],
        out_specs=[full, lo, lo],
        out_shape=[
            jax.ShapeDtypeStruct((m, D_MODEL), F32),
            jax.ShapeDtypeStruct((m, D_FF), F32),
            jax.ShapeDtypeStruct((m, D_FF), F32),
        ],
        scratch_shapes=[pltpu.VMEM((tm, D_MODEL), BF16), pltpu.VMEM((tm, D_MODEL), F32)],
        compiler_params=_cparams("arbitrary", "arbitrary"),
        name="conv_ffn_sample",
    )(x, conv_prev, conv_prev, g, sc, sh, gt, w_up, w_up, w_conv, w_conv, w_down, gp)


def _seq_tile(t, cap):
    tm = min(t, cap)
    assert t % tm == 0
    return tm


@jax.jit
def _forward(x_prompt, x_sample, c_prompt, c_sample, cache_k, cache_v, state_wkv, state_shift,
             state_ffn_conv, w_ada, b_ada, g_pre_mix, g_post_mix, g_pre_ffn, g_post_ffn, w_in,
             lam_q1, lam_k1, lam_q2, lam_k2, g_subln, mu_shift, w0, w_w2, a0, w_a2, w_g2, k_k, k_a,
             r_k, ln_x_w, ln_x_b, w_out, w_up, w_conv_ffn, w_down):
    bp, t, _ = x_prompt.shape
    bs, ts, _ = x_sample.shape
    past = cache_k.shape[2]
    assert w_ada.shape[0] == 1, "single-layer problem"

    n_c = bp + bs
    n_c_pad = -(-n_c // SUBLANES) * SUBLANES
    c_all = jnp.concatenate([c_prompt, c_sample, jnp.zeros((n_c_pad - n_c, D_MODEL), F32)], axis=0)
    mod = _ada_call(c_all, w_ada[0], b_ada)
    sh_m, sc_m, gt_m, sh_f, sc_f, gt_f = [mod[:, i * D_MODEL:(i + 1) * D_MODEL] for i in range(6)]
    p_mod = lambda a: a[:bp].reshape(bp, 1, D_MODEL)
    s_mod = lambda a: jnp.repeat(a[bp:n_c], ts, axis=0)

    w_in_b = w_in[0].astype(BF16)
    w_out_b = w_out[0].astype(BF16)
    w_up_b = w_up[0].astype(BF16)
    w_down_b = w_down[0].astype(BF16)
    lams = (lam_q1, lam_k1, lam_q2, lam_k2)
    prm = {
        "mu_shift": mu_shift, "w0": w0, "a0": a0, "k_k": k_k, "k_a": k_a,
        "r_k": r_k.reshape(1, B_WIDTH), "ln_x_w": ln_x_w, "ln_x_b": ln_x_b,
        "w_w2": w_w2[0].astype(BF16), "w_a2": w_a2[0].astype(BF16), "w_g2": w_g2[0].astype(BF16),
    }

    tm = _seq_tile(t, 512)
    tps = t // tm
    xp = x_prompt.reshape(bp * t, D_MODEL)
    kf, vf, qkvb, rw = _in_proj_call(xp, g_pre_mix, p_mod(sc_m), p_mod(sh_m), w_in_b, tm, t)
    oa = _pattn_call(lams, qkvb.reshape(bp, t, 3 * A_WIDTH), g_subln, _seq_tile(t, 512))
    ob, s_last = _rwkv_call(rw.reshape(bp, t, RW_COLS), jnp.zeros((bp, 1, RW_COLS), F32),
                            jnp.zeros((bp, N_GROUPS, MXU_DIM, B_HEAD), state_wkv.dtype), prm,
                            _seq_tile(t, 256), 32, 2 if bp % 2 == 0 else 1)
    x1 = _mixout_call(oa.reshape(bp * t, A_WIDTH), ob.reshape(bp * t, B_WIDTH), xp, w_out_b,
                      g_post_mix, p_mod(gt_m), tm, tps)
    yp, clg, clv = _ffn_call(x1, jnp.zeros((bp, FFN_CONV - 1, 2 * D_FF), F32), g_pre_ffn, p_mod(sc_f),
                             p_mod(sh_f), p_mod(gt_f), w_up_b, w_conv_ffn[0], w_down_b, g_post_ffn,
                             tm, tps)
    rw3 = rw.reshape(bp, t, RW_COLS)
    out_prompt = (
        yp.reshape(bp, t, D_MODEL),
        kf.reshape(1, bp, t, A_HEADS, 2 * A_DK),
        vf.reshape(1, bp, t, A_HEADS, A_DV),
        _groups_to_state(s_last)[None],
        rw3[:, -1][None],
        jnp.concatenate([clg[tps - 1::tps, -(FFN_CONV - 1):], clv[tps - 1::tps, -(FFN_CONV - 1):]],
                        axis=-1)[None],
    )

    ms = bs * ts
    xs = x_sample.reshape(ms, D_MODEL)
    kf_s, vf_s, qkvb_s, rw_s = _in_proj_call(xs, g_pre_mix, s_mod(sc_m), s_mod(sh_m), w_in_b, ms, ms)
    oa_s = _sattn_call(lams, qkvb_s.reshape(bs, ts, 3 * A_WIDTH),
                       cache_k[0].reshape(bs, past * A_HEADS, A_DV),
                       cache_v[0].reshape(bs, past * A_HEADS, A_DV), g_subln, _seq_tile(past, 2048))
    ob_s, s_last_s = _rwkv_call(rw_s.reshape(bs, ts, RW_COLS), state_shift[0][:, None, :],
                                _state_to_groups(state_wkv[0]), prm, ts, ts, 4 if bs % 4 == 0 else 1)
    x1_s = _mixout_call(oa_s.reshape(ms, A_WIDTH), ob_s.reshape(ms, B_WIDTH), xs, w_out_b,
                        g_post_mix, s_mod(gt_m), ms, 1)
    cprev = state_ffn_conv[0].reshape(bs * (FFN_CONV - 1), 2 * D_FF)
    ys, ug, uv = _ffn_s_call(x1_s, cprev, g_pre_ffn, s_mod(sc_f), s_mod(sh_f), s_mod(gt_f),
                             w_up_b, w_conv_ffn[0], w_down_b, g_post_ffn, ts, _seq_tile(ms, 512))
    u_all = jnp.concatenate([ug, uv], axis=-1).reshape(bs, ts, 2 * D_FF)
    out_sample = (
        ys.reshape(bs, ts, D_MODEL),
        kf_s.reshape(1, bs, ts, A_HEADS, 2 * A_DK),
        vf_s.reshape(1, bs, ts, A_HEADS, A_DV),
        _groups_to_state(s_last_s)[None],
        rw_s.reshape(bs, ts, RW_COLS)[:, -1][None],
        u_all[:, -(FFN_CONV - 1):][None],
    )
    return (out_prompt[0], out_sample[0]) + out_prompt[1:] + out_sample[1:]


def kernel(x_prompt, x_sample, c_prompt, c_sample, cache_k, cache_v, state_wkv, state_shift, state_ffn_conv, w_ada, b_ada, g_pre_mix, g_post_mix, g_pre_ffn, g_post_ffn, w_in, lam_q1, lam_k1, lam_q2, lam_k2, g_subln, mu_shift, w0, w_w2, a0, w_a2, w_g2, k_k, k_a, r_k, ln_x_w, ln_x_b, w_out, w_up, w_conv_ffn, w_down):
    return _forward(x_prompt, x_sample, c_prompt, c_sample, cache_k, cache_v, state_wkv, state_shift,
                    state_ffn_conv, w_ada, b_ada, g_pre_mix, g_post_mix, g_pre_ffn, g_post_ffn, w_in,
                    lam_q1, lam_k1, lam_q2, lam_k2, g_subln, mu_shift, w0, w_w2, a0, w_a2, w_g2, k_k,
                    k_a, r_k, ln_x_w, ln_x_b, w_out, w_up, w_conv_ffn, w_down)
```

```python
import functools
import math

import jax
import jax.numpy as jnp
from jax import lax
from jax.experimental import pallas as pl
from jax.experimental.pallas import tpu as pltpu

F32 = jnp.float32
BF16 = jnp.bfloat16

D_MODEL = 2048
A_HEADS = 8
A_DK = 64
A_DV = 128
A_WIDTH = A_HEADS * A_DV
B_HEAD = 64
B_WIDTH = D_MODEL - A_WIDTH
B_HEADS = B_WIDTH // B_HEAD
W_RANK = 64
A_RANK = 64
G_RANK = 128
RW_COLS = 3 * B_WIDTH + W_RANK + A_RANK + G_RANK
D_FF = 5632
FFN_CONV = 3
ATTN_CHUNK = 64
EPS = 1e-6
GN_EPS = 64e-5
LAM_INIT = 0.8 - 0.6 * math.exp(-0.3 * 0)
Q_SCALE = (A_DK ** -0.5) * math.log2(math.e)

LANES = 128
SUBLANES = 8
BF16_ROWS = 16
MXU_DIM = 256
GROUP_HEADS = MXU_DIM // B_HEAD
N_GROUPS = B_WIDTH // MXU_DIM
VMEM_LIMIT_BYTES = 56 * 1024 * 1024
NEG_BIG = -1e30


def _cparams(*sem):
    return pltpu.CompilerParams(dimension_semantics=sem, vmem_limit_bytes=VMEM_LIMIT_BYTES)


def _dot(a, b):
    return jnp.dot(a, b, preferred_element_type=F32)


def _dot_nt(a, b):
    return lax.dot_general(a, b, (((1,), (1,)), ((), ())), preferred_element_type=F32)


def _dot_tn(a, b):
    return lax.dot_general(a, b, (((0,), (0,)), ((), ())), preferred_element_type=F32)


def _idiv(x, n):
    assert n > 0 and n & (n - 1) == 0
    return x >> (n.bit_length() - 1)


def _sigmoid(x):
    return 1.0 / (1.0 + jnp.exp(-x))


def _split_bf16(x, terms):
    out = []
    rem = x
    for _ in range(terms):
        piece = rem.astype(BF16)
        out.append(piece)
        rem = rem - piece.astype(F32)
    return out


def _norm_mod(x, g, sc, sh):
    ms = jnp.mean(x * x, axis=-1, keepdims=True)
    return (x * lax.rsqrt(ms + EPS) * g) * (1.0 + sc) + sh


def _rms(x, g):
    ms = jnp.mean(x * x, axis=-1, keepdims=True)
    return x * lax.rsqrt(ms + EPS) * g


def _mod(ref):
    return ref[0] if len(ref.shape) == 3 else ref[...]


def _lam(q1_ref, k1_ref, q2_ref, k2_ref):
    s1 = jnp.sum(q1_ref[...] * k1_ref[...], axis=-1, keepdims=True)
    s2 = jnp.sum(q2_ref[...] * k2_ref[...], axis=-1, keepdims=True)
    return jnp.exp(s1) - jnp.exp(s2) + LAM_INIT


def _ada_kernel(c_ref, w_ref, b_ref, o_ref):
    c = c_ref[...]
    s = (c * _sigmoid(c)).astype(BF16)
    o_ref[...] = _dot(s, w_ref[...].astype(BF16)) + b_ref[...]


def _ada_call(c_all, w_ada, b_ada):
    rows = c_all.shape[0]
    n = w_ada.shape[1]
    tn = 1024
    return pl.pallas_call(
        _ada_kernel,
        grid=(n // tn,),
        in_specs=[
            pl.BlockSpec((rows, D_MODEL), lambda j: (0, 0)),
            pl.BlockSpec((D_MODEL, tn), lambda j: (0, j)),
            pl.BlockSpec((1, tn), lambda j: (0, j)),
        ],
        out_specs=pl.BlockSpec((rows, tn), lambda j: (0, j)),
        out_shape=jax.ShapeDtypeStruct((rows, n), F32),
        compiler_params=_cparams("arbitrary"),
        name="ada_mod",
    )(c_all, w_ada, b_ada)


def _qkv_kernel(x_ref, g_ref, sc_ref, sh_ref, w_ref, kf_ref, vf_ref, qkvb_ref):
    h = _norm_mod(x_ref[...], g_ref[...], _mod(sc_ref), _mod(sh_ref)).astype(BF16)
    q = _dot(h, w_ref[:, 0:A_WIDTH])
    qkvb_ref[:, 0:A_WIDTH] = (q * Q_SCALE).astype(BF16)
    k = _dot(h, w_ref[:, A_WIDTH:2 * A_WIDTH])
    kf_ref[...] = k
    qkvb_ref[:, A_WIDTH:2 * A_WIDTH] = k.astype(BF16)
    v = _dot(h, w_ref[:, 2 * A_WIDTH:3 * A_WIDTH])
    vf_ref[...] = v
    qkvb_ref[:, 2 * A_WIDTH:3 * A_WIDTH] = v.astype(BF16)


def _rw_kernel(x_ref, g_ref, sc_ref, sh_ref, w_ref, o_ref):
    h = _norm_mod(x_ref[...], g_ref[...], _mod(sc_ref), _mod(sh_ref)).astype(BF16)
    o_ref[...] = _dot(h, w_ref[:, 3 * A_WIDTH:])


def _mod_spec(mod, tm, tiles_per_seq, nidx):
    if mod.ndim == 3:
        if nidx == 2:
            return pl.BlockSpec((1, 1, D_MODEL), lambda i, j: (i // tiles_per_seq, 0, 0))
        return pl.BlockSpec((1, 1, D_MODEL), lambda i: (i // tiles_per_seq, 0, 0))
    if nidx == 2:
        return pl.BlockSpec((tm, D_MODEL), lambda i, j: (i, 0))
    return pl.BlockSpec((tm, D_MODEL), lambda i: (i, 0))


def _resident(shape):
    return pl.BlockSpec(shape, lambda i: (0,) * len(shape), pipeline_mode=pl.Buffered(1))


def _in_proj_call(x, g, sc, sh, w_in, tm, seq_len):
    m = x.shape[0]
    n_in = w_in.shape[1]
    common = [
        pl.BlockSpec((tm, D_MODEL), lambda i: (i, 0)),
        _resident((1, D_MODEL)),
        _mod_spec(sc, tm, seq_len // tm, 1),
        _mod_spec(sh, tm, seq_len // tm, 1),
    ]
    kf, vf, qkvb = pl.pallas_call(
        _qkv_kernel,
        grid=(m // tm,),
        in_specs=common + [_resident((D_MODEL, n_in))],
        out_specs=[
            pl.BlockSpec((tm, A_WIDTH), lambda i: (i, 0)),
            pl.BlockSpec((tm, A_WIDTH), lambda i: (i, 0)),
            pl.BlockSpec((tm, 3 * A_WIDTH), lambda i: (i, 0)),
        ],
        out_shape=[
            jax.ShapeDtypeStruct((m, A_WIDTH), F32),
            jax.ShapeDtypeStruct((m, A_WIDTH), F32),
            jax.ShapeDtypeStruct((m, 3 * A_WIDTH), BF16),
        ],
        compiler_params=_cparams("arbitrary"),
        name="in_proj_qkv",
    )(x, g, sc, sh, w_in)
    rw = pl.pallas_call(
        _rw_kernel,
        grid=(m // tm,),
        in_specs=common + [_resident((D_MODEL, n_in))],
        out_specs=pl.BlockSpec((tm, RW_COLS), lambda i: (i, 0)),
        out_shape=jax.ShapeDtypeStruct((m, RW_COLS), F32),
        compiler_params=_cparams("arbitrary"),
        name="in_proj_rw",
    )(x, g, sc, sh, w_in)
    return kf, vf, qkvb, rw


def _softmax_step(s, vb, m_ref, l_ref, acc_ref, idx):
    m_prev = m_ref[idx]
    m_new = jnp.maximum(m_prev, jnp.max(s, axis=-1, keepdims=True))
    alpha = jnp.exp2(m_prev - m_new)
    p = jnp.exp2(s - m_new)
    l_ref[idx] = alpha * l_ref[idx] + jnp.sum(p, axis=-1, keepdims=True)
    acc_ref[idx] = alpha * acc_ref[idx] + _dot(p.astype(BF16), vb)
    m_ref[idx] = m_new


def _subln(o, g):
    return _rms(o, g) * (1.0 - LAM_INIT)


PATTN_HEADS = 2
PATTN_QG = 512


def _pattn_kernel(q1_ref, k1_ref, q2_ref, k2_ref, q_ref, k_ref, v_ref, g_ref, o_ref,
                  m_scr, l_scr, acc_scr, s_scr, t_scr, *, tq):
    i = pl.program_id(2)
    lane = lax.broadcasted_iota(jnp.int32, (tq, A_DV), 1)
    m_scr[...] = jnp.full(m_scr.shape, NEG_BIG, F32)
    l_scr[...] = jnp.zeros(l_scr.shape, F32)
    acc_scr[...] = jnp.zeros(acc_scr.shape, F32)

    qg = min(PATTN_QG, 2 * tq)
    n_qg = 2 * tq // qg
    q_gs = []
    for hh in range(PATTN_HEADS):
        q = q_ref[0, :, hh * A_DV:(hh + 1) * A_DV]
        zero = jnp.zeros_like(q)
        q_st = jnp.concatenate([jnp.where(lane < A_DK, q, zero), jnp.where(lane >= A_DK, q, zero)], axis=0)
        q_gs.append([q_st[g * qg:(g + 1) * qg] for g in range(n_qg)])
    ones_k = jnp.ones((SUBLANES, tq), BF16)
    chains = [(hh, g) for g in range(n_qg) for hh in range(PATTN_HEADS)]

    def scores(j):
        r0 = pl.multiple_of(j * tq, tq)
        kbs = [k_ref[0, pl.ds(r0, tq), hh * A_DV:(hh + 1) * A_DV] for hh in range(PATTN_HEADS)]
        return [_dot_nt(kbs[hh], q_gs[hh][g]) for hh, g in chains]

    def update(j, s_ref, masks):
        r0 = pl.multiple_of(j * tq, tq)
        vbs = [v_ref[0, pl.ds(r0, tq), hh * A_DV:(hh + 1) * A_DV] for hh in range(PATTN_HEADS)]
        for ci, (hh, g) in enumerate(chains):
            cs = slice(g * qg, (g + 1) * qg)
            s_t = s_ref[ci]
            if masks is not None:
                s_t = jnp.where(masks[g], s_t, NEG_BIG)
            m_prev = m_scr[hh, :, cs]
            m_new = jnp.maximum(m_prev, jnp.max(s_t, axis=0, keepdims=True))
            alpha = jnp.exp2(m_prev - m_new)
            p_b = jnp.exp2(s_t - m_new).astype(BF16)
            l_scr[hh, :, cs] = alpha * l_scr[hh, :, cs] + _dot(ones_k, p_b)[0:1]
            acc_scr[hh, :, cs] = alpha * acc_scr[hh, :, cs] + _dot_tn(vbs[hh], p_b)
            m_scr[hh, :, cs] = m_new

    def put(s_ref, j):
        for ci, s_t in enumerate(scores(j)):
            s_ref[ci] = s_t

    rk = lax.broadcasted_iota(jnp.int32, (tq, qg), 0)
    cq = lax.broadcasted_iota(jnp.int32, (tq, qg), 1)
    diag = [_idiv(rk, ATTN_CHUNK) <= _idiv((cq + g * qg) & (tq - 1), ATTN_CHUNK) for g in range(n_qg)]

    put(s_scr, 0)

    def body(jp, carry):
        j = 2 * jp
        put(t_scr, j + 1)
        update(j, s_scr, None)
        put(s_scr, j + 2)
        update(j + 1, t_scr, None)
        return carry

    lax.fori_loop(0, i // 2, body, 0)

    @pl.when(i % 2 == 0)
    def _():
        update(i, s_scr, diag)

    @pl.when(i % 2 == 1)
    def _():
        put(t_scr, i)
        update(i - 1, s_scr, None)
        update(i, t_scr, diag)

    lam = _lam(q1_ref, k1_ref, q2_ref, k2_ref)
    for hh in range(PATTN_HEADS):
        o_t = (acc_scr[hh, :, 0:tq] / l_scr[hh, :, 0:tq]
               - lam * (acc_scr[hh, :, tq:2 * tq] / l_scr[hh, :, tq:2 * tq]))
        o_ref[0, :, hh * A_DV:(hh + 1) * A_DV] = _subln(o_t.T, g_ref[...]).astype(BF16)


def _lam_specs(nidx):
    if nidx == 3:
        return [pl.BlockSpec((1, A_DK), lambda b, h, i: (0, 0))] * 4
    return [pl.BlockSpec((1, A_DK), lambda b, j: (0, 0))] * 4


def _pattn_call(lams, qkvb, g_subln, tq):
    bsz, t, _ = qkvb.shape
    hw = PATTN_HEADS * A_DV
    ng = A_HEADS // PATTN_HEADS
    return pl.pallas_call(
        functools.partial(_pattn_kernel, tq=tq),
        grid=(bsz, ng, t // tq),
        in_specs=_lam_specs(3) + [
            pl.BlockSpec((1, tq, hw), lambda b, h, i: (b, i, h)),
            pl.BlockSpec((1, t, hw), lambda b, h, i: (b, 0, ng + h)),
            pl.BlockSpec((1, t, hw), lambda b, h, i: (b, 0, 2 * ng + h)),
            pl.BlockSpec((1, A_DV), lambda b, h, i: (0, 0)),
        ],
        out_specs=pl.BlockSpec((1, tq, hw), lambda b, h, i: (b, i, h)),
        out_shape=jax.ShapeDtypeStruct((bsz, t, A_WIDTH), BF16),
        scratch_shapes=[
            pltpu.VMEM((PATTN_HEADS, 1, 2 * tq), F32),
            pltpu.VMEM((PATTN_HEADS, 1, 2 * tq), F32),
            pltpu.VMEM((PATTN_HEADS, A_DV, 2 * tq), F32),
        ] + [pltpu.VMEM((PATTN_HEADS * (2 * tq // min(PATTN_QG, 2 * tq)), tq, min(PATTN_QG, 2 * tq)), F32)] * 2,
        compiler_params=_cparams("arbitrary", "arbitrary", "arbitrary"),
        name="prompt_attention",
    )(*lams, qkvb, qkvb, qkvb, g_subln)


N_VQ = 2 * A_HEADS


def _sattn_kernel(q1_ref, k1_ref, q2_ref, k2_ref, q_ref, kn_ref, vn_ref, ck_ref, cv_ref, g_ref, o_ref,
                  qb_scr, m_scr, l_scr, acc_scr, *, tnew):
    j = pl.program_id(1)
    nvq = N_VQ * tnew

    @pl.when(j == 0)
    def _():
        q = q_ref[0]
        qt = jnp.concatenate([q] * N_VQ, axis=0)
        row = lax.broadcasted_iota(jnp.int32, (nvq, A_WIDTH), 0)
        col = lax.broadcasted_iota(jnp.int32, (nvq, A_WIDTH), 1)
        qb_scr[...] = jnp.where(_idiv(row, tnew) == _idiv(col, A_DK), qt, jnp.zeros_like(qt))
        m_scr[...] = jnp.full(m_scr.shape, NEG_BIG, F32)
        l_scr[...] = jnp.zeros(l_scr.shape, F32)
        acc_scr[...] = jnp.zeros(acc_scr.shape, F32)

    def step(kb, vb):
        _softmax_step(_dot_nt(qb_scr[...], kb), vb, m_scr, l_scr, acc_scr, 0)

    def cached_rows(ref, p0, n):
        heads = [ref[0, pl.ds(p0 * A_HEADS + h, n, stride=A_HEADS), :] for h in range(A_HEADS)]
        return jnp.concatenate(heads, axis=-1).astype(BF16)

    tk = ck_ref.shape[1] // A_HEADS
    n_sub = max(1, tk // MXU_DIM)
    sub = tk // n_sub
    kbs = [cached_rows(ck_ref, c * sub, sub) for c in range(n_sub)]
    s_all = [_dot_nt(qb_scr[...], kb) for kb in kbs]
    for c in range(n_sub):
        _softmax_step(s_all[c], cached_rows(cv_ref, c * sub, sub), m_scr, l_scr, acc_scr, 0)

    @pl.when(j == pl.num_programs(1) - 1)
    def _():
        step(kn_ref[0], vn_ref[0])
        lam = _lam(q1_ref, k1_ref, q2_ref, k2_ref)
        g = g_ref[...]
        for h in range(A_HEADS):
            cs = slice(h * A_DV, (h + 1) * A_DV)
            r0 = slice((2 * h) * tnew, (2 * h + 1) * tnew)
            r1 = slice((2 * h + 1) * tnew, (2 * h + 2) * tnew)
            o0 = acc_scr[0, r0, cs] / l_scr[0, r0, :]
            o1 = acc_scr[0, r1, cs] / l_scr[0, r1, :]
            o_ref[0, :, cs] = _subln(o0 - lam * o1, g).astype(BF16)


def _sattn_call(lams, qkvb, cache_k, cache_v, g_subln, tk):
    bsz, tnew, _ = qkvb.shape
    past = cache_k.shape[1] // A_HEADS
    nvq = N_VQ * tnew
    return pl.pallas_call(
        functools.partial(_sattn_kernel, tnew=tnew),
        grid=(bsz, past // tk),
        in_specs=_lam_specs(2) + [
            pl.BlockSpec((1, tnew, A_WIDTH), lambda b, j: (b, 0, 0)),
            pl.BlockSpec((1, tnew, A_WIDTH), lambda b, j: (b, 0, 1)),
            pl.BlockSpec((1, tnew, A_WIDTH), lambda b, j: (b, 0, 2)),
            pl.BlockSpec((1, tk * A_HEADS, A_DV), lambda b, j: (b, j, 0)),
            pl.BlockSpec((1, tk * A_HEADS, A_DV), lambda b, j: (b, j, 0)),
            pl.BlockSpec((1, A_DV), lambda b, j: (0, 0)),
        ],
        out_specs=pl.BlockSpec((1, tnew, A_WIDTH), lambda b, j: (b, 0, 0)),
        out_shape=jax.ShapeDtypeStruct((bsz, tnew, A_WIDTH), BF16),
        scratch_shapes=[
            pltpu.VMEM((nvq, A_WIDTH), BF16),
            pltpu.VMEM((1, nvq, 1), F32),
            pltpu.VMEM((1, nvq, 1), F32),
            pltpu.VMEM((1, nvq, A_WIDTH), F32),
        ],
        compiler_params=_cparams("arbitrary", "arbitrary"),
        name="sample_attention",
    )(*lams, qkvb, qkvb, qkvb, cache_k, cache_v, g_subln)


def _head_sum(x, ones_bd):
    return _dot(x.astype(BF16), ones_bd)


def _rwkv_kernel(p_ref, ph_ref, st_ref, s0_ref, mu_ref, w0_ref, a0_ref, kk_ref, ka_ref, rk_ref,
                 lnw_ref, lnb_ref, ww2_ref, wa2_ref, wg2_ref, ob_ref, so_ref,
                 r_scr, k_scr, v_scr, na_scr, nb_scr, ld_scr, cum_scr, g_scr, bon_scr, y_scr, s_scr,
                 *, tb, chunk):
    i = pl.program_id(1)
    n_seq = p_ref.shape[0]
    c_len = chunk
    n_chunks = tb // c_len

    row_g = lax.broadcasted_iota(jnp.int32, (MXU_DIM, MXU_DIM), 0)
    col_g = lax.broadcasted_iota(jnp.int32, (MXU_DIM, MXU_DIM), 1)
    bd_mask = _idiv(row_g, B_HEAD) == _idiv(col_g, B_HEAD)
    ones_bd = jnp.where(bd_mask, 1.0, 0.0).astype(BF16)

    @pl.when(i == 0)
    def _():
        rep = jnp.where(
            lax.broadcasted_iota(jnp.int32, (B_HEAD, MXU_DIM), 0)
            == (lax.broadcasted_iota(jnp.int32, (B_HEAD, MXU_DIM), 1) & (B_HEAD - 1)), 1.0, 0.0).astype(BF16)
        for b in range(n_seq):
            for q in range(N_GROUPS):
                wide = None
                for piece in _split_bf16(s0_ref[b, q], 3):
                    d = _dot(piece, rep)
                    wide = d if wide is None else wide + d
                s_scr[b, q] = jnp.where(bd_mask, wide, 0.0)

    row_t = lax.broadcasted_iota(jnp.int32, (tb, tb), 0)
    col_t = lax.broadcasted_iota(jnp.int32, (tb, tb), 1)
    tri = jnp.where((_idiv(row_t, c_len) == _idiv(col_t, c_len)) & (col_t <= row_t), 1.0, 0.0).astype(BF16)
    w_cat = jnp.concatenate([ww2_ref[...], wa2_ref[...]], axis=0)

    for b in range(n_seq):
        p = p_ref[b]
        prev_row = jnp.where(i == 0, st_ref[b], ph_ref[b, SUBLANES - 1:SUBLANES, :])
        rowi = lax.broadcasted_iota(jnp.int32, p.shape, 0)
        p_prev = jnp.where(rowi == 0, prev_row, pltpu.roll(p, 1, 0))
        xs = p + (p_prev - p) * mu_ref[...]
        r = xs[:, 0:B_WIDTH]
        k = xs[:, B_WIDTH:2 * B_WIDTH]
        v = xs[:, 2 * B_WIDTH:3 * B_WIDTH]
        wa = xs[:, 3 * B_WIDTH:3 * B_WIDTH + W_RANK + A_RANK]
        gd = xs[:, 3 * B_WIDTH + W_RANK + A_RANK:]

        lane_wa = lax.broadcasted_iota(jnp.int32, wa.shape, 1)
        zero_wa = jnp.zeros_like(wa)
        w_lin = _dot(jnp.where(lane_wa < W_RANK, jnp.tanh(wa), zero_wa).astype(BF16), w_cat)
        a_lin = _dot(jnp.where(lane_wa >= W_RANK, wa, zero_wa).astype(BF16), w_cat)
        logd = -math.exp(-0.5) * _sigmoid(w0_ref[...] + w_lin)
        a = _sigmoid(a0_ref[...] + a_lin)
        g_scr[b] = _dot(_sigmoid(gd).astype(BF16), wg2_ref[...])

        k2 = k * (1.0 + (a - 1.0) * ka_ref[...])
        kk = k * kk_ref[...]
        rkk = r * k2 * rk_ref[...]
        for q in range(N_GROUPS):
            ls = slice(q * MXU_DIM, (q + 1) * MXU_DIM)
            kq = kk[:, ls]
            kn = kq * lax.rsqrt(jnp.maximum(_head_sum(kq * kq, ones_bd), 1e-24))
            na_scr[b, :, ls] = -kn
            nb_scr[b, :, ls] = kn * a[:, ls]
            bon_scr[b, :, ls] = _head_sum(rkk[:, ls], ones_bd) * v[:, ls]
        r_scr[b] = r
        k_scr[b] = k2
        v_scr[b] = v
        ld_scr[b] = logd

        cum = None
        for piece in _split_bf16(logd, 2):
            d = _dot(tri, piece)
            cum = d if cum is None else cum + d
        cum_scr[b] = cum

    gc = GROUP_HEADS * c_len
    lane_q = lax.broadcasted_iota(jnp.int32, (1, MXU_DIM), 1)
    head_masks = [_idiv(lane_q, B_HEAD) == j for j in range(GROUP_HEADS)]
    row_s = lax.broadcasted_iota(jnp.int32, (gc, gc), 0)
    col_s = lax.broadcasted_iota(jnp.int32, (gc, gc), 1)
    same_head = _idiv(row_s, c_len) == _idiv(col_s, c_len)
    strict = same_head & (col_s < row_s)
    incl = same_head & (col_s <= row_s)
    eye = jnp.where(col_s == row_s, 1.0, 0.0).astype(F32)
    n_double = int(math.log2(c_len)) - 1
    chains = [(b, q) for q in range(N_GROUPS) for b in range(n_seq)]
    groups = range(len(chains))

    def stack(x):
        z = jnp.zeros_like(x)
        return jnp.concatenate([jnp.where(head_masks[j], x, z) for j in range(GROUP_HEADS)], axis=0)

    def unstack(x):
        out = x[0:c_len]
        for j in range(1, GROUP_HEADS):
            out = out + x[j * c_len:(j + 1) * c_len]
        return out

    def chunk_body(c, carry):
        r0 = pl.multiple_of(c * c_len, c_len)
        rows = pl.ds(r0, c_len)
        a_s, r_s, b_s, k_s, be_s, ke_s, v_s, dec = [], [], [], [], [], [], [], []
        for b, q in chains:
            ls = slice(q * MXU_DIM, (q + 1) * MXU_DIM)
            cum_q = cum_scr[b, rows, ls]
            k_q = k_scr[b, rows, ls]
            nb_q = nb_scr[b, rows, ls]
            cum_last = cum_q[c_len - 1:c_len, :]
            e_out = jnp.exp(-cum_q)
            e_end = jnp.exp(cum_last - cum_q)
            a_s.append(stack((na_scr[b, rows, ls] * jnp.exp(cum_q - ld_scr[b, rows, ls])).astype(BF16)))
            r_s.append(stack((r_scr[b, rows, ls] * jnp.exp(cum_q)).astype(BF16)))
            b_s.append(stack((nb_q * e_out).astype(BF16)))
            k_s.append(stack((k_q * e_out).astype(BF16)))
            be_s.append(stack((nb_q * e_end).astype(BF16)))
            ke_s.append(stack((k_q * e_end).astype(BF16)))
            v_s.append(stack(v_scr[b, rows, ls].astype(BF16)))
            dec.append(jnp.exp(cum_last))

        if gc % LANES == 0:
            g_all = [_dot_nt(jnp.concatenate([a_s[q], r_s[q]], axis=0),
                             jnp.concatenate([b_s[q], k_s[q]], axis=0)) for q in groups]
            g_ab = [g[0:gc, 0:gc] for g in g_all]
            g_ak = [g[0:gc, gc:2 * gc] for g in g_all]
            g_rb = [g[gc:2 * gc, 0:gc] for g in g_all]
            g_rk = [g[gc:2 * gc, gc:2 * gc] for g in g_all]
        else:
            g_ab = [_dot_nt(a_s[q], b_s[q]) for q in groups]
            g_ak = [_dot_nt(a_s[q], k_s[q]) for q in groups]
            g_rb = [_dot_nt(r_s[q], b_s[q]) for q in groups]
            g_rk = [_dot_nt(r_s[q], k_s[q]) for q in groups]
        aab = [jnp.where(strict, g, 0.0) for g in g_ab]
        aak = [jnp.where(strict, g, 0.0).astype(BF16) for g in g_ak]
        arb = [jnp.where(incl, g, 0.0).astype(BF16) for g in g_rb]
        ark = [jnp.where(incl, g, 0.0).astype(BF16) for g in g_rk]

        t_m = [eye + x for x in aab]
        pw = [x.astype(BF16) for x in aab]
        pw = [_dot(x, x) for x in pw]
        for m in range(n_double):
            pb = [x.astype(BF16) for x in pw]
            if m + 1 < n_double:
                pw = [_dot(x, x) for x in pb]
            t_m = [t_m[q] + _dot(t_m[q].astype(BF16), pb[q]) for q in groups]
        t_b = [x.astype(BF16) for x in t_m]

        av = [_dot(aak[q], v_s[q]) for q in groups]

        s_prev = [s_scr[b, q] for b, q in chains]
        ar = [_dot_nt(jnp.concatenate([a_s[q], r_s[q]], axis=0), s_prev[q].astype(BF16)) for q in groups]
        u_b = [_dot(t_b[q], (ar[q][0:gc] + av[q]).astype(BF16)).astype(BF16) for q in groups]
        uv = [jnp.concatenate([u_b[q], v_s[q]], axis=0) for q in groups]
        if gc % LANES == 0:
            y_st = [ar[q][gc:2 * gc] + _dot(jnp.concatenate([arb[q], ark[q]], axis=1), uv[q])
                    for q in groups]
        else:
            y_st = [ar[q][gc:2 * gc] + _dot(arb[q], u_b[q]) + _dot(ark[q], v_s[q]) for q in groups]
        s_new = [s_prev[q] * dec[q]
                 + _dot_tn(uv[q], jnp.concatenate([be_s[q], ke_s[q]], axis=0)) for q in groups]
        for ci, (b, q) in enumerate(chains):
            s_scr[b, q] = s_new[ci]
            y_scr[b, rows, q * MXU_DIM:(q + 1) * MXU_DIM] = unstack(y_st[ci])
        return carry

    lax.fori_loop(0, n_chunks, chunk_body, 0)

    @pl.when(i == pl.num_programs(1) - 1)
    def _():
        rep_t = jnp.where(
            (lax.broadcasted_iota(jnp.int32, (MXU_DIM, B_HEAD), 0) & (B_HEAD - 1))
            == lax.broadcasted_iota(jnp.int32, (MXU_DIM, B_HEAD), 1), 1.0, 0.0).astype(BF16)
        for b, q in chains:
            narrow = None
            for piece in _split_bf16(s_scr[b, q], 3):
                d = _dot(piece, rep_t)
                narrow = d if narrow is None else narrow + d
            so_ref[b, q] = narrow

    inv_n = 1.0 / B_HEAD
    for b, q in chains:
        ls = slice(q * MXU_DIM, (q + 1) * MXU_DIM)
        y = y_scr[b, :, ls]
        mu = _head_sum(y, ones_bd) * inv_n
        d = y - mu
        var = _head_sum(d * d, ones_bd) * inv_n
        yn = d * lax.rsqrt(var + GN_EPS) * lnw_ref[:, ls] + lnb_ref[:, ls]
        ob_ref[b, :, ls] = ((yn + bon_scr[b, :, ls]) * g_scr[b, :, ls]).astype(BF16)


def _rwkv_call(rw, shift0, s0, prm, tb, chunk, n_seq):
    bsz, t, _ = rw.shape
    assert bsz % n_seq == 0
    nt = t // tb
    hb = tb // SUBLANES
    vec = lambda n: pl.BlockSpec((1, n), lambda b, i: (0, 0))
    return pl.pallas_call(
        functools.partial(_rwkv_kernel, tb=tb, chunk=chunk),
        grid=(bsz // n_seq, nt),
        in_specs=[
            pl.BlockSpec((n_seq, tb, RW_COLS), lambda b, i: (b, i, 0)),
            pl.BlockSpec((n_seq, SUBLANES, RW_COLS), lambda b, i: (b, jnp.maximum(i * hb - 1, 0), 0)),
            pl.BlockSpec((n_seq, 1, RW_COLS), lambda b, i: (b, 0, 0)),
            pl.BlockSpec((n_seq, N_GROUPS, MXU_DIM, B_HEAD), lambda b, i: (b, 0, 0, 0)),
            vec(RW_COLS), vec(B_WIDTH), vec(B_WIDTH), vec(B_WIDTH), vec(B_WIDTH), vec(B_WIDTH),
            vec(B_WIDTH), vec(B_WIDTH),
            pl.BlockSpec((W_RANK, B_WIDTH), lambda b, i: (0, 0)),
            pl.BlockSpec((A_RANK, B_WIDTH), lambda b, i: (0, 0)),
            pl.BlockSpec((G_RANK, B_WIDTH), lambda b, i: (0, 0)),
        ],
        out_specs=[
            pl.BlockSpec((n_seq, tb, B_WIDTH), lambda b, i: (b, i, 0)),
            pl.BlockSpec((n_seq, N_GROUPS, MXU_DIM, B_HEAD), lambda b, i: (b, 0, 0, 0)),
        ],
        out_shape=[
            jax.ShapeDtypeStruct((bsz, t, B_WIDTH), BF16),
            jax.ShapeDtypeStruct((bsz, N_GROUPS, MXU_DIM, B_HEAD), F32),
        ],
        scratch_shapes=[pltpu.VMEM((n_seq, tb, B_WIDTH), F32)] * 10
        + [pltpu.VMEM((n_seq, N_GROUPS, MXU_DIM, MXU_DIM), F32)],
        compiler_params=_cparams("arbitrary", "arbitrary"),
        name="rwkv7_mix",
    )(rw, rw, shift0, s0, prm["mu_shift"], prm["w0"], prm["a0"], prm["k_k"], prm["k_a"], prm["r_k"],
      prm["ln_x_w"], prm["ln_x_b"], prm["w_w2"], prm["w_a2"], prm["w_g2"])


def _state_to_groups(s):
    return s.reshape(s.shape[0], N_GROUPS, MXU_DIM, B_HEAD)


def _groups_to_state(sg):
    return sg.reshape(sg.shape[0], B_HEADS, B_HEAD, B_HEAD)


def _mixout_kernel(oa_ref, ob_ref, x_ref, w_ref, g_ref, gt_ref, o_ref):
    mix = _dot(oa_ref[...], w_ref[0:A_WIDTH, :]) + _dot(ob_ref[...], w_ref[A_WIDTH:D_MODEL, :])
    o_ref[...] = x_ref[...] + _mod(gt_ref) * _rms(mix, g_ref[...])


def _mixout_call(oa, ob, x, w_out, g, gt, tm, tiles_per_seq):
    m = x.shape[0]
    return pl.pallas_call(
        _mixout_kernel,
        grid=(m // tm,),
        in_specs=[
            pl.BlockSpec((tm, A_WIDTH), lambda i: (i, 0)),
            pl.BlockSpec((tm, B_WIDTH), lambda i: (i, 0)),
            pl.BlockSpec((tm, D_MODEL), lambda i: (i, 0)),
            pl.BlockSpec((D_MODEL, D_MODEL), lambda i: (0, 0)),
            pl.BlockSpec((1, D_MODEL), lambda i: (0, 0)),
            _mod_spec(gt, tm, tiles_per_seq, 1),
        ],
        out_specs=pl.BlockSpec((tm, D_MODEL), lambda i: (i, 0)),
        out_shape=jax.ShapeDtypeStruct((m, D_MODEL), F32),
        compiler_params=_cparams("arbitrary"),
        name="mix_out",
    )(oa, ob, x, w_out, g, gt)


FFN_HALO = BF16_ROWS
FFN_TF = 512


def _gated(zg, zv):
    return (zg * _sigmoid(zg) * zv).astype(BF16)


def _ffn_kernel(x_ref, xh_ref, cpg_ref, cpv_ref, g_ref, sc_ref, sh_ref, gt_ref, wg_ref, wv_ref,
                cg_ref, cv_ref, wd_ref, gp_ref, o_ref, clg_ref, clv_ref, h_scr,
                *, tm, tiles_per_seq):
    mi = pl.program_id(0)
    f = pl.program_id(1)
    first = (mi % tiles_per_seq) == 0

    @pl.when(f == 0)
    def _():
        g, sc, sh = g_ref[...], sc_ref[0], sh_ref[0]
        h_scr[FFN_HALO:, :] = _norm_mod(x_ref[...], g, sc, sh).astype(BF16)
        h_scr[:FFN_HALO, :] = _norm_mod(xh_ref[...], g, sc, sh).astype(BF16)
        o_ref[...] = jnp.zeros(o_ref.shape, F32)

    h = h_scr[...]
    row = lax.broadcasted_iota(jnp.int32, (FFN_HALO, FFN_TF), 0)

    def conv(u, cp_ref, c_ref, cl_ref):
        cp = cp_ref[0]
        head = u[:FFN_HALO]
        head_first = jnp.where(row == FFN_HALO - 2, cp[0:1],
                               jnp.where(row == FFN_HALO - 1, cp[1:2], head))
        ue = jnp.concatenate([jnp.where(first, head_first, head), u[FFN_HALO:]], axis=0)
        cl_ref[0] = ue[tm + FFN_HALO - SUBLANES:]
        c = c_ref[...]
        return (c[0:1] * pltpu.roll(ue, 2, 0)[FFN_HALO:] + c[1:2] * pltpu.roll(ue, 1, 0)[FFN_HALO:]
                + c[2:3] * ue[FFN_HALO:])

    zg = conv(_dot(h, wg_ref[...]), cpg_ref, cg_ref, clg_ref)
    zv = conv(_dot(h, wv_ref[...]), cpv_ref, cv_ref, clv_ref)
    o_ref[...] += _dot(_gated(zg, zv), wd_ref[...])

    @pl.when(f == pl.num_programs(1) - 1)
    def _():
        o_ref[...] = x_ref[...] + gt_ref[0] * _rms(o_ref[...], gp_ref[...])


def _ffn_call(x, conv_prev, g, sc, sh, gt, w_up, w_conv, w_down, gp, tm, tiles_per_seq):
    m = x.shape[0]
    nseq = conv_prev.shape[0]
    nf = D_FF // FFN_TF
    hb = tm // FFN_HALO
    seq = lambda i: i // tiles_per_seq
    modspec = pl.BlockSpec((1, 1, D_MODEL), lambda i, f: (seq(i), 0, 0))
    return pl.pallas_call(
        functools.partial(_ffn_kernel, tm=tm, tiles_per_seq=tiles_per_seq),
        grid=(m // tm, nf),
        in_specs=[
            pl.BlockSpec((tm, D_MODEL), lambda i, f: (i, 0), pipeline_mode=pl.Buffered(1)),
            pl.BlockSpec((FFN_HALO, D_MODEL), lambda i, f: (jnp.maximum(i * hb - 1, 0), 0)),
            pl.BlockSpec((1, FFN_CONV - 1, FFN_TF), lambda i, f: (seq(i), 0, f)),
            pl.BlockSpec((1, FFN_CONV - 1, FFN_TF), lambda i, f: (seq(i), 0, f + nf)),
            pl.BlockSpec((1, D_MODEL), lambda i, f: (0, 0)),
            modspec, modspec, modspec,
            pl.BlockSpec((D_MODEL, FFN_TF), lambda i, f: (0, f)),
            pl.BlockSpec((D_MODEL, FFN_TF), lambda i, f: (0, f + nf)),
            pl.BlockSpec((FFN_CONV, FFN_TF), lambda i, f: (0, f)),
            pl.BlockSpec((FFN_CONV, FFN_TF), lambda i, f: (0, f + nf)),
            pl.BlockSpec((FFN_TF, D_MODEL), lambda i, f: (f, 0)),
            pl.BlockSpec((1, D_MODEL), lambda i, f: (0, 0)),
        ],
        out_specs=[
            pl.BlockSpec((tm, D_MODEL), lambda i, f: (i, 0)),
            pl.BlockSpec((1, SUBLANES, FFN_TF), lambda i, f: (i, 0, f)),
            pl.BlockSpec((1, SUBLANES, FFN_TF), lambda i, f: (i, 0, f)),
        ],
        out_shape=[
            jax.ShapeDtypeStruct((m, D_MODEL), F32),
            jax.ShapeDtypeStruct((m // tm, SUBLANES, D_FF), F32),
            jax.ShapeDtypeStruct((m // tm, SUBLANES, D_FF), F32),
        ],
        scratch_shapes=[pltpu.VMEM((tm + FFN_HALO, D_MODEL), BF16)],
        compiler_params=_cparams("arbitrary", "arbitrary"),
        name="conv_ffn_prompt",
    )(x, x, conv_prev, conv_prev, g, sc, sh, gt, w_up, w_up, w_conv, w_conv, w_down, gp)


def _ffn_s_kernel(x_ref, cpg_ref, cpv_ref, g_ref, sc_ref, sh_ref, gt_ref, wg_ref, wv_ref,
                  cg_ref, cv_ref, wd_ref, gp_ref, o_ref, ug_ref, uv_ref, h_scr, acc_scr, *, seq_len):
    f = pl.program_id(1)

    @pl.when(f == 0)
    def _():
        h_scr[...] = _norm_mod(x_ref[...], g_ref[...], sc_ref[...], sh_ref[...]).astype(BF16)
        acc_scr[...] = jnp.zeros(acc_scr.shape, F32)

    h = h_scr[...]
    tm = h.shape[0]
    pos = lax.broadcasted_iota(jnp.int32, (tm, FFN_TF), 0) & (seq_len - 1)
    n_cp = cpg_ref.shape[0]
    trow = lax.broadcasted_iota(jnp.int32, (tm, n_cp), 0)
    ccol = lax.broadcasted_iota(jnp.int32, (tm, n_cp), 1)
    tpos = trow & (seq_len - 1)
    base = 2 * _idiv(trow, seq_len)
    e1 = jnp.where((tpos == 0) & (ccol == base + 1), 1.0, 0.0).astype(BF16)
    e2 = jnp.where(((tpos == 0) & (ccol == base)) | ((tpos == 1) & (ccol == base + 1)), 1.0, 0.0).astype(BF16)

    def conv(u, cp_ref, c_ref, u_ref):
        u_ref[...] = u
        p1 = p2 = None
        for piece in _split_bf16(cp_ref[...], 3):
            d1, d2 = _dot(e1, piece), _dot(e2, piece)
            p1 = d1 if p1 is None else p1 + d1
            p2 = d2 if p2 is None else p2 + d2
        u1 = jnp.where(pos >= 1, pltpu.roll(u, 1, 0), p1)
        u2 = jnp.where(pos >= 2, pltpu.roll(u, 2, 0), p2)
        c = c_ref[...]
        return c[0:1] * u2 + c[1:2] * u1 + c[2:3] * u

    zg = conv(_dot(h, wg_ref[...]), cpg_ref, cg_ref, ug_ref)
    zv = conv(_dot(h, wv_ref[...]), cpv_ref, cv_ref, uv_ref)
    acc_scr[...] += _dot(_gated(zg, zv), wd_ref[...])

    @pl.when(f == pl.num_programs(1) - 1)
    def _():
        o_ref[...] = x_ref[...] + gt_ref[...] * _rms(acc_scr[...], gp_ref[...])


def _ffn_s_call(x, conv_prev, g, sc, sh, gt, w_up, w_conv, w_down, gp, seq_len, tm):
    m = x.shape[0]
    assert tm % seq_len == 0 and seq_len & (seq_len - 1) == 0
    nf = D_FF // FFN_TF
    n_cp = (tm // seq_len) * (FFN_CONV - 1)
    full = pl.BlockSpec((tm, D_MODEL), lambda i, f: (i, 0))
    full_in = pl.BlockSpec((tm, D_MODEL), lambda i, f: (i, 0), pipeline_mode=pl.Buffered(1))
    lo = pl.BlockSpec((tm, FFN_TF), lambda i, f: (i, f))
    return pl.pallas_call(
        functools.partial(_ffn_s_kernel, seq_len=seq_len),
        grid=(m // tm, nf),
        in_specs=[
            full_in,
            pl.BlockSpec((n_cp, FFN_TF), lambda i, f: (i, f)),
            pl.BlockSpec((n_cp, FFN_TF), lambda i, f: (i, f + nf)),
            pl.BlockSpec((1, D_MODEL), lambda i, f: (0, 0)),
            full_in, full_in, full_in,
            pl.BlockSpec((D_MODEL, FFN_TF), lambda i, f: (0, f)),
            pl.BlockSpec((D_MODEL, FFN_TF), lambda i, f: (0, f + nf)),
            pl.BlockSpec((FFN_CONV, FFN_TF), lambda i, f: (0, f)),
            pl.BlockSpec((FFN_CONV, FFN_TF), lambda i, f: (0, f + nf)),
            pl.BlockSpec((FFN_TF, D_MODEL), lambda i, f: (f, 0)),
            pl.BlockSpec((1, D_MODEL), lambda i, f: (0, 0)),
        ],
        out_specs=[full, lo, lo],
        out_shape=[
            jax.ShapeDtypeStruct((m, D_MODEL), F32),
            jax.ShapeDtypeStruct((m, D_FF), F32),
            jax.ShapeDtypeStruct((m, D_FF), F32),
        ],
        scratch_shapes=[pltpu.VMEM((tm, D_MODEL), BF16), pltpu.VMEM((tm, D_MODEL), F32)],
        compiler_params=_cparams("arbitrary", "arbitrary"),
        name="conv_ffn_sample",
    )(x, conv_prev, conv_prev, g, sc, sh, gt, w_up, w_up, w_conv, w_conv, w_down, gp)


def _seq_tile(t, cap):
    tm = min(t, cap)
    assert t % tm == 0
    return tm


@jax.jit
def _forward(x_prompt, x_sample, c_prompt, c_sample, cache_k, cache_v, state_wkv, state_shift,
             state_ffn_conv, w_ada, b_ada, g_pre_mix, g_post_mix, g_pre_ffn, g_post_ffn, w_in,
             lam_q1, lam_k1, lam_q2, lam_k2, g_subln, mu_shift, w0, w_w2, a0, w_a2, w_g2, k_k, k_a,
             r_k, ln_x_w, ln_x_b, w_out, w_up, w_conv_ffn, w_down):
    bp, t, _ = x_prompt.shape
    bs, ts, _ = x_sample.shape
    past = cache_k.shape[2]
    assert w_ada.shape[0] == 1, "single-layer problem"

    n_c = bp + bs
    n_c_pad = -(-n_c // SUBLANES) * SUBLANES
    c_all = jnp.concatenate([c_prompt, c_sample, jnp.zeros((n_c_pad - n_c, D_MODEL), F32)], axis=0)
    mod = _ada_call(c_all, w_ada[0], b_ada)
    sh_m, sc_m, gt_m, sh_f, sc_f, gt_f = [mod[:, i * D_MODEL:(i + 1) * D_MODEL] for i in range(6)]
    p_mod = lambda a: a[:bp].reshape(bp, 1, D_MODEL)
    s_mod = lambda a: jnp.repeat(a[bp:n_c], ts, axis=0)

    w_in_b = w_in[0].astype(BF16)
    w_out_b = w_out[0].astype(BF16)
    w_up_b = w_up[0].astype(BF16)
    w_down_b = w_down[0].astype(BF16)
    lams = (lam_q1, lam_k1, lam_q2, lam_k2)
    prm = {
        "mu_shift": mu_shift, "w0": w0, "a0": a0, "k_k": k_k, "k_a": k_a,
        "r_k": r_k.reshape(1, B_WIDTH), "ln_x_w": ln_x_w, "ln_x_b": ln_x_b,
        "w_w2": w_w2[0].astype(BF16), "w_a2": w_a2[0].astype(BF16), "w_g2": w_g2[0].astype(BF16),
    }

    tm = _seq_tile(t, 512)
    tps = t // tm
    xp = x_prompt.reshape(bp * t, D_MODEL)
    kf, vf, qkvb, rw = _in_proj_call(xp, g_pre_mix, p_mod(sc_m), p_mod(sh_m), w_in_b, tm, t)
    oa = _pattn_call(lams, qkvb.reshape(bp, t, 3 * A_WIDTH), g_subln, _seq_tile(t, 512))
    ob, s_last = _rwkv_call(rw.reshape(bp, t, RW_COLS), jnp.zeros((bp, 1, RW_COLS), F32),
                            jnp.zeros((bp, N_GROUPS, MXU_DIM, B_HEAD), state_wkv.dtype), prm,
                            _seq_tile(t, 256), 32, 2 if bp % 2 == 0 else 1)
    x1 = _mixout_call(oa.reshape(bp * t, A_WIDTH), ob.reshape(bp * t, B_WIDTH), xp, w_out_b,
                      g_post_mix, p_mod(gt_m), tm, tps)
    tm_f = _seq_tile(t, 1024)
    tps_f = t // tm_f
    yp, clg, clv = _ffn_call(x1, jnp.zeros((bp, FFN_CONV - 1, 2 * D_FF), F32), g_pre_ffn, p_mod(sc_f),
                             p_mod(sh_f), p_mod(gt_f), w_up_b, w_conv_ffn[0], w_down_b, g_post_ffn,
                             tm_f, tps_f)
    rw3 = rw.reshape(bp, t, RW_COLS)
    out_prompt = (
        yp.reshape(bp, t, D_MODEL),
        kf.reshape(1, bp, t, A_HEADS, 2 * A_DK),
        vf.reshape(1, bp, t, A_HEADS, A_DV),
        _groups_to_state(s_last)[None],
        rw3[:, -1][None],
        jnp.concatenate([clg[tps_f - 1::tps_f, -(FFN_CONV - 1):], clv[tps_f - 1::tps_f, -(FFN_CONV - 1):]],
                        axis=-1)[None],
    )

    ms = bs * ts
    xs = x_sample.reshape(ms, D_MODEL)
    kf_s, vf_s, qkvb_s, rw_s = _in_proj_call(xs, g_pre_mix, s_mod(sc_m), s_mod(sh_m), w_in_b, ms, ms)
    oa_s = _sattn_call(lams, qkvb_s.reshape(bs, ts, 3 * A_WIDTH),
                       cache_k[0].reshape(bs, past * A_HEADS, A_DV),
                       cache_v[0].reshape(bs, past * A_HEADS, A_DV), g_subln, _seq_tile(past, 2048))
    ob_s, s_last_s = _rwkv_call(rw_s.reshape(bs, ts, RW_COLS), state_shift[0][:, None, :],
                                _state_to_groups(state_wkv[0]), prm, ts, ts, 4 if bs % 4 == 0 else 1)
    x1_s = _mixout_call(oa_s.reshape(ms, A_WIDTH), ob_s.reshape(ms, B_WIDTH), xs, w_out_b,
                        g_post_mix, s_mod(gt_m), ms, 1)
    cprev = state_ffn_conv[0].reshape(bs * (FFN_CONV - 1), 2 * D_FF)
    ys, ug, uv = _ffn_s_call(x1_s, cprev, g_pre_ffn, s_mod(sc_f), s_mod(sh_f), s_mod(gt_f),
                             w_up_b, w_conv_ffn[0], w_down_b, g_post_ffn, ts, _seq_tile(ms, 512))
    u_all = jnp.concatenate([ug, uv], axis=-1).reshape(bs, ts, 2 * D_FF)
    out_sample = (
        ys.reshape(bs, ts, D_MODEL),
        kf_s.reshape(1, bs, ts, A_HEADS, 2 * A_DK),
        vf_s.reshape(1, bs, ts, A_HEADS, A_DV),
        _groups_to_state(s_last_s)[None],
        rw_s.reshape(bs, ts, RW_COLS)[:, -1][None],
        u_all[:, -(FFN_CONV - 1):][None],
    )
    return (out_prompt[0], out_sample[0]) + out_prompt[1:] + out_sample[1:]


def kernel(x_prompt, x_sample, c_prompt, c_sample, cache_k, cache_v, state_wkv, state_shift, state_ffn_conv, w_ada, b_ada, g_pre_mix, g_post_mix, g_pre_ffn, g_post_ffn, w_in, lam_q1, lam_k1, lam_q2, lam_k2, g_subln, mu_shift, w0, w_w2, a0, w_a2, w_g2, k_k, k_a, r_k, ln_x_w, ln_x_b, w_out, w_up, w_conv_ffn, w_down):
    return _forward(x_prompt, x_sample, c_prompt, c_sample, cache_k, cache_v, state_wkv, state_shift,
                    state_ffn_conv, w_ada, b_ada, g_pre_mix, g_post_mix, g_pre_ffn, g_post_ffn, w_in,
                    lam_q1, lam_k1, lam_q2, lam_k2, g_subln, mu_shift, w0, w_w2, a0, w_a2, w_g2, k_k,
                    k_a, r_k, ln_x_w, ln_x_b, w_out, w_up, w_conv_ffn, w_down)
```

```python
import functools
import math
from typing import NamedTuple

import jax
import jax.numpy as jnp
from jax import lax
from jax.experimental import pallas as pl
from jax.experimental.pallas import tpu as pltpu

F32 = jnp.float32
BF16 = jnp.bfloat16

D_MODEL = 2048
A_HEADS = 8
A_DK = 64
A_DV = 128
A_WIDTH = A_HEADS * A_DV
B_HEAD = 64
B_WIDTH = D_MODEL - A_WIDTH
B_HEADS = B_WIDTH // B_HEAD
W_RANK = 64
A_RANK = 64
G_RANK = 128
RW_COLS = 3 * B_WIDTH + W_RANK + A_RANK + G_RANK
D_FF = 5632
FFN_CONV = 3
ATTN_CHUNK = 64
EPS = 1e-6
GN_EPS = 64e-5
LAM_INIT = 0.8 - 0.6 * math.exp(-0.3 * 0)
Q_SCALE = (A_DK ** -0.5) * math.log2(math.e)

LANES = 128
SUBLANES = 8
BF16_ROWS = 16
MXU_DIM = 256
GROUP_HEADS = MXU_DIM // B_HEAD
N_GROUPS = B_WIDTH // MXU_DIM
VMEM_LIMIT_BYTES = 56 * 1024 * 1024
NEG_BIG = -1e30


def _cparams(*sem):
    return pltpu.CompilerParams(dimension_semantics=sem, vmem_limit_bytes=VMEM_LIMIT_BYTES)


def _dot(a, b):
    return jnp.dot(a, b, preferred_element_type=F32)


def _dot_nt(a, b):
    return lax.dot_general(a, b, (((1,), (1,)), ((), ())), preferred_element_type=F32)


def _dot_tn(a, b):
    return lax.dot_general(a, b, (((0,), (0,)), ((), ())), preferred_element_type=F32)


def _idiv(x, n):
    assert n > 0 and n & (n - 1) == 0
    return x >> (n.bit_length() - 1)


def _sigmoid(x):
    return 1.0 / (1.0 + jnp.exp(-x))


def _split_bf16(x, terms):
    out = []
    rem = x
    for _ in range(terms):
        piece = rem.astype(BF16)
        out.append(piece)
        rem = rem - piece.astype(F32)
    return out


def _norm_mod(x, g, sc, sh):
    ms = jnp.mean(x * x, axis=-1, keepdims=True)
    return (x * lax.rsqrt(ms + EPS) * g) * (1.0 + sc) + sh


def _rms(x, g):
    ms = jnp.mean(x * x, axis=-1, keepdims=True)
    return x * lax.rsqrt(ms + EPS) * g


def _mod(ref):
    return ref[0] if len(ref.shape) == 3 else ref[...]


def _lam(q1_ref, k1_ref, q2_ref, k2_ref):
    s1 = jnp.sum(q1_ref[...] * k1_ref[...], axis=-1, keepdims=True)
    s2 = jnp.sum(q2_ref[...] * k2_ref[...], axis=-1, keepdims=True)
    return jnp.exp(s1) - jnp.exp(s2) + LAM_INIT


def _ada_kernel(c_ref, w_ref, b_ref, o_ref):
    c = c_ref[...]
    s = (c * _sigmoid(c)).astype(BF16)
    o_ref[...] = _dot(s, w_ref[...].astype(BF16)) + b_ref[...]


def _ada_call(c_all, w_ada, b_ada):
    rows = c_all.shape[0]
    n = w_ada.shape[1]
    tn = 1024
    return pl.pallas_call(
        _ada_kernel,
        grid=(n // tn,),
        in_specs=[
            pl.BlockSpec((rows, D_MODEL), lambda j: (0, 0)),
            pl.BlockSpec((D_MODEL, tn), lambda j: (0, j)),
            pl.BlockSpec((1, tn), lambda j: (0, j)),
        ],
        out_specs=pl.BlockSpec((rows, tn), lambda j: (0, j)),
        out_shape=jax.ShapeDtypeStruct((rows, n), F32),
        compiler_params=_cparams("arbitrary"),
        name="ada_mod",
    )(c_all, w_ada, b_ada)


def _qkv_kernel(x_ref, g_ref, sc_ref, sh_ref, w_ref, kf_ref, vf_ref, qkvb_ref):
    h = _norm_mod(x_ref[...], g_ref[...], _mod(sc_ref), _mod(sh_ref)).astype(BF16)
    q = _dot(h, w_ref[:, 0:A_WIDTH])
    qkvb_ref[:, 0:A_WIDTH] = (q * Q_SCALE).astype(BF16)
    k = _dot(h, w_ref[:, A_WIDTH:2 * A_WIDTH])
    kf_ref[...] = k
    qkvb_ref[:, A_WIDTH:2 * A_WIDTH] = k.astype(BF16)
    v = _dot(h, w_ref[:, 2 * A_WIDTH:3 * A_WIDTH])
    vf_ref[...] = v
    qkvb_ref[:, 2 * A_WIDTH:3 * A_WIDTH] = v.astype(BF16)


def _rw_kernel(x_ref, g_ref, sc_ref, sh_ref, w_ref, o_ref):
    h = _norm_mod(x_ref[...], g_ref[...], _mod(sc_ref), _mod(sh_ref)).astype(BF16)
    o_ref[...] = _dot(h, w_ref[:, 3 * A_WIDTH:])


def _mod_spec(mod, tm, tiles_per_seq, nidx):
    if mod.ndim == 3:
        if nidx == 2:
            return pl.BlockSpec((1, 1, D_MODEL), lambda i, j: (i // tiles_per_seq, 0, 0))
        return pl.BlockSpec((1, 1, D_MODEL), lambda i: (i // tiles_per_seq, 0, 0))
    if nidx == 2:
        return pl.BlockSpec((tm, D_MODEL), lambda i, j: (i, 0))
    return pl.BlockSpec((tm, D_MODEL), lambda i: (i, 0))


def _resident(shape):
    return pl.BlockSpec(shape, lambda i: (0,) * len(shape), pipeline_mode=pl.Buffered(1))


def _in_proj_call(x, g, sc, sh, w_in, tm, seq_len):
    m = x.shape[0]
    n_in = w_in.shape[1]
    common = [
        pl.BlockSpec((tm, D_MODEL), lambda i: (i, 0)),
        _resident((1, D_MODEL)),
        _mod_spec(sc, tm, seq_len // tm, 1),
        _mod_spec(sh, tm, seq_len // tm, 1),
    ]
    kf, vf, qkvb = pl.pallas_call(
        _qkv_kernel,
        grid=(m // tm,),
        in_specs=common + [_resident((D_MODEL, n_in))],
        out_specs=[
            pl.BlockSpec((tm, A_WIDTH), lambda i: (i, 0)),
            pl.BlockSpec((tm, A_WIDTH), lambda i: (i, 0)),
            pl.BlockSpec((tm, 3 * A_WIDTH), lambda i: (i, 0)),
        ],
        out_shape=[
            jax.ShapeDtypeStruct((m, A_WIDTH), F32),
            jax.ShapeDtypeStruct((m, A_WIDTH), F32),
            jax.ShapeDtypeStruct((m, 3 * A_WIDTH), BF16),
        ],
        compiler_params=_cparams("arbitrary"),
        name="in_proj_qkv",
    )(x, g, sc, sh, w_in)
    rw = pl.pallas_call(
        _rw_kernel,
        grid=(m // tm,),
        in_specs=common + [_resident((D_MODEL, n_in))],
        out_specs=pl.BlockSpec((tm, RW_COLS), lambda i: (i, 0)),
        out_shape=jax.ShapeDtypeStruct((m, RW_COLS), F32),
        compiler_params=_cparams("arbitrary"),
        name="in_proj_rw",
    )(x, g, sc, sh, w_in)
    return kf, vf, qkvb, rw


def _softmax_step(s, vb, m_ref, l_ref, acc_ref, idx):
    m_prev = m_ref[idx]
    m_new = jnp.maximum(m_prev, jnp.max(s, axis=-1, keepdims=True))
    alpha = jnp.exp2(m_prev - m_new)
    p = jnp.exp2(s - m_new)
    l_ref[idx] = alpha * l_ref[idx] + jnp.sum(p, axis=-1, keepdims=True)
    acc_ref[idx] = alpha * acc_ref[idx] + _dot(p.astype(BF16), vb)
    m_ref[idx] = m_new


def _subln(o, g):
    return _rms(o, g) * (1.0 - LAM_INIT)


PATTN_HEADS = 2
PATTN_QG = 512


def _pattn_kernel(q1_ref, k1_ref, q2_ref, k2_ref, q_ref, k_ref, v_ref, g_ref, o_ref,
                  m_scr, l_scr, acc_scr, s_scr, t_scr, *, tq):
    i = pl.program_id(2)
    lane = lax.broadcasted_iota(jnp.int32, (tq, A_DV), 1)
    m_scr[...] = jnp.full(m_scr.shape, NEG_BIG, F32)
    l_scr[...] = jnp.zeros(l_scr.shape, F32)
    acc_scr[...] = jnp.zeros(acc_scr.shape, F32)

    qg = min(PATTN_QG, 2 * tq)
    n_qg = 2 * tq // qg
    q_gs = []
    for hh in range(PATTN_HEADS):
        q = q_ref[0, :, hh * A_DV:(hh + 1) * A_DV]
        zero = jnp.zeros_like(q)
        q_st = jnp.concatenate([jnp.where(lane < A_DK, q, zero), jnp.where(lane >= A_DK, q, zero)], axis=0)
        q_gs.append([q_st[g * qg:(g + 1) * qg] for g in range(n_qg)])
    ones_k = jnp.ones((SUBLANES, tq), BF16)
    chains = [(hh, g) for g in range(n_qg) for hh in range(PATTN_HEADS)]

    def scores(j):
        r0 = pl.multiple_of(j * tq, tq)
        kbs = [k_ref[0, pl.ds(r0, tq), hh * A_DV:(hh + 1) * A_DV] for hh in range(PATTN_HEADS)]
        return [_dot_nt(kbs[hh], q_gs[hh][g]) for hh, g in chains]

    def update(j, s_ref, masks):
        r0 = pl.multiple_of(j * tq, tq)
        vbs = [v_ref[0, pl.ds(r0, tq), hh * A_DV:(hh + 1) * A_DV] for hh in range(PATTN_HEADS)]
        for ci, (hh, g) in enumerate(chains):
            cs = slice(g * qg, (g + 1) * qg)
            s_t = s_ref[ci]
            if masks is not None:
                s_t = jnp.where(masks[g], s_t, NEG_BIG)
            m_prev = m_scr[hh, :, cs]
            m_new = jnp.maximum(m_prev, jnp.max(s_t, axis=0, keepdims=True))
            alpha = jnp.exp2(m_prev - m_new)
            p_b = jnp.exp2(s_t - m_new).astype(BF16)
            l_scr[hh, :, cs] = alpha * l_scr[hh, :, cs] + _dot(ones_k, p_b)[0:1]
            acc_scr[hh, :, cs] = alpha * acc_scr[hh, :, cs] + _dot_tn(vbs[hh], p_b)
            m_scr[hh, :, cs] = m_new

    def put(s_ref, j):
        for ci, s_t in enumerate(scores(j)):
            s_ref[ci] = s_t

    rk = lax.broadcasted_iota(jnp.int32, (tq, qg), 0)
    cq = lax.broadcasted_iota(jnp.int32, (tq, qg), 1)
    diag = [_idiv(rk, ATTN_CHUNK) <= _idiv((cq + g * qg) & (tq - 1), ATTN_CHUNK) for g in range(n_qg)]

    put(s_scr, 0)

    def body(jp, carry):
        j = 2 * jp
        put(t_scr, j + 1)
        update(j, s_scr, None)
        put(s_scr, j + 2)
        update(j + 1, t_scr, None)
        return carry

    lax.fori_loop(0, i // 2, body, 0)

    @pl.when(i % 2 == 0)
    def _():
        update(i, s_scr, diag)

    @pl.when(i % 2 == 1)
    def _():
        put(t_scr, i)
        update(i - 1, s_scr, None)
        update(i, t_scr, diag)

    lam = _lam(q1_ref, k1_ref, q2_ref, k2_ref)
    for hh in range(PATTN_HEADS):
        o_t = (acc_scr[hh, :, 0:tq] / l_scr[hh, :, 0:tq]
               - lam * (acc_scr[hh, :, tq:2 * tq] / l_scr[hh, :, tq:2 * tq]))
        o_ref[0, :, hh * A_DV:(hh + 1) * A_DV] = _subln(o_t.T, g_ref[...]).astype(BF16)


def _lam_specs(nidx):
    if nidx == 3:
        return [pl.BlockSpec((1, A_DK), lambda b, h, i: (0, 0))] * 4
    return [pl.BlockSpec((1, A_DK), lambda b, j: (0, 0))] * 4


def _pattn_call(lams, qkvb, g_subln, tq):
    bsz, t, _ = qkvb.shape
    hw = PATTN_HEADS * A_DV
    ng = A_HEADS // PATTN_HEADS
    return pl.pallas_call(
        functools.partial(_pattn_kernel, tq=tq),
        grid=(bsz, ng, t // tq),
        in_specs=_lam_specs(3) + [
            pl.BlockSpec((1, tq, hw), lambda b, h, i: (b, i, h)),
            pl.BlockSpec((1, t, hw), lambda b, h, i: (b, 0, ng + h)),
            pl.BlockSpec((1, t, hw), lambda b, h, i: (b, 0, 2 * ng + h)),
            pl.BlockSpec((1, A_DV), lambda b, h, i: (0, 0)),
        ],
        out_specs=pl.BlockSpec((1, tq, hw), lambda b, h, i: (b, i, h)),
        out_shape=jax.ShapeDtypeStruct((bsz, t, A_WIDTH), BF16),
        scratch_shapes=[
            pltpu.VMEM((PATTN_HEADS, 1, 2 * tq), F32),
            pltpu.VMEM((PATTN_HEADS, 1, 2 * tq), F32),
            pltpu.VMEM((PATTN_HEADS, A_DV, 2 * tq), F32),
        ] + [pltpu.VMEM((PATTN_HEADS * (2 * tq // min(PATTN_QG, 2 * tq)), tq, min(PATTN_QG, 2 * tq)), F32)] * 2,
        compiler_params=_cparams("arbitrary", "arbitrary", "arbitrary"),
        name="prompt_attention",
    )(*lams, qkvb, qkvb, qkvb, g_subln)


N_VQ = 2 * A_HEADS


def _sattn_kernel(q1_ref, k1_ref, q2_ref, k2_ref, q_ref, kn_ref, vn_ref, ck_ref, cv_ref, g_ref, o_ref,
                  qb_scr, m_scr, l_scr, acc_scr, *, tnew):
    j = pl.program_id(1)
    nvq = N_VQ * tnew

    @pl.when(j == 0)
    def _():
        q = q_ref[0]
        qt = jnp.concatenate([q] * N_VQ, axis=0)
        row = lax.broadcasted_iota(jnp.int32, (nvq, A_WIDTH), 0)
        col = lax.broadcasted_iota(jnp.int32, (nvq, A_WIDTH), 1)
        qb_scr[...] = jnp.where(_idiv(row, tnew) == _idiv(col, A_DK), qt, jnp.zeros_like(qt))
        m_scr[...] = jnp.full(m_scr.shape, NEG_BIG, F32)
        l_scr[...] = jnp.zeros(l_scr.shape, F32)
        acc_scr[...] = jnp.zeros(acc_scr.shape, F32)

    def step(kb, vb):
        _softmax_step(_dot_nt(qb_scr[...], kb), vb, m_scr, l_scr, acc_scr, 0)

    def cached_rows(ref, p0, n):
        heads = [ref[0, pl.ds(p0 * A_HEADS + h, n, stride=A_HEADS), :] for h in range(A_HEADS)]
        return jnp.concatenate(heads, axis=-1).astype(BF16)

    tk = ck_ref.shape[1] // A_HEADS
    n_sub = max(1, tk // MXU_DIM)
    sub = tk // n_sub
    kbs = [cached_rows(ck_ref, c * sub, sub) for c in range(n_sub)]
    s_all = [_dot_nt(qb_scr[...], kb) for kb in kbs]
    for c in range(n_sub):
        _softmax_step(s_all[c], cached_rows(cv_ref, c * sub, sub), m_scr, l_scr, acc_scr, 0)

    @pl.when(j == pl.num_programs(1) - 1)
    def _():
        step(kn_ref[0], vn_ref[0])
        lam = _lam(q1_ref, k1_ref, q2_ref, k2_ref)
        g = g_ref[...]
        for h in range(A_HEADS):
            cs = slice(h * A_DV, (h + 1) * A_DV)
            r0 = slice((2 * h) * tnew, (2 * h + 1) * tnew)
            r1 = slice((2 * h + 1) * tnew, (2 * h + 2) * tnew)
            o0 = acc_scr[0, r0, cs] / l_scr[0, r0, :]
            o1 = acc_scr[0, r1, cs] / l_scr[0, r1, :]
            o_ref[0, :, cs] = _subln(o0 - lam * o1, g).astype(BF16)


def _sattn_call(lams, qkvb, cache_k, cache_v, g_subln, tk):
    bsz, tnew, _ = qkvb.shape
    past = cache_k.shape[1] // A_HEADS
    nvq = N_VQ * tnew
    return pl.pallas_call(
        functools.partial(_sattn_kernel, tnew=tnew),
        grid=(bsz, past // tk),
        in_specs=_lam_specs(2) + [
            pl.BlockSpec((1, tnew, A_WIDTH), lambda b, j: (b, 0, 0)),
            pl.BlockSpec((1, tnew, A_WIDTH), lambda b, j: (b, 0, 1)),
            pl.BlockSpec((1, tnew, A_WIDTH), lambda b, j: (b, 0, 2)),
            pl.BlockSpec((1, tk * A_HEADS, A_DV), lambda b, j: (b, j, 0)),
            pl.BlockSpec((1, tk * A_HEADS, A_DV), lambda b, j: (b, j, 0)),
            pl.BlockSpec((1, A_DV), lambda b, j: (0, 0)),
        ],
        out_specs=pl.BlockSpec((1, tnew, A_WIDTH), lambda b, j: (b, 0, 0)),
        out_shape=jax.ShapeDtypeStruct((bsz, tnew, A_WIDTH), BF16),
        scratch_shapes=[
            pltpu.VMEM((nvq, A_WIDTH), BF16),
            pltpu.VMEM((1, nvq, 1), F32),
            pltpu.VMEM((1, nvq, 1), F32),
            pltpu.VMEM((1, nvq, A_WIDTH), F32),
        ],
        compiler_params=_cparams("arbitrary", "arbitrary"),
        name="sample_attention",
    )(*lams, qkvb, qkvb, qkvb, cache_k, cache_v, g_subln)


def _head_sum(x, ones_bd):
    return _dot(x.astype(BF16), ones_bd)


def _rwkv_kernel(p_ref, ph_ref, st_ref, s0_ref, mu_ref, w0_ref, a0_ref, kk_ref, ka_ref, rk_ref,
                 lnw_ref, lnb_ref, ww2_ref, wa2_ref, wg2_ref, ob_ref, so_ref,
                 r_scr, k_scr, v_scr, na_scr, nb_scr, ld_scr, cum_scr, g_scr, bon_scr, y_scr, s_scr,
                 *, tb, chunk):
    i = pl.program_id(1)
    n_seq = p_ref.shape[0]
    c_len = chunk
    n_chunks = tb // c_len

    row_g = lax.broadcasted_iota(jnp.int32, (MXU_DIM, MXU_DIM), 0)
    col_g = lax.broadcasted_iota(jnp.int32, (MXU_DIM, MXU_DIM), 1)
    bd_mask = _idiv(row_g, B_HEAD) == _idiv(col_g, B_HEAD)
    ones_bd = jnp.where(bd_mask, 1.0, 0.0).astype(BF16)

    @pl.when(i == 0)
    def _():
        rep = jnp.where(
            lax.broadcasted_iota(jnp.int32, (B_HEAD, MXU_DIM), 0)
            == (lax.broadcasted_iota(jnp.int32, (B_HEAD, MXU_DIM), 1) & (B_HEAD - 1)), 1.0, 0.0).astype(BF16)
        for b in range(n_seq):
            for q in range(N_GROUPS):
                wide = None
                for piece in _split_bf16(s0_ref[b, q], 3):
                    d = _dot(piece, rep)
                    wide = d if wide is None else wide + d
                s_scr[b, q] = jnp.where(bd_mask, wide, 0.0)

    row_t = lax.broadcasted_iota(jnp.int32, (tb, tb), 0)
    col_t = lax.broadcasted_iota(jnp.int32, (tb, tb), 1)
    tri = jnp.where((_idiv(row_t, c_len) == _idiv(col_t, c_len)) & (col_t <= row_t), 1.0, 0.0).astype(BF16)
    w_cat = jnp.concatenate([ww2_ref[...], wa2_ref[...]], axis=0)

    for b in range(n_seq):
        p = p_ref[b]
        prev_row = jnp.where(i == 0, st_ref[b], ph_ref[b, SUBLANES - 1:SUBLANES, :])
        rowi = lax.broadcasted_iota(jnp.int32, p.shape, 0)
        p_prev = jnp.where(rowi == 0, prev_row, pltpu.roll(p, 1, 0))
        xs = p + (p_prev - p) * mu_ref[...]
        r = xs[:, 0:B_WIDTH]
        k = xs[:, B_WIDTH:2 * B_WIDTH]
        v = xs[:, 2 * B_WIDTH:3 * B_WIDTH]
        wa = xs[:, 3 * B_WIDTH:3 * B_WIDTH + W_RANK + A_RANK]
        gd = xs[:, 3 * B_WIDTH + W_RANK + A_RANK:]

        lane_wa = lax.broadcasted_iota(jnp.int32, wa.shape, 1)
        zero_wa = jnp.zeros_like(wa)
        w_lin = _dot(jnp.where(lane_wa < W_RANK, jnp.tanh(wa), zero_wa).astype(BF16), w_cat)
        a_lin = _dot(jnp.where(lane_wa >= W_RANK, wa, zero_wa).astype(BF16), w_cat)
        logd = -math.exp(-0.5) * _sigmoid(w0_ref[...] + w_lin)
        a = _sigmoid(a0_ref[...] + a_lin)
        g_scr[b] = _dot(_sigmoid(gd).astype(BF16), wg2_ref[...])

        k2 = k * (1.0 + (a - 1.0) * ka_ref[...])
        kk = k * kk_ref[...]
        rkk = r * k2 * rk_ref[...]
        for q in range(N_GROUPS):
            ls = slice(q * MXU_DIM, (q + 1) * MXU_DIM)
            kq = kk[:, ls]
            kn = kq * lax.rsqrt(jnp.maximum(_head_sum(kq * kq, ones_bd), 1e-24))
            na_scr[b, :, ls] = -kn
            nb_scr[b, :, ls] = kn * a[:, ls]
            bon_scr[b, :, ls] = _head_sum(rkk[:, ls], ones_bd) * v[:, ls]
        r_scr[b] = r
        k_scr[b] = k2
        v_scr[b] = v
        ld_scr[b] = logd

        cum = None
        for piece in _split_bf16(logd, 2):
            d = _dot(tri, piece)
            cum = d if cum is None else cum + d
        cum_scr[b] = cum

    gc = GROUP_HEADS * c_len
    lane_q = lax.broadcasted_iota(jnp.int32, (1, MXU_DIM), 1)
    head_masks = [_idiv(lane_q, B_HEAD) == j for j in range(GROUP_HEADS)]
    row_s = lax.broadcasted_iota(jnp.int32, (gc, gc), 0)
    col_s = lax.broadcasted_iota(jnp.int32, (gc, gc), 1)
    same_head = _idiv(row_s, c_len) == _idiv(col_s, c_len)
    strict = same_head & (col_s < row_s)
    incl = same_head & (col_s <= row_s)
    eye = jnp.where(col_s == row_s, 1.0, 0.0).astype(F32)
    n_double = int(math.log2(c_len)) - 1
    chains = [(b, q) for q in range(N_GROUPS) for b in range(n_seq)]
    groups = range(len(chains))

    def stack(x):
        z = jnp.zeros_like(x)
        return jnp.concatenate([jnp.where(head_masks[j], x, z) for j in range(GROUP_HEADS)], axis=0)

    def unstack(x):
        out = x[0:c_len]
        for j in range(1, GROUP_HEADS):
            out = out + x[j * c_len:(j + 1) * c_len]
        return out

    def chunk_body(c, carry):
        r0 = pl.multiple_of(c * c_len, c_len)
        rows = pl.ds(r0, c_len)
        a_s, r_s, b_s, k_s, be_s, ke_s, v_s, dec, ar_c = [], [], [], [], [], [], [], [], []
        for b, q in chains:
            ls = slice(q * MXU_DIM, (q + 1) * MXU_DIM)
            cum_q = cum_scr[b, rows, ls]
            k_q = k_scr[b, rows, ls]
            nb_q = nb_scr[b, rows, ls]
            cum_last = cum_q[c_len - 1:c_len, :]
            e_out = jnp.exp(-cum_q)
            e_end = jnp.exp(cum_last - cum_q)
            a_c = (na_scr[b, rows, ls] * jnp.exp(cum_q - ld_scr[b, rows, ls])).astype(BF16)
            r_c = (r_scr[b, rows, ls] * jnp.exp(cum_q)).astype(BF16)
            ar_c.append(jnp.concatenate([a_c, r_c], axis=0))
            a_s.append(stack(a_c))
            r_s.append(stack(r_c))
            b_s.append(stack((nb_q * e_out).astype(BF16)))
            k_s.append(stack((k_q * e_out).astype(BF16)))
            be_s.append(stack((nb_q * e_end).astype(BF16)))
            ke_s.append(stack((k_q * e_end).astype(BF16)))
            v_s.append(stack(v_scr[b, rows, ls].astype(BF16)))
            dec.append(jnp.exp(cum_last))

        if gc % LANES == 0:
            g_all = [_dot_nt(jnp.concatenate([a_s[q], r_s[q]], axis=0),
                             jnp.concatenate([b_s[q], k_s[q]], axis=0)) for q in groups]
            g_ab = [g[0:gc, 0:gc] for g in g_all]
            g_ak = [g[0:gc, gc:2 * gc] for g in g_all]
            g_rb = [g[gc:2 * gc, 0:gc] for g in g_all]
            g_rk = [g[gc:2 * gc, gc:2 * gc] for g in g_all]
        else:
            g_ab = [_dot_nt(a_s[q], b_s[q]) for q in groups]
            g_ak = [_dot_nt(a_s[q], k_s[q]) for q in groups]
            g_rb = [_dot_nt(r_s[q], b_s[q]) for q in groups]
            g_rk = [_dot_nt(r_s[q], k_s[q]) for q in groups]
        aab = [jnp.where(strict, g, 0.0) for g in g_ab]
        aak = [jnp.where(strict, g, 0.0).astype(BF16) for g in g_ak]
        arb = [jnp.where(incl, g, 0.0).astype(BF16) for g in g_rb]
        ark = [jnp.where(incl, g, 0.0).astype(BF16) for g in g_rk]

        t_m = [eye + x for x in aab]
        pw = [x.astype(BF16) for x in aab]
        pw = [_dot(x, x) for x in pw]
        for m in range(n_double):
            pb = [x.astype(BF16) for x in pw]
            if m + 1 < n_double:
                pw = [_dot(x, x) for x in pb]
            t_m = [t_m[q] + _dot(t_m[q].astype(BF16), pb[q]) for q in groups]
        t_b = [x.astype(BF16) for x in t_m]

        av = [_dot(aak[q], v_s[q]) for q in groups]

        s_prev = [s_scr[b, q] for b, q in chains]
        ar = [_dot_nt(ar_c[q], s_prev[q].astype(BF16)) for q in groups]
        u_b = [_dot(t_b[q], (stack(ar[q][0:c_len]) + av[q]).astype(BF16)).astype(BF16) for q in groups]
        uv = [jnp.concatenate([u_b[q], v_s[q]], axis=0) for q in groups]
        if gc % LANES == 0:
            y_st = [_dot(jnp.concatenate([arb[q], ark[q]], axis=1), uv[q]) for q in groups]
        else:
            y_st = [_dot(arb[q], u_b[q]) + _dot(ark[q], v_s[q]) for q in groups]
        s_new = [s_prev[q] * dec[q]
                 + _dot_tn(uv[q], jnp.concatenate([be_s[q], ke_s[q]], axis=0)) for q in groups]
        for ci, (b, q) in enumerate(chains):
            s_scr[b, q] = s_new[ci]
            y_scr[b, rows, q * MXU_DIM:(q + 1) * MXU_DIM] = ar[ci][c_len:2 * c_len] + unstack(y_st[ci])
        return carry

    lax.fori_loop(0, n_chunks, chunk_body, 0)

    @pl.when(i == pl.num_programs(1) - 1)
    def _():
        rep_t = jnp.where(
            (lax.broadcasted_iota(jnp.int32, (MXU_DIM, B_HEAD), 0) & (B_HEAD - 1))
            == lax.broadcasted_iota(jnp.int32, (MXU_DIM, B_HEAD), 1), 1.0, 0.0).astype(BF16)
        for b, q in chains:
            narrow = None
            for piece in _split_bf16(s_scr[b, q], 3):
                d = _dot(piece, rep_t)
                narrow = d if narrow is None else narrow + d
            so_ref[b, q] = narrow

    inv_n = 1.0 / B_HEAD
    for b, q in chains:
        ls = slice(q * MXU_DIM, (q + 1) * MXU_DIM)
        y = y_scr[b, :, ls]
        mu = _head_sum(y, ones_bd) * inv_n
        d = y - mu
        var = _head_sum(d * d, ones_bd) * inv_n
        yn = d * lax.rsqrt(var + GN_EPS) * lnw_ref[:, ls] + lnb_ref[:, ls]
        ob_ref[b, :, ls] = ((yn + bon_scr[b, :, ls]) * g_scr[b, :, ls]).astype(BF16)


def _rwkv_call(rw, shift0, s0, prm, tb, chunk, n_seq):
    bsz, t, _ = rw.shape
    assert bsz % n_seq == 0
    nt = t // tb
    hb = tb // SUBLANES
    vec = lambda n: pl.BlockSpec((1, n), lambda b, i: (0, 0))
    return pl.pallas_call(
        functools.partial(_rwkv_kernel, tb=tb, chunk=chunk),
        grid=(bsz // n_seq, nt),
        in_specs=[
            pl.BlockSpec((n_seq, tb, RW_COLS), lambda b, i: (b, i, 0)),
            pl.BlockSpec((n_seq, SUBLANES, RW_COLS), lambda b, i: (b, jnp.maximum(i * hb - 1, 0), 0)),
            pl.BlockSpec((n_seq, 1, RW_COLS), lambda b, i: (b, 0, 0)),
            pl.BlockSpec((n_seq, N_GROUPS, MXU_DIM, B_HEAD), lambda b, i: (b, 0, 0, 0)),
            vec(RW_COLS), vec(B_WIDTH), vec(B_WIDTH), vec(B_WIDTH), vec(B_WIDTH), vec(B_WIDTH),
            vec(B_WIDTH), vec(B_WIDTH),
            pl.BlockSpec((W_RANK, B_WIDTH), lambda b, i: (0, 0)),
            pl.BlockSpec((A_RANK, B_WIDTH), lambda b, i: (0, 0)),
            pl.BlockSpec((G_RANK, B_WIDTH), lambda b, i: (0, 0)),
        ],
        out_specs=[
            pl.BlockSpec((n_seq, tb, B_WIDTH), lambda b, i: (b, i, 0)),
            pl.BlockSpec((n_seq, N_GROUPS, MXU_DIM, B_HEAD), lambda b, i: (b, 0, 0, 0)),
        ],
        out_shape=[
            jax.ShapeDtypeStruct((bsz, t, B_WIDTH), BF16),
            jax.ShapeDtypeStruct((bsz, N_GROUPS, MXU_DIM, B_HEAD), F32),
        ],
        scratch_shapes=[pltpu.VMEM((n_seq, tb, B_WIDTH), F32)] * 10
        + [pltpu.VMEM((n_seq, N_GROUPS, MXU_DIM, MXU_DIM), F32)],
        compiler_params=_cparams("arbitrary", "arbitrary"),
        name="rwkv7_mix",
    )(rw, rw, shift0, s0, prm["mu_shift"], prm["w0"], prm["a0"], prm["k_k"], prm["k_a"], prm["r_k"],
      prm["ln_x_w"], prm["ln_x_b"], prm["w_w2"], prm["w_a2"], prm["w_g2"])


def _state_to_groups(s):
    return s.reshape(s.shape[0], N_GROUPS, MXU_DIM, B_HEAD)


def _groups_to_state(sg):
    return sg.reshape(sg.shape[0], B_HEADS, B_HEAD, B_HEAD)


def _mixout_kernel(oa_ref, ob_ref, x_ref, w_ref, g_ref, gt_ref, o_ref):
    mix = _dot(oa_ref[...], w_ref[0:A_WIDTH, :]) + _dot(ob_ref[...], w_ref[A_WIDTH:D_MODEL, :])
    o_ref[...] = x_ref[...] + _mod(gt_ref) * _rms(mix, g_ref[...])


def _mixout_call(oa, ob, x, w_out, g, gt, tm, tiles_per_seq):
    m = x.shape[0]
    return pl.pallas_call(
        _mixout_kernel,
        grid=(m // tm,),
        in_specs=[
            pl.BlockSpec((tm, A_WIDTH), lambda i: (i, 0)),
            pl.BlockSpec((tm, B_WIDTH), lambda i: (i, 0)),
            pl.BlockSpec((tm, D_MODEL), lambda i: (i, 0)),
            pl.BlockSpec((D_MODEL, D_MODEL), lambda i: (0, 0)),
            pl.BlockSpec((1, D_MODEL), lambda i: (0, 0)),
            _mod_spec(gt, tm, tiles_per_seq, 1),
        ],
        out_specs=pl.BlockSpec((tm, D_MODEL), lambda i: (i, 0)),
        out_shape=jax.ShapeDtypeStruct((m, D_MODEL), F32),
        compiler_params=_cparams("arbitrary"),
        name="mix_out",
    )(oa, ob, x, w_out, g, gt)


FFN_HALO = BF16_ROWS
FFN_TF = 512


def _gated(zg, zv):
    return (zg * _sigmoid(zg) * zv).astype(BF16)


def _ffn_kernel(x_ref, xh_ref, cpg_ref, cpv_ref, g_ref, sc_ref, sh_ref, gt_ref, wg_ref, wv_ref,
                cg_ref, cv_ref, wd_ref, gp_ref, o_ref, clg_ref, clv_ref, h_scr,
                *, tm, tiles_per_seq):
    mi = pl.program_id(0)
    f = pl.program_id(1)
    first = (mi % tiles_per_seq) == 0

    @pl.when(f == 0)
    def _():
        g, sc, sh = g_ref[...], sc_ref[0], sh_ref[0]
        h_scr[FFN_HALO:, :] = _norm_mod(x_ref[...], g, sc, sh).astype(BF16)
        h_scr[:FFN_HALO, :] = _norm_mod(xh_ref[...], g, sc, sh).astype(BF16)
        o_ref[...] = jnp.zeros(o_ref.shape, F32)

    h = h_scr[...]
    row = lax.broadcasted_iota(jnp.int32, (FFN_HALO, FFN_TF), 0)

    def conv(u, cp_ref, c_ref, cl_ref):
        cp = cp_ref[0]
        head = u[:FFN_HALO]
        head_first = jnp.where(row == FFN_HALO - 2, cp[0:1],
                               jnp.where(row == FFN_HALO - 1, cp[1:2], head))
        ue = jnp.concatenate([jnp.where(first, head_first, head), u[FFN_HALO:]], axis=0)
        cl_ref[0] = ue[tm + FFN_HALO - SUBLANES:]
        c = c_ref[...]
        return (c[0:1] * pltpu.roll(ue, 2, 0)[FFN_HALO:] + c[1:2] * pltpu.roll(ue, 1, 0)[FFN_HALO:]
                + c[2:3] * ue[FFN_HALO:])

    zg = conv(_dot(h, wg_ref[...]), cpg_ref, cg_ref, clg_ref)
    zv = conv(_dot(h, wv_ref[...]), cpv_ref, cv_ref, clv_ref)
    o_ref[...] += _dot(_gated(zg, zv), wd_ref[...])

    @pl.when(f == pl.num_programs(1) - 1)
    def _():
        o_ref[...] = x_ref[...] + gt_ref[0] * _rms(o_ref[...], gp_ref[...])


def _ffn_call(x, conv_prev, g, sc, sh, gt, w_up, w_conv, w_down, gp, tm, tiles_per_seq):
    m = x.shape[0]
    nseq = conv_prev.shape[0]
    nf = D_FF // FFN_TF
    hb = tm // FFN_HALO
    seq = lambda i: i // tiles_per_seq
    modspec = pl.BlockSpec((1, 1, D_MODEL), lambda i, f: (seq(i), 0, 0))
    return pl.pallas_call(
        functools.partial(_ffn_kernel, tm=tm, tiles_per_seq=tiles_per_seq),
        grid=(m // tm, nf),
        in_specs=[
            pl.BlockSpec((tm, D_MODEL), lambda i, f: (i, 0), pipeline_mode=pl.Buffered(1)),
            pl.BlockSpec((FFN_HALO, D_MODEL), lambda i, f: (jnp.maximum(i * hb - 1, 0), 0)),
            pl.BlockSpec((1, FFN_CONV - 1, FFN_TF), lambda i, f: (seq(i), 0, f)),
            pl.BlockSpec((1, FFN_CONV - 1, FFN_TF), lambda i, f: (seq(i), 0, f + nf)),
            pl.BlockSpec((1, D_MODEL), lambda i, f: (0, 0)),
            modspec, modspec, modspec,
            pl.BlockSpec((D_MODEL, FFN_TF), lambda i, f: (0, f)),
            pl.BlockSpec((D_MODEL, FFN_TF), lambda i, f: (0, f + nf)),
            pl.BlockSpec((FFN_CONV, FFN_TF), lambda i, f: (0, f)),
            pl.BlockSpec((FFN_CONV, FFN_TF), lambda i, f: (0, f + nf)),
            pl.BlockSpec((FFN_TF, D_MODEL), lambda i, f: (f, 0)),
            pl.BlockSpec((1, D_MODEL), lambda i, f: (0, 0)),
        ],
        out_specs=[
            pl.BlockSpec((tm, D_MODEL), lambda i, f: (i, 0)),
            pl.BlockSpec((1, SUBLANES, FFN_TF), lambda i, f: (i, 0, f)),
            pl.BlockSpec((1, SUBLANES, FFN_TF), lambda i, f: (i, 0, f)),
        ],
        out_shape=[
            jax.ShapeDtypeStruct((m, D_MODEL), F32),
            jax.ShapeDtypeStruct((m // tm, SUBLANES, D_FF), F32),
            jax.ShapeDtypeStruct((m // tm, SUBLANES, D_FF), F32),
        ],
        scratch_shapes=[pltpu.VMEM((tm + FFN_HALO, D_MODEL), BF16)],
        compiler_params=_cparams("arbitrary", "arbitrary"),
        name="conv_ffn_prompt",
    )(x, x, conv_prev, conv_prev, g, sc, sh, gt, w_up, w_up, w_conv, w_conv, w_down, gp)


def _ffn_s_kernel(x_ref, cpg_ref, cpv_ref, g_ref, sc_ref, sh_ref, gt_ref, wg_ref, wv_ref,
                  cg_ref, cv_ref, wd_ref, gp_ref, o_ref, ug_ref, uv_ref, h_scr, acc_scr, *, seq_len):
    f = pl.program_id(1)

    @pl.when(f == 0)
    def _():
        h_scr[...] = _norm_mod(x_ref[...], g_ref[...], sc_ref[...], sh_ref[...]).astype(BF16)
        acc_scr[...] = jnp.zeros(acc_scr.shape, F32)

    h = h_scr[...]
    tm = h.shape[0]
    pos = lax.broadcasted_iota(jnp.int32, (tm, FFN_TF), 0) & (seq_len - 1)
    n_cp = cpg_ref.shape[0]
    trow = lax.broadcasted_iota(jnp.int32, (tm, n_cp), 0)
    ccol = lax.broadcasted_iota(jnp.int32, (tm, n_cp), 1)
    tpos = trow & (seq_len - 1)
    base = 2 * _idiv(trow, seq_len)
    e1 = jnp.where((tpos == 0) & (ccol == base + 1), 1.0, 0.0).astype(BF16)
    e2 = jnp.where(((tpos == 0) & (ccol == base)) | ((tpos == 1) & (ccol == base + 1)), 1.0, 0.0).astype(BF16)

    def conv(u, cp_ref, c_ref, u_ref):
        u_ref[...] = u
        p1 = p2 = None
        for piece in _split_bf16(cp_ref[...], 3):
            d1, d2 = _dot(e1, piece), _dot(e2, piece)
            p1 = d1 if p1 is None else p1 + d1
            p2 = d2 if p2 is None else p2 + d2
        u1 = jnp.where(pos >= 1, pltpu.roll(u, 1, 0), p1)
        u2 = jnp.where(pos >= 2, pltpu.roll(u, 2, 0), p2)
        c = c_ref[...]
        return c[0:1] * u2 + c[1:2] * u1 + c[2:3] * u

    zg = conv(_dot(h, wg_ref[...]), cpg_ref, cg_ref, ug_ref)
    zv = conv(_dot(h, wv_ref[...]), cpv_ref, cv_ref, uv_ref)
    acc_scr[...] += _dot(_gated(zg, zv), wd_ref[...])

    @pl.when(f == pl.num_programs(1) - 1)
    def _():
        o_ref[...] = x_ref[...] + gt_ref[...] * _rms(acc_scr[...], gp_ref[...])


def _ffn_s_call(x, conv_prev, g, sc, sh, gt, w_up, w_conv, w_down, gp, seq_len, tm):
    m = x.shape[0]
    assert tm % seq_len == 0 and seq_len & (seq_len - 1) == 0
    nf = D_FF // FFN_TF
    n_cp = (tm // seq_len) * (FFN_CONV - 1)
    full = pl.BlockSpec((tm, D_MODEL), lambda i, f: (i, 0))
    full_in = pl.BlockSpec((tm, D_MODEL), lambda i, f: (i, 0), pipeline_mode=pl.Buffered(1))
    lo = pl.BlockSpec((tm, FFN_TF), lambda i, f: (i, f))
    return pl.pallas_call(
        functools.partial(_ffn_s_kernel, seq_len=seq_len),
        grid=(m // tm, nf),
        in_specs=[
            full_in,
            pl.BlockSpec((n_cp, FFN_TF), lambda i, f: (i, f)),
            pl.BlockSpec((n_cp, FFN_TF), lambda i, f: (i, f + nf)),
            pl.BlockSpec((1, D_MODEL), lambda i, f: (0, 0)),
            full_in, full_in, full_in,
            pl.BlockSpec((D_MODEL, FFN_TF), lambda i, f: (0, f)),
            pl.BlockSpec((D_MODEL, FFN_TF), lambda i, f: (0, f + nf)),
            pl.BlockSpec((FFN_CONV, FFN_TF), lambda i, f: (0, f)),
            pl.BlockSpec((FFN_CONV, FFN_TF), lambda i, f: (0, f + nf)),
            pl.BlockSpec((FFN_TF, D_MODEL), lambda i, f: (f, 0)),
            pl.BlockSpec((1, D_MODEL), lambda i, f: (0, 0)),
        ],
        out_specs=[full, lo, lo],
        out_shape=[
            jax.ShapeDtypeStruct((m, D_MODEL), F32),
            jax.ShapeDtypeStruct((m, D_FF), F32),
            jax.ShapeDtypeStruct((m, D_FF), F32),
        ],
        scratch_shapes=[pltpu.VMEM((tm, D_MODEL), BF16), pltpu.VMEM((tm, D_MODEL), F32)],
        compiler_params=_cparams("arbitrary", "arbitrary"),
        name="conv_ffn_sample",
    )(x, conv_prev, conv_prev, g, sc, sh, gt, w_up, w_up, w_conv, w_conv, w_down, gp)


def _seq_tile(t, cap):
    tm = min(t, cap)
    assert t % tm == 0
    return tm


class _Tiles(NamedTuple):
    rows: int
    attn: int
    wkv_rows: int
    wkv_chunk: int
    wkv_seqs: int
    ffn_rows: int


def _prompt_tiles(bp, t):
    return _Tiles(rows=_seq_tile(t, 512), attn=_seq_tile(t, 512), wkv_rows=_seq_tile(t, 256),
                  wkv_chunk=min(t, 32), wkv_seqs=2 if bp % 2 == 0 else 1, ffn_rows=_seq_tile(t, 1024))


def _sample_tiles(bs, ts, past):
    rows = bs * ts
    return _Tiles(rows=rows, attn=_seq_tile(past, 2048), wkv_rows=ts, wkv_chunk=ts,
                  wkv_seqs=4 if bs % 4 == 0 else 1, ffn_rows=_seq_tile(rows, 512))


@jax.jit
def _forward(x_prompt, x_sample, c_prompt, c_sample, cache_k, cache_v, state_wkv, state_shift,
             state_ffn_conv, w_ada, b_ada, g_pre_mix, g_post_mix, g_pre_ffn, g_post_ffn, w_in,
             lam_q1, lam_k1, lam_q2, lam_k2, g_subln, mu_shift, w0, w_w2, a0, w_a2, w_g2, k_k, k_a,
             r_k, ln_x_w, ln_x_b, w_out, w_up, w_conv_ffn, w_down):
    bp, t, _ = x_prompt.shape
    bs, ts, _ = x_sample.shape
    past = cache_k.shape[2]
    assert w_ada.shape[0] == 1, "single-layer problem"

    n_c = bp + bs
    n_c_pad = -(-n_c // SUBLANES) * SUBLANES
    c_all = jnp.concatenate([c_prompt, c_sample, jnp.zeros((n_c_pad - n_c, D_MODEL), F32)], axis=0)
    mod = _ada_call(c_all, w_ada[0], b_ada)
    sh_m, sc_m, gt_m, sh_f, sc_f, gt_f = [mod[:, i * D_MODEL:(i + 1) * D_MODEL] for i in range(6)]
    p_mod = lambda a: a[:bp].reshape(bp, 1, D_MODEL)
    s_mod = lambda a: jnp.repeat(a[bp:n_c], ts, axis=0)

    w_in_b = w_in[0].astype(BF16)
    w_out_b = w_out[0].astype(BF16)
    w_up_b = w_up[0].astype(BF16)
    w_down_b = w_down[0].astype(BF16)
    lams = (lam_q1, lam_k1, lam_q2, lam_k2)
    prm = {
        "mu_shift": mu_shift, "w0": w0, "a0": a0, "k_k": k_k, "k_a": k_a,
        "r_k": r_k.reshape(1, B_WIDTH), "ln_x_w": ln_x_w, "ln_x_b": ln_x_b,
        "w_w2": w_w2[0].astype(BF16), "w_a2": w_a2[0].astype(BF16), "w_g2": w_g2[0].astype(BF16),
    }

    pt = _prompt_tiles(bp, t)
    tps_f = t // pt.ffn_rows
    xp = x_prompt.reshape(bp * t, D_MODEL)
    kf, vf, qkvb, rw = _in_proj_call(xp, g_pre_mix, p_mod(sc_m), p_mod(sh_m), w_in_b, pt.rows, t)
    oa = _pattn_call(lams, qkvb.reshape(bp, t, 3 * A_WIDTH), g_subln, pt.attn)
    ob, s_last = _rwkv_call(rw.reshape(bp, t, RW_COLS), jnp.zeros((bp, 1, RW_COLS), F32),
                            jnp.zeros((bp, N_GROUPS, MXU_DIM, B_HEAD), state_wkv.dtype), prm,
                            pt.wkv_rows, pt.wkv_chunk, pt.wkv_seqs)
    x1 = _mixout_call(oa.reshape(bp * t, A_WIDTH), ob.reshape(bp * t, B_WIDTH), xp, w_out_b,
                      g_post_mix, p_mod(gt_m), pt.rows, t // pt.rows)
    yp, clg, clv = _ffn_call(x1, jnp.zeros((bp, FFN_CONV - 1, 2 * D_FF), F32), g_pre_ffn, p_mod(sc_f),
                             p_mod(sh_f), p_mod(gt_f), w_up_b, w_conv_ffn[0], w_down_b, g_post_ffn,
                             pt.ffn_rows, tps_f)
    rw3 = rw.reshape(bp, t, RW_COLS)
    out_prompt = (
        yp.reshape(bp, t, D_MODEL),
        kf.reshape(1, bp, t, A_HEADS, 2 * A_DK),
        vf.reshape(1, bp, t, A_HEADS, A_DV),
        _groups_to_state(s_last)[None],
        rw3[:, -1][None],
        jnp.concatenate([clg[tps_f - 1::tps_f, -(FFN_CONV - 1):], clv[tps_f - 1::tps_f, -(FFN_CONV - 1):]],
                        axis=-1)[None],
    )

    ms = bs * ts
    st = _sample_tiles(bs, ts, past)
    xs = x_sample.reshape(ms, D_MODEL)
    kf_s, vf_s, qkvb_s, rw_s = _in_proj_call(xs, g_pre_mix, s_mod(sc_m), s_mod(sh_m), w_in_b, st.rows, ms)
    oa_s = _sattn_call(lams, qkvb_s.reshape(bs, ts, 3 * A_WIDTH),
                       cache_k[0].reshape(bs, past * A_HEADS, A_DV),
                       cache_v[0].reshape(bs, past * A_HEADS, A_DV), g_subln, st.attn)
    ob_s, s_last_s = _rwkv_call(rw_s.reshape(bs, ts, RW_COLS), state_shift[0][:, None, :],
                                _state_to_groups(state_wkv[0]), prm, st.wkv_rows, st.wkv_chunk, st.wkv_seqs)
    x1_s = _mixout_call(oa_s.reshape(ms, A_WIDTH), ob_s.reshape(ms, B_WIDTH), xs, w_out_b,
                        g_post_mix, s_mod(gt_m), st.rows, 1)
    cprev = state_ffn_conv[0].reshape(bs * (FFN_CONV - 1), 2 * D_FF)
    ys, ug, uv = _ffn_s_call(x1_s, cprev, g_pre_ffn, s_mod(sc_f), s_mod(sh_f), s_mod(gt_f),
                             w_up_b, w_conv_ffn[0], w_down_b, g_post_ffn, ts, st.ffn_rows)
    u_all = jnp.concatenate([ug, uv], axis=-1).reshape(bs, ts, 2 * D_FF)
    out_sample = (
        ys.reshape(bs, ts, D_MODEL),
        kf_s.reshape(1, bs, ts, A_HEADS, 2 * A_DK),
        vf_s.reshape(1, bs, ts, A_HEADS, A_DV),
        _groups_to_state(s_last_s)[None],
        rw_s.reshape(bs, ts, RW_COLS)[:, -1][None],
        u_all[:, -(FFN_CONV - 1):][None],
    )
    return (out_prompt[0], out_sample[0]) + out_prompt[1:] + out_sample[1:]


def kernel(x_prompt, x_sample, c_prompt, c_sample, cache_k, cache_v, state_wkv, state_shift, state_ffn_conv, w_ada, b_ada, g_pre_mix, g_post_mix, g_pre_ffn, g_post_ffn, w_in, lam_q1, lam_k1, lam_q2, lam_k2, g_subln, mu_shift, w0, w_w2, a0, w_a2, w_g2, k_k, k_a, r_k, ln_x_w, ln_x_b, w_out, w_up, w_conv_ffn, w_down):
    return _forward(x_prompt, x_sample, c_prompt, c_sample, cache_k, cache_v, state_wkv, state_shift,
                    state_ffn_conv, w_ada, b_ada, g_pre_mix, g_post_mix, g_pre_ffn, g_post_ffn, w_in,
                    lam_q1, lam_k1, lam_q2, lam_k2, g_subln, mu_shift, w0, w_w2, a0, w_a2, w_g2, k_k,
                    k_a, r_k, ln_x_w, ln_x_b, w_out, w_up, w_conv_ffn, w_down)
```

```python
import functools
import math
from typing import NamedTuple

import jax
import jax.numpy as jnp
from jax import lax
from jax.experimental import pallas as pl
from jax.experimental.pallas import tpu as pltpu

F32 = jnp.float32
BF16 = jnp.bfloat16

D_MODEL = 2048
A_HEADS = 8
A_DK = 64
A_DV = 128
A_WIDTH = A_HEADS * A_DV
B_HEAD = 64
B_WIDTH = D_MODEL - A_WIDTH
B_HEADS = B_WIDTH // B_HEAD
W_RANK = 64
A_RANK = 64
G_RANK = 128
RW_COLS = 3 * B_WIDTH + W_RANK + A_RANK + G_RANK
D_FF = 5632
FFN_CONV = 3
ATTN_CHUNK = 64
EPS = 1e-6
GN_EPS = 64e-5
LAM_INIT = 0.8 - 0.6 * math.exp(-0.3 * 0)
Q_SCALE = (A_DK ** -0.5) * math.log2(math.e)

LANES = 128
SUBLANES = 8
BF16_ROWS = 16
MXU_DIM = 256
GROUP_HEADS = MXU_DIM // B_HEAD
N_GROUPS = B_WIDTH // MXU_DIM
VMEM_LIMIT_BYTES = 56 * 1024 * 1024
NEG_BIG = -1e30


def _cparams(*sem):
    return pltpu.CompilerParams(dimension_semantics=sem, vmem_limit_bytes=VMEM_LIMIT_BYTES)


def _dot(a, b):
    return jnp.dot(a, b, preferred_element_type=F32)


def _dot_nt(a, b):
    return lax.dot_general(a, b, (((1,), (1,)), ((), ())), preferred_element_type=F32)


def _dot_tn(a, b):
    return lax.dot_general(a, b, (((0,), (0,)), ((), ())), preferred_element_type=F32)


def _idiv(x, n):
    assert n > 0 and n & (n - 1) == 0
    return x >> (n.bit_length() - 1)


def _sigmoid(x):
    return 1.0 / (1.0 + jnp.exp(-x))


def _split_bf16(x, terms):
    out = []
    rem = x
    for _ in range(terms):
        piece = rem.astype(BF16)
        out.append(piece)
        rem = rem - piece.astype(F32)
    return out


def _norm_mod(x, g, sc, sh):
    ms = jnp.mean(x * x, axis=-1, keepdims=True)
    return (x * lax.rsqrt(ms + EPS) * g) * (1.0 + sc) + sh


def _rms(x, g):
    ms = jnp.mean(x * x, axis=-1, keepdims=True)
    return x * lax.rsqrt(ms + EPS) * g


def _mod(ref):
    return ref[0] if len(ref.shape) == 3 else ref[...]


def _lam(q1_ref, k1_ref, q2_ref, k2_ref):
    s1 = jnp.sum(q1_ref[...] * k1_ref[...], axis=-1, keepdims=True)
    s2 = jnp.sum(q2_ref[...] * k2_ref[...], axis=-1, keepdims=True)
    return jnp.exp(s1) - jnp.exp(s2) + LAM_INIT


def _ada_kernel(c_ref, w_ref, b_ref, o_ref):
    c = c_ref[...]
    s = (c * _sigmoid(c)).astype(BF16)
    o_ref[...] = _dot(s, w_ref[...].astype(BF16)) + b_ref[...]


def _ada_call(c_all, w_ada, b_ada):
    rows = c_all.shape[0]
    n = w_ada.shape[1]
    tn = 1024
    return pl.pallas_call(
        _ada_kernel,
        grid=(n // tn,),
        in_specs=[
            pl.BlockSpec((rows, D_MODEL), lambda j: (0, 0)),
            pl.BlockSpec((D_MODEL, tn), lambda j: (0, j)),
            pl.BlockSpec((1, tn), lambda j: (0, j)),
        ],
        out_specs=pl.BlockSpec((rows, tn), lambda j: (0, j)),
        out_shape=jax.ShapeDtypeStruct((rows, n), F32),
        compiler_params=_cparams("arbitrary"),
        name="ada_mod",
    )(c_all, w_ada, b_ada)


def _qkv_kernel(x_ref, g_ref, sc_ref, sh_ref, w_ref, kf_ref, vf_ref, qkvb_ref):
    h = _norm_mod(x_ref[...], g_ref[...], _mod(sc_ref), _mod(sh_ref)).astype(BF16)
    q = _dot(h, w_ref[:, 0:A_WIDTH])
    qkvb_ref[:, 0:A_WIDTH] = (q * Q_SCALE).astype(BF16)
    k = _dot(h, w_ref[:, A_WIDTH:2 * A_WIDTH])
    kf_ref[...] = k
    qkvb_ref[:, A_WIDTH:2 * A_WIDTH] = k.astype(BF16)
    v = _dot(h, w_ref[:, 2 * A_WIDTH:3 * A_WIDTH])
    vf_ref[...] = v
    qkvb_ref[:, 2 * A_WIDTH:3 * A_WIDTH] = v.astype(BF16)


def _rw_kernel(x_ref, g_ref, sc_ref, sh_ref, w_ref, o_ref):
    h = _norm_mod(x_ref[...], g_ref[...], _mod(sc_ref), _mod(sh_ref)).astype(BF16)
    o_ref[...] = _dot(h, w_ref[:, 3 * A_WIDTH:])


def _mod_spec(mod, tm, tiles_per_seq, nidx):
    if mod.ndim == 3:
        if nidx == 2:
            return pl.BlockSpec((1, 1, D_MODEL), lambda i, j: (i // tiles_per_seq, 0, 0))
        return pl.BlockSpec((1, 1, D_MODEL), lambda i: (i // tiles_per_seq, 0, 0))
    if nidx == 2:
        return pl.BlockSpec((tm, D_MODEL), lambda i, j: (i, 0))
    return pl.BlockSpec((tm, D_MODEL), lambda i: (i, 0))


def _resident(shape):
    return pl.BlockSpec(shape, lambda i: (0,) * len(shape), pipeline_mode=pl.Buffered(1))


def _in_proj_call(x, g, sc, sh, w_in, tm, seq_len):
    m = x.shape[0]
    n_in = w_in.shape[1]
    common = [
        pl.BlockSpec((tm, D_MODEL), lambda i: (i, 0)),
        _resident((1, D_MODEL)),
        _mod_spec(sc, tm, seq_len // tm, 1),
        _mod_spec(sh, tm, seq_len // tm, 1),
    ]
    kf, vf, qkvb = pl.pallas_call(
        _qkv_kernel,
        grid=(m // tm,),
        in_specs=common + [_resident((D_MODEL, n_in))],
        out_specs=[
            pl.BlockSpec((tm, A_WIDTH), lambda i: (i, 0)),
            pl.BlockSpec((tm, A_WIDTH), lambda i: (i, 0)),
            pl.BlockSpec((tm, 3 * A_WIDTH), lambda i: (i, 0)),
        ],
        out_shape=[
            jax.ShapeDtypeStruct((m, A_WIDTH), F32),
            jax.ShapeDtypeStruct((m, A_WIDTH), F32),
            jax.ShapeDtypeStruct((m, 3 * A_WIDTH), BF16),
        ],
        compiler_params=_cparams("arbitrary"),
        name="in_proj_qkv",
    )(x, g, sc, sh, w_in)
    rw = pl.pallas_call(
        _rw_kernel,
        grid=(m // tm,),
        in_specs=common + [_resident((D_MODEL, n_in))],
        out_specs=pl.BlockSpec((tm, RW_COLS), lambda i: (i, 0)),
        out_shape=jax.ShapeDtypeStruct((m, RW_COLS), F32),
        compiler_params=_cparams("arbitrary"),
        name="in_proj_rw",
    )(x, g, sc, sh, w_in)
    return kf, vf, qkvb, rw


def _softmax_step(s, vb, m_ref, l_ref, acc_ref, idx):
    m_prev = m_ref[idx]
    m_new = jnp.maximum(m_prev, jnp.max(s, axis=-1, keepdims=True))
    alpha = jnp.exp2(m_prev - m_new)
    p = jnp.exp2(s - m_new)
    l_ref[idx] = alpha * l_ref[idx] + jnp.sum(p, axis=-1, keepdims=True)
    acc_ref[idx] = alpha * acc_ref[idx] + _dot(p.astype(BF16), vb)
    m_ref[idx] = m_new


def _subln(o, g):
    return _rms(o, g) * (1.0 - LAM_INIT)


PATTN_HEADS = 2
PATTN_QG = 512


def _pattn_kernel(q1_ref, k1_ref, q2_ref, k2_ref, q_ref, k_ref, v_ref, g_ref, o_ref,
                  m_scr, l_scr, acc_scr, s_scr, t_scr, *, tq):
    i = pl.program_id(2)
    lane = lax.broadcasted_iota(jnp.int32, (tq, A_DV), 1)
    m_scr[...] = jnp.full(m_scr.shape, NEG_BIG, F32)
    l_scr[...] = jnp.zeros(l_scr.shape, F32)
    acc_scr[...] = jnp.zeros(acc_scr.shape, F32)

    qg = min(PATTN_QG, 2 * tq)
    n_qg = 2 * tq // qg
    q_gs = []
    for hh in range(PATTN_HEADS):
        q = q_ref[0, :, hh * A_DV:(hh + 1) * A_DV]
        zero = jnp.zeros_like(q)
        q_st = jnp.concatenate([jnp.where(lane < A_DK, q, zero), jnp.where(lane >= A_DK, q, zero)], axis=0)
        q_gs.append([q_st[g * qg:(g + 1) * qg] for g in range(n_qg)])
    ones_k = jnp.ones((SUBLANES, tq), BF16)
    chains = [(hh, g) for g in range(n_qg) for hh in range(PATTN_HEADS)]

    def scores(j):
        r0 = pl.multiple_of(j * tq, tq)
        kbs = [k_ref[0, pl.ds(r0, tq), hh * A_DV:(hh + 1) * A_DV] for hh in range(PATTN_HEADS)]
        return [_dot_nt(kbs[hh], q_gs[hh][g]) for hh, g in chains]

    def update(j, s_ref, masks):
        r0 = pl.multiple_of(j * tq, tq)
        vbs = [v_ref[0, pl.ds(r0, tq), hh * A_DV:(hh + 1) * A_DV] for hh in range(PATTN_HEADS)]
        for ci, (hh, g) in enumerate(chains):
            cs = slice(g * qg, (g + 1) * qg)
            s_t = s_ref[ci]
            if masks is not None:
                s_t = jnp.where(masks[g], s_t, NEG_BIG)
            m_prev = m_scr[hh, :, cs]
            m_new = jnp.maximum(m_prev, jnp.max(s_t, axis=0, keepdims=True))
            alpha = jnp.exp2(m_prev - m_new)
            p_b = jnp.exp2(s_t - m_new).astype(BF16)
            l_scr[hh, :, cs] = alpha * l_scr[hh, :, cs] + _dot(ones_k, p_b)[0:1]
            acc_scr[hh, :, cs] = alpha * acc_scr[hh, :, cs] + _dot_tn(vbs[hh], p_b)
            m_scr[hh, :, cs] = m_new

    def put(s_ref, j):
        for ci, s_t in enumerate(scores(j)):
            s_ref[ci] = s_t

    rk = lax.broadcasted_iota(jnp.int32, (tq, qg), 0)
    cq = lax.broadcasted_iota(jnp.int32, (tq, qg), 1)
    diag = [_idiv(rk, ATTN_CHUNK) <= _idiv((cq + g * qg) & (tq - 1), ATTN_CHUNK) for g in range(n_qg)]

    put(s_scr, 0)

    def body(jp, carry):
        j = 2 * jp
        put(t_scr, j + 1)
        update(j, s_scr, None)
        put(s_scr, j + 2)
        update(j + 1, t_scr, None)
        return carry

    lax.fori_loop(0, i // 2, body, 0)

    @pl.when(i % 2 == 0)
    def _():
        update(i, s_scr, diag)

    @pl.when(i % 2 == 1)
    def _():
        put(t_scr, i)
        update(i - 1, s_scr, None)
        update(i, t_scr, diag)

    lam = _lam(q1_ref, k1_ref, q2_ref, k2_ref)
    for hh in range(PATTN_HEADS):
        o_t = (acc_scr[hh, :, 0:tq] / l_scr[hh, :, 0:tq]
               - lam * (acc_scr[hh, :, tq:2 * tq] / l_scr[hh, :, tq:2 * tq]))
        o_ref[0, :, hh * A_DV:(hh + 1) * A_DV] = _subln(o_t.T, g_ref[...]).astype(BF16)


def _lam_specs(nidx):
    if nidx == 3:
        return [pl.BlockSpec((1, A_DK), lambda b, h, i: (0, 0))] * 4
    return [pl.BlockSpec((1, A_DK), lambda b, j: (0, 0))] * 4


def _pattn_call(lams, qkvb, g_subln, tq):
    bsz, t, _ = qkvb.shape
    hw = PATTN_HEADS * A_DV
    ng = A_HEADS // PATTN_HEADS
    return pl.pallas_call(
        functools.partial(_pattn_kernel, tq=tq),
        grid=(bsz, ng, t // tq),
        in_specs=_lam_specs(3) + [
            pl.BlockSpec((1, tq, hw), lambda b, h, i: (b, i, h)),
            pl.BlockSpec((1, t, hw), lambda b, h, i: (b, 0, ng + h)),
            pl.BlockSpec((1, t, hw), lambda b, h, i: (b, 0, 2 * ng + h)),
            pl.BlockSpec((1, A_DV), lambda b, h, i: (0, 0)),
        ],
        out_specs=pl.BlockSpec((1, tq, hw), lambda b, h, i: (b, i, h)),
        out_shape=jax.ShapeDtypeStruct((bsz, t, A_WIDTH), BF16),
        scratch_shapes=[
            pltpu.VMEM((PATTN_HEADS, 1, 2 * tq), F32),
            pltpu.VMEM((PATTN_HEADS, 1, 2 * tq), F32),
            pltpu.VMEM((PATTN_HEADS, A_DV, 2 * tq), F32),
        ] + [pltpu.VMEM((PATTN_HEADS * (2 * tq // min(PATTN_QG, 2 * tq)), tq, min(PATTN_QG, 2 * tq)), F32)] * 2,
        compiler_params=_cparams("arbitrary", "arbitrary", "arbitrary"),
        name="prompt_attention",
    )(*lams, qkvb, qkvb, qkvb, g_subln)


N_VQ = 2 * A_HEADS


def _sattn_kernel(q1_ref, k1_ref, q2_ref, k2_ref, q_ref, kn_ref, vn_ref, ck_ref, cv_ref, g_ref, o_ref,
                  qb_scr, m_scr, l_scr, acc_scr, *, tnew):
    j = pl.program_id(1)
    nvq = N_VQ * tnew

    @pl.when(j == 0)
    def _():
        q = q_ref[0]
        qt = jnp.concatenate([q] * N_VQ, axis=0)
        row = lax.broadcasted_iota(jnp.int32, (nvq, A_WIDTH), 0)
        col = lax.broadcasted_iota(jnp.int32, (nvq, A_WIDTH), 1)
        qb_scr[...] = jnp.where(_idiv(row, tnew) == _idiv(col, A_DK), qt, jnp.zeros_like(qt))
        m_scr[...] = jnp.full(m_scr.shape, NEG_BIG, F32)
        l_scr[...] = jnp.zeros(l_scr.shape, F32)
        acc_scr[...] = jnp.zeros(acc_scr.shape, F32)

    def step(kb, vb):
        _softmax_step(_dot_nt(qb_scr[...], kb), vb, m_scr, l_scr, acc_scr, 0)

    def cached_rows(ref, p0, n):
        heads = [ref[0, pl.ds(p0 * A_HEADS + h, n, stride=A_HEADS), :] for h in range(A_HEADS)]
        return jnp.concatenate(heads, axis=-1).astype(BF16)

    tk = ck_ref.shape[1] // A_HEADS
    n_sub = max(1, tk // MXU_DIM)
    sub = tk // n_sub
    kbs = [cached_rows(ck_ref, c * sub, sub) for c in range(n_sub)]
    s_all = [_dot_nt(qb_scr[...], kb) for kb in kbs]
    for c in range(n_sub):
        _softmax_step(s_all[c], cached_rows(cv_ref, c * sub, sub), m_scr, l_scr, acc_scr, 0)

    @pl.when(j == pl.num_programs(1) - 1)
    def _():
        step(kn_ref[0], vn_ref[0])
        lam = _lam(q1_ref, k1_ref, q2_ref, k2_ref)
        g = g_ref[...]
        for h in range(A_HEADS):
            cs = slice(h * A_DV, (h + 1) * A_DV)
            r0 = slice((2 * h) * tnew, (2 * h + 1) * tnew)
            r1 = slice((2 * h + 1) * tnew, (2 * h + 2) * tnew)
            o0 = acc_scr[0, r0, cs] / l_scr[0, r0, :]
            o1 = acc_scr[0, r1, cs] / l_scr[0, r1, :]
            o_ref[0, :, cs] = _subln(o0 - lam * o1, g).astype(BF16)


def _sattn_call(lams, qkvb, cache_k, cache_v, g_subln, tk):
    bsz, tnew, _ = qkvb.shape
    past = cache_k.shape[1] // A_HEADS
    nvq = N_VQ * tnew
    return pl.pallas_call(
        functools.partial(_sattn_kernel, tnew=tnew),
        grid=(bsz, past // tk),
        in_specs=_lam_specs(2) + [
            pl.BlockSpec((1, tnew, A_WIDTH), lambda b, j: (b, 0, 0)),
            pl.BlockSpec((1, tnew, A_WIDTH), lambda b, j: (b, 0, 1)),
            pl.BlockSpec((1, tnew, A_WIDTH), lambda b, j: (b, 0, 2)),
            pl.BlockSpec((1, tk * A_HEADS, A_DV), lambda b, j: (b, j, 0)),
            pl.BlockSpec((1, tk * A_HEADS, A_DV), lambda b, j: (b, j, 0)),
            pl.BlockSpec((1, A_DV), lambda b, j: (0, 0)),
        ],
        out_specs=pl.BlockSpec((1, tnew, A_WIDTH), lambda b, j: (b, 0, 0)),
        out_shape=jax.ShapeDtypeStruct((bsz, tnew, A_WIDTH), BF16),
        scratch_shapes=[
            pltpu.VMEM((nvq, A_WIDTH), BF16),
            pltpu.VMEM((1, nvq, 1), F32),
            pltpu.VMEM((1, nvq, 1), F32),
            pltpu.VMEM((1, nvq, A_WIDTH), F32),
        ],
        compiler_params=_cparams("arbitrary", "arbitrary"),
        name="sample_attention",
    )(*lams, qkvb, qkvb, qkvb, cache_k, cache_v, g_subln)


def _head_sum(x, ones_bd):
    return _dot(x.astype(BF16), ones_bd)


def _rwkv_kernel(p_ref, ph_ref, st_ref, s0_ref, mu_ref, w0_ref, a0_ref, kk_ref, ka_ref, rk_ref,
                 lnw_ref, lnb_ref, ww2_ref, wa2_ref, wg2_ref, ob_ref, so_ref,
                 r_scr, k_scr, v_scr, na_scr, nb_scr, ld_scr, cum_scr, g_scr, bon_scr, y_scr, s_scr,
                 *, tb, chunk):
    i = pl.program_id(1)
    n_seq = p_ref.shape[0]
    c_len = chunk
    n_chunks = tb // c_len

    row_g = lax.broadcasted_iota(jnp.int32, (MXU_DIM, MXU_DIM), 0)
    col_g = lax.broadcasted_iota(jnp.int32, (MXU_DIM, MXU_DIM), 1)
    bd_mask = _idiv(row_g, B_HEAD) == _idiv(col_g, B_HEAD)
    ones_bd = jnp.where(bd_mask, 1.0, 0.0).astype(BF16)

    @pl.when(i == 0)
    def _():
        for b in range(n_seq):
            for q in range(N_GROUPS):
                narrow = s0_ref[b, q]
                wide = jnp.concatenate([narrow] * GROUP_HEADS, axis=1)
                s_scr[b, q] = jnp.where(bd_mask, wide, 0.0)

    row_t = lax.broadcasted_iota(jnp.int32, (tb, tb), 0)
    col_t = lax.broadcasted_iota(jnp.int32, (tb, tb), 1)
    tri = jnp.where((_idiv(row_t, c_len) == _idiv(col_t, c_len)) & (col_t <= row_t), 1.0, 0.0).astype(BF16)
    w_cat = jnp.concatenate([ww2_ref[...], wa2_ref[...]], axis=0)

    for b in range(n_seq):
        p = p_ref[b]
        prev_row = jnp.where(i == 0, st_ref[b], ph_ref[b, SUBLANES - 1:SUBLANES, :])
        rowi = lax.broadcasted_iota(jnp.int32, p.shape, 0)
        p_prev = jnp.where(rowi == 0, prev_row, pltpu.roll(p, 1, 0))
        xs = p + (p_prev - p) * mu_ref[...]
        r = xs[:, 0:B_WIDTH]
        k = xs[:, B_WIDTH:2 * B_WIDTH]
        v = xs[:, 2 * B_WIDTH:3 * B_WIDTH]
        wa = xs[:, 3 * B_WIDTH:3 * B_WIDTH + W_RANK + A_RANK]
        gd = xs[:, 3 * B_WIDTH + W_RANK + A_RANK:]

        lane_wa = lax.broadcasted_iota(jnp.int32, wa.shape, 1)
        zero_wa = jnp.zeros_like(wa)
        w_lin = _dot(jnp.where(lane_wa < W_RANK, jnp.tanh(wa), zero_wa).astype(BF16), w_cat)
        a_lin = _dot(jnp.where(lane_wa >= W_RANK, wa, zero_wa).astype(BF16), w_cat)
        logd = -math.exp(-0.5) * _sigmoid(w0_ref[...] + w_lin)
        a = _sigmoid(a0_ref[...] + a_lin)
        g_scr[b] = _dot(_sigmoid(gd).astype(BF16), wg2_ref[...])

        k2 = k * (1.0 + (a - 1.0) * ka_ref[...])
        kk = k * kk_ref[...]
        rkk = r * k2 * rk_ref[...]
        for q in range(N_GROUPS):
            ls = slice(q * MXU_DIM, (q + 1) * MXU_DIM)
            kq = kk[:, ls]
            kn = kq * lax.rsqrt(jnp.maximum(_head_sum(kq * kq, ones_bd), 1e-24))
            na_scr[b, :, ls] = -kn
            nb_scr[b, :, ls] = kn * a[:, ls]
            bon_scr[b, :, ls] = _head_sum(rkk[:, ls], ones_bd) * v[:, ls]
        r_scr[b] = r
        k_scr[b] = k2
        v_scr[b] = v
        ld_scr[b] = logd

        cum = None
        for piece in _split_bf16(logd, 2):
            d = _dot(tri, piece)
            cum = d if cum is None else cum + d
        cum_scr[b] = cum

    gc = GROUP_HEADS * c_len
    lane_q = lax.broadcasted_iota(jnp.int32, (1, MXU_DIM), 1)
    head_masks = [_idiv(lane_q, B_HEAD) == j for j in range(GROUP_HEADS)]
    row_s = lax.broadcasted_iota(jnp.int32, (gc, gc), 0)
    col_s = lax.broadcasted_iota(jnp.int32, (gc, gc), 1)
    same_head = _idiv(row_s, c_len) == _idiv(col_s, c_len)
    strict = same_head & (col_s < row_s)
    incl = same_head & (col_s <= row_s)
    eye = jnp.where(col_s == row_s, 1.0, 0.0).astype(F32)
    n_double = int(math.log2(c_len)) - 1
    chains = [(b, q) for q in range(N_GROUPS) for b in range(n_seq)]
    groups = range(len(chains))

    def stack(x):
        z = jnp.zeros_like(x)
        return jnp.concatenate([jnp.where(head_masks[j], x, z) for j in range(GROUP_HEADS)], axis=0)

    def unstack(x):
        out = x[0:c_len]
        for j in range(1, GROUP_HEADS):
            out = out + x[j * c_len:(j + 1) * c_len]
        return out

    def chunk_body(c, carry):
        r0 = pl.multiple_of(c * c_len, c_len)
        rows = pl.ds(r0, c_len)
        a_s, r_s, b_s, k_s, be_s, ke_s, v_s, dec, ar_c = [], [], [], [], [], [], [], [], []
        for b, q in chains:
            ls = slice(q * MXU_DIM, (q + 1) * MXU_DIM)
            cum_q = cum_scr[b, rows, ls]
            k_q = k_scr[b, rows, ls]
            nb_q = nb_scr[b, rows, ls]
            cum_last = cum_q[c_len - 1:c_len, :]
            e_out = jnp.exp(-cum_q)
            e_end = jnp.exp(cum_last - cum_q)
            a_c = (na_scr[b, rows, ls] * jnp.exp(cum_q - ld_scr[b, rows, ls])).astype(BF16)
            r_c = (r_scr[b, rows, ls] * jnp.exp(cum_q)).astype(BF16)
            ar_c.append(jnp.concatenate([a_c, r_c], axis=0))
            a_s.append(stack(a_c))
            r_s.append(stack(r_c))
            b_s.append(stack((nb_q * e_out).astype(BF16)))
            k_s.append(stack((k_q * e_out).astype(BF16)))
            be_s.append(stack((nb_q * e_end).astype(BF16)))
            ke_s.append(stack((k_q * e_end).astype(BF16)))
            v_s.append(stack(v_scr[b, rows, ls].astype(BF16)))
            dec.append(jnp.exp(cum_last))

        if gc % LANES == 0:
            g_all = [_dot_nt(jnp.concatenate([a_s[q], r_s[q]], axis=0),
                             jnp.concatenate([b_s[q], k_s[q]], axis=0)) for q in groups]
            g_ab = [g[0:gc, 0:gc] for g in g_all]
            g_ak = [g[0:gc, gc:2 * gc] for g in g_all]
            g_rb = [g[gc:2 * gc, 0:gc] for g in g_all]
            g_rk = [g[gc:2 * gc, gc:2 * gc] for g in g_all]
        else:
            g_ab = [_dot_nt(a_s[q], b_s[q]) for q in groups]
            g_ak = [_dot_nt(a_s[q], k_s[q]) for q in groups]
            g_rb = [_dot_nt(r_s[q], b_s[q]) for q in groups]
            g_rk = [_dot_nt(r_s[q], k_s[q]) for q in groups]
        aab = [jnp.where(strict, g, 0.0) for g in g_ab]
        aak = [jnp.where(strict, g, 0.0).astype(BF16) for g in g_ak]
        arb = [jnp.where(incl, g, 0.0).astype(BF16) for g in g_rb]
        ark = [jnp.where(incl, g, 0.0).astype(BF16) for g in g_rk]

        t_m = [eye + x for x in aab]
        pw = [x.astype(BF16) for x in aab]
        pw = [_dot(x, x) for x in pw]
        for m in range(n_double):
            pb = [x.astype(BF16) for x in pw]
            if m + 1 < n_double:
                pw = [_dot(x, x) for x in pb]
            t_m = [t_m[q] + _dot(t_m[q].astype(BF16), pb[q]) for q in groups]
        t_b = [x.astype(BF16) for x in t_m]

        av = [_dot(aak[q], v_s[q]) for q in groups]

        s_prev = [s_scr[b, q] for b, q in chains]
        ar = [_dot_nt(ar_c[q], s_prev[q].astype(BF16)) for q in groups]
        u_b = [_dot(t_b[q], (stack(ar[q][0:c_len]) + av[q]).astype(BF16)).astype(BF16) for q in groups]
        uv = [jnp.concatenate([u_b[q], v_s[q]], axis=0) for q in groups]
        if gc % LANES == 0:
            y_st = [_dot(jnp.concatenate([arb[q], ark[q]], axis=1), uv[q]) for q in groups]
        else:
            y_st = [_dot(arb[q], u_b[q]) + _dot(ark[q], v_s[q]) for q in groups]
        s_new = [s_prev[q] * dec[q]
                 + _dot_tn(uv[q], jnp.concatenate([be_s[q], ke_s[q]], axis=0)) for q in groups]
        for ci, (b, q) in enumerate(chains):
            s_scr[b, q] = s_new[ci]
            y_scr[b, rows, q * MXU_DIM:(q + 1) * MXU_DIM] = ar[ci][c_len:2 * c_len] + unstack(y_st[ci])
        return carry

    lax.fori_loop(0, n_chunks, chunk_body, 0)

    @pl.when(i == pl.num_programs(1) - 1)
    def _():
        for b, q in chains:
            wide = s_scr[b, q]
            narrow = wide[:, 0:B_HEAD]
            for j in range(1, GROUP_HEADS):
                narrow = narrow + wide[:, j * B_HEAD:(j + 1) * B_HEAD]
            so_ref[b, q] = narrow

    inv_n = 1.0 / B_HEAD
    for b, q in chains:
        ls = slice(q * MXU_DIM, (q + 1) * MXU_DIM)
        y = y_scr[b, :, ls]
        mu = _head_sum(y, ones_bd) * inv_n
        d = y - mu
        var = _head_sum(d * d, ones_bd) * inv_n
        yn = d * lax.rsqrt(var + GN_EPS) * lnw_ref[:, ls] + lnb_ref[:, ls]
        ob_ref[b, :, ls] = ((yn + bon_scr[b, :, ls]) * g_scr[b, :, ls]).astype(BF16)


def _rwkv_call(rw, shift0, s0, prm, tb, chunk, n_seq):
    bsz, t, _ = rw.shape
    assert bsz % n_seq == 0
    nt = t // tb
    hb = tb // SUBLANES
    vec = lambda n: pl.BlockSpec((1, n), lambda b, i: (0, 0))
    return pl.pallas_call(
        functools.partial(_rwkv_kernel, tb=tb, chunk=chunk),
        grid=(bsz // n_seq, nt),
        in_specs=[
            pl.BlockSpec((n_seq, tb, RW_COLS), lambda b, i: (b, i, 0)),
            pl.BlockSpec((n_seq, SUBLANES, RW_COLS), lambda b, i: (b, jnp.maximum(i * hb - 1, 0), 0)),
            pl.BlockSpec((n_seq, 1, RW_COLS), lambda b, i: (b, 0, 0)),
            pl.BlockSpec((n_seq, N_GROUPS, MXU_DIM, B_HEAD), lambda b, i: (b, 0, 0, 0)),
            vec(RW_COLS), vec(B_WIDTH), vec(B_WIDTH), vec(B_WIDTH), vec(B_WIDTH), vec(B_WIDTH),
            vec(B_WIDTH), vec(B_WIDTH),
            pl.BlockSpec((W_RANK, B_WIDTH), lambda b, i: (0, 0)),
            pl.BlockSpec((A_RANK, B_WIDTH), lambda b, i: (0, 0)),
            pl.BlockSpec((G_RANK, B_WIDTH), lambda b, i: (0, 0)),
        ],
        out_specs=[
            pl.BlockSpec((n_seq, tb, B_WIDTH), lambda b, i: (b, i, 0)),
            pl.BlockSpec((n_seq, N_GROUPS, MXU_DIM, B_HEAD), lambda b, i: (b, 0, 0, 0)),
        ],
        out_shape=[
            jax.ShapeDtypeStruct((bsz, t, B_WIDTH), BF16),
            jax.ShapeDtypeStruct((bsz, N_GROUPS, MXU_DIM, B_HEAD), F32),
        ],
        scratch_shapes=[pltpu.VMEM((n_seq, tb, B_WIDTH), F32)] * 10
        + [pltpu.VMEM((n_seq, N_GROUPS, MXU_DIM, MXU_DIM), F32)],
        compiler_params=_cparams("arbitrary", "arbitrary"),
        name="rwkv7_mix",
    )(rw, rw, shift0, s0, prm["mu_shift"], prm["w0"], prm["a0"], prm["k_k"], prm["k_a"], prm["r_k"],
      prm["ln_x_w"], prm["ln_x_b"], prm["w_w2"], prm["w_a2"], prm["w_g2"])


def _state_to_groups(s):
    return s.reshape(s.shape[0], N_GROUPS, MXU_DIM, B_HEAD)


def _groups_to_state(sg):
    return sg.reshape(sg.shape[0], B_HEADS, B_HEAD, B_HEAD)


def _mixout_kernel(oa_ref, ob_ref, x_ref, w_ref, g_ref, gt_ref, o_ref):
    mix = _dot(oa_ref[...], w_ref[0:A_WIDTH, :]) + _dot(ob_ref[...], w_ref[A_WIDTH:D_MODEL, :])
    o_ref[...] = x_ref[...] + _mod(gt_ref) * _rms(mix, g_ref[...])


def _mixout_call(oa, ob, x, w_out, g, gt, tm, tiles_per_seq):
    m = x.shape[0]
    return pl.pallas_call(
        _mixout_kernel,
        grid=(m // tm,),
        in_specs=[
            pl.BlockSpec((tm, A_WIDTH), lambda i: (i, 0)),
            pl.BlockSpec((tm, B_WIDTH), lambda i: (i, 0)),
            pl.BlockSpec((tm, D_MODEL), lambda i: (i, 0)),
            pl.BlockSpec((D_MODEL, D_MODEL), lambda i: (0, 0)),
            pl.BlockSpec((1, D_MODEL), lambda i: (0, 0)),
            _mod_spec(gt, tm, tiles_per_seq, 1),
        ],
        out_specs=pl.BlockSpec((tm, D_MODEL), lambda i: (i, 0)),
        out_shape=jax.ShapeDtypeStruct((m, D_MODEL), F32),
        compiler_params=_cparams("arbitrary"),
        name="mix_out",
    )(oa, ob, x, w_out, g, gt)


FFN_HALO = BF16_ROWS
FFN_TF = 512


def _gated(zg, zv):
    return (zg * _sigmoid(zg) * zv).astype(BF16)


def _ffn_kernel(x_ref, xh_ref, cpg_ref, cpv_ref, g_ref, sc_ref, sh_ref, gt_ref, wg_ref, wv_ref,
                cg_ref, cv_ref, wd_ref, gp_ref, o_ref, clg_ref, clv_ref, h_scr,
                *, tm, tiles_per_seq):
    mi = pl.program_id(0)
    f = pl.program_id(1)
    first = (mi % tiles_per_seq) == 0

    @pl.when(f == 0)
    def _():
        g, sc, sh = g_ref[...], sc_ref[0], sh_ref[0]
        h_scr[FFN_HALO:, :] = _norm_mod(x_ref[...], g, sc, sh).astype(BF16)
        h_scr[:FFN_HALO, :] = _norm_mod(xh_ref[...], g, sc, sh).astype(BF16)
        o_ref[...] = jnp.zeros(o_ref.shape, F32)

    h = h_scr[...]
    row = lax.broadcasted_iota(jnp.int32, (FFN_HALO, FFN_TF), 0)

    def conv(u, cp_ref, c_ref, cl_ref):
        cp = cp_ref[0]
        head = u[:FFN_HALO]
        head_first = jnp.where(row == FFN_HALO - 2, cp[0:1],
                               jnp.where(row == FFN_HALO - 1, cp[1:2], head))
        ue = jnp.concatenate([jnp.where(first, head_first, head), u[FFN_HALO:]], axis=0)
        cl_ref[0] = ue[tm + FFN_HALO - SUBLANES:]
        c = c_ref[...]
        return (c[0:1] * pltpu.roll(ue, 2, 0)[FFN_HALO:] + c[1:2] * pltpu.roll(ue, 1, 0)[FFN_HALO:]
                + c[2:3] * ue[FFN_HALO:])

    zg = conv(_dot(h, wg_ref[...]), cpg_ref, cg_ref, clg_ref)
    zv = conv(_dot(h, wv_ref[...]), cpv_ref, cv_ref, clv_ref)
    o_ref[...] += _dot(_gated(zg, zv), wd_ref[...])

    @pl.when(f == pl.num_programs(1) - 1)
    def _():
        o_ref[...] = x_ref[...] + gt_ref[0] * _rms(o_ref[...], gp_ref[...])


def _ffn_call(x, conv_prev, g, sc, sh, gt, w_up, w_conv, w_down, gp, tm, tiles_per_seq):
    m = x.shape[0]
    nseq = conv_prev.shape[0]
    nf = D_FF // FFN_TF
    hb = tm // FFN_HALO
    seq = lambda i: i // tiles_per_seq
    modspec = pl.BlockSpec((1, 1, D_MODEL), lambda i, f: (seq(i), 0, 0))
    return pl.pallas_call(
        functools.partial(_ffn_kernel, tm=tm, tiles_per_seq=tiles_per_seq),
        grid=(m // tm, nf),
        in_specs=[
            pl.BlockSpec((tm, D_MODEL), lambda i, f: (i, 0), pipeline_mode=pl.Buffered(1)),
            pl.BlockSpec((FFN_HALO, D_MODEL), lambda i, f: (jnp.maximum(i * hb - 1, 0), 0)),
            pl.BlockSpec((1, FFN_CONV - 1, FFN_TF), lambda i, f: (seq(i), 0, f)),
            pl.BlockSpec((1, FFN_CONV - 1, FFN_TF), lambda i, f: (seq(i), 0, f + nf)),
            pl.BlockSpec((1, D_MODEL), lambda i, f: (0, 0)),
            modspec, modspec, modspec,
            pl.BlockSpec((D_MODEL, FFN_TF), lambda i, f: (0, f)),
            pl.BlockSpec((D_MODEL, FFN_TF), lambda i, f: (0, f + nf)),
            pl.BlockSpec((FFN_CONV, FFN_TF), lambda i, f: (0, f)),
            pl.BlockSpec((FFN_CONV, FFN_TF), lambda i, f: (0, f + nf)),
            pl.BlockSpec((FFN_TF, D_MODEL), lambda i, f: (f, 0)),
            pl.BlockSpec((1, D_MODEL), lambda i, f: (0, 0)),
        ],
        out_specs=[
            pl.BlockSpec((tm, D_MODEL), lambda i, f: (i, 0)),
            pl.BlockSpec((1, SUBLANES, FFN_TF), lambda i, f: (i, 0, f)),
            pl.BlockSpec((1, SUBLANES, FFN_TF), lambda i, f: (i, 0, f)),
        ],
        out_shape=[
            jax.ShapeDtypeStruct((m, D_MODEL), F32),
            jax.ShapeDtypeStruct((m // tm, SUBLANES, D_FF), F32),
            jax.ShapeDtypeStruct((m // tm, SUBLANES, D_FF), F32),
        ],
        scratch_shapes=[pltpu.VMEM((tm + FFN_HALO, D_MODEL), BF16)],
        compiler_params=_cparams("arbitrary", "arbitrary"),
        name="conv_ffn_prompt",
    )(x, x, conv_prev, conv_prev, g, sc, sh, gt, w_up, w_up, w_conv, w_conv, w_down, gp)


def _ffn_s_kernel(x_ref, cpg_ref, cpv_ref, g_ref, sc_ref, sh_ref, gt_ref, wg_ref, wv_ref,
                  cg_ref, cv_ref, wd_ref, gp_ref, o_ref, ug_ref, uv_ref, h_scr, acc_scr, *, seq_len):
    f = pl.program_id(1)

    @pl.when(f == 0)
    def _():
        h_scr[...] = _norm_mod(x_ref[...], g_ref[...], sc_ref[...], sh_ref[...]).astype(BF16)
        acc_scr[...] = jnp.zeros(acc_scr.shape, F32)

    h = h_scr[...]
    tm = h.shape[0]
    pos = lax.broadcasted_iota(jnp.int32, (seq_len, FFN_TF), 0)

    def conv(u, cp_ref, c_ref, u_ref):
        u_ref[...] = u
        cp = cp_ref[...]
        u1 = pltpu.roll(u, 1, 0)
        u2 = pltpu.roll(u, 2, 0)
        p1, p2 = [], []
        for s in range(tm // seq_len):
            rs = slice(s * seq_len, (s + 1) * seq_len)
            c0, c1 = cp[2 * s:2 * s + 1], cp[2 * s + 1:2 * s + 2]
            p1.append(jnp.where(pos == 0, c1, u1[rs]))
            p2.append(jnp.where(pos == 0, c0, jnp.where(pos == 1, c1, u2[rs])))
        c = c_ref[...]
        return c[0:1] * jnp.concatenate(p2, axis=0) + c[1:2] * jnp.concatenate(p1, axis=0) + c[2:3] * u

    zg = conv(_dot(h, wg_ref[...]), cpg_ref, cg_ref, ug_ref)
    zv = conv(_dot(h, wv_ref[...]), cpv_ref, cv_ref, uv_ref)
    acc_scr[...] += _dot(_gated(zg, zv), wd_ref[...])

    @pl.when(f == pl.num_programs(1) - 1)
    def _():
        o_ref[...] = x_ref[...] + gt_ref[...] * _rms(acc_scr[...], gp_ref[...])


def _ffn_s_call(x, conv_prev, g, sc, sh, gt, w_up, w_conv, w_down, gp, seq_len, tm):
    m = x.shape[0]
    assert tm % seq_len == 0 and seq_len & (seq_len - 1) == 0
    nf = D_FF // FFN_TF
    n_cp = (tm // seq_len) * (FFN_CONV - 1)
    full = pl.BlockSpec((tm, D_MODEL), lambda i, f: (i, 0))
    full_in = pl.BlockSpec((tm, D_MODEL), lambda i, f: (i, 0), pipeline_mode=pl.Buffered(1))
    lo = pl.BlockSpec((tm, FFN_TF), lambda i, f: (i, f))
    return pl.pallas_call(
        functools.partial(_ffn_s_kernel, seq_len=seq_len),
        grid=(m // tm, nf),
        in_specs=[
            full_in,
            pl.BlockSpec((n_cp, FFN_TF), lambda i, f: (i, f)),
            pl.BlockSpec((n_cp, FFN_TF), lambda i, f: (i, f + nf)),
            pl.BlockSpec((1, D_MODEL), lambda i, f: (0, 0)),
            full_in, full_in, full_in,
            pl.BlockSpec((D_MODEL, FFN_TF), lambda i, f: (0, f)),
            pl.BlockSpec((D_MODEL, FFN_TF), lambda i, f: (0, f + nf)),
            pl.BlockSpec((FFN_CONV, FFN_TF), lambda i, f: (0, f)),
            pl.BlockSpec((FFN_CONV, FFN_TF), lambda i, f: (0, f + nf)),
            pl.BlockSpec((FFN_TF, D_MODEL), lambda i, f: (f, 0)),
            pl.BlockSpec((1, D_MODEL), lambda i, f: (0, 0)),
        ],
        out_specs=[full, lo, lo],
        out_shape=[
            jax.ShapeDtypeStruct((m, D_MODEL), F32),
            jax.ShapeDtypeStruct((m, D_FF), F32),
            jax.ShapeDtypeStruct((m, D_FF), F32),
        ],
        scratch_shapes=[pltpu.VMEM((tm, D_MODEL), BF16), pltpu.VMEM((tm, D_MODEL), F32)],
        compiler_params=_cparams("arbitrary", "arbitrary"),
        name="conv_ffn_sample",
    )(x, conv_prev, conv_prev, g, sc, sh, gt, w_up, w_up, w_conv, w_conv, w_down, gp)


def _seq_tile(t, cap):
    tm = min(t, cap)
    assert t % tm == 0
    return tm


class _Tiles(NamedTuple):
    rows: int
    attn: int
    wkv_rows: int
    wkv_chunk: int
    wkv_seqs: int
    ffn_rows: int


def _prompt_tiles(bp, t):
    return _Tiles(rows=_seq_tile(t, 512), attn=_seq_tile(t, 512), wkv_rows=_seq_tile(t, 256),
                  wkv_chunk=min(t, 32), wkv_seqs=2 if bp % 2 == 0 else 1, ffn_rows=_seq_tile(t, 1024))


def _sample_tiles(bs, ts, past):
    rows = bs * ts
    return _Tiles(rows=rows, attn=_seq_tile(past, 2048), wkv_rows=ts, wkv_chunk=ts,
                  wkv_seqs=4 if bs % 4 == 0 else 1, ffn_rows=_seq_tile(rows, 512))


@jax.jit
def _forward(x_prompt, x_sample, c_prompt, c_sample, cache_k, cache_v, state_wkv, state_shift,
             state_ffn_conv, w_ada, b_ada, g_pre_mix, g_post_mix, g_pre_ffn, g_post_ffn, w_in,
             lam_q1, lam_k1, lam_q2, lam_k2, g_subln, mu_shift, w0, w_w2, a0, w_a2, w_g2, k_k, k_a,
             r_k, ln_x_w, ln_x_b, w_out, w_up, w_conv_ffn, w_down):
    bp, t, _ = x_prompt.shape
    bs, ts, _ = x_sample.shape
    past = cache_k.shape[2]
    assert w_ada.shape[0] == 1, "single-layer problem"

    n_c = bp + bs
    n_c_pad = -(-n_c // SUBLANES) * SUBLANES
    c_all = jnp.concatenate([c_prompt, c_sample, jnp.zeros((n_c_pad - n_c, D_MODEL), F32)], axis=0)
    mod = _ada_call(c_all, w_ada[0], b_ada)
    sh_m, sc_m, gt_m, sh_f, sc_f, gt_f = [mod[:, i * D_MODEL:(i + 1) * D_MODEL] for i in range(6)]
    p_mod = lambda a: a[:bp].reshape(bp, 1, D_MODEL)
    s_mod = lambda a: jnp.repeat(a[bp:n_c], ts, axis=0)

    w_in_b = w_in[0].astype(BF16)
    w_out_b = w_out[0].astype(BF16)
    w_up_b = w_up[0].astype(BF16)
    w_down_b = w_down[0].astype(BF16)
    lams = (lam_q1, lam_k1, lam_q2, lam_k2)
    prm = {
        "mu_shift": mu_shift, "w0": w0, "a0": a0, "k_k": k_k, "k_a": k_a,
        "r_k": r_k.reshape(1, B_WIDTH), "ln_x_w": ln_x_w, "ln_x_b": ln_x_b,
        "w_w2": w_w2[0].astype(BF16), "w_a2": w_a2[0].astype(BF16), "w_g2": w_g2[0].astype(BF16),
    }

    pt = _prompt_tiles(bp, t)
    tps_f = t // pt.ffn_rows
    xp = x_prompt.reshape(bp * t, D_MODEL)
    kf, vf, qkvb, rw = _in_proj_call(xp, g_pre_mix, p_mod(sc_m), p_mod(sh_m), w_in_b, pt.rows, t)
    oa = _pattn_call(lams, qkvb.reshape(bp, t, 3 * A_WIDTH), g_subln, pt.attn)
    ob, s_last = _rwkv_call(rw.reshape(bp, t, RW_COLS), jnp.zeros((bp, 1, RW_COLS), F32),
                            jnp.zeros((bp, N_GROUPS, MXU_DIM, B_HEAD), state_wkv.dtype), prm,
                            pt.wkv_rows, pt.wkv_chunk, pt.wkv_seqs)
    x1 = _mixout_call(oa.reshape(bp * t, A_WIDTH), ob.reshape(bp * t, B_WIDTH), xp, w_out_b,
                      g_post_mix, p_mod(gt_m), pt.rows, t // pt.rows)
    yp, clg, clv = _ffn_call(x1, jnp.zeros((bp, FFN_CONV - 1, 2 * D_FF), F32), g_pre_ffn, p_mod(sc_f),
                             p_mod(sh_f), p_mod(gt_f), w_up_b, w_conv_ffn[0], w_down_b, g_post_ffn,
                             pt.ffn_rows, tps_f)
    rw3 = rw.reshape(bp, t, RW_COLS)
    out_prompt = (
        yp.reshape(bp, t, D_MODEL),
        kf.reshape(1, bp, t, A_HEADS, 2 * A_DK),
        vf.reshape(1, bp, t, A_HEADS, A_DV),
        _groups_to_state(s_last)[None],
        rw3[:, -1][None],
        jnp.concatenate([clg[tps_f - 1::tps_f, -(FFN_CONV - 1):], clv[tps_f - 1::tps_f, -(FFN_CONV - 1):]],
                        axis=-1)[None],
    )

    ms = bs * ts
    st = _sample_tiles(bs, ts, past)
    xs = x_sample.reshape(ms, D_MODEL)
    kf_s, vf_s, qkvb_s, rw_s = _in_proj_call(xs, g_pre_mix, s_mod(sc_m), s_mod(sh_m), w_in_b, st.rows, ms)
    oa_s = _sattn_call(lams, qkvb_s.reshape(bs, ts, 3 * A_WIDTH),
                       cache_k[0].reshape(bs, past * A_HEADS, A_DV),
                       cache_v[0].reshape(bs, past * A_HEADS, A_DV), g_subln, st.attn)
    ob_s, s_last_s = _rwkv_call(rw_s.reshape(bs, ts, RW_COLS), state_shift[0][:, None, :],
                                _state_to_groups(state_wkv[0]), prm, st.wkv_rows, st.wkv_chunk, st.wkv_seqs)
    x1_s = _mixout_call(oa_s.reshape(ms, A_WIDTH), ob_s.reshape(ms, B_WIDTH), xs, w_out_b,
                        g_post_mix, s_mod(gt_m), st.rows, 1)
    cprev = state_ffn_conv[0].reshape(bs * (FFN_CONV - 1), 2 * D_FF)
    ys, ug, uv = _ffn_s_call(x1_s, cprev, g_pre_ffn, s_mod(sc_f), s_mod(sh_f), s_mod(gt_f),
                             w_up_b, w_conv_ffn[0], w_down_b, g_post_ffn, ts, st.ffn_rows)
    u_tail = jnp.concatenate([ug.reshape(bs, ts, D_FF)[:, -(FFN_CONV - 1):],
                              uv.reshape(bs, ts, D_FF)[:, -(FFN_CONV - 1):]], axis=-1)
    out_sample = (
        ys.reshape(bs, ts, D_MODEL),
        kf_s.reshape(1, bs, ts, A_HEADS, 2 * A_DK),
        vf_s.reshape(1, bs, ts, A_HEADS, A_DV),
        _groups_to_state(s_last_s)[None],
        rw_s.reshape(bs, ts, RW_COLS)[:, -1][None],
        u_tail[None],
    )
    return (out_prompt[0], out_sample[0]) + out_prompt[1:] + out_sample[1:]


def kernel(x_prompt, x_sample, c_prompt, c_sample, cache_k, cache_v, state_wkv, state_shift, state_ffn_conv, w_ada, b_ada, g_pre_mix, g_post_mix, g_pre_ffn, g_post_ffn, w_in, lam_q1, lam_k1, lam_q2, lam_k2, g_subln, mu_shift, w0, w_w2, a0, w_a2, w_g2, k_k, k_a, r_k, ln_x_w, ln_x_b, w_out, w_up, w_conv_ffn, w_down):
    return _forward(x_prompt, x_sample, c_prompt, c_sample, cache_k, cache_v, state_wkv, state_shift,
                    state_ffn_conv, w_ada, b_ada, g_pre_mix, g_post_mix, g_pre_ffn, g_post_ffn, w_in,
                    lam_q1, lam_k1, lam_q2, lam_k2, g_subln, mu_shift, w0, w_w2, a0, w_a2, w_g2, k_k,
                    k_a, r_k, ln_x_w, ln_x_b, w_out, w_up, w_conv_ffn, w_down)
```

```python
import functools
import math
from typing import NamedTuple

import jax
import jax.numpy as jnp
from jax import lax
from jax.experimental import pallas as pl
from jax.experimental.pallas import tpu as pltpu

F32 = jnp.float32
BF16 = jnp.bfloat16

D_MODEL = 2048
A_HEADS = 8
A_DK = 64
A_DV = 128
A_WIDTH = A_HEADS * A_DV
B_HEAD = 64
B_WIDTH = D_MODEL - A_WIDTH
B_HEADS = B_WIDTH // B_HEAD
W_RANK = 64
A_RANK = 64
G_RANK = 128
RW_COLS = 3 * B_WIDTH + W_RANK + A_RANK + G_RANK
D_FF = 5632
FFN_CONV = 3
ATTN_CHUNK = 64
EPS = 1e-6
GN_EPS = 64e-5
LAM_INIT = 0.8 - 0.6 * math.exp(-0.3 * 0)
Q_SCALE = (A_DK ** -0.5) * math.log2(math.e)

LANES = 128
SUBLANES = 8
BF16_ROWS = 16
MXU_DIM = 256
GROUP_HEADS = MXU_DIM // B_HEAD
N_GROUPS = B_WIDTH // MXU_DIM
VMEM_LIMIT_BYTES = 56 * 1024 * 1024
NEG_BIG = -1e30


def _cparams(*sem):
    return pltpu.CompilerParams(dimension_semantics=sem, vmem_limit_bytes=VMEM_LIMIT_BYTES)


def _dot(a, b):
    return jnp.dot(a, b, preferred_element_type=F32)


def _dot_nt(a, b):
    return lax.dot_general(a, b, (((1,), (1,)), ((), ())), preferred_element_type=F32)


def _dot_tn(a, b):
    return lax.dot_general(a, b, (((0,), (0,)), ((), ())), preferred_element_type=F32)


def _idiv(x, n):
    assert n > 0 and n & (n - 1) == 0
    return x >> (n.bit_length() - 1)


def _sigmoid(x):
    return 1.0 / (1.0 + jnp.exp(-x))


def _split_bf16(x, terms):
    out = []
    rem = x
    for _ in range(terms):
        piece = rem.astype(BF16)
        out.append(piece)
        rem = rem - piece.astype(F32)
    return out


def _norm_mod(x, g, sc, sh):
    ms = jnp.mean(x * x, axis=-1, keepdims=True)
    return (x * lax.rsqrt(ms + EPS) * g) * (1.0 + sc) + sh


def _rms(x, g):
    ms = jnp.mean(x * x, axis=-1, keepdims=True)
    return x * lax.rsqrt(ms + EPS) * g


def _mod(ref):
    return ref[0] if len(ref.shape) == 3 else ref[...]


def _lam(q1_ref, k1_ref, q2_ref, k2_ref):
    s1 = jnp.sum(q1_ref[...] * k1_ref[...], axis=-1, keepdims=True)
    s2 = jnp.sum(q2_ref[...] * k2_ref[...], axis=-1, keepdims=True)
    return jnp.exp(s1) - jnp.exp(s2) + LAM_INIT


def _ada_kernel(c_ref, w_ref, b_ref, o_ref):
    c = c_ref[...]
    s = (c * _sigmoid(c)).astype(BF16)
    o_ref[...] = _dot(s, w_ref[...].astype(BF16)) + b_ref[...]


def _ada_call(c_all, w_ada, b_ada):
    rows = c_all.shape[0]
    n = w_ada.shape[1]
    tn = 1024
    return pl.pallas_call(
        _ada_kernel,
        grid=(n // tn,),
        in_specs=[
            pl.BlockSpec((rows, D_MODEL), lambda j: (0, 0)),
            pl.BlockSpec((D_MODEL, tn), lambda j: (0, j)),
            pl.BlockSpec((1, tn), lambda j: (0, j)),
        ],
        out_specs=pl.BlockSpec((rows, tn), lambda j: (0, j)),
        out_shape=jax.ShapeDtypeStruct((rows, n), F32),
        compiler_params=_cparams("arbitrary"),
        name="ada_mod",
    )(c_all, w_ada, b_ada)


def _qkv_kernel(x_ref, g_ref, sc_ref, sh_ref, w_ref, kf_ref, vf_ref, qkvb_ref):
    h = _norm_mod(x_ref[...], g_ref[...], _mod(sc_ref), _mod(sh_ref)).astype(BF16)
    q = _dot(h, w_ref[:, 0:A_WIDTH])
    qkvb_ref[:, 0:A_WIDTH] = (q * Q_SCALE).astype(BF16)
    k = _dot(h, w_ref[:, A_WIDTH:2 * A_WIDTH])
    kf_ref[...] = k
    qkvb_ref[:, A_WIDTH:2 * A_WIDTH] = k.astype(BF16)
    v = _dot(h, w_ref[:, 2 * A_WIDTH:3 * A_WIDTH])
    vf_ref[...] = v
    qkvb_ref[:, 2 * A_WIDTH:3 * A_WIDTH] = v.astype(BF16)


def _rw_kernel(x_ref, g_ref, sc_ref, sh_ref, w_ref, o_ref):
    h = _norm_mod(x_ref[...], g_ref[...], _mod(sc_ref), _mod(sh_ref)).astype(BF16)
    o_ref[...] = _dot(h, w_ref[:, 3 * A_WIDTH:])


def _mod_spec(mod, tm, tiles_per_seq, nidx):
    if mod.ndim == 3:
        if nidx == 2:
            return pl.BlockSpec((1, 1, D_MODEL), lambda i, j: (i // tiles_per_seq, 0, 0))
        return pl.BlockSpec((1, 1, D_MODEL), lambda i: (i // tiles_per_seq, 0, 0))
    if nidx == 2:
        return pl.BlockSpec((tm, D_MODEL), lambda i, j: (i, 0))
    return pl.BlockSpec((tm, D_MODEL), lambda i: (i, 0))


def _resident(shape):
    return pl.BlockSpec(shape, lambda i: (0,) * len(shape), pipeline_mode=pl.Buffered(1))


def _in_proj_call(x, g, sc, sh, w_in, tm, seq_len):
    m = x.shape[0]
    n_in = w_in.shape[1]
    common = [
        pl.BlockSpec((tm, D_MODEL), lambda i: (i, 0)),
        _resident((1, D_MODEL)),
        _mod_spec(sc, tm, seq_len // tm, 1),
        _mod_spec(sh, tm, seq_len // tm, 1),
    ]
    kf, vf, qkvb = pl.pallas_call(
        _qkv_kernel,
        grid=(m // tm,),
        in_specs=common + [_resident((D_MODEL, n_in))],
        out_specs=[
            pl.BlockSpec((tm, A_WIDTH), lambda i: (i, 0)),
            pl.BlockSpec((tm, A_WIDTH), lambda i: (i, 0)),
            pl.BlockSpec((tm, 3 * A_WIDTH), lambda i: (i, 0)),
        ],
        out_shape=[
            jax.ShapeDtypeStruct((m, A_WIDTH), F32),
            jax.ShapeDtypeStruct((m, A_WIDTH), F32),
            jax.ShapeDtypeStruct((m, 3 * A_WIDTH), BF16),
        ],
        compiler_params=_cparams("arbitrary"),
        name="in_proj_qkv",
    )(x, g, sc, sh, w_in)
    rw = pl.pallas_call(
        _rw_kernel,
        grid=(m // tm,),
        in_specs=common + [_resident((D_MODEL, n_in))],
        out_specs=pl.BlockSpec((tm, RW_COLS), lambda i: (i, 0)),
        out_shape=jax.ShapeDtypeStruct((m, RW_COLS), F32),
        compiler_params=_cparams("arbitrary"),
        name="in_proj_rw",
    )(x, g, sc, sh, w_in)
    return kf, vf, qkvb, rw


def _softmax_step(s, vb, m_ref, l_ref, acc_ref, idx):
    m_prev = m_ref[idx]
    m_new = jnp.maximum(m_prev, jnp.max(s, axis=-1, keepdims=True))
    alpha = jnp.exp2(m_prev - m_new)
    p = jnp.exp2(s - m_new)
    l_ref[idx] = alpha * l_ref[idx] + jnp.sum(p, axis=-1, keepdims=True)
    acc_ref[idx] = alpha * acc_ref[idx] + _dot(p.astype(BF16), vb)
    m_ref[idx] = m_new


def _subln(o, g):
    return _rms(o, g) * (1.0 - LAM_INIT)


PATTN_HEADS = 2
PATTN_QG = 512


def _pattn_kernel(q1_ref, k1_ref, q2_ref, k2_ref, q_ref, k_ref, v_ref, g_ref, o_ref,
                  m_scr, l_scr, acc_scr, s_scr, t_scr, *, tq):
    i = pl.program_id(2)
    lane = lax.broadcasted_iota(jnp.int32, (tq, A_DV), 1)
    m_scr[...] = jnp.full(m_scr.shape, NEG_BIG, F32)
    l_scr[...] = jnp.zeros(l_scr.shape, F32)
    acc_scr[...] = jnp.zeros(acc_scr.shape, F32)

    qg = min(PATTN_QG, 2 * tq)
    n_qg = 2 * tq // qg
    q_gs = []
    for hh in range(PATTN_HEADS):
        q = q_ref[0, :, hh * A_DV:(hh + 1) * A_DV]
        zero = jnp.zeros_like(q)
        q_st = jnp.concatenate([jnp.where(lane < A_DK, q, zero), jnp.where(lane >= A_DK, q, zero)], axis=0)
        q_gs.append([q_st[g * qg:(g + 1) * qg] for g in range(n_qg)])
    ones_k = jnp.ones((SUBLANES, tq), BF16)
    chains = [(hh, g) for g in range(n_qg) for hh in range(PATTN_HEADS)]

    def scores(j):
        r0 = pl.multiple_of(j * tq, tq)
        kbs = [k_ref[0, pl.ds(r0, tq), hh * A_DV:(hh + 1) * A_DV] for hh in range(PATTN_HEADS)]
        return [_dot_nt(kbs[hh], q_gs[hh][g]) for hh, g in chains]

    def update(j, s_ref, masks):
        r0 = pl.multiple_of(j * tq, tq)
        vbs = [v_ref[0, pl.ds(r0, tq), hh * A_DV:(hh + 1) * A_DV] for hh in range(PATTN_HEADS)]
        for ci, (hh, g) in enumerate(chains):
            cs = slice(g * qg, (g + 1) * qg)
            s_t = s_ref[ci]
            if masks is not None:
                s_t = jnp.where(masks[g], s_t, NEG_BIG)
            m_prev = m_scr[hh, :, cs]
            m_new = jnp.maximum(m_prev, jnp.max(s_t, axis=0, keepdims=True))
            alpha = jnp.exp2(m_prev - m_new)
            p_b = jnp.exp2(s_t - m_new).astype(BF16)
            l_scr[hh, :, cs] = alpha * l_scr[hh, :, cs] + _dot(ones_k, p_b)[0:1]
            acc_scr[hh, :, cs] = alpha * acc_scr[hh, :, cs] + _dot_tn(vbs[hh], p_b)
            m_scr[hh, :, cs] = m_new

    def put(s_ref, j):
        for ci, s_t in enumerate(scores(j)):
            s_ref[ci] = s_t

    rk = lax.broadcasted_iota(jnp.int32, (tq, qg), 0)
    cq = lax.broadcasted_iota(jnp.int32, (tq, qg), 1)
    diag = [_idiv(rk, ATTN_CHUNK) <= _idiv((cq + g * qg) & (tq - 1), ATTN_CHUNK) for g in range(n_qg)]

    put(s_scr, 0)

    def body(jp, carry):
        j = 2 * jp
        put(t_scr, j + 1)
        update(j, s_scr, None)
        put(s_scr, j + 2)
        update(j + 1, t_scr, None)
        return carry

    lax.fori_loop(0, i // 2, body, 0)

    @pl.when(i % 2 == 0)
    def _():
        update(i, s_scr, diag)

    @pl.when(i % 2 == 1)
    def _():
        put(t_scr, i)
        update(i - 1, s_scr, None)
        update(i, t_scr, diag)

    lam = _lam(q1_ref, k1_ref, q2_ref, k2_ref)
    for hh in range(PATTN_HEADS):
        o_t = (acc_scr[hh, :, 0:tq] / l_scr[hh, :, 0:tq]
               - lam * (acc_scr[hh, :, tq:2 * tq] / l_scr[hh, :, tq:2 * tq]))
        o_ref[0, :, hh * A_DV:(hh + 1) * A_DV] = _subln(o_t.T, g_ref[...]).astype(BF16)


def _lam_specs(nidx):
    if nidx == 3:
        return [pl.BlockSpec((1, A_DK), lambda b, h, i: (0, 0))] * 4
    return [pl.BlockSpec((1, A_DK), lambda b, j: (0, 0))] * 4


def _pattn_call(lams, qkvb, g_subln, tq):
    bsz, t, _ = qkvb.shape
    hw = PATTN_HEADS * A_DV
    ng = A_HEADS // PATTN_HEADS
    return pl.pallas_call(
        functools.partial(_pattn_kernel, tq=tq),
        grid=(bsz, ng, t // tq),
        in_specs=_lam_specs(3) + [
            pl.BlockSpec((1, tq, hw), lambda b, h, i: (b, i, h)),
            pl.BlockSpec((1, t, hw), lambda b, h, i: (b, 0, ng + h)),
            pl.BlockSpec((1, t, hw), lambda b, h, i: (b, 0, 2 * ng + h)),
            pl.BlockSpec((1, A_DV), lambda b, h, i: (0, 0)),
        ],
        out_specs=pl.BlockSpec((1, tq, hw), lambda b, h, i: (b, i, h)),
        out_shape=jax.ShapeDtypeStruct((bsz, t, A_WIDTH), BF16),
        scratch_shapes=[
            pltpu.VMEM((PATTN_HEADS, 1, 2 * tq), F32),
            pltpu.VMEM((PATTN_HEADS, 1, 2 * tq), F32),
            pltpu.VMEM((PATTN_HEADS, A_DV, 2 * tq), F32),
        ] + [pltpu.VMEM((PATTN_HEADS * (2 * tq // min(PATTN_QG, 2 * tq)), tq, min(PATTN_QG, 2 * tq)), F32)] * 2,
        compiler_params=_cparams("arbitrary", "arbitrary", "arbitrary"),
        name="prompt_attention",
    )(*lams, qkvb, qkvb, qkvb, g_subln)


N_VQ = 2 * A_HEADS
SATTN_SUB = 1024


def _sattn_kernel(q1_ref, k1_ref, q2_ref, k2_ref, q_ref, kn_ref, vn_ref, ck_ref, cv_ref, g_ref, o_ref,
                  qb_scr, m_scr, l_scr, acc_scr, *, tnew):
    j = pl.program_id(1)
    nvq = N_VQ * tnew

    @pl.when(j == 0)
    def _():
        q = q_ref[0]
        qt = jnp.concatenate([q] * N_VQ, axis=0)
        row = lax.broadcasted_iota(jnp.int32, (nvq, A_WIDTH), 0)
        col = lax.broadcasted_iota(jnp.int32, (nvq, A_WIDTH), 1)
        qb_scr[...] = jnp.where(_idiv(row, tnew) == _idiv(col, A_DK), qt, jnp.zeros_like(qt))
        m_scr[...] = jnp.full(m_scr.shape, NEG_BIG, F32)
        l_scr[...] = jnp.zeros(l_scr.shape, F32)
        acc_scr[...] = jnp.zeros(acc_scr.shape, F32)

    def step(kb, vb):
        _softmax_step(_dot_nt(qb_scr[...], kb), vb, m_scr, l_scr, acc_scr, 0)

    def cached_rows(ref, p0, n):
        heads = [ref[0, pl.ds(p0 * A_HEADS + h, n, stride=A_HEADS), :] for h in range(A_HEADS)]
        return jnp.concatenate(heads, axis=-1).astype(BF16)

    tk = ck_ref.shape[1] // A_HEADS
    n_sub = max(1, tk // SATTN_SUB)
    sub = tk // n_sub
    kbs = [cached_rows(ck_ref, c * sub, sub) for c in range(n_sub)]
    s_all = [_dot_nt(qb_scr[...], kb) for kb in kbs]
    for c in range(n_sub):
        _softmax_step(s_all[c], cached_rows(cv_ref, c * sub, sub), m_scr, l_scr, acc_scr, 0)

    @pl.when(j == pl.num_programs(1) - 1)
    def _():
        step(kn_ref[0], vn_ref[0])
        lam = _lam(q1_ref, k1_ref, q2_ref, k2_ref)
        g = g_ref[...]
        outs = []
        for h in range(A_HEADS):
            cs = slice(h * A_DV, (h + 1) * A_DV)
            r0 = slice((2 * h) * tnew, (2 * h + 1) * tnew)
            r1 = slice((2 * h + 1) * tnew, (2 * h + 2) * tnew)
            outs.append(acc_scr[0, r0, cs] / l_scr[0, r0, :] - lam * (acc_scr[0, r1, cs] / l_scr[0, r1, :]))
        mss = [jnp.mean(o * o, axis=-1, keepdims=True) for o in outs]
        for h in range(A_HEADS):
            o_ref[0, :, h * A_DV:(h + 1) * A_DV] = (
                outs[h] * lax.rsqrt(mss[h] + EPS) * g * (1.0 - LAM_INIT)).astype(BF16)


def _sattn_call(lams, qkvb, cache_k, cache_v, g_subln, tk):
    bsz, tnew, _ = qkvb.shape
    past = cache_k.shape[1] // A_HEADS
    nvq = N_VQ * tnew
    return pl.pallas_call(
        functools.partial(_sattn_kernel, tnew=tnew),
        grid=(bsz, past // tk),
        in_specs=_lam_specs(2) + [
            pl.BlockSpec((1, tnew, A_WIDTH), lambda b, j: (b, 0, 0)),
            pl.BlockSpec((1, tnew, A_WIDTH), lambda b, j: (b, 0, 1)),
            pl.BlockSpec((1, tnew, A_WIDTH), lambda b, j: (b, 0, 2)),
            pl.BlockSpec((1, tk * A_HEADS, A_DV), lambda b, j: (b, j, 0)),
            pl.BlockSpec((1, tk * A_HEADS, A_DV), lambda b, j: (b, j, 0)),
            pl.BlockSpec((1, A_DV), lambda b, j: (0, 0)),
        ],
        out_specs=pl.BlockSpec((1, tnew, A_WIDTH), lambda b, j: (b, 0, 0)),
        out_shape=jax.ShapeDtypeStruct((bsz, tnew, A_WIDTH), BF16),
        scratch_shapes=[
            pltpu.VMEM((nvq, A_WIDTH), BF16),
            pltpu.VMEM((1, nvq, 1), F32),
            pltpu.VMEM((1, nvq, 1), F32),
            pltpu.VMEM((1, nvq, A_WIDTH), F32),
        ],
        compiler_params=_cparams("arbitrary", "arbitrary"),
        name="sample_attention",
    )(*lams, qkvb, qkvb, qkvb, cache_k, cache_v, g_subln)


def _head_sum(x, ones_bd):
    return _dot(x.astype(BF16), ones_bd)


def _rwkv_kernel(p_ref, ph_ref, st_ref, s0_ref, mu_ref, w0_ref, a0_ref, kk_ref, ka_ref, rk_ref,
                 lnw_ref, lnb_ref, ww2_ref, wa2_ref, wg2_ref, ob_ref, so_ref,
                 r_scr, k_scr, v_scr, na_scr, nb_scr, ld_scr, cum_scr, g_scr, bon_scr, y_scr, s_scr,
                 *, tb, chunk):
    i = pl.program_id(1)
    n_seq = p_ref.shape[0]
    c_len = chunk
    n_chunks = tb // c_len

    row_g = lax.broadcasted_iota(jnp.int32, (MXU_DIM, MXU_DIM), 0)
    col_g = lax.broadcasted_iota(jnp.int32, (MXU_DIM, MXU_DIM), 1)
    bd_mask = _idiv(row_g, B_HEAD) == _idiv(col_g, B_HEAD)
    ones_bd = jnp.where(bd_mask, 1.0, 0.0).astype(BF16)

    @pl.when(i == 0)
    def _():
        for b in range(n_seq):
            for q in range(N_GROUPS):
                narrow = s0_ref[b, q]
                wide = jnp.concatenate([narrow] * GROUP_HEADS, axis=1)
                s_scr[b, q] = jnp.where(bd_mask, wide, 0.0)

    row_t = lax.broadcasted_iota(jnp.int32, (tb, tb), 0)
    col_t = lax.broadcasted_iota(jnp.int32, (tb, tb), 1)
    tri = jnp.where((_idiv(row_t, c_len) == _idiv(col_t, c_len)) & (col_t <= row_t), 1.0, 0.0).astype(BF16)
    w_cat = jnp.concatenate([ww2_ref[...], wa2_ref[...]], axis=0)

    for b in range(n_seq):
        p = p_ref[b]
        prev_row = jnp.where(i == 0, st_ref[b], ph_ref[b, SUBLANES - 1:SUBLANES, :])
        rowi = lax.broadcasted_iota(jnp.int32, p.shape, 0)
        p_prev = jnp.where(rowi == 0, prev_row, pltpu.roll(p, 1, 0))
        xs = p + (p_prev - p) * mu_ref[...]
        r = xs[:, 0:B_WIDTH]
        k = xs[:, B_WIDTH:2 * B_WIDTH]
        v = xs[:, 2 * B_WIDTH:3 * B_WIDTH]
        wa = xs[:, 3 * B_WIDTH:3 * B_WIDTH + W_RANK + A_RANK]
        gd = xs[:, 3 * B_WIDTH + W_RANK + A_RANK:]

        lane_wa = lax.broadcasted_iota(jnp.int32, wa.shape, 1)
        zero_wa = jnp.zeros_like(wa)
        w_lin = _dot(jnp.where(lane_wa < W_RANK, jnp.tanh(wa), zero_wa).astype(BF16), w_cat)
        a_lin = _dot(jnp.where(lane_wa >= W_RANK, wa, zero_wa).astype(BF16), w_cat)
        logd = -math.exp(-0.5) * _sigmoid(w0_ref[...] + w_lin)
        a = _sigmoid(a0_ref[...] + a_lin)
        g_scr[b] = _dot(_sigmoid(gd).astype(BF16), wg2_ref[...])

        k2 = k * (1.0 + (a - 1.0) * ka_ref[...])
        kk = k * kk_ref[...]
        rkk = r * k2 * rk_ref[...]
        for q in range(N_GROUPS):
            ls = slice(q * MXU_DIM, (q + 1) * MXU_DIM)
            kq = kk[:, ls]
            kn = kq * lax.rsqrt(jnp.maximum(_head_sum(kq * kq, ones_bd), 1e-24))
            na_scr[b, :, ls] = -kn
            nb_scr[b, :, ls] = kn * a[:, ls]
            bon_scr[b, :, ls] = _head_sum(rkk[:, ls], ones_bd) * v[:, ls]
        r_scr[b] = r
        k_scr[b] = k2
        v_scr[b] = v
        ld_scr[b] = logd

        cum = None
        for piece in _split_bf16(logd, 2):
            d = _dot(tri, piece)
            cum = d if cum is None else cum + d
        cum_scr[b] = cum

    gc = GROUP_HEADS * c_len
    lane_q = lax.broadcasted_iota(jnp.int32, (1, MXU_DIM), 1)
    head_masks = [_idiv(lane_q, B_HEAD) == j for j in range(GROUP_HEADS)]
    row_s = lax.broadcasted_iota(jnp.int32, (gc, gc), 0)
    col_s = lax.broadcasted_iota(jnp.int32, (gc, gc), 1)
    same_head = _idiv(row_s, c_len) == _idiv(col_s, c_len)
    strict = same_head & (col_s < row_s)
    incl = same_head & (col_s <= row_s)
    eye = jnp.where(col_s == row_s, 1.0, 0.0).astype(F32)
    n_double = int(math.log2(c_len)) - 1
    chains = [(b, q) for q in range(N_GROUPS) for b in range(n_seq)]
    groups = range(len(chains))

    def stack(x):
        z = jnp.zeros_like(x)
        return jnp.concatenate([jnp.where(head_masks[j], x, z) for j in range(GROUP_HEADS)], axis=0)

    def unstack(x):
        out = x[0:c_len]
        for j in range(1, GROUP_HEADS):
            out = out + x[j * c_len:(j + 1) * c_len]
        return out

    def chunk_body(c, carry):
        r0 = pl.multiple_of(c * c_len, c_len)
        rows = pl.ds(r0, c_len)
        a_s, r_s, b_s, k_s, be_s, ke_s, v_s, dec, ar_c = [], [], [], [], [], [], [], [], []
        for b, q in chains:
            ls = slice(q * MXU_DIM, (q + 1) * MXU_DIM)
            cum_q = cum_scr[b, rows, ls]
            k_q = k_scr[b, rows, ls]
            nb_q = nb_scr[b, rows, ls]
            cum_last = cum_q[c_len - 1:c_len, :]
            e_out = jnp.exp(-cum_q)
            e_end = jnp.exp(cum_last - cum_q)
            a_c = (na_scr[b, rows, ls] * jnp.exp(cum_q - ld_scr[b, rows, ls])).astype(BF16)
            r_c = (r_scr[b, rows, ls] * jnp.exp(cum_q)).astype(BF16)
            ar_c.append(jnp.concatenate([a_c, r_c], axis=0))
            a_s.append(stack(a_c))
            r_s.append(stack(r_c))
            b_s.append(stack((nb_q * e_out).astype(BF16)))
            k_s.append(stack((k_q * e_out).astype(BF16)))
            be_s.append(stack((nb_q * e_end).astype(BF16)))
            ke_s.append(stack((k_q * e_end).astype(BF16)))
            v_s.append(stack(v_scr[b, rows, ls].astype(BF16)))
            dec.append(jnp.exp(cum_last))

        if gc % LANES == 0:
            g_all = [_dot_nt(jnp.concatenate([a_s[q], r_s[q]], axis=0),
                             jnp.concatenate([b_s[q], k_s[q]], axis=0)) for q in groups]
            g_ab = [g[0:gc, 0:gc] for g in g_all]
            g_ak = [g[0:gc, gc:2 * gc] for g in g_all]
            g_rb = [g[gc:2 * gc, 0:gc] for g in g_all]
            g_rk = [g[gc:2 * gc, gc:2 * gc] for g in g_all]
        else:
            g_ab = [_dot_nt(a_s[q], b_s[q]) for q in groups]
            g_ak = [_dot_nt(a_s[q], k_s[q]) for q in groups]
            g_rb = [_dot_nt(r_s[q], b_s[q]) for q in groups]
            g_rk = [_dot_nt(r_s[q], k_s[q]) for q in groups]
        aab = [jnp.where(strict, g, 0.0) for g in g_ab]
        aak = [jnp.where(strict, g, 0.0).astype(BF16) for g in g_ak]
        arb = [jnp.where(incl, g, 0.0).astype(BF16) for g in g_rb]
        ark = [jnp.where(incl, g, 0.0).astype(BF16) for g in g_rk]

        t_m = [eye + x for x in aab]
        pw = [x.astype(BF16) for x in aab]
        pw = [_dot(x, x) for x in pw]
        for m in range(n_double):
            pb = [x.astype(BF16) for x in pw]
            if m + 1 < n_double:
                pw = [_dot(x, x) for x in pb]
            t_m = [t_m[q] + _dot(t_m[q].astype(BF16), pb[q]) for q in groups]
        t_b = [x.astype(BF16) for x in t_m]

        av = [_dot(aak[q], v_s[q]) for q in groups]

        s_prev = [s_scr[b, q] for b, q in chains]
        ar = [_dot_nt(ar_c[q], s_prev[q].astype(BF16)) for q in groups]
        u_b = [_dot(t_b[q], (stack(ar[q][0:c_len]) + av[q]).astype(BF16)).astype(BF16) for q in groups]
        uv = [jnp.concatenate([u_b[q], v_s[q]], axis=0) for q in groups]
        if gc % LANES == 0:
            y_st = [_dot(jnp.concatenate([arb[q], ark[q]], axis=1), uv[q]) for q in groups]
        else:
            y_st = [_dot(arb[q], u_b[q]) + _dot(ark[q], v_s[q]) for q in groups]
        s_new = [s_prev[q] * dec[q]
                 + _dot_tn(uv[q], jnp.concatenate([be_s[q], ke_s[q]], axis=0)) for q in groups]
        for ci, (b, q) in enumerate(chains):
            s_scr[b, q] = s_new[ci]
            y_scr[b, rows, q * MXU_DIM:(q + 1) * MXU_DIM] = ar[ci][c_len:2 * c_len] + unstack(y_st[ci])
        return carry

    lax.fori_loop(0, n_chunks, chunk_body, 0)

    @pl.when(i == pl.num_programs(1) - 1)
    def _():
        for b, q in chains:
            wide = s_scr[b, q]
            narrow = wide[:, 0:B_HEAD]
            for j in range(1, GROUP_HEADS):
                narrow = narrow + wide[:, j * B_HEAD:(j + 1) * B_HEAD]
            so_ref[b, q] = narrow

    inv_n = 1.0 / B_HEAD
    for b, q in chains:
        ls = slice(q * MXU_DIM, (q + 1) * MXU_DIM)
        y = y_scr[b, :, ls]
        mu = _head_sum(y, ones_bd) * inv_n
        d = y - mu
        var = _head_sum(d * d, ones_bd) * inv_n
        yn = d * lax.rsqrt(var + GN_EPS) * lnw_ref[:, ls] + lnb_ref[:, ls]
        ob_ref[b, :, ls] = ((yn + bon_scr[b, :, ls]) * g_scr[b, :, ls]).astype(BF16)


def _rwkv_call(rw, shift0, s0, prm, tb, chunk, n_seq):
    bsz, t, _ = rw.shape
    assert bsz % n_seq == 0
    nt = t // tb
    hb = tb // SUBLANES
    vec = lambda n: pl.BlockSpec((1, n), lambda b, i: (0, 0))
    return pl.pallas_call(
        functools.partial(_rwkv_kernel, tb=tb, chunk=chunk),
        grid=(bsz // n_seq, nt),
        in_specs=[
            pl.BlockSpec((n_seq, tb, RW_COLS), lambda b, i: (b, i, 0)),
            pl.BlockSpec((n_seq, SUBLANES, RW_COLS), lambda b, i: (b, jnp.maximum(i * hb - 1, 0), 0)),
            pl.BlockSpec((n_seq, 1, RW_COLS), lambda b, i: (b, 0, 0)),
            pl.BlockSpec((n_seq, N_GROUPS, MXU_DIM, B_HEAD), lambda b, i: (b, 0, 0, 0)),
            vec(RW_COLS), vec(B_WIDTH), vec(B_WIDTH), vec(B_WIDTH), vec(B_WIDTH), vec(B_WIDTH),
            vec(B_WIDTH), vec(B_WIDTH),
            pl.BlockSpec((W_RANK, B_WIDTH), lambda b, i: (0, 0)),
            pl.BlockSpec((A_RANK, B_WIDTH), lambda b, i: (0, 0)),
            pl.BlockSpec((G_RANK, B_WIDTH), lambda b, i: (0, 0)),
        ],
        out_specs=[
            pl.BlockSpec((n_seq, tb, B_WIDTH), lambda b, i: (b, i, 0)),
            pl.BlockSpec((n_seq, N_GROUPS, MXU_DIM, B_HEAD), lambda b, i: (b, 0, 0, 0)),
        ],
        out_shape=[
            jax.ShapeDtypeStruct((bsz, t, B_WIDTH), BF16),
            jax.ShapeDtypeStruct((bsz, N_GROUPS, MXU_DIM, B_HEAD), F32),
        ],
        scratch_shapes=[pltpu.VMEM((n_seq, tb, B_WIDTH), F32)] * 10
        + [pltpu.VMEM((n_seq, N_GROUPS, MXU_DIM, MXU_DIM), F32)],
        compiler_params=_cparams("arbitrary", "arbitrary"),
        name="rwkv7_mix",
    )(rw, rw, shift0, s0, prm["mu_shift"], prm["w0"], prm["a0"], prm["k_k"], prm["k_a"], prm["r_k"],
      prm["ln_x_w"], prm["ln_x_b"], prm["w_w2"], prm["w_a2"], prm["w_g2"])


def _state_to_groups(s):
    return s.reshape(s.shape[0], N_GROUPS, MXU_DIM, B_HEAD)


def _groups_to_state(sg):
    return sg.reshape(sg.shape[0], B_HEADS, B_HEAD, B_HEAD)


def _mixout_kernel(oa_ref, ob_ref, x_ref, w_ref, g_ref, gt_ref, o_ref):
    mix = _dot(oa_ref[...], w_ref[0:A_WIDTH, :]) + _dot(ob_ref[...], w_ref[A_WIDTH:D_MODEL, :])
    o_ref[...] = x_ref[...] + _mod(gt_ref) * _rms(mix, g_ref[...])


def _mixout_call(oa, ob, x, w_out, g, gt, tm, tiles_per_seq):
    m = x.shape[0]
    return pl.pallas_call(
        _mixout_kernel,
        grid=(m // tm,),
        in_specs=[
            pl.BlockSpec((tm, A_WIDTH), lambda i: (i, 0)),
            pl.BlockSpec((tm, B_WIDTH), lambda i: (i, 0)),
            pl.BlockSpec((tm, D_MODEL), lambda i: (i, 0)),
            pl.BlockSpec((D_MODEL, D_MODEL), lambda i: (0, 0)),
            pl.BlockSpec((1, D_MODEL), lambda i: (0, 0)),
            _mod_spec(gt, tm, tiles_per_seq, 1),
        ],
        out_specs=pl.BlockSpec((tm, D_MODEL), lambda i: (i, 0)),
        out_shape=jax.ShapeDtypeStruct((m, D_MODEL), F32),
        compiler_params=_cparams("arbitrary"),
        name="mix_out",
    )(oa, ob, x, w_out, g, gt)


FFN_HALO = BF16_ROWS
FFN_TF = 512


def _gated(zg, zv):
    return (zg * _sigmoid(zg) * zv).astype(BF16)


def _ffn_kernel(x_ref, xh_ref, cpg_ref, cpv_ref, g_ref, sc_ref, sh_ref, gt_ref, wg_ref, wv_ref,
                cg_ref, cv_ref, wd_ref, gp_ref, o_ref, clg_ref, clv_ref, h_scr,
                *, tm, tiles_per_seq):
    mi = pl.program_id(0)
    f = pl.program_id(1)
    first = (mi % tiles_per_seq) == 0

    @pl.when(f == 0)
    def _():
        g, sc, sh = g_ref[...], sc_ref[0], sh_ref[0]
        h_scr[FFN_HALO:, :] = _norm_mod(x_ref[...], g, sc, sh).astype(BF16)
        h_scr[:FFN_HALO, :] = _norm_mod(xh_ref[...], g, sc, sh).astype(BF16)
        o_ref[...] = jnp.zeros(o_ref.shape, F32)

    h = h_scr[...]
    row = lax.broadcasted_iota(jnp.int32, (FFN_HALO, FFN_TF), 0)

    def conv(u, cp_ref, c_ref, cl_ref):
        cp = cp_ref[0]
        head = u[:FFN_HALO]
        head_first = jnp.where(row == FFN_HALO - 2, cp[0:1],
                               jnp.where(row == FFN_HALO - 1, cp[1:2], head))
        ue = jnp.concatenate([jnp.where(first, head_first, head), u[FFN_HALO:]], axis=0)
        cl_ref[0] = ue[tm + FFN_HALO - SUBLANES:]
        c = c_ref[...]
        return (c[0:1] * pltpu.roll(ue, 2, 0)[FFN_HALO:] + c[1:2] * pltpu.roll(ue, 1, 0)[FFN_HALO:]
                + c[2:3] * ue[FFN_HALO:])

    zg = conv(_dot(h, wg_ref[...]), cpg_ref, cg_ref, clg_ref)
    zv = conv(_dot(h, wv_ref[...]), cpv_ref, cv_ref, clv_ref)
    o_ref[...] += _dot(_gated(zg, zv), wd_ref[...])

    @pl.when(f == pl.num_programs(1) - 1)
    def _():
        o_ref[...] = x_ref[...] + gt_ref[0] * _rms(o_ref[...], gp_ref[...])


def _ffn_call(x, conv_prev, g, sc, sh, gt, w_up, w_conv, w_down, gp, tm, tiles_per_seq):
    m = x.shape[0]
    nseq = conv_prev.shape[0]
    nf = D_FF // FFN_TF
    hb = tm // FFN_HALO
    seq = lambda i: i // tiles_per_seq
    modspec = pl.BlockSpec((1, 1, D_MODEL), lambda i, f: (seq(i), 0, 0))
    return pl.pallas_call(
        functools.partial(_ffn_kernel, tm=tm, tiles_per_seq=tiles_per_seq),
        grid=(m // tm, nf),
        in_specs=[
            pl.BlockSpec((tm, D_MODEL), lambda i, f: (i, 0), pipeline_mode=pl.Buffered(1)),
            pl.BlockSpec((FFN_HALO, D_MODEL), lambda i, f: (jnp.maximum(i * hb - 1, 0), 0)),
            pl.BlockSpec((1, FFN_CONV - 1, FFN_TF), lambda i, f: (seq(i), 0, f)),
            pl.BlockSpec((1, FFN_CONV - 1, FFN_TF), lambda i, f: (seq(i), 0, f + nf)),
            pl.BlockSpec((1, D_MODEL), lambda i, f: (0, 0)),
            modspec, modspec, modspec,
            pl.BlockSpec((D_MODEL, FFN_TF), lambda i, f: (0, f)),
            pl.BlockSpec((D_MODEL, FFN_TF), lambda i, f: (0, f + nf)),
            pl.BlockSpec((FFN_CONV, FFN_TF), lambda i, f: (0, f)),
            pl.BlockSpec((FFN_CONV, FFN_TF), lambda i, f: (0, f + nf)),
            pl.BlockSpec((FFN_TF, D_MODEL), lambda i, f: (f, 0)),
            pl.BlockSpec((1, D_MODEL), lambda i, f: (0, 0)),
        ],
        out_specs=[
            pl.BlockSpec((tm, D_MODEL), lambda i, f: (i, 0)),
            pl.BlockSpec((1, SUBLANES, FFN_TF), lambda i, f: (i, 0, f)),
            pl.BlockSpec((1, SUBLANES, FFN_TF), lambda i, f: (i, 0, f)),
        ],
        out_shape=[
            jax.ShapeDtypeStruct((m, D_MODEL), F32),
            jax.ShapeDtypeStruct((m // tm, SUBLANES, D_FF), F32),
            jax.ShapeDtypeStruct((m // tm, SUBLANES, D_FF), F32),
        ],
        scratch_shapes=[pltpu.VMEM((tm + FFN_HALO, D_MODEL), BF16)],
        compiler_params=_cparams("arbitrary", "arbitrary"),
        name="conv_ffn_prompt",
    )(x, x, conv_prev, conv_prev, g, sc, sh, gt, w_up, w_up, w_conv, w_conv, w_down, gp)


def _ffn_s_kernel(x_ref, cpg_ref, cpv_ref, g_ref, sc_ref, sh_ref, gt_ref, wg_ref, wv_ref,
                  cg_ref, cv_ref, wd_ref, gp_ref, o_ref, ug_ref, uv_ref, h_scr, acc_scr, *, seq_len):
    f = pl.program_id(1)

    @pl.when(f == 0)
    def _():
        h_scr[...] = _norm_mod(x_ref[...], g_ref[...], sc_ref[...], sh_ref[...]).astype(BF16)
        acc_scr[...] = jnp.zeros(acc_scr.shape, F32)

    h = h_scr[...]
    tm = h.shape[0]
    pos = lax.broadcasted_iota(jnp.int32, (seq_len, FFN_TF), 0)

    def conv(u, cp_ref, c_ref, u_ref):
        u_ref[...] = u
        cp = cp_ref[...]
        u1 = pltpu.roll(u, 1, 0)
        u2 = pltpu.roll(u, 2, 0)
        p1, p2 = [], []
        for s in range(tm // seq_len):
            rs = slice(s * seq_len, (s + 1) * seq_len)
            c0, c1 = cp[2 * s:2 * s + 1], cp[2 * s + 1:2 * s + 2]
            p1.append(jnp.where(pos == 0, c1, u1[rs]))
            p2.append(jnp.where(pos == 0, c0, jnp.where(pos == 1, c1, u2[rs])))
        c = c_ref[...]
        return c[0:1] * jnp.concatenate(p2, axis=0) + c[1:2] * jnp.concatenate(p1, axis=0) + c[2:3] * u

    zg = conv(_dot(h, wg_ref[...]), cpg_ref, cg_ref, ug_ref)
    zv = conv(_dot(h, wv_ref[...]), cpv_ref, cv_ref, uv_ref)
    acc_scr[...] += _dot(_gated(zg, zv), wd_ref[...])

    @pl.when(f == pl.num_programs(1) - 1)
    def _():
        o_ref[...] = x_ref[...] + gt_ref[...] * _rms(acc_scr[...], gp_ref[...])


def _ffn_s_call(x, conv_prev, g, sc, sh, gt, w_up, w_conv, w_down, gp, seq_len, tm):
    m = x.shape[0]
    assert tm % seq_len == 0 and seq_len & (seq_len - 1) == 0
    nf = D_FF // FFN_TF
    n_cp = (tm // seq_len) * (FFN_CONV - 1)
    full = pl.BlockSpec((tm, D_MODEL), lambda i, f: (i, 0))
    full_in = pl.BlockSpec((tm, D_MODEL), lambda i, f: (i, 0), pipeline_mode=pl.Buffered(1))
    lo = pl.BlockSpec((tm, FFN_TF), lambda i, f: (i, f))
    return pl.pallas_call(
        functools.partial(_ffn_s_kernel, seq_len=seq_len),
        grid=(m // tm, nf),
        in_specs=[
            full_in,
            pl.BlockSpec((n_cp, FFN_TF), lambda i, f: (i, f)),
            pl.BlockSpec((n_cp, FFN_TF), lambda i, f: (i, f + nf)),
            pl.BlockSpec((1, D_MODEL), lambda i, f: (0, 0)),
            full_in, full_in, full_in,
            pl.BlockSpec((D_MODEL, FFN_TF), lambda i, f: (0, f)),
            pl.BlockSpec((D_MODEL, FFN_TF), lambda i, f: (0, f + nf)),
            pl.BlockSpec((FFN_CONV, FFN_TF), lambda i, f: (0, f)),
            pl.BlockSpec((FFN_CONV, FFN_TF), lambda i, f: (0, f + nf)),
            pl.BlockSpec((FFN_TF, D_MODEL), lambda i, f: (f, 0)),
            pl.BlockSpec((1, D_MODEL), lambda i, f: (0, 0)),
        ],
        out_specs=[full, lo, lo],
        out_shape=[
            jax.ShapeDtypeStruct((m, D_MODEL), F32),
            jax.ShapeDtypeStruct((m, D_FF), F32),
            jax.ShapeDtypeStruct((m, D_FF), F32),
        ],
        scratch_shapes=[pltpu.VMEM((tm, D_MODEL), BF16), pltpu.VMEM((tm, D_MODEL), F32)],
        compiler_params=_cparams("arbitrary", "arbitrary"),
        name="conv_ffn_sample",
    )(x, conv_prev, conv_prev, g, sc, sh, gt, w_up, w_up, w_conv, w_conv, w_down, gp)


def _seq_tile(t, cap):
    tm = min(t, cap)
    assert t % tm == 0
    return tm


class _Tiles(NamedTuple):
    rows: int
    attn: int
    wkv_rows: int
    wkv_chunk: int
    wkv_seqs: int
    ffn_rows: int


def _prompt_tiles(bp, t):
    return _Tiles(rows=_seq_tile(t, 512), attn=_seq_tile(t, 512), wkv_rows=_seq_tile(t, 256),
                  wkv_chunk=min(t, 32), wkv_seqs=2 if bp % 2 == 0 else 1, ffn_rows=_seq_tile(t, 1024))


def _sample_tiles(bs, ts, past):
    rows = bs * ts
    return _Tiles(rows=rows, attn=_seq_tile(past, 2048), wkv_rows=ts, wkv_chunk=ts,
                  wkv_seqs=4 if bs % 4 == 0 else 1, ffn_rows=_seq_tile(rows, 512))


@jax.jit
def _forward(x_prompt, x_sample, c_prompt, c_sample, cache_k, cache_v, state_wkv, state_shift,
             state_ffn_conv, w_ada, b_ada, g_pre_mix, g_post_mix, g_pre_ffn, g_post_ffn, w_in,
             lam_q1, lam_k1, lam_q2, lam_k2, g_subln, mu_shift, w0, w_w2, a0, w_a2, w_g2, k_k, k_a,
             r_k, ln_x_w, ln_x_b, w_out, w_up, w_conv_ffn, w_down):
    bp, t, _ = x_prompt.shape
    bs, ts, _ = x_sample.shape
    past = cache_k.shape[2]
    assert w_ada.shape[0] == 1, "single-layer problem"

    n_c = bp + bs
    n_c_pad = -(-n_c // SUBLANES) * SUBLANES
    c_all = jnp.concatenate([c_prompt, c_sample, jnp.zeros((n_c_pad - n_c, D_MODEL), F32)], axis=0)
    mod = _ada_call(c_all, w_ada[0], b_ada)
    sh_m, sc_m, gt_m, sh_f, sc_f, gt_f = [mod[:, i * D_MODEL:(i + 1) * D_MODEL] for i in range(6)]
    p_mod = lambda a: a[:bp].reshape(bp, 1, D_MODEL)
    s_mod = lambda a: jnp.repeat(a[bp:n_c], ts, axis=0)

    w_in_b = w_in[0].astype(BF16)
    w_out_b = w_out[0].astype(BF16)
    w_up_b = w_up[0].astype(BF16)
    w_down_b = w_down[0].astype(BF16)
    lams = (lam_q1, lam_k1, lam_q2, lam_k2)
    prm = {
        "mu_shift": mu_shift, "w0": w0, "a0": a0, "k_k": k_k, "k_a": k_a,
        "r_k": r_k.reshape(1, B_WIDTH), "ln_x_w": ln_x_w, "ln_x_b": ln_x_b,
        "w_w2": w_w2[0].astype(BF16), "w_a2": w_a2[0].astype(BF16), "w_g2": w_g2[0].astype(BF16),
    }

    pt = _prompt_tiles(bp, t)
    tps_f = t // pt.ffn_rows
    xp = x_prompt.reshape(bp * t, D_MODEL)
    kf, vf, qkvb, rw = _in_proj_call(xp, g_pre_mix, p_mod(sc_m), p_mod(sh_m), w_in_b, pt.rows, t)
    oa = _pattn_call(lams, qkvb.reshape(bp, t, 3 * A_WIDTH), g_subln, pt.attn)
    ob, s_last = _rwkv_call(rw.reshape(bp, t, RW_COLS), jnp.zeros((bp, 1, RW_COLS), F32),
                            jnp.zeros((bp, N_GROUPS, MXU_DIM, B_HEAD), state_wkv.dtype), prm,
                            pt.wkv_rows, pt.wkv_chunk, pt.wkv_seqs)
    x1 = _mixout_call(oa.reshape(bp * t, A_WIDTH), ob.reshape(bp * t, B_WIDTH), xp, w_out_b,
                      g_post_mix, p_mod(gt_m), pt.rows, t // pt.rows)
    yp, clg, clv = _ffn_call(x1, jnp.zeros((bp, FFN_CONV - 1, 2 * D_FF), F32), g_pre_ffn, p_mod(sc_f),
                             p_mod(sh_f), p_mod(gt_f), w_up_b, w_conv_ffn[0], w_down_b, g_post_ffn,
                             pt.ffn_rows, tps_f)
    rw3 = rw.reshape(bp, t, RW_COLS)
    out_prompt = (
        yp.reshape(bp, t, D_MODEL),
        kf.reshape(1, bp, t, A_HEADS, 2 * A_DK),
        vf.reshape(1, bp, t, A_HEADS, A_DV),
        _groups_to_state(s_last)[None],
        rw3[:, -1][None],
        jnp.concatenate([clg[tps_f - 1::tps_f, -(FFN_CONV - 1):], clv[tps_f - 1::tps_f, -(FFN_CONV - 1):]],
                        axis=-1)[None],
    )

    ms = bs * ts
    st = _sample_tiles(bs, ts, past)
    xs = x_sample.reshape(ms, D_MODEL)
    kf_s, vf_s, qkvb_s, rw_s = _in_proj_call(xs, g_pre_mix, s_mod(sc_m), s_mod(sh_m), w_in_b, st.rows, ms)
    oa_s = _sattn_call(lams, qkvb_s.reshape(bs, ts, 3 * A_WIDTH),
                       cache_k[0].reshape(bs, past * A_HEADS, A_DV),
                       cache_v[0].reshape(bs, past * A_HEADS, A_DV), g_subln, st.attn)
    ob_s, s_last_s = _rwkv_call(rw_s.reshape(bs, ts, RW_COLS), state_shift[0][:, None, :],
                                _state_to_groups(state_wkv[0]), prm, st.wkv_rows, st.wkv_chunk, st.wkv_seqs)
    x1_s = _mixout_call(oa_s.reshape(ms, A_WIDTH), ob_s.reshape(ms, B_WIDTH), xs, w_out_b,
                        g_post_mix, s_mod(gt_m), st.rows, 1)
    cprev = state_ffn_conv[0].reshape(bs * (FFN_CONV - 1), 2 * D_FF)
    ys, ug, uv = _ffn_s_call(x1_s, cprev, g_pre_ffn, s_mod(sc_f), s_mod(sh_f), s_mod(gt_f),
                             w_up_b, w_conv_ffn[0], w_down_b, g_post_ffn, ts, st.ffn_rows)
    u_tail = jnp.concatenate([ug.reshape(bs, ts, D_FF)[:, -(FFN_CONV - 1):],
                              uv.reshape(bs, ts, D_FF)[:, -(FFN_CONV - 1):]], axis=-1)
    out_sample = (
        ys.reshape(bs, ts, D_MODEL),
        kf_s.reshape(1, bs, ts, A_HEADS, 2 * A_DK),
        vf_s.reshape(1, bs, ts, A_HEADS, A_DV),
        _groups_to_state(s_last_s)[None],
        rw_s.reshape(bs, ts, RW_COLS)[:, -1][None],
        u_tail[None],
    )
    return (out_prompt[0], out_sample[0]) + out_prompt[1:] + out_sample[1:]


def kernel(x_prompt, x_sample, c_prompt, c_sample, cache_k, cache_v, state_wkv, state_shift, state_ffn_conv, w_ada, b_ada, g_pre_mix, g_post_mix, g_pre_ffn, g_post_ffn, w_in, lam_q1, lam_k1, lam_q2, lam_k2, g_subln, mu_shift, w0, w_w2, a0, w_a2, w_g2, k_k, k_a, r_k, ln_x_w, ln_x_b, w_out, w_up, w_conv_ffn, w_down):
    return _forward(x_prompt, x_sample, c_prompt, c_sample, cache_k, cache_v, state_wkv, state_shift,
                    state_ffn_conv, w_ada, b_ada, g_pre_mix, g_post_mix, g_pre_ffn, g_post_ffn, w_in,
                    lam_q1, lam_k1, lam_q2, lam_k2, g_subln, mu_shift, w0, w_w2, a0, w_a2, w_g2, k_k,
                    k_a, r_k, ln_x_w, ln_x_b, w_out, w_up, w_conv_ffn, w_down)
```

```python
import functools
import math
from typing import NamedTuple

import jax
import jax.numpy as jnp
from jax import lax
from jax.experimental import pallas as pl
from jax.experimental.pallas import tpu as pltpu

F32 = jnp.float32
BF16 = jnp.bfloat16

D_MODEL = 2048
A_HEADS = 8
A_DK = 64
A_DV = 128
A_WIDTH = A_HEADS * A_DV
B_HEAD = 64
B_WIDTH = D_MODEL - A_WIDTH
B_HEADS = B_WIDTH // B_HEAD
W_RANK = 64
A_RANK = 64
G_RANK = 128
RW_COLS = 3 * B_WIDTH + W_RANK + A_RANK + G_RANK
D_FF = 5632
FFN_CONV = 3
ATTN_CHUNK = 64
EPS = 1e-6
GN_EPS = 64e-5
LAM_INIT = 0.8 - 0.6 * math.exp(-0.3 * 0)
Q_SCALE = (A_DK ** -0.5) * math.log2(math.e)

LANES = 128
SUBLANES = 8
BF16_ROWS = 16
MXU_DIM = 256
GROUP_HEADS = MXU_DIM // B_HEAD
N_GROUPS = B_WIDTH // MXU_DIM
VMEM_LIMIT_BYTES = 56 * 1024 * 1024
NEG_BIG = -1e30


def _cparams(*sem):
    return pltpu.CompilerParams(dimension_semantics=sem, vmem_limit_bytes=VMEM_LIMIT_BYTES)


def _dot(a, b):
    return jnp.dot(a, b, preferred_element_type=F32)


def _dot_nt(a, b):
    return lax.dot_general(a, b, (((1,), (1,)), ((), ())), preferred_element_type=F32)


def _dot_tn(a, b):
    return lax.dot_general(a, b, (((0,), (0,)), ((), ())), preferred_element_type=F32)


def _idiv(x, n):
    assert n > 0 and n & (n - 1) == 0
    return x >> (n.bit_length() - 1)


def _sigmoid(x):
    return 1.0 / (1.0 + jnp.exp(-x))


def _split_bf16(x, terms):
    out = []
    rem = x
    for _ in range(terms):
        piece = rem.astype(BF16)
        out.append(piece)
        rem = rem - piece.astype(F32)
    return out


def _norm_mod(x, g, sc, sh):
    ms = jnp.mean(x * x, axis=-1, keepdims=True)
    return (x * lax.rsqrt(ms + EPS) * g) * (1.0 + sc) + sh


def _rms(x, g):
    ms = jnp.mean(x * x, axis=-1, keepdims=True)
    return x * lax.rsqrt(ms + EPS) * g


def _mod(ref):
    return ref[0] if len(ref.shape) == 3 else ref[...]


def _lam(q1_ref, k1_ref, q2_ref, k2_ref):
    s1 = jnp.sum(q1_ref[...] * k1_ref[...], axis=-1, keepdims=True)
    s2 = jnp.sum(q2_ref[...] * k2_ref[...], axis=-1, keepdims=True)
    return jnp.exp(s1) - jnp.exp(s2) + LAM_INIT


def _ada_kernel(c_ref, w_ref, b_ref, o_ref):
    c = c_ref[...]
    s = (c * _sigmoid(c)).astype(BF16)
    o_ref[...] = _dot(s, w_ref[...].astype(BF16)) + b_ref[...]


def _ada_call(c_all, w_ada, b_ada):
    rows = c_all.shape[0]
    n = w_ada.shape[1]
    tn = 1024
    return pl.pallas_call(
        _ada_kernel,
        grid=(n // tn,),
        in_specs=[
            pl.BlockSpec((rows, D_MODEL), lambda j: (0, 0)),
            pl.BlockSpec((D_MODEL, tn), lambda j: (0, j)),
            pl.BlockSpec((1, tn), lambda j: (0, j)),
        ],
        out_specs=pl.BlockSpec((rows, tn), lambda j: (0, j)),
        out_shape=jax.ShapeDtypeStruct((rows, n), F32),
        compiler_params=_cparams("arbitrary"),
        name="ada_mod",
    )(c_all, w_ada, b_ada)


def _qkv_kernel(x_ref, g_ref, sc_ref, sh_ref, w_ref, kf_ref, vf_ref, qkvb_ref):
    h = _norm_mod(x_ref[...], g_ref[...], _mod(sc_ref), _mod(sh_ref)).astype(BF16)
    q = _dot(h, w_ref[:, 0:A_WIDTH])
    qkvb_ref[:, 0:A_WIDTH] = (q * Q_SCALE).astype(BF16)
    k = _dot(h, w_ref[:, A_WIDTH:2 * A_WIDTH])
    kf_ref[...] = k
    qkvb_ref[:, A_WIDTH:2 * A_WIDTH] = k.astype(BF16)
    v = _dot(h, w_ref[:, 2 * A_WIDTH:3 * A_WIDTH])
    vf_ref[...] = v
    qkvb_ref[:, 2 * A_WIDTH:3 * A_WIDTH] = v.astype(BF16)


def _rw_kernel(x_ref, g_ref, sc_ref, sh_ref, w_ref, o_ref):
    h = _norm_mod(x_ref[...], g_ref[...], _mod(sc_ref), _mod(sh_ref)).astype(BF16)
    o_ref[...] = _dot(h, w_ref[:, 3 * A_WIDTH:])


def _mod_spec(mod, tm, tiles_per_seq, nidx):
    if mod.ndim == 3:
        if nidx == 2:
            return pl.BlockSpec((1, 1, D_MODEL), lambda i, j: (i // tiles_per_seq, 0, 0))
        return pl.BlockSpec((1, 1, D_MODEL), lambda i: (i // tiles_per_seq, 0, 0))
    if nidx == 2:
        return pl.BlockSpec((tm, D_MODEL), lambda i, j: (i, 0))
    return pl.BlockSpec((tm, D_MODEL), lambda i: (i, 0))


def _resident(shape):
    return pl.BlockSpec(shape, lambda i: (0,) * len(shape), pipeline_mode=pl.Buffered(1))


def _in_proj_call(x, g, sc, sh, w_in, tm, seq_len):
    m = x.shape[0]
    n_in = w_in.shape[1]
    common = [
        pl.BlockSpec((tm, D_MODEL), lambda i: (i, 0)),
        _resident((1, D_MODEL)),
        _mod_spec(sc, tm, seq_len // tm, 1),
        _mod_spec(sh, tm, seq_len // tm, 1),
    ]
    kf, vf, qkvb = pl.pallas_call(
        _qkv_kernel,
        grid=(m // tm,),
        in_specs=common + [_resident((D_MODEL, n_in))],
        out_specs=[
            pl.BlockSpec((tm, A_WIDTH), lambda i: (i, 0)),
            pl.BlockSpec((tm, A_WIDTH), lambda i: (i, 0)),
            pl.BlockSpec((tm, 3 * A_WIDTH), lambda i: (i, 0)),
        ],
        out_shape=[
            jax.ShapeDtypeStruct((m, A_WIDTH), F32),
            jax.ShapeDtypeStruct((m, A_WIDTH), F32),
            jax.ShapeDtypeStruct((m, 3 * A_WIDTH), BF16),
        ],
        compiler_params=_cparams("arbitrary"),
        name="in_proj_qkv",
    )(x, g, sc, sh, w_in)
    rw = pl.pallas_call(
        _rw_kernel,
        grid=(m // tm,),
        in_specs=common + [_resident((D_MODEL, n_in))],
        out_specs=pl.BlockSpec((tm, RW_COLS), lambda i: (i, 0)),
        out_shape=jax.ShapeDtypeStruct((m, RW_COLS), F32),
        compiler_params=_cparams("arbitrary"),
        name="in_proj_rw",
    )(x, g, sc, sh, w_in)
    return kf, vf, qkvb, rw


def _softmax_step(s, vb, m_ref, l_ref, acc_ref, idx):
    m_prev = m_ref[idx]
    m_new = jnp.maximum(m_prev, jnp.max(s, axis=-1, keepdims=True))
    alpha = jnp.exp2(m_prev - m_new)
    p = jnp.exp2(s - m_new)
    l_ref[idx] = alpha * l_ref[idx] + jnp.sum(p, axis=-1, keepdims=True)
    acc_ref[idx] = alpha * acc_ref[idx] + _dot(p.astype(BF16), vb)
    m_ref[idx] = m_new


def _subln(o, g):
    return _rms(o, g) * (1.0 - LAM_INIT)


PATTN_HEADS = 2
PATTN_QG = 512


def _pattn_kernel(q1_ref, k1_ref, q2_ref, k2_ref, q_ref, k_ref, v_ref, g_ref, o_ref,
                  m_scr, l_scr, acc_scr, s_scr, t_scr, *, tq):
    i = pl.program_id(2)
    lane = lax.broadcasted_iota(jnp.int32, (tq, A_DV), 1)
    m_scr[...] = jnp.full(m_scr.shape, NEG_BIG, F32)
    l_scr[...] = jnp.zeros(l_scr.shape, F32)
    acc_scr[...] = jnp.zeros(acc_scr.shape, F32)

    qg = min(PATTN_QG, 2 * tq)
    n_qg = 2 * tq // qg
    q_gs = []
    for hh in range(PATTN_HEADS):
        q = q_ref[0, :, hh * A_DV:(hh + 1) * A_DV]
        zero = jnp.zeros_like(q)
        q_st = jnp.concatenate([jnp.where(lane < A_DK, q, zero), jnp.where(lane >= A_DK, q, zero)], axis=0)
        q_gs.append([q_st[g * qg:(g + 1) * qg] for g in range(n_qg)])
    ones_k = jnp.ones((SUBLANES, tq), BF16)
    chains = [(hh, g) for g in range(n_qg) for hh in range(PATTN_HEADS)]

    def scores(j):
        r0 = pl.multiple_of(j * tq, tq)
        kbs = [k_ref[0, pl.ds(r0, tq), hh * A_DV:(hh + 1) * A_DV] for hh in range(PATTN_HEADS)]
        return [_dot_nt(kbs[hh], q_gs[hh][g]) for hh, g in chains]

    def update(j, s_ref, masks):
        r0 = pl.multiple_of(j * tq, tq)
        vbs = [v_ref[0, pl.ds(r0, tq), hh * A_DV:(hh + 1) * A_DV] for hh in range(PATTN_HEADS)]
        for ci, (hh, g) in enumerate(chains):
            cs = slice(g * qg, (g + 1) * qg)
            s_t = s_ref[ci]
            if masks is not None:
                s_t = jnp.where(masks[g], s_t, NEG_BIG)
            m_prev = m_scr[hh, :, cs]
            m_new = jnp.maximum(m_prev, jnp.max(s_t, axis=0, keepdims=True))
            alpha = jnp.exp2(m_prev - m_new)
            p_b = jnp.exp2(s_t - m_new).astype(BF16)
            l_scr[hh, :, cs] = alpha * l_scr[hh, :, cs] + _dot(ones_k, p_b)[0:1]
            acc_scr[hh, :, cs] = alpha * acc_scr[hh, :, cs] + _dot_tn(vbs[hh], p_b)
            m_scr[hh, :, cs] = m_new

    def put(s_ref, j):
        for ci, s_t in enumerate(scores(j)):
            s_ref[ci] = s_t

    rk = lax.broadcasted_iota(jnp.int32, (tq, qg), 0)
    cq = lax.broadcasted_iota(jnp.int32, (tq, qg), 1)
    diag = [_idiv(rk, ATTN_CHUNK) <= _idiv((cq + g * qg) & (tq - 1), ATTN_CHUNK) for g in range(n_qg)]

    put(s_scr, 0)

    def body(jp, carry):
        j = 2 * jp
        put(t_scr, j + 1)
        update(j, s_scr, None)
        put(s_scr, j + 2)
        update(j + 1, t_scr, None)
        return carry

    lax.fori_loop(0, i // 2, body, 0)

    @pl.when(i % 2 == 0)
    def _():
        update(i, s_scr, diag)

    @pl.when(i % 2 == 1)
    def _():
        put(t_scr, i)
        update(i - 1, s_scr, None)
        update(i, t_scr, diag)

    lam = _lam(q1_ref, k1_ref, q2_ref, k2_ref)
    for hh in range(PATTN_HEADS):
        o_t = (acc_scr[hh, :, 0:tq] / l_scr[hh, :, 0:tq]
               - lam * (acc_scr[hh, :, tq:2 * tq] / l_scr[hh, :, tq:2 * tq]))
        o_ref[0, :, hh * A_DV:(hh + 1) * A_DV] = _subln(o_t.T, g_ref[...]).astype(BF16)


def _lam_specs(nidx):
    if nidx == 3:
        return [pl.BlockSpec((1, A_DK), lambda b, h, i: (0, 0))] * 4
    return [pl.BlockSpec((1, A_DK), lambda b, j: (0, 0))] * 4


def _pattn_call(lams, qkvb, g_subln, tq):
    bsz, t, _ = qkvb.shape
    hw = PATTN_HEADS * A_DV
    ng = A_HEADS // PATTN_HEADS
    return pl.pallas_call(
        functools.partial(_pattn_kernel, tq=tq),
        grid=(bsz, ng, t // tq),
        in_specs=_lam_specs(3) + [
            pl.BlockSpec((1, tq, hw), lambda b, h, i: (b, i, h)),
            pl.BlockSpec((1, t, hw), lambda b, h, i: (b, 0, ng + h)),
            pl.BlockSpec((1, t, hw), lambda b, h, i: (b, 0, 2 * ng + h)),
            pl.BlockSpec((1, A_DV), lambda b, h, i: (0, 0)),
        ],
        out_specs=pl.BlockSpec((1, tq, hw), lambda b, h, i: (b, i, h)),
        out_shape=jax.ShapeDtypeStruct((bsz, t, A_WIDTH), BF16),
        scratch_shapes=[
            pltpu.VMEM((PATTN_HEADS, 1, 2 * tq), F32),
            pltpu.VMEM((PATTN_HEADS, 1, 2 * tq), F32),
            pltpu.VMEM((PATTN_HEADS, A_DV, 2 * tq), F32),
        ] + [pltpu.VMEM((PATTN_HEADS * (2 * tq // min(PATTN_QG, 2 * tq)), tq, min(PATTN_QG, 2 * tq)), F32)] * 2,
        compiler_params=_cparams("arbitrary", "arbitrary", "arbitrary"),
        name="prompt_attention",
    )(*lams, qkvb, qkvb, qkvb, g_subln)


N_VQ = 2 * A_HEADS
SATTN_SUB = 1024


def _sattn_kernel(q1_ref, k1_ref, q2_ref, k2_ref, q_ref, kn_ref, vn_ref, ck_ref, cv_ref, g_ref, o_ref,
                  qb_scr, m_scr, l_scr, acc_scr, *, tnew):
    j = pl.program_id(1)
    nvq = N_VQ * tnew

    @pl.when(j == 0)
    def _():
        q = q_ref[0]
        qt = jnp.concatenate([q] * N_VQ, axis=0)
        row = lax.broadcasted_iota(jnp.int32, (nvq, A_WIDTH), 0)
        col = lax.broadcasted_iota(jnp.int32, (nvq, A_WIDTH), 1)
        qb_scr[...] = jnp.where(_idiv(row, tnew) == _idiv(col, A_DK), qt, jnp.zeros_like(qt))
        m_scr[...] = jnp.full(m_scr.shape, NEG_BIG, F32)
        l_scr[...] = jnp.zeros(l_scr.shape, F32)
        acc_scr[...] = jnp.zeros(acc_scr.shape, F32)

    def step(kb, vb):
        _softmax_step(_dot_nt(qb_scr[...], kb), vb, m_scr, l_scr, acc_scr, 0)

    def cached_rows(ref, p0, n):
        heads = [ref[0, pl.ds(p0 * A_HEADS + h, n, stride=A_HEADS), :] for h in range(A_HEADS)]
        return jnp.concatenate(heads, axis=-1).astype(BF16)

    tk = ck_ref.shape[1] // A_HEADS
    n_sub = max(1, tk // SATTN_SUB)
    sub = tk // n_sub
    kbs = [cached_rows(ck_ref, c * sub, sub) for c in range(n_sub)]
    s_all = [_dot_nt(qb_scr[...], kb) for kb in kbs]
    for c in range(n_sub):
        _softmax_step(s_all[c], cached_rows(cv_ref, c * sub, sub), m_scr, l_scr, acc_scr, 0)

    @pl.when(j == pl.num_programs(1) - 1)
    def _():
        step(kn_ref[0], vn_ref[0])
        lam = _lam(q1_ref, k1_ref, q2_ref, k2_ref)
        g = g_ref[...]
        outs = []
        for h in range(A_HEADS):
            cs = slice(h * A_DV, (h + 1) * A_DV)
            r0 = slice((2 * h) * tnew, (2 * h + 1) * tnew)
            r1 = slice((2 * h + 1) * tnew, (2 * h + 2) * tnew)
            outs.append(acc_scr[0, r0, cs] / l_scr[0, r0, :] - lam * (acc_scr[0, r1, cs] / l_scr[0, r1, :]))
        mss = [jnp.mean(o * o, axis=-1, keepdims=True) for o in outs]
        for h in range(A_HEADS):
            o_ref[0, :, h * A_DV:(h + 1) * A_DV] = (
                outs[h] * lax.rsqrt(mss[h] + EPS) * g * (1.0 - LAM_INIT)).astype(BF16)


def _sattn_call(lams, qkvb, cache_k, cache_v, g_subln, tk):
    bsz, tnew, _ = qkvb.shape
    past = cache_k.shape[1] // A_HEADS
    nvq = N_VQ * tnew
    return pl.pallas_call(
        functools.partial(_sattn_kernel, tnew=tnew),
        grid=(bsz, past // tk),
        in_specs=_lam_specs(2) + [
            pl.BlockSpec((1, tnew, A_WIDTH), lambda b, j: (b, 0, 0)),
            pl.BlockSpec((1, tnew, A_WIDTH), lambda b, j: (b, 0, 1)),
            pl.BlockSpec((1, tnew, A_WIDTH), lambda b, j: (b, 0, 2)),
            pl.BlockSpec((1, tk * A_HEADS, A_DV), lambda b, j: (b, j, 0)),
            pl.BlockSpec((1, tk * A_HEADS, A_DV), lambda b, j: (b, j, 0)),
            pl.BlockSpec((1, A_DV), lambda b, j: (0, 0)),
        ],
        out_specs=pl.BlockSpec((1, tnew, A_WIDTH), lambda b, j: (b, 0, 0)),
        out_shape=jax.ShapeDtypeStruct((bsz, tnew, A_WIDTH), BF16),
        scratch_shapes=[
            pltpu.VMEM((nvq, A_WIDTH), BF16),
            pltpu.VMEM((1, nvq, 1), F32),
            pltpu.VMEM((1, nvq, 1), F32),
            pltpu.VMEM((1, nvq, A_WIDTH), F32),
        ],
        compiler_params=_cparams("arbitrary", "arbitrary"),
        name="sample_attention",
    )(*lams, qkvb, qkvb, qkvb, cache_k, cache_v, g_subln)


def _head_sum(x, ones_bd):
    return _dot(x.astype(BF16), ones_bd)


def _rwkv_kernel(p_ref, ph_ref, st_ref, s0_ref, mu_ref, w0_ref, a0_ref, kk_ref, ka_ref, rk_ref,
                 lnw_ref, lnb_ref, ww2_ref, wa2_ref, wg2_ref, ob_ref, so_ref,
                 r_scr, k_scr, v_scr, na_scr, nb_scr, ld_scr, cum_scr, g_scr, bon_scr, y_scr, s_scr,
                 *, tb, chunk):
    i = pl.program_id(1)
    n_seq = p_ref.shape[0]
    c_len = chunk
    n_chunks = tb // c_len

    row_g = lax.broadcasted_iota(jnp.int32, (MXU_DIM, MXU_DIM), 0)
    col_g = lax.broadcasted_iota(jnp.int32, (MXU_DIM, MXU_DIM), 1)
    bd_mask = _idiv(row_g, B_HEAD) == _idiv(col_g, B_HEAD)
    ones_bd = jnp.where(bd_mask, 1.0, 0.0).astype(BF16)

    @pl.when(i == 0)
    def _():
        for b in range(n_seq):
            for q in range(N_GROUPS):
                narrow = s0_ref[b, q]
                wide = jnp.concatenate([narrow] * GROUP_HEADS, axis=1)
                s_scr[b, q] = jnp.where(bd_mask, wide, 0.0)

    row_t = lax.broadcasted_iota(jnp.int32, (tb, tb), 0)
    col_t = lax.broadcasted_iota(jnp.int32, (tb, tb), 1)
    tri = jnp.where((_idiv(row_t, c_len) == _idiv(col_t, c_len)) & (col_t <= row_t), 1.0, 0.0).astype(BF16)
    w_cat = jnp.concatenate([ww2_ref[...], wa2_ref[...]], axis=0)

    for b in range(n_seq):
        p = p_ref[b]
        prev_row = jnp.where(i == 0, st_ref[b], ph_ref[b, SUBLANES - 1:SUBLANES, :])
        rowi = lax.broadcasted_iota(jnp.int32, p.shape, 0)
        p_prev = jnp.where(rowi == 0, prev_row, pltpu.roll(p, 1, 0))
        xs = p + (p_prev - p) * mu_ref[...]
        r = xs[:, 0:B_WIDTH]
        k = xs[:, B_WIDTH:2 * B_WIDTH]
        v = xs[:, 2 * B_WIDTH:3 * B_WIDTH]
        wa = xs[:, 3 * B_WIDTH:3 * B_WIDTH + W_RANK + A_RANK]
        gd = xs[:, 3 * B_WIDTH + W_RANK + A_RANK:]

        lane_wa = lax.broadcasted_iota(jnp.int32, wa.shape, 1)
        zero_wa = jnp.zeros_like(wa)
        w_lin = _dot(jnp.where(lane_wa < W_RANK, jnp.tanh(wa), zero_wa).astype(BF16), w_cat)
        a_lin = _dot(jnp.where(lane_wa >= W_RANK, wa, zero_wa).astype(BF16), w_cat)
        logd = -math.exp(-0.5) * _sigmoid(w0_ref[...] + w_lin)
        a = _sigmoid(a0_ref[...] + a_lin)
        g_scr[b] = _dot(_sigmoid(gd).astype(BF16), wg2_ref[...])

        k2 = k * (1.0 + (a - 1.0) * ka_ref[...])
        kk = k * kk_ref[...]
        rkk = r * k2 * rk_ref[...]
        for q in range(N_GROUPS):
            ls = slice(q * MXU_DIM, (q + 1) * MXU_DIM)
            kq = kk[:, ls]
            kn = kq * lax.rsqrt(jnp.maximum(_head_sum(kq * kq, ones_bd), 1e-24))
            na_scr[b, :, ls] = -kn
            nb_scr[b, :, ls] = kn * a[:, ls]
            bon_scr[b, :, ls] = _head_sum(rkk[:, ls], ones_bd) * v[:, ls]
        r_scr[b] = r
        k_scr[b] = k2
        v_scr[b] = v
        ld_scr[b] = logd

        cum = None
        for piece in _split_bf16(logd, 2):
            d = _dot(tri, piece)
            cum = d if cum is None else cum + d
        cum_scr[b] = cum

    gc = GROUP_HEADS * c_len
    lane_q = lax.broadcasted_iota(jnp.int32, (1, MXU_DIM), 1)
    head_masks = [_idiv(lane_q, B_HEAD) == j for j in range(GROUP_HEADS)]
    row_s = lax.broadcasted_iota(jnp.int32, (gc, gc), 0)
    col_s = lax.broadcasted_iota(jnp.int32, (gc, gc), 1)
    same_head = _idiv(row_s, c_len) == _idiv(col_s, c_len)
    strict = same_head & (col_s < row_s)
    incl = same_head & (col_s <= row_s)
    eye = jnp.where(col_s == row_s, 1.0, 0.0).astype(F32)
    n_double = int(math.log2(c_len)) - 1
    chains = [(b, q) for q in range(N_GROUPS) for b in range(n_seq)]
    n_un = 2 if n_chunks % 2 == 0 else 1
    work = [(u, b, q) for (b, q) in chains for u in range(n_un)]
    groups = range(len(work))

    def stack(x):
        z = jnp.zeros_like(x)
        return jnp.concatenate([jnp.where(head_masks[j], x, z) for j in range(GROUP_HEADS)], axis=0)

    def unstack(x):
        out = x[0:c_len]
        for j in range(1, GROUP_HEADS):
            out = out + x[j * c_len:(j + 1) * c_len]
        return out

    def chunk_body(cp, carry):
        rows_u = [pl.ds(pl.multiple_of((cp * n_un + u) * c_len, c_len), c_len) for u in range(n_un)]
        a_s, r_s, b_s, k_s, be_s, ke_s, v_s, dec, ar_c = [], [], [], [], [], [], [], [], []
        for u, b, q in work:
            rows = rows_u[u]
            ls = slice(q * MXU_DIM, (q + 1) * MXU_DIM)
            cum_q = cum_scr[b, rows, ls]
            k_q = k_scr[b, rows, ls]
            nb_q = nb_scr[b, rows, ls]
            cum_last = cum_q[c_len - 1:c_len, :]
            e_out = jnp.exp(-cum_q)
            e_end = jnp.exp(cum_last - cum_q)
            a_c = (na_scr[b, rows, ls] * jnp.exp(cum_q - ld_scr[b, rows, ls])).astype(BF16)
            r_c = (r_scr[b, rows, ls] * jnp.exp(cum_q)).astype(BF16)
            ar_c.append(jnp.concatenate([a_c, r_c], axis=0))
            a_s.append(stack(a_c))
            r_s.append(stack(r_c))
            b_s.append(stack((nb_q * e_out).astype(BF16)))
            k_s.append(stack((k_q * e_out).astype(BF16)))
            be_s.append(stack((nb_q * e_end).astype(BF16)))
            ke_s.append(stack((k_q * e_end).astype(BF16)))
            v_s.append(stack(v_scr[b, rows, ls].astype(BF16)))
            dec.append(jnp.exp(cum_last))

        if gc % LANES == 0:
            g_all = [_dot_nt(jnp.concatenate([a_s[q], r_s[q]], axis=0),
                             jnp.concatenate([b_s[q], k_s[q]], axis=0)) for q in groups]
            g_ab = [g[0:gc, 0:gc] for g in g_all]
            g_ak = [g[0:gc, gc:2 * gc] for g in g_all]
            g_rb = [g[gc:2 * gc, 0:gc] for g in g_all]
            g_rk = [g[gc:2 * gc, gc:2 * gc] for g in g_all]
        else:
            g_ab = [_dot_nt(a_s[q], b_s[q]) for q in groups]
            g_ak = [_dot_nt(a_s[q], k_s[q]) for q in groups]
            g_rb = [_dot_nt(r_s[q], b_s[q]) for q in groups]
            g_rk = [_dot_nt(r_s[q], k_s[q]) for q in groups]
        aab = [jnp.where(strict, g, 0.0) for g in g_ab]
        aak = [jnp.where(strict, g, 0.0).astype(BF16) for g in g_ak]
        arb = [jnp.where(incl, g, 0.0).astype(BF16) for g in g_rb]
        ark = [jnp.where(incl, g, 0.0).astype(BF16) for g in g_rk]

        t_m = [eye + x for x in aab]
        pw = [x.astype(BF16) for x in aab]
        pw = [_dot(x, x) for x in pw]
        for m in range(n_double):
            pb = [x.astype(BF16) for x in pw]
            if m + 1 < n_double:
                pw = [_dot(x, x) for x in pb]
            t_m = [t_m[q] + _dot(t_m[q].astype(BF16), pb[q]) for q in groups]
        t_b = [x.astype(BF16) for x in t_m]

        av = [_dot(aak[q], v_s[q]) for q in groups]

        state = {bq: s_scr[bq[0], bq[1]] for bq in chains}
        for u in range(n_un):
            idx = [wi for wi in groups if work[wi][0] == u]
            s_prev = [state[work[wi][1:]] for wi in idx]
            ar = [_dot_nt(ar_c[wi], s.astype(BF16)) for wi, s in zip(idx, s_prev)]
            u_b = [_dot(t_b[wi], (stack(a[0:c_len]) + av[wi]).astype(BF16)).astype(BF16)
                   for wi, a in zip(idx, ar)]
            uv = [jnp.concatenate([ub, v_s[wi]], axis=0) for wi, ub in zip(idx, u_b)]
            if gc % LANES == 0:
                y_st = [_dot(jnp.concatenate([arb[wi], ark[wi]], axis=1), x) for wi, x in zip(idx, uv)]
            else:
                y_st = [_dot(arb[wi], ub) + _dot(ark[wi], v_s[wi]) for wi, ub in zip(idx, u_b)]
            s_new = [s * dec[wi] + _dot_tn(x, jnp.concatenate([be_s[wi], ke_s[wi]], axis=0))
                     for wi, s, x in zip(idx, s_prev, uv)]
            for k, wi in enumerate(idx):
                _, b, q = work[wi]
                state[(b, q)] = s_new[k]
                y_scr[b, rows_u[u], q * MXU_DIM:(q + 1) * MXU_DIM] = ar[k][c_len:2 * c_len] + unstack(y_st[k])
        for (b, q), s in state.items():
            s_scr[b, q] = s
        return carry

    lax.fori_loop(0, n_chunks // n_un, chunk_body, 0)

    @pl.when(i == pl.num_programs(1) - 1)
    def _():
        for b, q in chains:
            wide = s_scr[b, q]
            narrow = wide[:, 0:B_HEAD]
            for j in range(1, GROUP_HEADS):
                narrow = narrow + wide[:, j * B_HEAD:(j + 1) * B_HEAD]
            so_ref[b, q] = narrow

    inv_n = 1.0 / B_HEAD
    for b, q in chains:
        ls = slice(q * MXU_DIM, (q + 1) * MXU_DIM)
        y = y_scr[b, :, ls]
        mu = _head_sum(y, ones_bd) * inv_n
        d = y - mu
        var = _head_sum(d * d, ones_bd) * inv_n
        yn = d * lax.rsqrt(var + GN_EPS) * lnw_ref[:, ls] + lnb_ref[:, ls]
        ob_ref[b, :, ls] = ((yn + bon_scr[b, :, ls]) * g_scr[b, :, ls]).astype(BF16)


def _rwkv_call(rw, shift0, s0, prm, tb, chunk, n_seq):
    bsz, t, _ = rw.shape
    assert bsz % n_seq == 0
    nt = t // tb
    hb = tb // SUBLANES
    vec = lambda n: pl.BlockSpec((1, n), lambda b, i: (0, 0))
    return pl.pallas_call(
        functools.partial(_rwkv_kernel, tb=tb, chunk=chunk),
        grid=(bsz // n_seq, nt),
        in_specs=[
            pl.BlockSpec((n_seq, tb, RW_COLS), lambda b, i: (b, i, 0)),
            pl.BlockSpec((n_seq, SUBLANES, RW_COLS), lambda b, i: (b, jnp.maximum(i * hb - 1, 0), 0)),
            pl.BlockSpec((n_seq, 1, RW_COLS), lambda b, i: (b, 0, 0)),
            pl.BlockSpec((n_seq, N_GROUPS, MXU_DIM, B_HEAD), lambda b, i: (b, 0, 0, 0)),
            vec(RW_COLS), vec(B_WIDTH), vec(B_WIDTH), vec(B_WIDTH), vec(B_WIDTH), vec(B_WIDTH),
            vec(B_WIDTH), vec(B_WIDTH),
            pl.BlockSpec((W_RANK, B_WIDTH), lambda b, i: (0, 0)),
            pl.BlockSpec((A_RANK, B_WIDTH), lambda b, i: (0, 0)),
            pl.BlockSpec((G_RANK, B_WIDTH), lambda b, i: (0, 0)),
        ],
        out_specs=[
            pl.BlockSpec((n_seq, tb, B_WIDTH), lambda b, i: (b, i, 0)),
            pl.BlockSpec((n_seq, N_GROUPS, MXU_DIM, B_HEAD), lambda b, i: (b, 0, 0, 0)),
        ],
        out_shape=[
            jax.ShapeDtypeStruct((bsz, t, B_WIDTH), BF16),
            jax.ShapeDtypeStruct((bsz, N_GROUPS, MXU_DIM, B_HEAD), F32),
        ],
        scratch_shapes=[pltpu.VMEM((n_seq, tb, B_WIDTH), F32)] * 10
        + [pltpu.VMEM((n_seq, N_GROUPS, MXU_DIM, MXU_DIM), F32)],
        compiler_params=_cparams("arbitrary", "arbitrary"),
        name="rwkv7_mix",
    )(rw, rw, shift0, s0, prm["mu_shift"], prm["w0"], prm["a0"], prm["k_k"], prm["k_a"], prm["r_k"],
      prm["ln_x_w"], prm["ln_x_b"], prm["w_w2"], prm["w_a2"], prm["w_g2"])


def _state_to_groups(s):
    return s.reshape(s.shape[0], N_GROUPS, MXU_DIM, B_HEAD)


def _groups_to_state(sg):
    return sg.reshape(sg.shape[0], B_HEADS, B_HEAD, B_HEAD)


def _mixout_kernel(oa_ref, ob_ref, x_ref, w_ref, g_ref, gt_ref, o_ref):
    mix = _dot(oa_ref[...], w_ref[0:A_WIDTH, :]) + _dot(ob_ref[...], w_ref[A_WIDTH:D_MODEL, :])
    o_ref[...] = x_ref[...] + _mod(gt_ref) * _rms(mix, g_ref[...])


def _mixout_call(oa, ob, x, w_out, g, gt, tm, tiles_per_seq):
    m = x.shape[0]
    return pl.pallas_call(
        _mixout_kernel,
        grid=(m // tm,),
        in_specs=[
            pl.BlockSpec((tm, A_WIDTH), lambda i: (i, 0)),
            pl.BlockSpec((tm, B_WIDTH), lambda i: (i, 0)),
            pl.BlockSpec((tm, D_MODEL), lambda i: (i, 0)),
            pl.BlockSpec((D_MODEL, D_MODEL), lambda i: (0, 0)),
            pl.BlockSpec((1, D_MODEL), lambda i: (0, 0)),
            _mod_spec(gt, tm, tiles_per_seq, 1),
        ],
        out_specs=pl.BlockSpec((tm, D_MODEL), lambda i: (i, 0)),
        out_shape=jax.ShapeDtypeStruct((m, D_MODEL), F32),
        compiler_params=_cparams("arbitrary"),
        name="mix_out",
    )(oa, ob, x, w_out, g, gt)


FFN_HALO = BF16_ROWS
FFN_TF = 512


def _gated(zg, zv):
    return (zg * _sigmoid(zg) * zv).astype(BF16)


def _ffn_kernel(x_ref, xh_ref, cpg_ref, cpv_ref, g_ref, sc_ref, sh_ref, gt_ref, wg_ref, wv_ref,
                cg_ref, cv_ref, wd_ref, gp_ref, o_ref, clg_ref, clv_ref, h_scr,
                *, tm, tiles_per_seq):
    mi = pl.program_id(0)
    f = pl.program_id(1)
    first = (mi % tiles_per_seq) == 0

    @pl.when(f == 0)
    def _():
        g, sc, sh = g_ref[...], sc_ref[0], sh_ref[0]
        h_scr[FFN_HALO:, :] = _norm_mod(x_ref[...], g, sc, sh).astype(BF16)
        h_scr[:FFN_HALO, :] = _norm_mod(xh_ref[...], g, sc, sh).astype(BF16)
        o_ref[...] = jnp.zeros(o_ref.shape, F32)

    h = h_scr[...]
    row = lax.broadcasted_iota(jnp.int32, (FFN_HALO, FFN_TF), 0)

    def conv(u, cp_ref, c_ref, cl_ref):
        cp = cp_ref[0]
        head = u[:FFN_HALO]
        head_first = jnp.where(row == FFN_HALO - 2, cp[0:1],
                               jnp.where(row == FFN_HALO - 1, cp[1:2], head))
        ue = jnp.concatenate([jnp.where(first, head_first, head), u[FFN_HALO:]], axis=0)
        cl_ref[0] = ue[tm + FFN_HALO - SUBLANES:]
        c = c_ref[...]
        return (c[0:1] * pltpu.roll(ue, 2, 0)[FFN_HALO:] + c[1:2] * pltpu.roll(ue, 1, 0)[FFN_HALO:]
                + c[2:3] * ue[FFN_HALO:])

    zg = conv(_dot(h, wg_ref[...]), cpg_ref, cg_ref, clg_ref)
    zv = conv(_dot(h, wv_ref[...]), cpv_ref, cv_ref, clv_ref)
    o_ref[...] += _dot(_gated(zg, zv), wd_ref[...])

    @pl.when(f == pl.num_programs(1) - 1)
    def _():
        o_ref[...] = x_ref[...] + gt_ref[0] * _rms(o_ref[...], gp_ref[...])


def _ffn_call(x, conv_prev, g, sc, sh, gt, w_up, w_conv, w_down, gp, tm, tiles_per_seq):
    m = x.shape[0]
    nseq = conv_prev.shape[0]
    nf = D_FF // FFN_TF
    hb = tm // FFN_HALO
    seq = lambda i: i // tiles_per_seq
    modspec = pl.BlockSpec((1, 1, D_MODEL), lambda i, f: (seq(i), 0, 0))
    return pl.pallas_call(
        functools.partial(_ffn_kernel, tm=tm, tiles_per_seq=tiles_per_seq),
        grid=(m // tm, nf),
        in_specs=[
            pl.BlockSpec((tm, D_MODEL), lambda i, f: (i, 0), pipeline_mode=pl.Buffered(1)),
            pl.BlockSpec((FFN_HALO, D_MODEL), lambda i, f: (jnp.maximum(i * hb - 1, 0), 0)),
            pl.BlockSpec((1, FFN_CONV - 1, FFN_TF), lambda i, f: (seq(i), 0, f)),
            pl.BlockSpec((1, FFN_CONV - 1, FFN_TF), lambda i, f: (seq(i), 0, f + nf)),
            pl.BlockSpec((1, D_MODEL), lambda i, f: (0, 0)),
            modspec, modspec, modspec,
            pl.BlockSpec((D_MODEL, FFN_TF), lambda i, f: (0, f)),
            pl.BlockSpec((D_MODEL, FFN_TF), lambda i, f: (0, f + nf)),
            pl.BlockSpec((FFN_CONV, FFN_TF), lambda i, f: (0, f)),
            pl.BlockSpec((FFN_CONV, FFN_TF), lambda i, f: (0, f + nf)),
            pl.BlockSpec((FFN_TF, D_MODEL), lambda i, f: (f, 0)),
            pl.BlockSpec((1, D_MODEL), lambda i, f: (0, 0)),
        ],
        out_specs=[
            pl.BlockSpec((tm, D_MODEL), lambda i, f: (i, 0)),
            pl.BlockSpec((1, SUBLANES, FFN_TF), lambda i, f: (i, 0, f)),
            pl.BlockSpec((1, SUBLANES, FFN_TF), lambda i, f: (i, 0, f)),
        ],
        out_shape=[
            jax.ShapeDtypeStruct((m, D_MODEL), F32),
            jax.ShapeDtypeStruct((m // tm, SUBLANES, D_FF), F32),
            jax.ShapeDtypeStruct((m // tm, SUBLANES, D_FF), F32),
        ],
        scratch_shapes=[pltpu.VMEM((tm + FFN_HALO, D_MODEL), BF16)],
        compiler_params=_cparams("arbitrary", "arbitrary"),
        name="conv_ffn_prompt",
    )(x, x, conv_prev, conv_prev, g, sc, sh, gt, w_up, w_up, w_conv, w_conv, w_down, gp)


def _ffn_s_kernel(x_ref, cpg_ref, cpv_ref, g_ref, sc_ref, sh_ref, gt_ref, wg_ref, wv_ref,
                  cg_ref, cv_ref, wd_ref, gp_ref, o_ref, ug_ref, uv_ref, h_scr, acc_scr, *, seq_len):
    f = pl.program_id(1)

    @pl.when(f == 0)
    def _():
        h_scr[...] = _norm_mod(x_ref[...], g_ref[...], sc_ref[...], sh_ref[...]).astype(BF16)
        acc_scr[...] = jnp.zeros(acc_scr.shape, F32)

    h = h_scr[...]
    tm = h.shape[0]
    pos = lax.broadcasted_iota(jnp.int32, (seq_len, FFN_TF), 0)

    def conv(u, cp_ref, c_ref, u_ref):
        u_ref[...] = u
        cp = cp_ref[...]
        u1 = pltpu.roll(u, 1, 0)
        u2 = pltpu.roll(u, 2, 0)
        p1, p2 = [], []
        for s in range(tm // seq_len):
            rs = slice(s * seq_len, (s + 1) * seq_len)
            c0, c1 = cp[2 * s:2 * s + 1], cp[2 * s + 1:2 * s + 2]
            p1.append(jnp.where(pos == 0, c1, u1[rs]))
            p2.append(jnp.where(pos == 0, c0, jnp.where(pos == 1, c1, u2[rs])))
        c = c_ref[...]
        return c[0:1] * jnp.concatenate(p2, axis=0) + c[1:2] * jnp.concatenate(p1, axis=0) + c[2:3] * u

    zg = conv(_dot(h, wg_ref[...]), cpg_ref, cg_ref, ug_ref)
    zv = conv(_dot(h, wv_ref[...]), cpv_ref, cv_ref, uv_ref)
    acc_scr[...] += _dot(_gated(zg, zv), wd_ref[...])

    @pl.when(f == pl.num_programs(1) - 1)
    def _():
        o_ref[...] = x_ref[...] + gt_ref[...] * _rms(acc_scr[...], gp_ref[...])


def _ffn_s_call(x, conv_prev, g, sc, sh, gt, w_up, w_conv, w_down, gp, seq_len, tm):
    m = x.shape[0]
    assert tm % seq_len == 0 and seq_len & (seq_len - 1) == 0
    nf = D_FF // FFN_TF
    n_cp = (tm // seq_len) * (FFN_CONV - 1)
    full = pl.BlockSpec((tm, D_MODEL), lambda i, f: (i, 0))
    full_in = pl.BlockSpec((tm, D_MODEL), lambda i, f: (i, 0), pipeline_mode=pl.Buffered(1))
    lo = pl.BlockSpec((tm, FFN_TF), lambda i, f: (i, f))
    return pl.pallas_call(
        functools.partial(_ffn_s_kernel, seq_len=seq_len),
        grid=(m // tm, nf),
        in_specs=[
            full_in,
            pl.BlockSpec((n_cp, FFN_TF), lambda i, f: (i, f)),
            pl.BlockSpec((n_cp, FFN_TF), lambda i, f: (i, f + nf)),
            pl.BlockSpec((1, D_MODEL), lambda i, f: (0, 0)),
            full_in, full_in, full_in,
            pl.BlockSpec((D_MODEL, FFN_TF), lambda i, f: (0, f)),
            pl.BlockSpec((D_MODEL, FFN_TF), lambda i, f: (0, f + nf)),
            pl.BlockSpec((FFN_CONV, FFN_TF), lambda i, f: (0, f)),
            pl.BlockSpec((FFN_CONV, FFN_TF), lambda i, f: (0, f + nf)),
            pl.BlockSpec((FFN_TF, D_MODEL), lambda i, f: (f, 0)),
            pl.BlockSpec((1, D_MODEL), lambda i, f: (0, 0)),
        ],
        out_specs=[full, lo, lo],
        out_shape=[
            jax.ShapeDtypeStruct((m, D_MODEL), F32),
            jax.ShapeDtypeStruct((m, D_FF), F32),
            jax.ShapeDtypeStruct((m, D_FF), F32),
        ],
        scratch_shapes=[pltpu.VMEM((tm, D_MODEL), BF16), pltpu.VMEM((tm, D_MODEL), F32)],
        compiler_params=_cparams("arbitrary", "arbitrary"),
        name="conv_ffn_sample",
    )(x, conv_prev, conv_prev, g, sc, sh, gt, w_up, w_up, w_conv, w_conv, w_down, gp)


def _seq_tile(t, cap):
    tm = min(t, cap)
    assert t % tm == 0
    return tm


class _Tiles(NamedTuple):
    rows: int
    attn: int
    wkv_rows: int
    wkv_chunk: int
    wkv_seqs: int
    ffn_rows: int


def _prompt_tiles(bp, t):
    return _Tiles(rows=_seq_tile(t, 512), attn=_seq_tile(t, 512), wkv_rows=_seq_tile(t, 256),
                  wkv_chunk=min(t, 32), wkv_seqs=2 if bp % 2 == 0 else 1, ffn_rows=_seq_tile(t, 1024))


def _sample_tiles(bs, ts, past):
    rows = bs * ts
    return _Tiles(rows=rows, attn=_seq_tile(past, 2048), wkv_rows=ts, wkv_chunk=ts,
                  wkv_seqs=4 if bs % 4 == 0 else 1, ffn_rows=_seq_tile(rows, 512))


@jax.jit
def _forward(x_prompt, x_sample, c_prompt, c_sample, cache_k, cache_v, state_wkv, state_shift,
             state_ffn_conv, w_ada, b_ada, g_pre_mix, g_post_mix, g_pre_ffn, g_post_ffn, w_in,
             lam_q1, lam_k1, lam_q2, lam_k2, g_subln, mu_shift, w0, w_w2, a0, w_a2, w_g2, k_k, k_a,
             r_k, ln_x_w, ln_x_b, w_out, w_up, w_conv_ffn, w_down):
    bp, t, _ = x_prompt.shape
    bs, ts, _ = x_sample.shape
    past = cache_k.shape[2]
    assert w_ada.shape[0] == 1, "single-layer problem"

    n_c = bp + bs
    n_c_pad = -(-n_c // SUBLANES) * SUBLANES
    c_all = jnp.concatenate([c_prompt, c_sample, jnp.zeros((n_c_pad - n_c, D_MODEL), F32)], axis=0)
    mod = _ada_call(c_all, w_ada[0], b_ada)
    sh_m, sc_m, gt_m, sh_f, sc_f, gt_f = [mod[:, i * D_MODEL:(i + 1) * D_MODEL] for i in range(6)]
    p_mod = lambda a: a[:bp].reshape(bp, 1, D_MODEL)
    s_mod = lambda a: jnp.repeat(a[bp:n_c], ts, axis=0)

    w_in_b = w_in[0].astype(BF16)
    w_out_b = w_out[0].astype(BF16)
    w_up_b = w_up[0].astype(BF16)
    w_down_b = w_down[0].astype(BF16)
    lams = (lam_q1, lam_k1, lam_q2, lam_k2)
    prm = {
        "mu_shift": mu_shift, "w0": w0, "a0": a0, "k_k": k_k, "k_a": k_a,
        "r_k": r_k.reshape(1, B_WIDTH), "ln_x_w": ln_x_w, "ln_x_b": ln_x_b,
        "w_w2": w_w2[0].astype(BF16), "w_a2": w_a2[0].astype(BF16), "w_g2": w_g2[0].astype(BF16),
    }

    pt = _prompt_tiles(bp, t)
    tps_f = t // pt.ffn_rows
    xp = x_prompt.reshape(bp * t, D_MODEL)
    kf, vf, qkvb, rw = _in_proj_call(xp, g_pre_mix, p_mod(sc_m), p_mod(sh_m), w_in_b, pt.rows, t)
    oa = _pattn_call(lams, qkvb.reshape(bp, t, 3 * A_WIDTH), g_subln, pt.attn)
    ob, s_last = _rwkv_call(rw.reshape(bp, t, RW_COLS), jnp.zeros((bp, 1, RW_COLS), F32),
                            jnp.zeros((bp, N_GROUPS, MXU_DIM, B_HEAD), state_wkv.dtype), prm,
                            pt.wkv_rows, pt.wkv_chunk, pt.wkv_seqs)
    x1 = _mixout_call(oa.reshape(bp * t, A_WIDTH), ob.reshape(bp * t, B_WIDTH), xp, w_out_b,
                      g_post_mix, p_mod(gt_m), pt.rows, t // pt.rows)
    yp, clg, clv = _ffn_call(x1, jnp.zeros((bp, FFN_CONV - 1, 2 * D_FF), F32), g_pre_ffn, p_mod(sc_f),
                             p_mod(sh_f), p_mod(gt_f), w_up_b, w_conv_ffn[0], w_down_b, g_post_ffn,
                             pt.ffn_rows, tps_f)
    rw3 = rw.reshape(bp, t, RW_COLS)
    out_prompt = (
        yp.reshape(bp, t, D_MODEL),
        kf.reshape(1, bp, t, A_HEADS, 2 * A_DK),
        vf.reshape(1, bp, t, A_HEADS, A_DV),
        _groups_to_state(s_last)[None],
        rw3[:, -1][None],
        jnp.concatenate([clg[tps_f - 1::tps_f, -(FFN_CONV - 1):], clv[tps_f - 1::tps_f, -(FFN_CONV - 1):]],
                        axis=-1)[None],
    )

    ms = bs * ts
    st = _sample_tiles(bs, ts, past)
    xs = x_sample.reshape(ms, D_MODEL)
    kf_s, vf_s, qkvb_s, rw_s = _in_proj_call(xs, g_pre_mix, s_mod(sc_m), s_mod(sh_m), w_in_b, st.rows, ms)
    oa_s = _sattn_call(lams, qkvb_s.reshape(bs, ts, 3 * A_WIDTH),
                       cache_k[0].reshape(bs, past * A_HEADS, A_DV),
                       cache_v[0].reshape(bs, past * A_HEADS, A_DV), g_subln, st.attn)
    ob_s, s_last_s = _rwkv_call(rw_s.reshape(bs, ts, RW_COLS), state_shift[0][:, None, :],
                                _state_to_groups(state_wkv[0]), prm, st.wkv_rows, st.wkv_chunk, st.wkv_seqs)
    x1_s = _mixout_call(oa_s.reshape(ms, A_WIDTH), ob_s.reshape(ms, B_WIDTH), xs, w_out_b,
                        g_post_mix, s_mod(gt_m), st.rows, 1)
    cprev = state_ffn_conv[0].reshape(bs * (FFN_CONV - 1), 2 * D_FF)
    ys, ug, uv = _ffn_s_call(x1_s, cprev, g_pre_ffn, s_mod(sc_f), s_mod(sh_f), s_mod(gt_f),
                             w_up_b, w_conv_ffn[0], w_down_b, g_post_ffn, ts, st.ffn_rows)
    u_tail = jnp.concatenate([ug.reshape(bs, ts, D_FF)[:, -(FFN_CONV - 1):],
                              uv.reshape(bs, ts, D_FF)[:, -(FFN_CONV - 1):]], axis=-1)
    out_sample = (
        ys.reshape(bs, ts, D_MODEL),
        kf_s.reshape(1, bs, ts, A_HEADS, 2 * A_DK),
        vf_s.reshape(1, bs, ts, A_HEADS, A_DV),
        _groups_to_state(s_last_s)[None],
        rw_s.reshape(bs, ts, RW_COLS)[:, -1][None],
        u_tail[None],
    )
    return (out_prompt[0], out_sample[0]) + out_prompt[1:] + out_sample[1:]


def kernel(x_prompt, x_sample, c_prompt, c_sample, cache_k, cache_v, state_wkv, state_shift, state_ffn_conv, w_ada, b_ada, g_pre_mix, g_post_mix, g_pre_ffn, g_post_ffn, w_in, lam_q1, lam_k1, lam_q2, lam_k2, g_subln, mu_shift, w0, w_w2, a0, w_a2, w_g2, k_k, k_a, r_k, ln_x_w, ln_x_b, w_out, w_up, w_conv_ffn, w_down):
    return _forward(x_prompt, x_sample, c_prompt, c_sample, cache_k, cache_v, state_wkv, state_shift,
                    state_ffn_conv, w_ada, b_ada, g_pre_mix, g_post_mix, g_pre_ffn, g_post_ffn, w_in,
                    lam_q1, lam_k1, lam_q2, lam_k2, g_subln, mu_shift, w0, w_w2, a0, w_a2, w_g2, k_k,
                    k_a, r_k, ln_x_w, ln_x_b, w_out, w_up, w_conv_ffn, w_down)
```

```python
import functools
import math
from typing import NamedTuple

import jax
import jax.numpy as jnp
from jax import lax
from jax.experimental import pallas as pl
from jax.experimental.pallas import tpu as pltpu

F32 = jnp.float32
BF16 = jnp.bfloat16

D_MODEL = 2048
A_HEADS = 8
A_DK = 64
A_DV = 128
A_WIDTH = A_HEADS * A_DV
B_HEAD = 64
B_WIDTH = D_MODEL - A_WIDTH
B_HEADS = B_WIDTH // B_HEAD
W_RANK = 64
A_RANK = 64
G_RANK = 128
RW_COLS = 3 * B_WIDTH + W_RANK + A_RANK + G_RANK
D_FF = 5632
FFN_CONV = 3
ATTN_CHUNK = 64
EPS = 1e-6
GN_EPS = 64e-5
LAM_INIT = 0.8 - 0.6 * math.exp(-0.3 * 0)
Q_SCALE = (A_DK ** -0.5) * math.log2(math.e)

LANES = 128
SUBLANES = 8
BF16_ROWS = 16
MXU_DIM = 256
GROUP_HEADS = MXU_DIM // B_HEAD
N_GROUPS = B_WIDTH // MXU_DIM
VMEM_LIMIT_BYTES = 56 * 1024 * 1024
NEG_BIG = -1e30


def _cparams(*sem):
    return pltpu.CompilerParams(dimension_semantics=sem, vmem_limit_bytes=VMEM_LIMIT_BYTES)


def _dot(a, b):
    return jnp.dot(a, b, preferred_element_type=F32)


def _dot_nt(a, b):
    return lax.dot_general(a, b, (((1,), (1,)), ((), ())), preferred_element_type=F32)


def _dot_tn(a, b):
    return lax.dot_general(a, b, (((0,), (0,)), ((), ())), preferred_element_type=F32)


def _idiv(x, n):
    assert n > 0 and n & (n - 1) == 0
    return x >> (n.bit_length() - 1)


def _sigmoid(x):
    return 1.0 / (1.0 + jnp.exp(-x))


def _split_bf16(x, terms):
    out = []
    rem = x
    for _ in range(terms):
        piece = rem.astype(BF16)
        out.append(piece)
        rem = rem - piece.astype(F32)
    return out


def _norm_mod(x, g, sc, sh):
    ms = jnp.mean(x * x, axis=-1, keepdims=True)
    return (x * lax.rsqrt(ms + EPS) * g) * (1.0 + sc) + sh


def _rms(x, g):
    ms = jnp.mean(x * x, axis=-1, keepdims=True)
    return x * lax.rsqrt(ms + EPS) * g


def _mod(ref):
    return ref[0] if len(ref.shape) == 3 else ref[...]


def _lam(q1_ref, k1_ref, q2_ref, k2_ref):
    s1 = jnp.sum(q1_ref[...] * k1_ref[...], axis=-1, keepdims=True)
    s2 = jnp.sum(q2_ref[...] * k2_ref[...], axis=-1, keepdims=True)
    return jnp.exp(s1) - jnp.exp(s2) + LAM_INIT


def _ada_kernel(c_ref, w_ref, b_ref, o_ref):
    c = c_ref[...]
    s = (c * _sigmoid(c)).astype(BF16)
    o_ref[...] = _dot(s, w_ref[...].astype(BF16)) + b_ref[...]


def _ada_call(c_all, w_ada, b_ada):
    rows = c_all.shape[0]
    n = w_ada.shape[1]
    tn = 1024
    return pl.pallas_call(
        _ada_kernel,
        grid=(n // tn,),
        in_specs=[
            pl.BlockSpec((rows, D_MODEL), lambda j: (0, 0)),
            pl.BlockSpec((D_MODEL, tn), lambda j: (0, j)),
            pl.BlockSpec((1, tn), lambda j: (0, j)),
        ],
        out_specs=pl.BlockSpec((rows, tn), lambda j: (0, j)),
        out_shape=jax.ShapeDtypeStruct((rows, n), F32),
        compiler_params=_cparams("arbitrary"),
        name="ada_mod",
    )(c_all, w_ada, b_ada)


def _qkv_kernel(x_ref, g_ref, sc_ref, sh_ref, w_ref, kf_ref, vf_ref, qkvb_ref):
    h = _norm_mod(x_ref[...], g_ref[...], _mod(sc_ref), _mod(sh_ref)).astype(BF16)
    q = _dot(h, w_ref[:, 0:A_WIDTH])
    qkvb_ref[:, 0:A_WIDTH] = (q * Q_SCALE).astype(BF16)
    k = _dot(h, w_ref[:, A_WIDTH:2 * A_WIDTH])
    kf_ref[...] = k
    qkvb_ref[:, A_WIDTH:2 * A_WIDTH] = k.astype(BF16)
    v = _dot(h, w_ref[:, 2 * A_WIDTH:3 * A_WIDTH])
    vf_ref[...] = v
    qkvb_ref[:, 2 * A_WIDTH:3 * A_WIDTH] = v.astype(BF16)


def _rw_kernel(x_ref, g_ref, sc_ref, sh_ref, w_ref, o_ref):
    h = _norm_mod(x_ref[...], g_ref[...], _mod(sc_ref), _mod(sh_ref)).astype(BF16)
    o_ref[...] = _dot(h, w_ref[:, 3 * A_WIDTH:])


def _mod_spec(mod, tm, tiles_per_seq, nidx):
    if mod.ndim == 3:
        if nidx == 2:
            return pl.BlockSpec((1, 1, D_MODEL), lambda i, j: (i // tiles_per_seq, 0, 0))
        return pl.BlockSpec((1, 1, D_MODEL), lambda i: (i // tiles_per_seq, 0, 0))
    if nidx == 2:
        return pl.BlockSpec((tm, D_MODEL), lambda i, j: (i, 0))
    return pl.BlockSpec((tm, D_MODEL), lambda i: (i, 0))


def _resident(shape):
    return pl.BlockSpec(shape, lambda i: (0,) * len(shape), pipeline_mode=pl.Buffered(1))


def _in_proj_call(x, g, sc, sh, w_in, tm, seq_len):
    m = x.shape[0]
    n_in = w_in.shape[1]
    common = [
        pl.BlockSpec((tm, D_MODEL), lambda i: (i, 0)),
        _resident((1, D_MODEL)),
        _mod_spec(sc, tm, seq_len // tm, 1),
        _mod_spec(sh, tm, seq_len // tm, 1),
    ]
    kf, vf, qkvb = pl.pallas_call(
        _qkv_kernel,
        grid=(m // tm,),
        in_specs=common + [_resident((D_MODEL, n_in))],
        out_specs=[
            pl.BlockSpec((tm, A_WIDTH), lambda i: (i, 0)),
            pl.BlockSpec((tm, A_WIDTH), lambda i: (i, 0)),
            pl.BlockSpec((tm, 3 * A_WIDTH), lambda i: (i, 0)),
        ],
        out_shape=[
            jax.ShapeDtypeStruct((m, A_WIDTH), F32),
            jax.ShapeDtypeStruct((m, A_WIDTH), F32),
            jax.ShapeDtypeStruct((m, 3 * A_WIDTH), BF16),
        ],
        compiler_params=_cparams("arbitrary"),
        name="in_proj_qkv",
    )(x, g, sc, sh, w_in)
    rw = pl.pallas_call(
        _rw_kernel,
        grid=(m // tm,),
        in_specs=common + [_resident((D_MODEL, n_in))],
        out_specs=pl.BlockSpec((tm, RW_COLS), lambda i: (i, 0)),
        out_shape=jax.ShapeDtypeStruct((m, RW_COLS), F32),
        compiler_params=_cparams("arbitrary"),
        name="in_proj_rw",
    )(x, g, sc, sh, w_in)
    return kf, vf, qkvb, rw


def _softmax_step(s, vb, m_ref, l_ref, acc_ref, idx):
    m_prev = m_ref[idx]
    m_new = jnp.maximum(m_prev, jnp.max(s, axis=-1, keepdims=True))
    alpha = jnp.exp2(m_prev - m_new)
    p = jnp.exp2(s - m_new)
    l_ref[idx] = alpha * l_ref[idx] + jnp.sum(p, axis=-1, keepdims=True)
    acc_ref[idx] = alpha * acc_ref[idx] + _dot(p.astype(BF16), vb)
    m_ref[idx] = m_new


def _subln(o, g):
    return _rms(o, g) * (1.0 - LAM_INIT)


PATTN_HEADS = 2
PATTN_QG = 512


def _pattn_kernel(q1_ref, k1_ref, q2_ref, k2_ref, q_ref, k_ref, v_ref, g_ref, o_ref,
                  m_scr, l_scr, acc_scr, s_scr, t_scr, *, tq):
    i = pl.program_id(2)
    lane = lax.broadcasted_iota(jnp.int32, (tq, A_DV), 1)
    m_scr[...] = jnp.full(m_scr.shape, NEG_BIG, F32)
    l_scr[...] = jnp.zeros(l_scr.shape, F32)
    acc_scr[...] = jnp.zeros(acc_scr.shape, F32)

    qg = min(PATTN_QG, 2 * tq)
    n_qg = 2 * tq // qg
    q_gs = []
    for hh in range(PATTN_HEADS):
        q = q_ref[0, :, hh * A_DV:(hh + 1) * A_DV]
        zero = jnp.zeros_like(q)
        q_st = jnp.concatenate([jnp.where(lane < A_DK, q, zero), jnp.where(lane >= A_DK, q, zero)], axis=0)
        q_gs.append([q_st[g * qg:(g + 1) * qg] for g in range(n_qg)])
    ones_k = jnp.ones((SUBLANES, tq), BF16)
    chains = [(hh, g) for g in range(n_qg) for hh in range(PATTN_HEADS)]

    def scores(j):
        r0 = pl.multiple_of(j * tq, tq)
        kbs = [k_ref[0, pl.ds(r0, tq), hh * A_DV:(hh + 1) * A_DV] for hh in range(PATTN_HEADS)]
        return [_dot_nt(kbs[hh], q_gs[hh][g]) for hh, g in chains]

    def update(j, s_ref, masks):
        r0 = pl.multiple_of(j * tq, tq)
        vbs = [v_ref[0, pl.ds(r0, tq), hh * A_DV:(hh + 1) * A_DV] for hh in range(PATTN_HEADS)]
        for ci, (hh, g) in enumerate(chains):
            cs = slice(g * qg, (g + 1) * qg)
            s_t = s_ref[ci]
            if masks is not None:
                s_t = jnp.where(masks[g], s_t, NEG_BIG)
            m_prev = m_scr[hh, :, cs]
            m_new = jnp.maximum(m_prev, jnp.max(s_t, axis=0, keepdims=True))
            alpha = jnp.exp2(m_prev - m_new)
            p_b = jnp.exp2(s_t - m_new).astype(BF16)
            l_scr[hh, :, cs] = alpha * l_scr[hh, :, cs] + _dot(ones_k, p_b)[0:1]
            acc_scr[hh, :, cs] = alpha * acc_scr[hh, :, cs] + _dot_tn(vbs[hh], p_b)
            m_scr[hh, :, cs] = m_new

    def put(s_ref, j):
        for ci, s_t in enumerate(scores(j)):
            s_ref[ci] = s_t

    rk = lax.broadcasted_iota(jnp.int32, (tq, qg), 0)
    cq = lax.broadcasted_iota(jnp.int32, (tq, qg), 1)
    diag = [_idiv(rk, ATTN_CHUNK) <= _idiv((cq + g * qg) & (tq - 1), ATTN_CHUNK) for g in range(n_qg)]

    put(s_scr, 0)

    def body(jp, carry):
        j = 2 * jp
        put(t_scr, j + 1)
        update(j, s_scr, None)
        put(s_scr, j + 2)
        update(j + 1, t_scr, None)
        return carry

    lax.fori_loop(0, i // 2, body, 0)

    @pl.when(i % 2 == 0)
    def _():
        update(i, s_scr, diag)

    @pl.when(i % 2 == 1)
    def _():
        put(t_scr, i)
        update(i - 1, s_scr, None)
        update(i, t_scr, diag)

    lam = _lam(q1_ref, k1_ref, q2_ref, k2_ref)
    for hh in range(PATTN_HEADS):
        o_t = (acc_scr[hh, :, 0:tq] / l_scr[hh, :, 0:tq]
               - lam * (acc_scr[hh, :, tq:2 * tq] / l_scr[hh, :, tq:2 * tq]))
        o_ref[0, :, hh * A_DV:(hh + 1) * A_DV] = _subln(o_t.T, g_ref[...]).astype(BF16)


def _lam_specs(nidx):
    if nidx == 3:
        return [pl.BlockSpec((1, A_DK), lambda b, h, i: (0, 0))] * 4
    return [pl.BlockSpec((1, A_DK), lambda b, j: (0, 0))] * 4


def _pattn_call(lams, qkvb, g_subln, tq):
    bsz, t, _ = qkvb.shape
    hw = PATTN_HEADS * A_DV
    ng = A_HEADS // PATTN_HEADS
    return pl.pallas_call(
        functools.partial(_pattn_kernel, tq=tq),
        grid=(bsz, ng, t // tq),
        in_specs=_lam_specs(3) + [
            pl.BlockSpec((1, tq, hw), lambda b, h, i: (b, i, h)),
            pl.BlockSpec((1, t, hw), lambda b, h, i: (b, 0, ng + h)),
            pl.BlockSpec((1, t, hw), lambda b, h, i: (b, 0, 2 * ng + h)),
            pl.BlockSpec((1, A_DV), lambda b, h, i: (0, 0)),
        ],
        out_specs=pl.BlockSpec((1, tq, hw), lambda b, h, i: (b, i, h)),
        out_shape=jax.ShapeDtypeStruct((bsz, t, A_WIDTH), BF16),
        scratch_shapes=[
            pltpu.VMEM((PATTN_HEADS, 1, 2 * tq), F32),
            pltpu.VMEM((PATTN_HEADS, 1, 2 * tq), F32),
            pltpu.VMEM((PATTN_HEADS, A_DV, 2 * tq), F32),
        ] + [pltpu.VMEM((PATTN_HEADS * (2 * tq // min(PATTN_QG, 2 * tq)), tq, min(PATTN_QG, 2 * tq)), F32)] * 2,
        compiler_params=_cparams("arbitrary", "arbitrary", "arbitrary"),
        name="prompt_attention",
    )(*lams, qkvb, qkvb, qkvb, g_subln)


N_VQ = 2 * A_HEADS
SATTN_SUB = 1024
SATTN_BUFS = 3


def _sattn_kernel(q1_ref, k1_ref, q2_ref, k2_ref, q_ref, kn_ref, vn_ref, ck_hbm, cv_hbm, g_ref, o_ref,
                  qb_scr, m_scr, l_scr, acc_scr, kbuf, vbuf, sems, *, tnew, tk, n_steps):
    j = pl.program_id(1)
    nvq = N_VQ * tnew
    step_id = pl.program_id(0) * n_steps + j
    total = pl.num_programs(0) * n_steps
    rows = tk * A_HEADS

    def cache_copies(sid, slot):
        bb = sid // n_steps
        r0 = pl.multiple_of((sid % n_steps) * rows, rows)
        return (pltpu.make_async_copy(ck_hbm.at[bb, pl.ds(r0, rows), :], kbuf.at[slot], sems.at[0, slot]),
                pltpu.make_async_copy(cv_hbm.at[bb, pl.ds(r0, rows), :], vbuf.at[slot], sems.at[1, slot]))

    @pl.when(step_id == 0)
    def _():
        for d in range(SATTN_BUFS - 1):
            for cp in cache_copies(d, d):
                cp.start()

    ahead = step_id + (SATTN_BUFS - 1)

    @pl.when(ahead < total)
    def _():
        for cp in cache_copies(ahead, ahead % SATTN_BUFS):
            cp.start()

    slot = step_id % SATTN_BUFS
    for cp in cache_copies(step_id, slot):
        cp.wait()

    @pl.when(j == 0)
    def _():
        q = q_ref[0]
        qt = jnp.concatenate([q] * N_VQ, axis=0)
        row = lax.broadcasted_iota(jnp.int32, (nvq, A_WIDTH), 0)
        col = lax.broadcasted_iota(jnp.int32, (nvq, A_WIDTH), 1)
        qb_scr[...] = jnp.where(_idiv(row, tnew) == _idiv(col, A_DK), qt, jnp.zeros_like(qt))
        m_scr[...] = jnp.full(m_scr.shape, NEG_BIG, F32)
        l_scr[...] = jnp.zeros(l_scr.shape, F32)
        acc_scr[...] = jnp.zeros(acc_scr.shape, F32)

    def step(kb, vb):
        _softmax_step(_dot_nt(qb_scr[...], kb), vb, m_scr, l_scr, acc_scr, 0)

    def cached_rows(buf, p0, n):
        heads = [buf[slot, pl.ds(p0 * A_HEADS + h, n, stride=A_HEADS), :] for h in range(A_HEADS)]
        return jnp.concatenate(heads, axis=-1).astype(BF16)

    n_sub = max(1, tk // SATTN_SUB)
    sub = tk // n_sub
    kbs = [cached_rows(kbuf, c * sub, sub) for c in range(n_sub)]
    s_all = [_dot_nt(qb_scr[...], kb) for kb in kbs]
    for c in range(n_sub):
        _softmax_step(s_all[c], cached_rows(vbuf, c * sub, sub), m_scr, l_scr, acc_scr, 0)

    @pl.when(j == pl.num_programs(1) - 1)
    def _():
        step(kn_ref[0], vn_ref[0])
        lam = _lam(q1_ref, k1_ref, q2_ref, k2_ref)
        g = g_ref[...]
        outs = []
        for h in range(A_HEADS):
            cs = slice(h * A_DV, (h + 1) * A_DV)
            r0 = slice((2 * h) * tnew, (2 * h + 1) * tnew)
            r1 = slice((2 * h + 1) * tnew, (2 * h + 2) * tnew)
            outs.append(acc_scr[0, r0, cs] / l_scr[0, r0, :] - lam * (acc_scr[0, r1, cs] / l_scr[0, r1, :]))
        mss = [jnp.mean(o * o, axis=-1, keepdims=True) for o in outs]
        for h in range(A_HEADS):
            o_ref[0, :, h * A_DV:(h + 1) * A_DV] = (
                outs[h] * lax.rsqrt(mss[h] + EPS) * g * (1.0 - LAM_INIT)).astype(BF16)


def _sattn_call(lams, qkvb, cache_k, cache_v, g_subln, tk):
    bsz, tnew, _ = qkvb.shape
    past = cache_k.shape[1] // A_HEADS
    nvq = N_VQ * tnew
    n_steps = past // tk
    assert bsz * n_steps >= SATTN_BUFS - 1
    return pl.pallas_call(
        functools.partial(_sattn_kernel, tnew=tnew, tk=tk, n_steps=n_steps),
        grid=(bsz, n_steps),
        in_specs=_lam_specs(2) + [
            pl.BlockSpec((1, tnew, A_WIDTH), lambda b, j: (b, 0, 0)),
            pl.BlockSpec((1, tnew, A_WIDTH), lambda b, j: (b, 0, 1)),
            pl.BlockSpec((1, tnew, A_WIDTH), lambda b, j: (b, 0, 2)),
            pl.BlockSpec(memory_space=pl.ANY),
            pl.BlockSpec(memory_space=pl.ANY),
            pl.BlockSpec((1, A_DV), lambda b, j: (0, 0)),
        ],
        out_specs=pl.BlockSpec((1, tnew, A_WIDTH), lambda b, j: (b, 0, 0)),
        out_shape=jax.ShapeDtypeStruct((bsz, tnew, A_WIDTH), BF16),
        scratch_shapes=[
            pltpu.VMEM((nvq, A_WIDTH), BF16),
            pltpu.VMEM((1, nvq, 1), F32),
            pltpu.VMEM((1, nvq, 1), F32),
            pltpu.VMEM((1, nvq, A_WIDTH), F32),
            pltpu.VMEM((SATTN_BUFS, tk * A_HEADS, A_DV), cache_k.dtype),
            pltpu.VMEM((SATTN_BUFS, tk * A_HEADS, A_DV), cache_v.dtype),
            pltpu.SemaphoreType.DMA((2, SATTN_BUFS)),
        ],
        compiler_params=_cparams("arbitrary", "arbitrary"),
        name="sample_attention",
    )(*lams, qkvb, qkvb, qkvb, cache_k, cache_v, g_subln)


def _head_sum(x, ones_bd):
    return _dot(x.astype(BF16), ones_bd)


def _rwkv_kernel(p_ref, ph_ref, st_ref, s0_ref, mu_ref, w0_ref, a0_ref, kk_ref, ka_ref, rk_ref,
                 lnw_ref, lnb_ref, ww2_ref, wa2_ref, wg2_ref, ob_ref, so_ref,
                 r_scr, k_scr, v_scr, na_scr, nb_scr, ld_scr, cum_scr, g_scr, bon_scr, y_scr, s_scr,
                 *, tb, chunk):
    i = pl.program_id(1)
    n_seq = p_ref.shape[0]
    c_len = chunk
    n_chunks = tb // c_len

    row_g = lax.broadcasted_iota(jnp.int32, (MXU_DIM, MXU_DIM), 0)
    col_g = lax.broadcasted_iota(jnp.int32, (MXU_DIM, MXU_DIM), 1)
    bd_mask = _idiv(row_g, B_HEAD) == _idiv(col_g, B_HEAD)
    ones_bd = jnp.where(bd_mask, 1.0, 0.0).astype(BF16)

    @pl.when(i == 0)
    def _():
        for b in range(n_seq):
            for q in range(N_GROUPS):
                narrow = s0_ref[b, q]
                wide = jnp.concatenate([narrow] * GROUP_HEADS, axis=1)
                s_scr[b, q] = jnp.where(bd_mask, wide, 0.0)

    row_t = lax.broadcasted_iota(jnp.int32, (tb, tb), 0)
    col_t = lax.broadcasted_iota(jnp.int32, (tb, tb), 1)
    tri = jnp.where((_idiv(row_t, c_len) == _idiv(col_t, c_len)) & (col_t <= row_t), 1.0, 0.0).astype(BF16)
    w_cat = jnp.concatenate([ww2_ref[...], wa2_ref[...]], axis=0)

    for b in range(n_seq):
        p = p_ref[b]
        prev_row = jnp.where(i == 0, st_ref[b], ph_ref[b, SUBLANES - 1:SUBLANES, :])
        rowi = lax.broadcasted_iota(jnp.int32, p.shape, 0)
        p_prev = jnp.where(rowi == 0, prev_row, pltpu.roll(p, 1, 0))
        xs = p + (p_prev - p) * mu_ref[...]
        r = xs[:, 0:B_WIDTH]
        k = xs[:, B_WIDTH:2 * B_WIDTH]
        v = xs[:, 2 * B_WIDTH:3 * B_WIDTH]
        wa = xs[:, 3 * B_WIDTH:3 * B_WIDTH + W_RANK + A_RANK]
        gd = xs[:, 3 * B_WIDTH + W_RANK + A_RANK:]

        lane_wa = lax.broadcasted_iota(jnp.int32, wa.shape, 1)
        zero_wa = jnp.zeros_like(wa)
        w_lin = _dot(jnp.where(lane_wa < W_RANK, jnp.tanh(wa), zero_wa).astype(BF16), w_cat)
        a_lin = _dot(jnp.where(lane_wa >= W_RANK, wa, zero_wa).astype(BF16), w_cat)
        logd = -math.exp(-0.5) * _sigmoid(w0_ref[...] + w_lin)
        a = _sigmoid(a0_ref[...] + a_lin)
        g_scr[b] = _dot(_sigmoid(gd).astype(BF16), wg2_ref[...])

        k2 = k * (1.0 + (a - 1.0) * ka_ref[...])
        kk = k * kk_ref[...]
        rkk = r * k2 * rk_ref[...]
        for q in range(N_GROUPS):
            ls = slice(q * MXU_DIM, (q + 1) * MXU_DIM)
            kq = kk[:, ls]
            kn = kq * lax.rsqrt(jnp.maximum(_head_sum(kq * kq, ones_bd), 1e-24))
            na_scr[b, :, ls] = -kn
            nb_scr[b, :, ls] = kn * a[:, ls]
            bon_scr[b, :, ls] = _head_sum(rkk[:, ls], ones_bd) * v[:, ls]
        r_scr[b] = r
        k_scr[b] = k2
        v_scr[b] = v
        ld_scr[b] = logd

        cum = None
        for piece in _split_bf16(logd, 2):
            d = _dot(tri, piece)
            cum = d if cum is None else cum + d
        cum_scr[b] = cum

    gc = GROUP_HEADS * c_len
    lane_q = lax.broadcasted_iota(jnp.int32, (1, MXU_DIM), 1)
    head_masks = [_idiv(lane_q, B_HEAD) == j for j in range(GROUP_HEADS)]
    row_s = lax.broadcasted_iota(jnp.int32, (gc, gc), 0)
    col_s = lax.broadcasted_iota(jnp.int32, (gc, gc), 1)
    same_head = _idiv(row_s, c_len) == _idiv(col_s, c_len)
    strict = same_head & (col_s < row_s)
    incl = same_head & (col_s <= row_s)
    eye = jnp.where(col_s == row_s, 1.0, 0.0).astype(F32)
    n_double = int(math.log2(c_len)) - 1
    chains = [(b, q) for q in range(N_GROUPS) for b in range(n_seq)]
    n_un = 2 if n_chunks % 2 == 0 else 1
    work = [(u, b, q) for (b, q) in chains for u in range(n_un)]
    groups = range(len(work))

    def stack(x):
        z = jnp.zeros_like(x)
        return jnp.concatenate([jnp.where(head_masks[j], x, z) for j in range(GROUP_HEADS)], axis=0)

    def unstack(x):
        out = x[0:c_len]
        for j in range(1, GROUP_HEADS):
            out = out + x[j * c_len:(j + 1) * c_len]
        return out

    def chunk_body(cp, carry):
        rows_u = [pl.ds(pl.multiple_of((cp * n_un + u) * c_len, c_len), c_len) for u in range(n_un)]
        a_s, r_s, b_s, k_s, be_s, ke_s, v_s, dec, ar_c = [], [], [], [], [], [], [], [], []
        for u, b, q in work:
            rows = rows_u[u]
            ls = slice(q * MXU_DIM, (q + 1) * MXU_DIM)
            cum_q = cum_scr[b, rows, ls]
            k_q = k_scr[b, rows, ls]
            nb_q = nb_scr[b, rows, ls]
            cum_last = cum_q[c_len - 1:c_len, :]
            e_out = jnp.exp(-cum_q)
            e_end = jnp.exp(cum_last - cum_q)
            a_c = (na_scr[b, rows, ls] * jnp.exp(cum_q - ld_scr[b, rows, ls])).astype(BF16)
            r_c = (r_scr[b, rows, ls] * jnp.exp(cum_q)).astype(BF16)
            ar_c.append(jnp.concatenate([a_c, r_c], axis=0))
            a_s.append(stack(a_c))
            r_s.append(stack(r_c))
            b_s.append(stack((nb_q * e_out).astype(BF16)))
            k_s.append(stack((k_q * e_out).astype(BF16)))
            be_s.append(stack((nb_q * e_end).astype(BF16)))
            ke_s.append(stack((k_q * e_end).astype(BF16)))
            v_s.append(stack(v_scr[b, rows, ls].astype(BF16)))
            dec.append(jnp.exp(cum_last))

        if gc % LANES == 0:
            g_all = [_dot_nt(jnp.concatenate([a_s[q], r_s[q]], axis=0),
                             jnp.concatenate([b_s[q], k_s[q]], axis=0)) for q in groups]
            g_ab = [g[0:gc, 0:gc] for g in g_all]
            g_ak = [g[0:gc, gc:2 * gc] for g in g_all]
            g_rb = [g[gc:2 * gc, 0:gc] for g in g_all]
            g_rk = [g[gc:2 * gc, gc:2 * gc] for g in g_all]
        else:
            g_ab = [_dot_nt(a_s[q], b_s[q]) for q in groups]
            g_ak = [_dot_nt(a_s[q], k_s[q]) for q in groups]
            g_rb = [_dot_nt(r_s[q], b_s[q]) for q in groups]
            g_rk = [_dot_nt(r_s[q], k_s[q]) for q in groups]
        aab = [jnp.where(strict, g, 0.0) for g in g_ab]
        aak = [jnp.where(strict, g, 0.0).astype(BF16) for g in g_ak]
        arb = [jnp.where(incl, g, 0.0).astype(BF16) for g in g_rb]
        ark = [jnp.where(incl, g, 0.0).astype(BF16) for g in g_rk]

        t_m = [eye + x for x in aab]
        pw = [x.astype(BF16) for x in aab]
        pw = [_dot(x, x) for x in pw]
        for m in range(n_double):
            pb = [x.astype(BF16) for x in pw]
            if m + 1 < n_double:
                pw = [_dot(x, x) for x in pb]
            t_m = [t_m[q] + _dot(t_m[q].astype(BF16), pb[q]) for q in groups]
        t_b = [x.astype(BF16) for x in t_m]

        av = [_dot(aak[q], v_s[q]) for q in groups]

        state = {bq: s_scr[bq[0], bq[1]] for bq in chains}
        for u in range(n_un):
            idx = [wi for wi in groups if work[wi][0] == u]
            s_prev = [state[work[wi][1:]] for wi in idx]
            ar = [_dot_nt(ar_c[wi], s.astype(BF16)) for wi, s in zip(idx, s_prev)]
            u_b = [_dot(t_b[wi], (stack(a[0:c_len]) + av[wi]).astype(BF16)).astype(BF16)
                   for wi, a in zip(idx, ar)]
            uv = [jnp.concatenate([ub, v_s[wi]], axis=0) for wi, ub in zip(idx, u_b)]
            if gc % LANES == 0:
                y_st = [_dot(jnp.concatenate([arb[wi], ark[wi]], axis=1), x) for wi, x in zip(idx, uv)]
            else:
                y_st = [_dot(arb[wi], ub) + _dot(ark[wi], v_s[wi]) for wi, ub in zip(idx, u_b)]
            s_new = [s * dec[wi] + _dot_tn(x, jnp.concatenate([be_s[wi], ke_s[wi]], axis=0))
                     for wi, s, x in zip(idx, s_prev, uv)]
            for k, wi in enumerate(idx):
                _, b, q = work[wi]
                state[(b, q)] = s_new[k]
                y_scr[b, rows_u[u], q * MXU_DIM:(q + 1) * MXU_DIM] = ar[k][c_len:2 * c_len] + unstack(y_st[k])
        for (b, q), s in state.items():
            s_scr[b, q] = s
        return carry

    lax.fori_loop(0, n_chunks // n_un, chunk_body, 0)

    @pl.when(i == pl.num_programs(1) - 1)
    def _():
        for b, q in chains:
            wide = s_scr[b, q]
            narrow = wide[:, 0:B_HEAD]
            for j in range(1, GROUP_HEADS):
                narrow = narrow + wide[:, j * B_HEAD:(j + 1) * B_HEAD]
            so_ref[b, q] = narrow

    inv_n = 1.0 / B_HEAD
    for b, q in chains:
        ls = slice(q * MXU_DIM, (q + 1) * MXU_DIM)
        y = y_scr[b, :, ls]
        mu = _head_sum(y, ones_bd) * inv_n
        d = y - mu
        var = _head_sum(d * d, ones_bd) * inv_n
        yn = d * lax.rsqrt(var + GN_EPS) * lnw_ref[:, ls] + lnb_ref[:, ls]
        ob_ref[b, :, ls] = ((yn + bon_scr[b, :, ls]) * g_scr[b, :, ls]).astype(BF16)


def _rwkv_call(rw, shift0, s0, prm, tb, chunk, n_seq):
    bsz, t, _ = rw.shape
    assert bsz % n_seq == 0
    nt = t // tb
    hb = tb // SUBLANES
    vec = lambda n: pl.BlockSpec((1, n), lambda b, i: (0, 0))
    return pl.pallas_call(
        functools.partial(_rwkv_kernel, tb=tb, chunk=chunk),
        grid=(bsz // n_seq, nt),
        in_specs=[
            pl.BlockSpec((n_seq, tb, RW_COLS), lambda b, i: (b, i, 0)),
            pl.BlockSpec((n_seq, SUBLANES, RW_COLS), lambda b, i: (b, jnp.maximum(i * hb - 1, 0), 0)),
            pl.BlockSpec((n_seq, 1, RW_COLS), lambda b, i: (b, 0, 0)),
            pl.BlockSpec((n_seq, N_GROUPS, MXU_DIM, B_HEAD), lambda b, i: (b, 0, 0, 0)),
            vec(RW_COLS), vec(B_WIDTH), vec(B_WIDTH), vec(B_WIDTH), vec(B_WIDTH), vec(B_WIDTH),
            vec(B_WIDTH), vec(B_WIDTH),
            pl.BlockSpec((W_RANK, B_WIDTH), lambda b, i: (0, 0)),
            pl.BlockSpec((A_RANK, B_WIDTH), lambda b, i: (0, 0)),
            pl.BlockSpec((G_RANK, B_WIDTH), lambda b, i: (0, 0)),
        ],
        out_specs=[
            pl.BlockSpec((n_seq, tb, B_WIDTH), lambda b, i: (b, i, 0)),
            pl.BlockSpec((n_seq, N_GROUPS, MXU_DIM, B_HEAD), lambda b, i: (b, 0, 0, 0)),
        ],
        out_shape=[
            jax.ShapeDtypeStruct((bsz, t, B_WIDTH), BF16),
            jax.ShapeDtypeStruct((bsz, N_GROUPS, MXU_DIM, B_HEAD), F32),
        ],
        scratch_shapes=[pltpu.VMEM((n_seq, tb, B_WIDTH), F32)] * 10
        + [pltpu.VMEM((n_seq, N_GROUPS, MXU_DIM, MXU_DIM), F32)],
        compiler_params=_cparams("arbitrary", "arbitrary"),
        name="rwkv7_mix",
    )(rw, rw, shift0, s0, prm["mu_shift"], prm["w0"], prm["a0"], prm["k_k"], prm["k_a"], prm["r_k"],
      prm["ln_x_w"], prm["ln_x_b"], prm["w_w2"], prm["w_a2"], prm["w_g2"])


def _state_to_groups(s):
    return s.reshape(s.shape[0], N_GROUPS, MXU_DIM, B_HEAD)


def _groups_to_state(sg):
    return sg.reshape(sg.shape[0], B_HEADS, B_HEAD, B_HEAD)


def _mixout_kernel(oa_ref, ob_ref, x_ref, w_ref, g_ref, gt_ref, o_ref):
    mix = _dot(oa_ref[...], w_ref[0:A_WIDTH, :]) + _dot(ob_ref[...], w_ref[A_WIDTH:D_MODEL, :])
    o_ref[...] = x_ref[...] + _mod(gt_ref) * _rms(mix, g_ref[...])


def _mixout_call(oa, ob, x, w_out, g, gt, tm, tiles_per_seq):
    m = x.shape[0]
    return pl.pallas_call(
        _mixout_kernel,
        grid=(m // tm,),
        in_specs=[
            pl.BlockSpec((tm, A_WIDTH), lambda i: (i, 0)),
            pl.BlockSpec((tm, B_WIDTH), lambda i: (i, 0)),
            pl.BlockSpec((tm, D_MODEL), lambda i: (i, 0)),
            pl.BlockSpec((D_MODEL, D_MODEL), lambda i: (0, 0)),
            pl.BlockSpec((1, D_MODEL), lambda i: (0, 0)),
            _mod_spec(gt, tm, tiles_per_seq, 1),
        ],
        out_specs=pl.BlockSpec((tm, D_MODEL), lambda i: (i, 0)),
        out_shape=jax.ShapeDtypeStruct((m, D_MODEL), F32),
        compiler_params=_cparams("arbitrary"),
        name="mix_out",
    )(oa, ob, x, w_out, g, gt)


FFN_HALO = BF16_ROWS
FFN_TF = 512


def _gated(zg, zv):
    return (zg * _sigmoid(zg) * zv).astype(BF16)


def _ffn_kernel(x_ref, xh_ref, cpg_ref, cpv_ref, g_ref, sc_ref, sh_ref, gt_ref, wg_ref, wv_ref,
                cg_ref, cv_ref, wd_ref, gp_ref, o_ref, clg_ref, clv_ref, h_scr,
                *, tm, tiles_per_seq):
    mi = pl.program_id(0)
    f = pl.program_id(1)
    first = (mi % tiles_per_seq) == 0

    @pl.when(f == 0)
    def _():
        g, sc, sh = g_ref[...], sc_ref[0], sh_ref[0]
        h_scr[FFN_HALO:, :] = _norm_mod(x_ref[...], g, sc, sh).astype(BF16)
        h_scr[:FFN_HALO, :] = _norm_mod(xh_ref[...], g, sc, sh).astype(BF16)
        o_ref[...] = jnp.zeros(o_ref.shape, F32)

    h = h_scr[...]
    row = lax.broadcasted_iota(jnp.int32, (FFN_HALO, FFN_TF), 0)

    def conv(u, cp_ref, c_ref, cl_ref):
        cp = cp_ref[0]
        head = u[:FFN_HALO]
        head_first = jnp.where(row == FFN_HALO - 2, cp[0:1],
                               jnp.where(row == FFN_HALO - 1, cp[1:2], head))
        ue = jnp.concatenate([jnp.where(first, head_first, head), u[FFN_HALO:]], axis=0)
        cl_ref[0] = ue[tm + FFN_HALO - SUBLANES:]
        c = c_ref[...]
        return (c[0:1] * pltpu.roll(ue, 2, 0)[FFN_HALO:] + c[1:2] * pltpu.roll(ue, 1, 0)[FFN_HALO:]
                + c[2:3] * ue[FFN_HALO:])

    zg = conv(_dot(h, wg_ref[...]), cpg_ref, cg_ref, clg_ref)
    zv = conv(_dot(h, wv_ref[...]), cpv_ref, cv_ref, clv_ref)
    o_ref[...] += _dot(_gated(zg, zv), wd_ref[...])

    @pl.when(f == pl.num_programs(1) - 1)
    def _():
        o_ref[...] = x_ref[...] + gt_ref[0] * _rms(o_ref[...], gp_ref[...])


def _ffn_call(x, conv_prev, g, sc, sh, gt, w_up, w_conv, w_down, gp, tm, tiles_per_seq):
    m = x.shape[0]
    nseq = conv_prev.shape[0]
    nf = D_FF // FFN_TF
    hb = tm // FFN_HALO
    seq = lambda i: i // tiles_per_seq
    modspec = pl.BlockSpec((1, 1, D_MODEL), lambda i, f: (seq(i), 0, 0))
    return pl.pallas_call(
        functools.partial(_ffn_kernel, tm=tm, tiles_per_seq=tiles_per_seq),
        grid=(m // tm, nf),
        in_specs=[
            pl.BlockSpec((tm, D_MODEL), lambda i, f: (i, 0), pipeline_mode=pl.Buffered(1)),
            pl.BlockSpec((FFN_HALO, D_MODEL), lambda i, f: (jnp.maximum(i * hb - 1, 0), 0)),
            pl.BlockSpec((1, FFN_CONV - 1, FFN_TF), lambda i, f: (seq(i), 0, f)),
            pl.BlockSpec((1, FFN_CONV - 1, FFN_TF), lambda i, f: (seq(i), 0, f + nf)),
            pl.BlockSpec((1, D_MODEL), lambda i, f: (0, 0)),
            modspec, modspec, modspec,
            pl.BlockSpec((D_MODEL, FFN_TF), lambda i, f: (0, f)),
            pl.BlockSpec((D_MODEL, FFN_TF), lambda i, f: (0, f + nf)),
            pl.BlockSpec((FFN_CONV, FFN_TF), lambda i, f: (0, f)),
            pl.BlockSpec((FFN_CONV, FFN_TF), lambda i, f: (0, f + nf)),
            pl.BlockSpec((FFN_TF, D_MODEL), lambda i, f: (f, 0)),
            pl.BlockSpec((1, D_MODEL), lambda i, f: (0, 0)),
        ],
        out_specs=[
            pl.BlockSpec((tm, D_MODEL), lambda i, f: (i, 0)),
            pl.BlockSpec((1, SUBLANES, FFN_TF), lambda i, f: (i, 0, f)),
            pl.BlockSpec((1, SUBLANES, FFN_TF), lambda i, f: (i, 0, f)),
        ],
        out_shape=[
            jax.ShapeDtypeStruct((m, D_MODEL), F32),
            jax.ShapeDtypeStruct((m // tm, SUBLANES, D_FF), F32),
            jax.ShapeDtypeStruct((m // tm, SUBLANES, D_FF), F32),
        ],
        scratch_shapes=[pltpu.VMEM((tm + FFN_HALO, D_MODEL), BF16)],
        compiler_params=_cparams("arbitrary", "arbitrary"),
        name="conv_ffn_prompt",
    )(x, x, conv_prev, conv_prev, g, sc, sh, gt, w_up, w_up, w_conv, w_conv, w_down, gp)


def _ffn_s_kernel(x_ref, cpg_ref, cpv_ref, g_ref, sc_ref, sh_ref, gt_ref, wg_ref, wv_ref,
                  cg_ref, cv_ref, wd_ref, gp_ref, o_ref, ug_ref, uv_ref, h_scr, acc_scr, *, seq_len):
    f = pl.program_id(1)

    @pl.when(f == 0)
    def _():
        h_scr[...] = _norm_mod(x_ref[...], g_ref[...], sc_ref[...], sh_ref[...]).astype(BF16)
        acc_scr[...] = jnp.zeros(acc_scr.shape, F32)

    h = h_scr[...]
    tm = h.shape[0]
    pos = lax.broadcasted_iota(jnp.int32, (seq_len, FFN_TF), 0)

    def conv(u, cp_ref, c_ref, u_ref):
        u_ref[...] = u
        cp = cp_ref[...]
        u1 = pltpu.roll(u, 1, 0)
        u2 = pltpu.roll(u, 2, 0)
        p1, p2 = [], []
        for s in range(tm // seq_len):
            rs = slice(s * seq_len, (s + 1) * seq_len)
            c0, c1 = cp[2 * s:2 * s + 1], cp[2 * s + 1:2 * s + 2]
            p1.append(jnp.where(pos == 0, c1, u1[rs]))
            p2.append(jnp.where(pos == 0, c0, jnp.where(pos == 1, c1, u2[rs])))
        c = c_ref[...]
        return c[0:1] * jnp.concatenate(p2, axis=0) + c[1:2] * jnp.concatenate(p1, axis=0) + c[2:3] * u

    zg = conv(_dot(h, wg_ref[...]), cpg_ref, cg_ref, ug_ref)
    zv = conv(_dot(h, wv_ref[...]), cpv_ref, cv_ref, uv_ref)
    acc_scr[...] += _dot(_gated(zg, zv), wd_ref[...])

    @pl.when(f == pl.num_programs(1) - 1)
    def _():
        o_ref[...] = x_ref[...] + gt_ref[...] * _rms(acc_scr[...], gp_ref[...])


def _ffn_s_call(x, conv_prev, g, sc, sh, gt, w_up, w_conv, w_down, gp, seq_len, tm):
    m = x.shape[0]
    assert tm % seq_len == 0 and seq_len & (seq_len - 1) == 0
    nf = D_FF // FFN_TF
    n_cp = (tm // seq_len) * (FFN_CONV - 1)
    full = pl.BlockSpec((tm, D_MODEL), lambda i, f: (i, 0))
    full_in = pl.BlockSpec((tm, D_MODEL), lambda i, f: (i, 0), pipeline_mode=pl.Buffered(1))
    lo = pl.BlockSpec((tm, FFN_TF), lambda i, f: (i, f))
    return pl.pallas_call(
        functools.partial(_ffn_s_kernel, seq_len=seq_len),
        grid=(m // tm, nf),
        in_specs=[
            full_in,
            pl.BlockSpec((n_cp, FFN_TF), lambda i, f: (i, f)),
            pl.BlockSpec((n_cp, FFN_TF), lambda i, f: (i, f + nf)),
            pl.BlockSpec((1, D_MODEL), lambda i, f: (0, 0)),
            full_in, full_in, full_in,
            pl.BlockSpec((D_MODEL, FFN_TF), lambda i, f: (0, f)),
            pl.BlockSpec((D_MODEL, FFN_TF), lambda i, f: (0, f + nf)),
            pl.BlockSpec((FFN_CONV, FFN_TF), lambda i, f: (0, f)),
            pl.BlockSpec((FFN_CONV, FFN_TF), lambda i, f: (0, f + nf)),
            pl.BlockSpec((FFN_TF, D_MODEL), lambda i, f: (f, 0)),
            pl.BlockSpec((1, D_MODEL), lambda i, f: (0, 0)),
        ],
        out_specs=[full, lo, lo],
        out_shape=[
            jax.ShapeDtypeStruct((m, D_MODEL), F32),
            jax.ShapeDtypeStruct((m, D_FF), F32),
            jax.ShapeDtypeStruct((m, D_FF), F32),
        ],
        scratch_shapes=[pltpu.VMEM((tm, D_MODEL), BF16), pltpu.VMEM((tm, D_MODEL), F32)],
        compiler_params=_cparams("arbitrary", "arbitrary"),
        name="conv_ffn_sample",
    )(x, conv_prev, conv_prev, g, sc, sh, gt, w_up, w_up, w_conv, w_conv, w_down, gp)


def _seq_tile(t, cap):
    tm = min(t, cap)
    assert t % tm == 0
    return tm


class _Tiles(NamedTuple):
    rows: int
    attn: int
    wkv_rows: int
    wkv_chunk: int
    wkv_seqs: int
    ffn_rows: int


def _prompt_tiles(bp, t):
    return _Tiles(rows=_seq_tile(t, 512), attn=_seq_tile(t, 512), wkv_rows=_seq_tile(t, 256),
                  wkv_chunk=min(t, 32), wkv_seqs=2 if bp % 2 == 0 else 1, ffn_rows=_seq_tile(t, 1024))


def _sample_tiles(bs, ts, past):
    rows = bs * ts
    return _Tiles(rows=rows, attn=_seq_tile(past, 1024), wkv_rows=ts, wkv_chunk=ts,
                  wkv_seqs=4 if bs % 4 == 0 else 1, ffn_rows=_seq_tile(rows, 512))


@jax.jit
def _forward(x_prompt, x_sample, c_prompt, c_sample, cache_k, cache_v, state_wkv, state_shift,
             state_ffn_conv, w_ada, b_ada, g_pre_mix, g_post_mix, g_pre_ffn, g_post_ffn, w_in,
             lam_q1, lam_k1, lam_q2, lam_k2, g_subln, mu_shift, w0, w_w2, a0, w_a2, w_g2, k_k, k_a,
             r_k, ln_x_w, ln_x_b, w_out, w_up, w_conv_ffn, w_down):
    bp, t, _ = x_prompt.shape
    bs, ts, _ = x_sample.shape
    past = cache_k.shape[2]
    assert w_ada.shape[0] == 1, "single-layer problem"

    n_c = bp + bs
    n_c_pad = -(-n_c // SUBLANES) * SUBLANES
    c_all = jnp.concatenate([c_prompt, c_sample, jnp.zeros((n_c_pad - n_c, D_MODEL), F32)], axis=0)
    mod = _ada_call(c_all, w_ada[0], b_ada)
    sh_m, sc_m, gt_m, sh_f, sc_f, gt_f = [mod[:, i * D_MODEL:(i + 1) * D_MODEL] for i in range(6)]
    p_mod = lambda a: a[:bp].reshape(bp, 1, D_MODEL)
    s_mod = lambda a: jnp.repeat(a[bp:n_c], ts, axis=0)

    w_in_b = w_in[0].astype(BF16)
    w_out_b = w_out[0].astype(BF16)
    w_up_b = w_up[0].astype(BF16)
    w_down_b = w_down[0].astype(BF16)
    lams = (lam_q1, lam_k1, lam_q2, lam_k2)
    prm = {
        "mu_shift": mu_shift, "w0": w0, "a0": a0, "k_k": k_k, "k_a": k_a,
        "r_k": r_k.reshape(1, B_WIDTH), "ln_x_w": ln_x_w, "ln_x_b": ln_x_b,
        "w_w2": w_w2[0].astype(BF16), "w_a2": w_a2[0].astype(BF16), "w_g2": w_g2[0].astype(BF16),
    }

    pt = _prompt_tiles(bp, t)
    tps_f = t // pt.ffn_rows
    xp = x_prompt.reshape(bp * t, D_MODEL)
    kf, vf, qkvb, rw = _in_proj_call(xp, g_pre_mix, p_mod(sc_m), p_mod(sh_m), w_in_b, pt.rows, t)
    oa = _pattn_call(lams, qkvb.reshape(bp, t, 3 * A_WIDTH), g_subln, pt.attn)
    ob, s_last = _rwkv_call(rw.reshape(bp, t, RW_COLS), jnp.zeros((bp, 1, RW_COLS), F32),
                            jnp.zeros((bp, N_GROUPS, MXU_DIM, B_HEAD), state_wkv.dtype), prm,
                            pt.wkv_rows, pt.wkv_chunk, pt.wkv_seqs)
    x1 = _mixout_call(oa.reshape(bp * t, A_WIDTH), ob.reshape(bp * t, B_WIDTH), xp, w_out_b,
                      g_post_mix, p_mod(gt_m), pt.rows, t // pt.rows)
    yp, clg, clv = _ffn_call(x1, jnp.zeros((bp, FFN_CONV - 1, 2 * D_FF), F32), g_pre_ffn, p_mod(sc_f),
                             p_mod(sh_f), p_mod(gt_f), w_up_b, w_conv_ffn[0], w_down_b, g_post_ffn,
                             pt.ffn_rows, tps_f)
    rw3 = rw.reshape(bp, t, RW_COLS)
    out_prompt = (
        yp.reshape(bp, t, D_MODEL),
        kf.reshape(1, bp, t, A_HEADS, 2 * A_DK),
        vf.reshape(1, bp, t, A_HEADS, A_DV),
        _groups_to_state(s_last)[None],
        rw3[:, -1][None],
        jnp.concatenate([clg[tps_f - 1::tps_f, -(FFN_CONV - 1):], clv[tps_f - 1::tps_f, -(FFN_CONV - 1):]],
                        axis=-1)[None],
    )

    ms = bs * ts
    st = _sample_tiles(bs, ts, past)
    xs = x_sample.reshape(ms, D_MODEL)
    kf_s, vf_s, qkvb_s, rw_s = _in_proj_call(xs, g_pre_mix, s_mod(sc_m), s_mod(sh_m), w_in_b, st.rows, ms)
    oa_s = _sattn_call(lams, qkvb_s.reshape(bs, ts, 3 * A_WIDTH),
                       cache_k[0].reshape(bs, past * A_HEADS, A_DV),
                       cache_v[0].reshape(bs, past * A_HEADS, A_DV), g_subln, st.attn)
    ob_s, s_last_s = _rwkv_call(rw_s.reshape(bs, ts, RW_COLS), state_shift[0][:, None, :],
                                _state_to_groups(state_wkv[0]), prm, st.wkv_rows, st.wkv_chunk, st.wkv_seqs)
    x1_s = _mixout_call(oa_s.reshape(ms, A_WIDTH), ob_s.reshape(ms, B_WIDTH), xs, w_out_b,
                        g_post_mix, s_mod(gt_m), st.rows, 1)
    cprev = state_ffn_conv[0].reshape(bs * (FFN_CONV - 1), 2 * D_FF)
    ys, ug, uv = _ffn_s_call(x1_s, cprev, g_pre_ffn, s_mod(sc_f), s_mod(sh_f), s_mod(gt_f),
                             w_up_b, w_conv_ffn[0], w_down_b, g_post_ffn, ts, st.ffn_rows)
    u_tail = jnp.concatenate([ug.reshape(bs, ts, D_FF)[:, -(FFN_CONV - 1):],
                              uv.reshape(bs, ts, D_FF)[:, -(FFN_CONV - 1):]], axis=-1)
    out_sample = (
        ys.reshape(bs, ts, D_MODEL),
        kf_s.reshape(1, bs, ts, A_HEADS, 2 * A_DK),
        vf_s.reshape(1, bs, ts, A_HEADS, A_DV),
        _groups_to_state(s_last_s)[None],
        rw_s.reshape(bs, ts, RW_COLS)[:, -1][None],
        u_tail[None],
    )
    return (out_prompt[0], out_sample[0]) + out_prompt[1:] + out_sample[1:]


def kernel(x_prompt, x_sample, c_prompt, c_sample, cache_k, cache_v, state_wkv, state_shift, state_ffn_conv, w_ada, b_ada, g_pre_mix, g_post_mix, g_pre_ffn, g_post_ffn, w_in, lam_q1, lam_k1, lam_q2, lam_k2, g_subln, mu_shift, w0, w_w2, a0, w_a2, w_g2, k_k, k_a, r_k, ln_x_w, ln_x_b, w_out, w_up, w_conv_ffn, w_down):
    return _forward(x_prompt, x_sample, c_prompt, c_sample, cache_k, cache_v, state_wkv, state_shift,
                    state_ffn_conv, w_ada, b_ada, g_pre_mix, g_post_mix, g_pre_ffn, g_post_ffn, w_in,
                    lam_q1, lam_k1, lam_q2, lam_k2, g_subln, mu_shift, w0, w_w2, a0, w_a2, w_g2, k_k,
                    k_a, r_k, ln_x_w, ln_x_b, w_out, w_up, w_conv_ffn, w_down)
```
